```python
import jax, jax.numpy as jnp
from jax import lax
import numpy as np

D_MODEL = 1024
BATCH = 32
SEQ = 256
DEPTH = 1
DEC_BATCH = 2
DEC_SEQ = 1024
PAST_LEN = 512

GRID_W = 64
D_MIX = D_MODEL
D_FOURIER = D_MIX // 2
N_FOURIER_GROUPS = 4
FOURIER_GROUP_W = D_FOURIER // N_FOURIER_GROUPS
D_RET = D_MIX - D_FOURIER
N_RET_HEADS = 4
HEAD_DIM = D_RET // N_RET_HEADS
CHUNK = 128
N_EXPERTS = 16
EC_CAPACITY_FACTOR = 2
D_EXPERT_FF = 2816
ROPE_BASE = 10000.0
EPS = 1e-6
D_IN_PROJ = D_FOURIER + 5 * D_RET
LOG_GAMMA_FWD = np.log(1.0 - 2.0 ** (-5.0 - np.arange(N_RET_HEADS))).astype(np.float32)
LOG_GAMMA_BWD = np.log(1.0 - 2.0 ** (-5.5 - np.arange(N_RET_HEADS))).astype(np.float32)

kernel_name = "hybrid_fnet_retention_ec_moe_diffusion_step"


def rmsnorm(x, g):
    xf = x.astype(jnp.float32)
    r = lax.rsqrt(jnp.mean(xf * xf, axis=-1, keepdims=True) + EPS)
    return (xf * r).astype(x.dtype) * g


def fourier_mix(u, w_fmix):
    b, n, _ = u.shape
    ug = u.reshape(b, n, N_FOURIER_GROUPS, FOURIER_GROUP_W).astype(jnp.float32)
    f = jnp.fft.fft2(ug, axes=(1, 3), norm="ortho").real.astype(u.dtype)
    y = jnp.einsum('bngw,gwv->bngv', f, w_fmix)
    return y.reshape(b, n, D_FOURIER)


def _rotate(t, ang):
    p = ang.shape[-1]
    cos = jnp.cos(ang)[None, :, None, :]
    sin = jnp.sin(ang)[None, :, None, :]
    t1, t2 = t[..., :p], t[..., p:]
    return jnp.concatenate([t1 * cos - t2 * sin, t1 * sin + t2 * cos], axis=-1)


def axial_rope(t):
    n = t.shape[1]
    rows_n = n // GRID_W
    row = jnp.repeat(jnp.arange(rows_n, dtype=jnp.float32), GRID_W)
    col = jnp.tile(jnp.arange(GRID_W, dtype=jnp.float32), rows_n)
    half = HEAD_DIM // 2
    n_pairs = half // 2
    freqs = ROPE_BASE ** (-jnp.arange(n_pairs, dtype=jnp.float32) / n_pairs)
    tr = _rotate(t[..., :half], row[:, None] * freqs[None, :])
    tc = _rotate(t[..., half:], col[:, None] * freqs[None, :])
    return jnp.concatenate([tr, tc], axis=-1)


def chunk_retention(q, k, v, log_gamma, s0):
    b, h, n, dh = q.shape
    nc = n // CHUNK
    lg = jnp.asarray(log_gamma)
    i = jnp.arange(CHUNK, dtype=jnp.float32)
    diff = i[:, None] - i[None, :]
    dmat = jnp.where(diff[None] >= 0, jnp.exp(lg[:, None, None] * jnp.maximum(diff, 0.0)[None]), 0.0)
    q_dec = jnp.exp(lg[:, None] * (i[None, :] + 1.0))
    k_dec = jnp.exp(lg[:, None] * (CHUNK - 1.0 - i[None, :]))
    s_dec = jnp.exp(lg * CHUNK)

    def split(t):
        return t.reshape(b, h, nc, CHUNK, dh).transpose(2, 0, 1, 3, 4)

    def step(s, qkv):
        qc, kc, vc = qkv
        att = jnp.einsum('bhid,bhjd->bhij', qc, kc) * dmat
        o = (jnp.einsum('bhij,bhje->bhie', att, vc)
             + jnp.einsum('bhid,bhde->bhie', qc * q_dec[:, :, None], s))
        s = s * s_dec[:, None, None] + jnp.einsum('bhjd,bhje->bhde', kc * k_dec[:, :, None], vc)
        return s, o

    s_end, o = lax.scan(step, s0, (split(q), split(k), split(v)))
    o = o.transpose(1, 2, 0, 3, 4).reshape(b, h, n, dh)
    return o, s_end


def head_groupnorm(o):
    mu = jnp.mean(o, axis=-1, keepdims=True)
    var = jnp.mean(jnp.square(o - mu), axis=-1, keepdims=True)
    return (o - mu) * lax.rsqrt(var + EPS)


def retention_mixer(p, s_fwd0, s_bwd0, use_rope):
    b, n, _ = p.shape
    q, k, v, gf, gb = jnp.split(p.astype(jnp.float32), 5, axis=-1)
    q = q.reshape(b, n, N_RET_HEADS, HEAD_DIM)
    k = k.reshape(b, n, N_RET_HEADS, HEAD_DIM)
    v = v.reshape(b, n, N_RET_HEADS, HEAD_DIM)
    if use_rope:
        q = axial_rope(q)
        k = axial_rope(k)
    k = k * (HEAD_DIM ** -0.5)
    q, k, v = (t.transpose(0, 2, 1, 3) for t in (q, k, v))
    of, sf = chunk_retention(q, k, v, LOG_GAMMA_FWD, s_fwd0)
    ob, sb = chunk_retention(jnp.flip(q, 2), jnp.flip(k, 2), jnp.flip(v, 2), LOG_GAMMA_BWD, s_bwd0)
    ob = jnp.flip(ob, 2)
    of = head_groupnorm(of).transpose(0, 2, 1, 3).reshape(b, n, D_RET)
    ob = head_groupnorm(ob).transpose(0, 2, 1, 3).reshape(b, n, D_RET)
    y = jax.nn.silu(gf) * of + jax.nn.silu(gb) * ob
    return y.astype(p.dtype), sf, sb


def ec_moe(h, w_router, w_gate, w_up, w_down):
    b, n, d = h.shape
    t = b * n
    ht = h.reshape(t, d)
    aff = jax.nn.softmax(ht.astype(jnp.float32) @ w_router.astype(jnp.float32), axis=-1)
    cap = EC_CAPACITY_FACTOR * t // N_EXPERTS
    gates, idx = lax.top_k(aff.T, cap)
    xs = ht[idx]
    a = jnp.einsum('ecd,edf->ecf', xs, w_gate)
    u = jnp.einsum('ecd,edf->ecf', xs, w_up)
    y = jnp.einsum('ecf,efd->ecd', jax.nn.silu(a) * u, w_down) * gates[..., None].astype(h.dtype)
    out = jnp.zeros((t, d), h.dtype).at[idx.reshape(-1)].add(y.reshape(-1, d))
    return out.reshape(b, n, d)


def trunk_layer(x, mod, s_fwd0, s_bwd0, use_rope, norm1_g, norm2_g, w_in, w_fmix, w_out,
                w_router, w_gate, w_up, w_down):
    shift1, scale1, gate1, shift2, scale2, gate2 = jnp.split(mod, 6, axis=-1)
    h = rmsnorm(x, norm1_g) * (1.0 + scale1) + shift1
    p = h @ w_in
    yf = fourier_mix(p[..., :D_FOURIER], w_fmix)
    yr, sf, sb = retention_mixer(p[..., D_FOURIER:], s_fwd0, s_bwd0, use_rope)
    x = x + gate1 * (jnp.concatenate([yf, yr], axis=-1) @ w_out)
    h2 = rmsnorm(x, norm2_g) * (1.0 + scale2) + shift2
    x = x + gate2 * ec_moe(h2, w_router, w_gate, w_up, w_down)
    return x, sf, sb


def setup_inputs(seed: int = 0) -> dict:
    key = jax.random.key(seed)
    ks = jax.random.split(key, 20)
    f32 = jnp.float32
    nrm = lambda k, shape, s: jax.random.normal(k, shape, f32) * s
    return {
        "x_prompt": nrm(ks[0], (BATCH, SEQ, D_MODEL), 1.0),
        "x_sample": nrm(ks[1], (DEC_BATCH, DEC_SEQ, D_MODEL), 1.0),
        "state_ret": nrm(ks[2], (DEC_BATCH, DEPTH, 2, N_RET_HEADS, HEAD_DIM, HEAD_DIM), 0.5),
        "c": nrm(ks[3], (DEC_BATCH, D_MODEL), 1.0),
        "c_ctx": nrm(ks[4], (D_MODEL,), 1.0),
        "norm1_g": 1.0 + nrm(ks[5], (DEPTH, D_MODEL), 0.01),
        "norm2_g": 1.0 + nrm(ks[6], (DEPTH, D_MODEL), 0.01),
        "final_g": 1.0 + nrm(ks[7], (D_MODEL,), 0.01),
        "w_mod": nrm(ks[8], (DEPTH, D_MODEL, 6 * D_MODEL), D_MODEL ** -0.5),
        "b_mod": nrm(ks[9], (DEPTH, 6 * D_MODEL), 0.01),
        "w_in": nrm(ks[10], (DEPTH, D_MODEL, D_IN_PROJ), D_MODEL ** -0.5),
        "w_fmix": nrm(ks[11], (DEPTH, N_FOURIER_GROUPS, FOURIER_GROUP_W, FOURIER_GROUP_W), FOURIER_GROUP_W ** -0.5),
        "w_out": nrm(ks[12], (DEPTH, D_MIX, D_MODEL), D_MIX ** -0.5),
        "w_router": nrm(ks[13], (DEPTH, D_MODEL, N_EXPERTS), D_MODEL ** -0.5),
        "w_gate": nrm(ks[14], (DEPTH, N_EXPERTS, D_MODEL, D_EXPERT_FF), D_MODEL ** -0.5),
        "w_up": nrm(ks[15], (DEPTH, N_EXPERTS, D_MODEL, D_EXPERT_FF), D_MODEL ** -0.5),
        "w_down": nrm(ks[16], (DEPTH, N_EXPERTS, D_EXPERT_FF, D_MODEL), D_EXPERT_FF ** -0.5),
    }


def reference(x_prompt, x_sample, state_ret, c, c_ctx, norm1_g, norm2_g, final_g, w_mod, b_mod,
              w_in, w_fmix, w_out, w_router, w_gate, w_up, w_down):
    xp = x_prompt
    bp = x_prompt.shape[0]
    zeros_s = jnp.zeros((bp, N_RET_HEADS, HEAD_DIM, HEAD_DIM), jnp.float32)
    xs = x_sample
    ctx_states = []
    for l in range(DEPTH):
        mod_ctx = (jax.nn.silu(c_ctx) @ w_mod[l] + b_mod[l])[None, None, :]
        mod_lat = (jax.nn.silu(c) @ w_mod[l] + b_mod[l])[:, None, :]
        xp, sf, sb = trunk_layer(xp, mod_ctx, zeros_s, zeros_s, False, norm1_g[l], norm2_g[l],
                                 w_in[l], w_fmix[l], w_out[l], w_router[l], w_gate[l], w_up[l], w_down[l])
        ctx_states.append(jnp.stack([sf, sb], axis=1))
        s_f0 = state_ret[:, l, 0].astype(jnp.float32)
        s_b0 = state_ret[:, l, 1].astype(jnp.float32)
        xs, _, _ = trunk_layer(xs, mod_lat, s_f0, s_b0, True, norm1_g[l], norm2_g[l],
                               w_in[l], w_fmix[l], w_out[l], w_router[l], w_gate[l], w_up[l], w_down[l])
    y_prompt = rmsnorm(xp, final_g)
    y_sample = rmsnorm(xs, final_g)
    state_ret_new = jnp.stack(ctx_states, axis=1).astype(x_prompt.dtype)
    return (y_prompt, y_sample, state_ret_new)
```

```python
import functools
import math

import jax
import jax.numpy as jnp
import numpy as np
from jax import lax
from jax.experimental import pallas as pl
from jax.experimental.pallas import tpu as pltpu

D_MODEL = 1024
D_FOURIER = 512
N_FOURIER_GROUPS = 4
FOURIER_GROUP_W = 128
D_RET = 512
N_RET_HEADS = 4
HEAD_DIM = 128
CHUNK = 128
GRID_W = 64
N_EXPERTS = 16
EC_CAPACITY_FACTOR = 2
D_EXPERT_FF = 2816
ROPE_BASE = 10000.0
EPS = 1e-6
D_IN_PROJ = D_FOURIER + 5 * D_RET
LOG_GAMMA_FWD = np.log(1.0 - 2.0 ** (-5.0 - np.arange(N_RET_HEADS))).astype(np.float32)
LOG_GAMMA_BWD = np.log(1.0 - 2.0 ** (-5.5 - np.arange(N_RET_HEADS))).astype(np.float32)

TOKEN_BLOCK = 256
FF_TILE = 256
MOD_TILE = 512
VMEM_LIMIT = 56 * 1024 * 1024

F32 = jnp.float32
BF16 = jnp.bfloat16


def _dot(a, b):
    return jnp.dot(a, b, preferred_element_type=F32)


def _dot_nt(a, b):
    return lax.dot_general(a, b, (((1,), (1,)), ((), ())), preferred_element_type=F32)


def _silu(x):
    return x * jax.nn.sigmoid(x)


def _mod_kernel(condt_ref, w_ref, b_ref, out_ref, *, n_cond):
    s = _silu(condt_ref[...])
    w = w_ref[...]
    out_ref[...] = jnp.zeros(out_ref.shape, F32)
    for r in range(n_cond):
        out_ref[r:r + 1, :] = jnp.sum(w * s[:, r:r + 1], axis=0, keepdims=True) + b_ref[...]


def _modulation(cond_rows, w_mod, b_mod):
    n_cond = cond_rows.shape[0]
    condt = jnp.zeros((D_MODEL, 8), F32).at[:, :n_cond].set(cond_rows.T)
    n_out = w_mod.shape[1]
    return pl.pallas_call(
        functools.partial(_mod_kernel, n_cond=n_cond),
        out_shape=jax.ShapeDtypeStruct((8, n_out), F32),
        grid=(n_out // MOD_TILE,),
        in_specs=[
            pl.BlockSpec((D_MODEL, 8), lambda j: (0, 0)),
            pl.BlockSpec((D_MODEL, MOD_TILE), lambda j: (0, j)),
            pl.BlockSpec((1, MOD_TILE), lambda j: (0, j)),
        ],
        out_specs=pl.BlockSpec((8, MOD_TILE), lambda j: (0, j)),
        compiler_params=pltpu.CompilerParams(dimension_semantics=("arbitrary",)),
        name="mod",
    )(condt, w_mod, b_mod.reshape(1, n_out))


def _rms(x):
    return x * lax.rsqrt(jnp.mean(x * x, axis=-1, keepdims=True) + EPS)


def _groupnorm(o):
    mu = jnp.mean(o, axis=-1, keepdims=True)
    c = o - mu
    return c * lax.rsqrt(jnp.mean(c * c, axis=-1, keepdims=True) + EPS)


def _split_hi_lo(x):
    hi = x.astype(BF16)
    lo = (x - hi.astype(F32)).astype(BF16)
    return hi, lo


def _mixer_kernel(*refs, n, use_rope, has_state_in, emit_state):
    it = iter(refs)
    x_ref, mod_ref, g1_ref, win_ref, wfmix_ref = (next(it) for _ in range(5))
    cw_ref, cn_ref, sn_ref, dmat_ref, qdec_ref, kdec_ref, sdec_ref = (next(it) for _ in range(7))
    cos_ref = sin_ref = s0_ref = st_ref = None
    if use_rope:
        cos_ref, sin_ref = next(it), next(it)
    if has_state_in:
        s0_ref = next(it)
    mix_ref = next(it)
    if emit_state:
        st_ref = next(it)
    p_ref, of_ref, ob_ref = next(it), next(it), next(it)

    nc = n // CHUNK
    mod = mod_ref[0]
    shift1 = mod[:, 0:D_MODEL]
    scale1 = mod[:, D_MODEL:2 * D_MODEL]

    h = (_rms(x_ref[0]) * g1_ref[...] * (1.0 + scale1) + shift1).astype(BF16)
    for j in range(D_IN_PROJ // 512):
        p_ref[:, j * 512:(j + 1) * 512] = _dot(h, win_ref[:, j * 512:(j + 1) * 512])

    xf = p_ref[:, 0:D_FOURIER].astype(BF16)
    xc, xs = [], []
    cw = cw_ref[...].astype(BF16)
    for g in range(N_FOURIER_GROUPS):
        t = _dot(xf[:, g * FOURIER_GROUP_W:(g + 1) * FOURIER_GROUP_W], cw)
        xc.append(t[:, :FOURIER_GROUP_W].astype(BF16))
        xs.append(t[:, FOURIER_GROUP_W:].astype(BF16))
    xc = jnp.concatenate(xc, axis=1)
    xs = jnp.concatenate(xs, axis=1)
    fre = _dot(cn_ref[...].astype(BF16), xc) - _dot(sn_ref[...].astype(BF16), xs)
    fre = fre * (1.0 / math.sqrt(n * FOURIER_GROUP_W))
    fre = fre.astype(BF16)
    for g in range(N_FOURIER_GROUPS):
        sl = slice(g * FOURIER_GROUP_W, (g + 1) * FOURIER_GROUP_W)
        mix_ref[0, :, sl] = _dot(fre[:, sl], wfmix_ref[g].astype(BF16)).astype(BF16)

    for hh in range(N_RET_HEADS):
        base = D_FOURIER + hh * HEAD_DIM
        q = p_ref[:, base:base + HEAD_DIM]
        k = p_ref[:, base + D_RET:base + D_RET + HEAD_DIM]
        v = p_ref[:, base + 2 * D_RET:base + 2 * D_RET + HEAD_DIM]
        if use_rope:
            lane = lax.broadcasted_iota(jnp.int32, (n, HEAD_DIM), 1)
            first = (lane % 64) < 32

            def rope(t):
                swapped = jnp.where(first, pltpu.roll(t, HEAD_DIM - 32, 1), pltpu.roll(t, 32, 1))
                return t * cos_ref[...] + swapped * sin_ref[...]

            q, k = rope(q), rope(k)
        k = k * (HEAD_DIM ** -0.5)
        qb, vb = q.astype(BF16), v.astype(BF16)
        kb = k.astype(BF16)

        if has_state_in:
            sf = s0_ref[0, 0, hh]
            sb = s0_ref[0, 1, hh]
        else:
            sf = jnp.zeros((HEAD_DIM, HEAD_DIM), F32)
            sb = jnp.zeros((HEAD_DIM, HEAD_DIM), F32)

        for c in range(nc):
            rs = slice(c * CHUNK, (c + 1) * CHUNK)
            qc, kc, vc = qb[rs], kb[rs], vb[rs]
            qk = _dot_nt(qc, kc)
            o = _dot((qk * dmat_ref[0, hh]).astype(BF16), vc)
            if has_state_in or c > 0:
                o = o + qdec_ref[0, hh] * _dot(qc, sf.astype(BF16))
            of_ref[rs, :] = o
            ob_ref[rs, :] = _dot((qk * dmat_ref[1, hh]).astype(BF16), vc)
            kd = (k[rs] * kdec_ref[0, hh]).T.astype(BF16)
            sf = sf * sdec_ref[0, hh] + _dot(kd, vc)
        for c in reversed(range(nc)):
            rs = slice(c * CHUNK, (c + 1) * CHUNK)
            qc, vc = qb[rs], vb[rs]
            if has_state_in or c < nc - 1:
                ob_ref[rs, :] = ob_ref[rs, :] + qdec_ref[1, hh] * _dot(qc, sb.astype(BF16))
            kd = (k[rs] * kdec_ref[1, hh]).T.astype(BF16)
            sb = sb * sdec_ref[1, hh] + _dot(kd, vc)
        if emit_state:
            st_ref[0, 0, hh] = sf
            st_ref[0, 1, hh] = sb

        gf = p_ref[:, base + 3 * D_RET:base + 3 * D_RET + HEAD_DIM]
        gb = p_ref[:, base + 4 * D_RET:base + 4 * D_RET + HEAD_DIM]
        y = _silu(gf) * _groupnorm(of_ref[...]) + _silu(gb) * _groupnorm(ob_ref[...])
        mix_ref[0, :, base:base + HEAD_DIM] = y.astype(BF16)


def _post_kernel(x_ref, mix_ref, mod_ref, g2_ref, wout_ref, wr_ref, x1_ref, h2_ref, aff_ref):
    mod = mod_ref[0]
    gate1 = mod[:, 2 * D_MODEL:3 * D_MODEL]
    shift2 = mod[:, 3 * D_MODEL:4 * D_MODEL]
    scale2 = mod[:, 4 * D_MODEL:5 * D_MODEL]
    x1 = x_ref[...] + gate1 * _dot(mix_ref[...], wout_ref[...])
    x1_ref[...] = x1
    h2 = _rms(x1) * g2_ref[...] * (1.0 + scale2) + shift2
    h2_hi, h2_lo = _split_hi_lo(h2)
    h2_ref[...] = h2_hi
    wr_hi, wr_lo = _split_hi_lo(wr_ref[...])
    logits = _dot(h2_hi, wr_hi) + (_dot(h2_lo, wr_hi) + _dot(h2_hi, wr_lo))
    z = jnp.exp(logits - jnp.max(logits, axis=-1, keepdims=True))
    aff_ref[...] = z / jnp.sum(z, axis=-1, keepdims=True)


def _dft_consts(n):
    w = FOURIER_GROUP_W
    jw = np.arange(w)
    angw = 2.0 * np.pi * np.outer(jw, jw) / w
    cw = np.concatenate([np.cos(angw), np.sin(angw)], axis=1)
    jn = np.arange(n)
    angn = 2.0 * np.pi * (np.outer(jn, jn) % n) / n
    return (jnp.asarray(cw, F32), jnp.asarray(np.cos(angn), F32), jnp.asarray(np.sin(angn), F32))


def _retention_consts():
    i = np.arange(CHUNK, dtype=np.float64)
    diff = i[:, None] - i[None, :]
    dmat = np.zeros((2, N_RET_HEADS, CHUNK, CHUNK))
    qdec = np.zeros((2, N_RET_HEADS, CHUNK, HEAD_DIM))
    kdec = np.zeros((2, N_RET_HEADS, CHUNK, HEAD_DIM))
    sdec = np.zeros((2, N_RET_HEADS, CHUNK, HEAD_DIM))
    for hh in range(N_RET_HEADS):
        lf = float(LOG_GAMMA_FWD[hh])
        lb = float(LOG_GAMMA_BWD[hh])
        dmat[0, hh] = np.where(diff >= 0, np.exp(lf * np.maximum(diff, 0.0)), 0.0)
        dmat[1, hh] = np.where(diff <= 0, np.exp(lb * np.maximum(-diff, 0.0)), 0.0)
        qdec[0, hh] = np.exp(lf * (i + 1.0))[:, None]
        qdec[1, hh] = np.exp(lb * (CHUNK - i))[:, None]
        kdec[0, hh] = np.exp(lf * (CHUNK - 1.0 - i))[:, None]
        kdec[1, hh] = np.exp(lb * i)[:, None]
        sdec[0, hh] = math.exp(lf * CHUNK)
        sdec[1, hh] = math.exp(lb * CHUNK)
    return tuple(jnp.asarray(a, F32) for a in (dmat, qdec, kdec, sdec))


def _rope_consts(n):
    rows_n = n // GRID_W
    row = np.repeat(np.arange(rows_n, dtype=np.float64), GRID_W)
    col = np.tile(np.arange(GRID_W, dtype=np.float64), rows_n)
    n_pairs = HEAD_DIM // 4
    freqs = (np.float32(ROPE_BASE) ** (-np.arange(n_pairs, dtype=np.float32) / n_pairs)).astype(np.float64)
    ar = row[:, None] * freqs[None, :]
    ac = col[:, None] * freqs[None, :]
    cos = np.concatenate([np.cos(ar), np.cos(ar), np.cos(ac), np.cos(ac)], axis=1)
    sin = np.concatenate([-np.sin(ar), np.sin(ar), -np.sin(ac), np.sin(ac)], axis=1)
    return jnp.asarray(cos, F32), jnp.asarray(sin, F32)


def _const_spec(shape):
    nd = len(shape)
    return pl.BlockSpec(shape, lambda b, _nd=nd: (0,) * _nd, pipeline_mode=pl.Buffered(1))


def _mixer(x, mod_rows, mod_per_batch, state_in, emit_state, use_rope, g1, w_in_bf, w_fmix):
    nb, n, _ = x.shape
    has_state_in = state_in is not None
    cw, cn, sn = _dft_consts(n)
    dmat, qdec, kdec, sdec = _retention_consts()
    consts = [cw, cn, sn, dmat, qdec, kdec, sdec]
    if use_rope:
        consts += list(_rope_consts(n))
    weights = [g1.reshape(1, D_MODEL), w_in_bf, w_fmix]

    mod_map = (lambda b: (b, 0, 0)) if mod_per_batch else (lambda b: (0, 0, 0))
    state_spec = pl.BlockSpec((1, 2, N_RET_HEADS, HEAD_DIM, HEAD_DIM), lambda b: (b, 0, 0, 0, 0))
    in_specs = [pl.BlockSpec((1, n, D_MODEL), lambda b: (b, 0, 0)),
                pl.BlockSpec((1, 1, 6 * D_MODEL), mod_map)]
    in_specs += [_const_spec(a.shape) for a in weights + consts]
    args = [x, mod_rows] + weights + consts
    if has_state_in:
        in_specs.append(state_spec)
        args.append(state_in)

    out_shape = [jax.ShapeDtypeStruct((nb, n, D_MODEL), BF16)]
    out_specs = [pl.BlockSpec((1, n, D_MODEL), lambda b: (b, 0, 0))]
    if emit_state:
        out_shape.append(jax.ShapeDtypeStruct((nb, 2, N_RET_HEADS, HEAD_DIM, HEAD_DIM), F32))
        out_specs.append(state_spec)

    return pl.pallas_call(
        functools.partial(_mixer_kernel, n=n, use_rope=use_rope, has_state_in=has_state_in,
                          emit_state=emit_state),
        out_shape=out_shape,
        grid=(nb,),
        in_specs=in_specs,
        out_specs=out_specs,
        scratch_shapes=[pltpu.VMEM((n, D_IN_PROJ), F32),
                        pltpu.VMEM((n, HEAD_DIM), F32), pltpu.VMEM((n, HEAD_DIM), F32)],
        compiler_params=pltpu.CompilerParams(dimension_semantics=("arbitrary",),
                                             vmem_limit_bytes=VMEM_LIMIT),
        name="mixer_rope" if use_rope else "mixer",
    )(*args)


def _post(x, mix, mod_rows, blocks_per_mod_row, g2, w_out_bf, w_router):
    t = x.shape[0]
    if blocks_per_mod_row is None:
        mod_map = lambda b: (0, 0, 0)
    else:
        mod_map = lambda b: (b // blocks_per_mod_row, 0, 0)
    row_spec = pl.BlockSpec((TOKEN_BLOCK, D_MODEL), lambda b: (b, 0))
    return pl.pallas_call(
        _post_kernel,
        out_shape=[jax.ShapeDtypeStruct((t, D_MODEL), F32),
                   jax.ShapeDtypeStruct((t, D_MODEL), BF16),
                   jax.ShapeDtypeStruct((t, N_EXPERTS), F32)],
        grid=(t // TOKEN_BLOCK,),
        in_specs=[row_spec, row_spec,
                  pl.BlockSpec((1, 1, 6 * D_MODEL), mod_map),
                  _const_spec((1, D_MODEL)), _const_spec((D_MODEL, D_MODEL)),
                  _const_spec((D_MODEL, N_EXPERTS))],
        out_specs=[row_spec, row_spec, pl.BlockSpec((TOKEN_BLOCK, N_EXPERTS), lambda b: (b, 0))],
        compiler_params=pltpu.CompilerParams(dimension_semantics=("arbitrary",)),
        name="post",
    )(x, mix, mod_rows, g2.reshape(1, D_MODEL), w_out_bf, w_router)


def _route_kernel(aff_ref, u_ref, slot_ref, starts_ref, *, t, cap):
    aff = aff_ref[...]

    def count(mask):
        return jnp.sum(mask.astype(jnp.int32), axis=1, keepdims=True)

    def as_float(word):
        return lax.bitcast_convert_type(word, F32)

    def value_step(i, cur):
        cand = cur | jnp.left_shift(jnp.int32(1), 30 - i)
        return jnp.where(count(aff >= as_float(cand)) >= cap, cand, cur)

    thr = lax.fori_loop(0, 31, value_step, jnp.zeros((N_EXPERTS, 1), jnp.int32))
    gt = aff >= as_float(thr + 1)
    eq = (aff >= as_float(thr)) & jnp.logical_not(gt)
    need = cap - count(gt)
    tok = lax.broadcasted_iota(jnp.int32, (N_EXPERTS, t), 1)
    nbits = t.bit_length() - 1

    def index_step(i, cur):
        cand = cur | jnp.left_shift(jnp.int32(1), nbits - 1 - i)
        return jnp.where(count(eq & (tok < cand)) < need, cand, cur)

    last = lax.fori_loop(0, nbits, index_step, jnp.zeros((N_EXPERTS, 1), jnp.int32))
    self = jnp.where(gt | (eq & (tok <= last)), 1.0, 0.0).astype(F32)

    carry = jnp.zeros((N_EXPERTS, 1), F32)
    for b in range(t // TOKEN_BLOCK):
        sl = slice(b * TOKEN_BLOCK, (b + 1) * TOKEN_BLOCK)
        sbf = self[:, sl]
        pre = _dot(sbf.astype(BF16), u_ref[...]) + carry
        slot_ref[:, sl] = jnp.where(sbf > 0.5, pre.astype(jnp.int32), -1)
        starts_ref[b] = jnp.broadcast_to(carry, (N_EXPERTS, 128)).astype(jnp.int32)
        carry = carry + jnp.sum(sbf, axis=1, keepdims=True)
    starts_ref[t // TOKEN_BLOCK] = jnp.broadcast_to(carry, (N_EXPERTS, 128)).astype(jnp.int32)


def _route(aff_te):
    t = aff_te.shape[0]
    cap = EC_CAPACITY_FACTOR * t // N_EXPERTS
    nblk = t // TOKEN_BLOCK
    upper = jnp.asarray(np.triu(np.ones((TOKEN_BLOCK, TOKEN_BLOCK)), 1), BF16)
    slot_et, starts = pl.pallas_call(
        functools.partial(_route_kernel, t=t, cap=cap),
        out_shape=[jax.ShapeDtypeStruct((N_EXPERTS, t), jnp.int32),
                   jax.ShapeDtypeStruct((nblk + 1, N_EXPERTS, 128), jnp.int32)],
        grid=(1,),
        in_specs=[pl.BlockSpec((N_EXPERTS, t), lambda i: (0, 0)),
                  pl.BlockSpec((TOKEN_BLOCK, TOKEN_BLOCK), lambda i: (0, 0))],
        out_specs=[pl.BlockSpec((N_EXPERTS, t), lambda i: (0, 0)),
                   pl.BlockSpec((nblk + 1, N_EXPERTS, 128), lambda i: (0, 0, 0))],
        compiler_params=pltpu.CompilerParams(dimension_semantics=("arbitrary",)),
        name="route",
    )(aff_te.T, upper)
    return slot_et, starts[:, :, 0]


def _gather_group(e, starts_ref, slot_ref, h2_ref, xs_ref, row0, t, cap):
    nblk = t // TOKEN_BLOCK
    for sblk in range(cap // TOKEN_BLOCK):
        lo = sblk * TOKEN_BLOCK
        rows = slice(row0 + lo, row0 + lo + TOKEN_BLOCK)

        def body(b, carry, lo=lo, rows=rows):
            a0 = starts_ref[b, e]
            a1 = starts_ref[b + 1, e]

            @pl.when((a1 > a0) & (a1 > lo) & (a0 < lo + TOKEN_BLOCK))
            def _():
                srow = slot_ref[0, pl.ds(b, 1), :]
                want = lo + lax.broadcasted_iota(jnp.int32, (TOKEN_BLOCK, TOKEN_BLOCK), 0)
                onehot = jnp.where(srow == want, 1.0, 0.0).astype(BF16)
                hb = h2_ref[pl.ds(pl.multiple_of(b * TOKEN_BLOCK, TOKEN_BLOCK), TOKEN_BLOCK), :]
                xs_ref[rows, :] = (xs_ref[rows, :].astype(F32) + _dot(onehot, hb)).astype(BF16)

            return carry

        lax.fori_loop(0, nblk, body, 0)


def _experts_kernel(sp_ref, ss_ref, h2p_ref, h2s_ref, slotp_ref, slots_ref, wg_ref, wu_ref, wd_ref,
                    yp_ref, ys_ref, xs_ref, acc_ref, *, tp, ts, capp, caps):
    e = pl.program_id(0)
    f = pl.program_id(1)

    @pl.when(f == 0)
    def _():
        xs_ref[...] = jnp.zeros(xs_ref.shape, BF16)
        _gather_group(e, sp_ref, slotp_ref, h2p_ref, xs_ref, 0, tp, capp)
        _gather_group(e, ss_ref, slots_ref, h2s_ref, xs_ref, capp, ts, caps)

    x = xs_ref[...]
    a = _dot(x, wg_ref[0].astype(BF16))
    u = _dot(x, wu_ref[0].astype(BF16))
    contrib = _dot((_silu(a) * u).astype(BF16), wd_ref[0].astype(BF16))

    @pl.when(f == 0)
    def _():
        acc_ref[...] = contrib

    @pl.when(f > 0)
    def _():
        acc_ref[...] += contrib

    @pl.when(f == pl.num_programs(1) - 1)
    def _():
        yp_ref[0] = acc_ref[0:capp, :].astype(BF16)
        ys_ref[0] = acc_ref[capp:capp + caps, :].astype(BF16)


def _experts(starts_p, starts_s, h2p, h2s, slot_p, slot_s, w_gate, w_up, w_down):
    tp, ts = h2p.shape[0], h2s.shape[0]
    capp = EC_CAPACITY_FACTOR * tp // N_EXPERTS
    caps = EC_CAPACITY_FACTOR * ts // N_EXPERTS
    rows = capp + caps
    nbp, nbs = tp // TOKEN_BLOCK, ts // TOKEN_BLOCK
    grid_spec = pltpu.PrefetchScalarGridSpec(
        num_scalar_prefetch=2,
        grid=(N_EXPERTS, D_EXPERT_FF // FF_TILE),
        in_specs=[
            pl.BlockSpec(memory_space=pltpu.VMEM),
            pl.BlockSpec(memory_space=pltpu.VMEM),
            pl.BlockSpec((1, nbp, TOKEN_BLOCK), lambda e, f, *_: (e, 0, 0)),
            pl.BlockSpec((1, nbs, TOKEN_BLOCK), lambda e, f, *_: (e, 0, 0)),
            pl.BlockSpec((1, D_MODEL, FF_TILE), lambda e, f, *_: (e, 0, f)),
            pl.BlockSpec((1, D_MODEL, FF_TILE), lambda e, f, *_: (e, 0, f)),
            pl.BlockSpec((1, FF_TILE, D_MODEL), lambda e, f, *_: (e, f, 0)),
        ],
        out_specs=[
            pl.BlockSpec((1, capp, D_MODEL), lambda e, f, *_: (e, 0, 0)),
            pl.BlockSpec((1, caps, D_MODEL), lambda e, f, *_: (e, 0, 0)),
        ],
        scratch_shapes=[pltpu.VMEM((rows, D_MODEL), BF16), pltpu.VMEM((rows, D_MODEL), F32)],
    )
    return pl.pallas_call(
        functools.partial(_experts_kernel, tp=tp, ts=ts, capp=capp, caps=caps),
        out_shape=[jax.ShapeDtypeStruct((N_EXPERTS, capp, D_MODEL), BF16),
                   jax.ShapeDtypeStruct((N_EXPERTS, caps, D_MODEL), BF16)],
        grid_spec=grid_spec,
        compiler_params=pltpu.CompilerParams(dimension_semantics=("arbitrary", "arbitrary"),
                                             vmem_limit_bytes=VMEM_LIMIT),
        name="experts",
    )(starts_p, starts_s, h2p, h2s,
      slot_p.reshape(N_EXPERTS, nbp, TOKEN_BLOCK), slot_s.reshape(N_EXPERTS, nbs, TOKEN_BLOCK),
      w_gate, w_up, w_down)


def _combine_kernel(st_ref, x1_ref, slot_ref, aff_ref, mod_ref, fg_ref, y_ref, out_ref, acc_ref, *, cap):
    b = pl.program_id(0)
    slot = slot_ref[...]
    gate = aff_ref[...]
    acc_ref[...] = jnp.zeros(acc_ref.shape, F32)
    lane = lax.broadcasted_iota(jnp.int32, (TOKEN_BLOCK, TOKEN_BLOCK), 1)
    for e in range(N_EXPERTS):
        a0 = st_ref[b, e]
        a1 = st_ref[b + 1, e]
        w0 = jnp.minimum((a0 // 16) * 16, cap - TOKEN_BLOCK)
        sc = slot[:, e:e + 1]
        ge = gate[:, e:e + 1]

        @pl.when(a1 > a0)
        def _(e=e, w0=w0, sc=sc, ge=ge):
            onehot = jnp.where(sc == w0 + lane, 1.0, 0.0).astype(BF16)
            yw = y_ref[e, pl.ds(pl.multiple_of(w0, 16), TOKEN_BLOCK), :]
            acc_ref[...] += ge * _dot(onehot, yw)

        if cap > TOKEN_BLOCK:
            w1 = jnp.minimum(w0 + TOKEN_BLOCK, cap - TOKEN_BLOCK)

            @pl.when(a1 > w0 + TOKEN_BLOCK)
            def _(e=e, w0=w0, w1=w1, sc=sc, ge=ge):
                hit = (sc == w1 + lane) & (sc >= w0 + TOKEN_BLOCK)
                onehot = jnp.where(hit, 1.0, 0.0).astype(BF16)
                yw = y_ref[e, pl.ds(pl.multiple_of(w1, 16), TOKEN_BLOCK), :]
                acc_ref[...] += ge * _dot(onehot, yw)

    gate2 = mod_ref[0][:, 5 * D_MODEL:6 * D_MODEL]
    x2 = x1_ref[...] + gate2 * acc_ref[...]
    out_ref[...] = _rms(x2) * fg_ref[...]


def _combine(starts, x1, slot_te, aff_te, mod_rows, blocks_per_mod_row, final_g, y):
    t = x1.shape[0]
    cap = y.shape[1]
    nblk = t // TOKEN_BLOCK
    if blocks_per_mod_row is None:
        mod_map = lambda b, *_: (0, 0, 0)
    else:
        mod_map = lambda b, *_: (b // blocks_per_mod_row, 0, 0)
    grid_spec = pltpu.PrefetchScalarGridSpec(
        num_scalar_prefetch=1,
        grid=(nblk,),
        in_specs=[
            pl.BlockSpec((TOKEN_BLOCK, D_MODEL), lambda b, *_: (b, 0)),
            pl.BlockSpec((TOKEN_BLOCK, N_EXPERTS), lambda b, *_: (b, 0)),
            pl.BlockSpec((TOKEN_BLOCK, N_EXPERTS), lambda b, *_: (b, 0)),
            pl.BlockSpec((1, 1, 6 * D_MODEL), mod_map),
            pl.BlockSpec((1, D_MODEL), lambda b, *_: (0, 0)),
            pl.BlockSpec(memory_space=pltpu.VMEM),
        ],
        out_specs=pl.BlockSpec((TOKEN_BLOCK, D_MODEL), lambda b, *_: (b, 0)),
        scratch_shapes=[pltpu.VMEM((TOKEN_BLOCK, D_MODEL), F32)],
    )
    return pl.pallas_call(
        functools.partial(_combine_kernel, cap=cap),
        out_shape=jax.ShapeDtypeStruct((t, D_MODEL), F32),
        grid_spec=grid_spec,
        compiler_params=pltpu.CompilerParams(dimension_semantics=("arbitrary",),
                                             vmem_limit_bytes=VMEM_LIMIT),
        name="combine",
    )(starts, x1, slot_te, aff_te, mod_rows, final_g.reshape(1, D_MODEL), y)


def kernel(x_prompt, x_sample, state_ret, c, c_ctx, norm1_g, norm2_g, final_g, w_mod, b_mod, w_in,
           w_fmix, w_out, w_router, w_gate, w_up, w_down):
    bp, seq, _ = x_prompt.shape
    bs, dec_seq, _ = x_sample.shape
    assert w_mod.shape[0] == 1, "single-layer trunk"
    tp, ts = bp * seq, bs * dec_seq

    cond = jnp.concatenate([c_ctx[None, :], c], axis=0)
    mod = _modulation(cond, w_mod[0], b_mod[0])
    mod_ctx = mod[0:1].reshape(1, 1, 6 * D_MODEL)
    mod_lat = mod[1:1 + bs].reshape(bs, 1, 6 * D_MODEL)

    w_in_bf = w_in[0].astype(BF16)
    w_out_bf = w_out[0].astype(BF16)
    mix_p, states = _mixer(x_prompt, mod_ctx, False, None, True, False, norm1_g[0], w_in_bf, w_fmix[0])
    (mix_s,) = _mixer(x_sample, mod_lat, True, state_ret[:, 0], False, True, norm1_g[0], w_in_bf,
                      w_fmix[0])
    blocks_per_seq = dec_seq // TOKEN_BLOCK
    x1p, h2p, affp = _post(x_prompt.reshape(tp, D_MODEL), mix_p.reshape(tp, D_MODEL), mod_ctx, None,
                           norm2_g[0], w_out_bf, w_router[0])
    x1s, h2s, affs = _post(x_sample.reshape(ts, D_MODEL), mix_s.reshape(ts, D_MODEL), mod_lat,
                           blocks_per_seq, norm2_g[0], w_out_bf, w_router[0])

    slot_p, starts_p = _route(affp)
    slot_s, starts_s = _route(affs)

    yp, ys = _experts(starts_p, starts_s, h2p, h2s, slot_p, slot_s, w_gate[0], w_up[0], w_down[0])

    out_p = _combine(starts_p, x1p, slot_p.T, affp, mod_ctx, None, final_g, yp)
    out_s = _combine(starts_s, x1s, slot_s.T, affs, mod_lat, blocks_per_seq, final_g, ys)

    y_prompt = out_p.reshape(bp, seq, D_MODEL)
    y_sample = out_s.reshape(bs, dec_seq, D_MODEL)
    state_new = states.reshape(bp, 1, 2, N_RET_HEADS, HEAD_DIM, HEAD_DIM).astype(x_prompt.dtype)
    return (y_prompt, y_sample, state_new)
```

```python
import functools
import math

import jax
import jax.numpy as jnp
import numpy as np
from jax import lax
from jax.experimental import pallas as pl
from jax.experimental.pallas import tpu as pltpu

D_MODEL = 1024
D_FOURIER = 512
N_FOURIER_GROUPS = 4
FOURIER_GROUP_W = 128
D_RET = 512
N_RET_HEADS = 4
HEAD_DIM = 128
CHUNK = 128
GRID_W = 64
N_EXPERTS = 16
EC_CAPACITY_FACTOR = 2
D_EXPERT_FF = 2816
ROPE_BASE = 10000.0
EPS = 1e-6
D_IN_PROJ = D_FOURIER + 5 * D_RET
LOG_GAMMA_FWD = np.log(1.0 - 2.0 ** (-5.0 - np.arange(N_RET_HEADS))).astype(np.float32)
LOG_GAMMA_BWD = np.log(1.0 - 2.0 ** (-5.5 - np.arange(N_RET_HEADS))).astype(np.float32)

TOKEN_BLOCK = 256
SLOT_WINDOW = 64
PACK = TOKEN_BLOCK // SLOT_WINDOW
FF_TILE = 256
MOD_TILE = 512
VMEM_LIMIT = 56 * 1024 * 1024

F32 = jnp.float32
BF16 = jnp.bfloat16


def _dot(a, b):
    return jnp.dot(a, b, preferred_element_type=F32)


def _dot_nt(a, b):
    return lax.dot_general(a, b, (((1,), (1,)), ((), ())), preferred_element_type=F32)


def _silu(x):
    return x * jax.nn.sigmoid(x)


def _mod_kernel(condt_ref, w_ref, b_ref, out_ref, *, n_cond):
    s = _silu(condt_ref[...])
    w = w_ref[...]
    out_ref[...] = jnp.zeros(out_ref.shape, F32)
    for r in range(n_cond):
        out_ref[r:r + 1, :] = jnp.sum(w * s[:, r:r + 1], axis=0, keepdims=True) + b_ref[...]


def _modulation(cond_rows, w_mod, b_mod):
    n_cond = cond_rows.shape[0]
    condt = jnp.zeros((D_MODEL, 8), F32).at[:, :n_cond].set(cond_rows.T)
    n_out = w_mod.shape[1]
    return pl.pallas_call(
        functools.partial(_mod_kernel, n_cond=n_cond),
        out_shape=jax.ShapeDtypeStruct((8, n_out), F32),
        grid=(n_out // MOD_TILE,),
        in_specs=[
            pl.BlockSpec((D_MODEL, 8), lambda j: (0, 0)),
            pl.BlockSpec((D_MODEL, MOD_TILE), lambda j: (0, j)),
            pl.BlockSpec((1, MOD_TILE), lambda j: (0, j)),
        ],
        out_specs=pl.BlockSpec((8, MOD_TILE), lambda j: (0, j)),
        compiler_params=pltpu.CompilerParams(dimension_semantics=("arbitrary",)),
        name="mod",
    )(condt, w_mod, b_mod.reshape(1, n_out))


def _rms(x):
    return x * lax.rsqrt(jnp.mean(x * x, axis=-1, keepdims=True) + EPS)


def _groupnorm(o):
    mu = jnp.mean(o, axis=-1, keepdims=True)
    c = o - mu
    return c * lax.rsqrt(jnp.mean(c * c, axis=-1, keepdims=True) + EPS)


def _split_hi_lo(x):
    hi = x.astype(BF16)
    lo = (x - hi.astype(F32)).astype(BF16)
    return hi, lo


def _mixer_kernel(*refs, n, use_rope, has_state_in, emit_state):
    it = iter(refs)
    x_ref, mod_ref, g1_ref, win_ref, wfmix_ref = (next(it) for _ in range(5))
    cw_ref, cn_ref, sn_ref, dmat_ref, qdec_ref, kdec_ref, sdec_ref = (next(it) for _ in range(7))
    cos_ref = sin_ref = s0_ref = st_ref = None
    if use_rope:
        cos_ref, sin_ref = next(it), next(it)
    if has_state_in:
        s0_ref = next(it)
    mix_ref = next(it)
    if emit_state:
        st_ref = next(it)
    p_ref, of_ref, ob_ref = next(it), next(it), next(it)

    nc = n // CHUNK
    mod = mod_ref[0]
    shift1 = mod[:, 0:D_MODEL]
    scale1 = mod[:, D_MODEL:2 * D_MODEL]

    h = (_rms(x_ref[0]) * g1_ref[...] * (1.0 + scale1) + shift1).astype(BF16)
    for j in range(D_IN_PROJ // 512):
        p_ref[:, j * 512:(j + 1) * 512] = _dot(h, win_ref[:, j * 512:(j + 1) * 512])

    xf = p_ref[:, 0:D_FOURIER].astype(BF16)
    xc, xs = [], []
    cw = cw_ref[...].astype(BF16)
    for g in range(N_FOURIER_GROUPS):
        t = _dot(xf[:, g * FOURIER_GROUP_W:(g + 1) * FOURIER_GROUP_W], cw)
        xc.append(t[:, :FOURIER_GROUP_W].astype(BF16))
        xs.append(t[:, FOURIER_GROUP_W:].astype(BF16))
    xc = jnp.concatenate(xc, axis=1)
    xs = jnp.concatenate(xs, axis=1)
    fre = _dot(cn_ref[...].astype(BF16), xc) - _dot(sn_ref[...].astype(BF16), xs)
    fre = fre * (1.0 / math.sqrt(n * FOURIER_GROUP_W))
    fre = fre.astype(BF16)
    for g in range(N_FOURIER_GROUPS):
        sl = slice(g * FOURIER_GROUP_W, (g + 1) * FOURIER_GROUP_W)
        mix_ref[0, :, sl] = _dot(fre[:, sl], wfmix_ref[g].astype(BF16)).astype(BF16)

    for hh in range(N_RET_HEADS):
        base = D_FOURIER + hh * HEAD_DIM
        q = p_ref[:, base:base + HEAD_DIM]
        k = p_ref[:, base + D_RET:base + D_RET + HEAD_DIM]
        v = p_ref[:, base + 2 * D_RET:base + 2 * D_RET + HEAD_DIM]
        if use_rope:
            lane = lax.broadcasted_iota(jnp.int32, (n, HEAD_DIM), 1)
            first = (lane % 64) < 32

            def rope(t):
                swapped = jnp.where(first, pltpu.roll(t, HEAD_DIM - 32, 1), pltpu.roll(t, 32, 1))
                return t * cos_ref[...] + swapped * sin_ref[...]

            q, k = rope(q), rope(k)
        k = k * (HEAD_DIM ** -0.5)
        qb, vb = q.astype(BF16), v.astype(BF16)
        kb = k.astype(BF16)

        if has_state_in:
            sf = s0_ref[0, 0, hh]
            sb = s0_ref[0, 1, hh]
        else:
            sf = jnp.zeros((HEAD_DIM, HEAD_DIM), F32)
            sb = jnp.zeros((HEAD_DIM, HEAD_DIM), F32)

        for c in range(nc):
            rs = slice(c * CHUNK, (c + 1) * CHUNK)
            qc, kc, vc = qb[rs], kb[rs], vb[rs]
            qk = _dot_nt(qc, kc)
            o = _dot((qk * dmat_ref[0, hh]).astype(BF16), vc)
            if has_state_in or c > 0:
                o = o + qdec_ref[0, hh] * _dot(qc, sf.astype(BF16))
            of_ref[rs, :] = o
            ob_ref[rs, :] = _dot((qk * dmat_ref[1, hh]).astype(BF16), vc)
            kd = (k[rs] * kdec_ref[0, hh]).T.astype(BF16)
            sf = sf * sdec_ref[0, hh] + _dot(kd, vc)
        for c in reversed(range(nc)):
            rs = slice(c * CHUNK, (c + 1) * CHUNK)
            qc, vc = qb[rs], vb[rs]
            if has_state_in or c < nc - 1:
                ob_ref[rs, :] = ob_ref[rs, :] + qdec_ref[1, hh] * _dot(qc, sb.astype(BF16))
            kd = (k[rs] * kdec_ref[1, hh]).T.astype(BF16)
            sb = sb * sdec_ref[1, hh] + _dot(kd, vc)
        if emit_state:
            st_ref[0, 0, hh] = sf
            st_ref[0, 1, hh] = sb

        gf = p_ref[:, base + 3 * D_RET:base + 3 * D_RET + HEAD_DIM]
        gb = p_ref[:, base + 4 * D_RET:base + 4 * D_RET + HEAD_DIM]
        y = _silu(gf) * _groupnorm(of_ref[...]) + _silu(gb) * _groupnorm(ob_ref[...])
        mix_ref[0, :, base:base + HEAD_DIM] = y.astype(BF16)


def _post_kernel(x_ref, mix_ref, mod_ref, g2_ref, wout_ref, wr_ref, x1_ref, h2_ref, aff_ref):
    mod = mod_ref[0]
    gate1 = mod[:, 2 * D_MODEL:3 * D_MODEL]
    shift2 = mod[:, 3 * D_MODEL:4 * D_MODEL]
    scale2 = mod[:, 4 * D_MODEL:5 * D_MODEL]
    x1 = x_ref[...] + gate1 * _dot(mix_ref[...], wout_ref[...])
    x1_ref[...] = x1
    h2 = _rms(x1) * g2_ref[...] * (1.0 + scale2) + shift2
    h2_hi, h2_lo = _split_hi_lo(h2)
    h2_ref[...] = h2_hi
    wr_hi, wr_lo = _split_hi_lo(wr_ref[...])
    logits = _dot(h2_hi, wr_hi) + (_dot(h2_lo, wr_hi) + _dot(h2_hi, wr_lo))
    z = jnp.exp(logits - jnp.max(logits, axis=-1, keepdims=True))
    aff_ref[...] = z / jnp.sum(z, axis=-1, keepdims=True)


def _dft_consts(n):
    w = FOURIER_GROUP_W
    jw = np.arange(w)
    angw = 2.0 * np.pi * np.outer(jw, jw) / w
    cw = np.concatenate([np.cos(angw), np.sin(angw)], axis=1)
    jn = np.arange(n)
    angn = 2.0 * np.pi * (np.outer(jn, jn) % n) / n
    return (jnp.asarray(cw, F32), jnp.asarray(np.cos(angn), F32), jnp.asarray(np.sin(angn), F32))


def _retention_consts():
    i = np.arange(CHUNK, dtype=np.float64)
    diff = i[:, None] - i[None, :]
    dmat = np.zeros((2, N_RET_HEADS, CHUNK, CHUNK))
    qdec = np.zeros((2, N_RET_HEADS, CHUNK, HEAD_DIM))
    kdec = np.zeros((2, N_RET_HEADS, CHUNK, HEAD_DIM))
    sdec = np.zeros((2, N_RET_HEADS, CHUNK, HEAD_DIM))
    for hh in range(N_RET_HEADS):
        lf = float(LOG_GAMMA_FWD[hh])
        lb = float(LOG_GAMMA_BWD[hh])
        dmat[0, hh] = np.where(diff >= 0, np.exp(lf * np.maximum(diff, 0.0)), 0.0)
        dmat[1, hh] = np.where(diff <= 0, np.exp(lb * np.maximum(-diff, 0.0)), 0.0)
        qdec[0, hh] = np.exp(lf * (i + 1.0))[:, None]
        qdec[1, hh] = np.exp(lb * (CHUNK - i))[:, None]
        kdec[0, hh] = np.exp(lf * (CHUNK - 1.0 - i))[:, None]
        kdec[1, hh] = np.exp(lb * i)[:, None]
        sdec[0, hh] = math.exp(lf * CHUNK)
        sdec[1, hh] = math.exp(lb * CHUNK)
    return tuple(jnp.asarray(a, F32) for a in (dmat, qdec, kdec, sdec))


def _rope_consts(n):
    rows_n = n // GRID_W
    row = np.repeat(np.arange(rows_n, dtype=np.float64), GRID_W)
    col = np.tile(np.arange(GRID_W, dtype=np.float64), rows_n)
    n_pairs = HEAD_DIM // 4
    freqs = (np.float32(ROPE_BASE) ** (-np.arange(n_pairs, dtype=np.float32) / n_pairs)).astype(np.float64)
    ar = row[:, None] * freqs[None, :]
    ac = col[:, None] * freqs[None, :]
    cos = np.concatenate([np.cos(ar), np.cos(ar), np.cos(ac), np.cos(ac)], axis=1)
    sin = np.concatenate([-np.sin(ar), np.sin(ar), -np.sin(ac), np.sin(ac)], axis=1)
    return jnp.asarray(cos, F32), jnp.asarray(sin, F32)


def _const_spec(shape):
    nd = len(shape)
    return pl.BlockSpec(shape, lambda b, _nd=nd: (0,) * _nd, pipeline_mode=pl.Buffered(1))


def _mixer(x, mod_rows, mod_per_batch, state_in, emit_state, use_rope, g1, w_in_bf, w_fmix):
    nb, n, _ = x.shape
    has_state_in = state_in is not None
    cw, cn, sn = _dft_consts(n)
    dmat, qdec, kdec, sdec = _retention_consts()
    consts = [cw, cn, sn, dmat, qdec, kdec, sdec]
    if use_rope:
        consts += list(_rope_consts(n))
    weights = [g1.reshape(1, D_MODEL), w_in_bf, w_fmix]

    mod_map = (lambda b: (b, 0, 0)) if mod_per_batch else (lambda b: (0, 0, 0))
    state_spec = pl.BlockSpec((1, 2, N_RET_HEADS, HEAD_DIM, HEAD_DIM), lambda b: (b, 0, 0, 0, 0))
    in_specs = [pl.BlockSpec((1, n, D_MODEL), lambda b: (b, 0, 0)),
                pl.BlockSpec((1, 1, 6 * D_MODEL), mod_map)]
    in_specs += [_const_spec(a.shape) for a in weights + consts]
    args = [x, mod_rows] + weights + consts
    if has_state_in:
        in_specs.append(state_spec)
        args.append(state_in)

    out_shape = [jax.ShapeDtypeStruct((nb, n, D_MODEL), BF16)]
    out_specs = [pl.BlockSpec((1, n, D_MODEL), lambda b: (b, 0, 0))]
    if emit_state:
        out_shape.append(jax.ShapeDtypeStruct((nb, 2, N_RET_HEADS, HEAD_DIM, HEAD_DIM), F32))
        out_specs.append(state_spec)

    return pl.pallas_call(
        functools.partial(_mixer_kernel, n=n, use_rope=use_rope, has_state_in=has_state_in,
                          emit_state=emit_state),
        out_shape=out_shape,
        grid=(nb,),
        in_specs=in_specs,
        out_specs=out_specs,
        scratch_shapes=[pltpu.VMEM((n, D_IN_PROJ), F32),
                        pltpu.VMEM((n, HEAD_DIM), F32), pltpu.VMEM((n, HEAD_DIM), F32)],
        compiler_params=pltpu.CompilerParams(dimension_semantics=("arbitrary",),
                                             vmem_limit_bytes=VMEM_LIMIT),
        name="mixer_rope" if use_rope else "mixer",
    )(*args)


def _post(x, mix, mod_rows, blocks_per_mod_row, g2, w_out_bf, w_router):
    t = x.shape[0]
    if blocks_per_mod_row is None:
        mod_map = lambda b: (0, 0, 0)
    else:
        mod_map = lambda b: (b // blocks_per_mod_row, 0, 0)
    row_spec = pl.BlockSpec((TOKEN_BLOCK, D_MODEL), lambda b: (b, 0))
    return pl.pallas_call(
        _post_kernel,
        out_shape=[jax.ShapeDtypeStruct((t, D_MODEL), F32),
                   jax.ShapeDtypeStruct((t, D_MODEL), BF16),
                   jax.ShapeDtypeStruct((t, N_EXPERTS), F32)],
        grid=(t // TOKEN_BLOCK,),
        in_specs=[row_spec, row_spec,
                  pl.BlockSpec((1, 1, 6 * D_MODEL), mod_map),
                  _const_spec((1, D_MODEL)), _const_spec((D_MODEL, D_MODEL)),
                  _const_spec((D_MODEL, N_EXPERTS))],
        out_specs=[row_spec, row_spec, pl.BlockSpec((TOKEN_BLOCK, N_EXPERTS), lambda b: (b, 0))],
        compiler_params=pltpu.CompilerParams(dimension_semantics=("arbitrary",)),
        name="post",
    )(x, mix, mod_rows, g2.reshape(1, D_MODEL), w_out_bf, w_router)


def _route_kernel(aff_ref, u_ref, slot_ref, starts_ref, *, t, cap):
    aff = aff_ref[...]

    def count(mask):
        return jnp.sum(mask.astype(jnp.int32), axis=1, keepdims=True)

    def as_float(word):
        return lax.bitcast_convert_type(word, F32)

    def value_step(i, cur):
        cand = cur | jnp.left_shift(jnp.int32(1), 30 - i)
        return jnp.where(count(aff >= as_float(cand)) >= cap, cand, cur)

    thr = lax.fori_loop(0, 31, value_step, jnp.zeros((N_EXPERTS, 1), jnp.int32))
    gt = aff >= as_float(thr + 1)
    eq = (aff >= as_float(thr)) & jnp.logical_not(gt)
    need = cap - count(gt)
    tok = lax.broadcasted_iota(jnp.int32, (N_EXPERTS, t), 1)
    nbits = t.bit_length() - 1

    def index_step(i, cur):
        cand = cur | jnp.left_shift(jnp.int32(1), nbits - 1 - i)
        return jnp.where(count(eq & (tok < cand)) < need, cand, cur)

    last = lax.fori_loop(0, nbits, index_step, jnp.zeros((N_EXPERTS, 1), jnp.int32))
    self = jnp.where(gt | (eq & (tok <= last)), 1.0, 0.0).astype(F32)

    carry = jnp.zeros((N_EXPERTS, 1), F32)
    for b in range(t // TOKEN_BLOCK):
        sl = slice(b * TOKEN_BLOCK, (b + 1) * TOKEN_BLOCK)
        sbf = self[:, sl]
        pre = _dot(sbf.astype(BF16), u_ref[...]) + carry
        slot_ref[:, sl] = jnp.where(sbf > 0.5, pre.astype(jnp.int32), -1)
        starts_ref[b] = jnp.broadcast_to(carry, (N_EXPERTS, 128)).astype(jnp.int32)
        carry = carry + jnp.sum(sbf, axis=1, keepdims=True)
    starts_ref[t // TOKEN_BLOCK] = jnp.broadcast_to(carry, (N_EXPERTS, 128)).astype(jnp.int32)


def _route(aff_et):
    t = aff_et.shape[1]
    cap = EC_CAPACITY_FACTOR * t // N_EXPERTS
    nblk = t // TOKEN_BLOCK
    upper = jnp.asarray(np.triu(np.ones((TOKEN_BLOCK, TOKEN_BLOCK)), 1), BF16)
    slot_et, starts = pl.pallas_call(
        functools.partial(_route_kernel, t=t, cap=cap),
        out_shape=[jax.ShapeDtypeStruct((N_EXPERTS, t), jnp.int32),
                   jax.ShapeDtypeStruct((nblk + 1, N_EXPERTS, 128), jnp.int32)],
        grid=(1,),
        in_specs=[pl.BlockSpec((N_EXPERTS, t), lambda i: (0, 0)),
                  pl.BlockSpec((TOKEN_BLOCK, TOKEN_BLOCK), lambda i: (0, 0))],
        out_specs=[pl.BlockSpec((N_EXPERTS, t), lambda i: (0, 0)),
                   pl.BlockSpec((nblk + 1, N_EXPERTS, 128), lambda i: (0, 0, 0))],
        compiler_params=pltpu.CompilerParams(dimension_semantics=("arbitrary",)),
        name="route",
    )(aff_et, upper)
    return slot_et, starts[:, :, 0]


def _gather_group(e, starts_ref, slot_ref, gate_ref, h2_ref, xs_ref, gs_ref, row0, t, cap):
    nblk = t // TOKEN_BLOCK
    for sblk in range(cap // TOKEN_BLOCK):
        lo = sblk * TOKEN_BLOCK
        rows = slice(row0 + lo, row0 + lo + TOKEN_BLOCK)

        def body(b, carry, lo=lo, rows=rows):
            a0 = starts_ref[b, e]
            a1 = starts_ref[b + 1, e]

            @pl.when((a1 > a0) & (a1 > lo) & (a0 < lo + TOKEN_BLOCK))
            def _():
                srow = slot_ref[0, pl.ds(b, 1), :]
                grow = gate_ref[0, pl.ds(b, 1), :]
                want = lo + lax.broadcasted_iota(jnp.int32, (TOKEN_BLOCK, TOKEN_BLOCK), 0)
                hit = srow == want
                hb = h2_ref[pl.ds(pl.multiple_of(b * TOKEN_BLOCK, TOKEN_BLOCK), TOKEN_BLOCK), :]
                got = _dot(jnp.where(hit, 1.0, 0.0).astype(BF16), hb)
                xs_ref[rows, :] = (xs_ref[rows, :].astype(F32) + got).astype(BF16)
                gs_ref[rows, :] += jnp.sum(jnp.where(hit, grow, 0.0), axis=1, keepdims=True)

            return carry

        lax.fori_loop(0, nblk, body, 0)


def _experts_kernel(sp_ref, ss_ref, h2p_ref, h2s_ref, slotp_ref, slots_ref, gatep_ref, gates_ref,
                    wg_ref, wu_ref, wd_ref, yp_ref, ys_ref, xs_ref, gs_ref, acc_ref,
                    *, tp, ts, capp, caps):
    e = pl.program_id(0)
    f = pl.program_id(1)

    @pl.when(f == 0)
    def _():
        xs_ref[...] = jnp.zeros(xs_ref.shape, BF16)
        gs_ref[...] = jnp.zeros(gs_ref.shape, F32)
        acc_ref[...] = jnp.zeros(acc_ref.shape, F32)
        _gather_group(e, sp_ref, slotp_ref, gatep_ref, h2p_ref, xs_ref, gs_ref, 0, tp, capp)
        _gather_group(e, ss_ref, slots_ref, gates_ref, h2s_ref, xs_ref, gs_ref, capp, ts, caps)

    x = xs_ref[...]
    a = _dot(x, wg_ref[0].astype(BF16))
    u = _dot(x, wu_ref[0].astype(BF16))
    acc_ref[...] += _dot((_silu(a) * u).astype(BF16), wd_ref[0].astype(BF16))

    @pl.when(f == pl.num_programs(1) - 1)
    def _():
        yp_ref[0] = (acc_ref[0:capp, :] * gs_ref[0:capp, :]).astype(BF16)
        ys_ref[0] = (acc_ref[capp:capp + caps, :] * gs_ref[capp:capp + caps, :]).astype(BF16)


def _experts(starts_p, starts_s, h2p, h2s, slot_p, slot_s, gate_p, gate_s, w_gate, w_up, w_down):
    tp, ts = h2p.shape[0], h2s.shape[0]
    capp = EC_CAPACITY_FACTOR * tp // N_EXPERTS
    caps = EC_CAPACITY_FACTOR * ts // N_EXPERTS
    rows = capp + caps
    nbp, nbs = tp // TOKEN_BLOCK, ts // TOKEN_BLOCK
    grid_spec = pltpu.PrefetchScalarGridSpec(
        num_scalar_prefetch=2,
        grid=(N_EXPERTS, D_EXPERT_FF // FF_TILE),
        in_specs=[
            pl.BlockSpec(memory_space=pltpu.VMEM),
            pl.BlockSpec(memory_space=pltpu.VMEM),
            pl.BlockSpec((1, nbp, TOKEN_BLOCK), lambda e, f, *_: (e, 0, 0)),
            pl.BlockSpec((1, nbs, TOKEN_BLOCK), lambda e, f, *_: (e, 0, 0)),
            pl.BlockSpec((1, nbp, TOKEN_BLOCK), lambda e, f, *_: (e, 0, 0)),
            pl.BlockSpec((1, nbs, TOKEN_BLOCK), lambda e, f, *_: (e, 0, 0)),
            pl.BlockSpec((1, D_MODEL, FF_TILE), lambda e, f, *_: (e, 0, f)),
            pl.BlockSpec((1, D_MODEL, FF_TILE), lambda e, f, *_: (e, 0, f)),
            pl.BlockSpec((1, FF_TILE, D_MODEL), lambda e, f, *_: (e, f, 0)),
        ],
        out_specs=[
            pl.BlockSpec((1, capp, D_MODEL), lambda e, f, *_: (e, 0, 0)),
            pl.BlockSpec((1, caps, D_MODEL), lambda e, f, *_: (e, 0, 0)),
        ],
        scratch_shapes=[pltpu.VMEM((rows, D_MODEL), BF16), pltpu.VMEM((rows, 1), F32),
                        pltpu.VMEM((rows, D_MODEL), F32)],
    )
    return pl.pallas_call(
        functools.partial(_experts_kernel, tp=tp, ts=ts, capp=capp, caps=caps),
        out_shape=[jax.ShapeDtypeStruct((N_EXPERTS, capp, D_MODEL), BF16),
                   jax.ShapeDtypeStruct((N_EXPERTS, caps, D_MODEL), BF16)],
        grid_spec=grid_spec,
        compiler_params=pltpu.CompilerParams(dimension_semantics=("arbitrary", "arbitrary"),
                                             vmem_limit_bytes=VMEM_LIMIT),
        name="experts",
    )(starts_p, starts_s, h2p, h2s,
      slot_p.reshape(N_EXPERTS, nbp, TOKEN_BLOCK), slot_s.reshape(N_EXPERTS, nbs, TOKEN_BLOCK),
      gate_p.reshape(N_EXPERTS, nbp, TOKEN_BLOCK), gate_s.reshape(N_EXPERTS, nbs, TOKEN_BLOCK),
      w_gate, w_up, w_down)


def _combine_kernel(st_ref, x1_ref, slot_ref, mod_ref, fg_ref, y_ref, out_ref, acc_ref, *, cap):
    b = pl.program_id(0)
    slot = slot_ref[...]
    acc_ref[...] = jnp.zeros(acc_ref.shape, F32)
    lane = lax.broadcasted_iota(jnp.int32, (TOKEN_BLOCK, PACK * SLOT_WINDOW), 1)
    for g in range(N_EXPERTS // PACK):
        experts = range(g * PACK, (g + 1) * PACK)
        first = [jnp.minimum((st_ref[b, e] // 16) * 16, cap - SLOT_WINDOW) for e in experts]
        rows = [jnp.where(st_ref[b + 1, e] > st_ref[b, e], st_ref[b + 1, e] - w, 0)
                for e, w in zip(experts, first)]
        n_windows = pl.cdiv(functools.reduce(jnp.maximum, rows), SLOT_WINDOW)

        def body(i, carry, experts=experts, first=first):
            target = None
            windows = []
            for j, e in enumerate(experts):
                lo = first[j] + i * SLOT_WINDOW
                w = jnp.minimum(lo, cap - SLOT_WINDOW)
                sc = slot[:, e:e + 1]
                col = jnp.where(sc >= lo, sc - w + j * SLOT_WINDOW, -1)
                target = col if target is None else jnp.where(lane < j * SLOT_WINDOW, target, col)
                windows.append(y_ref[e, pl.ds(pl.multiple_of(w, 16), SLOT_WINDOW), :])
            onehot = jnp.where(target == lane, 1.0, 0.0).astype(BF16)
            acc_ref[...] += _dot(onehot, jnp.concatenate(windows, axis=0))
            return carry

        lax.fori_loop(0, n_windows, body, 0)

    gate2 = mod_ref[0][:, 5 * D_MODEL:6 * D_MODEL]
    x2 = x1_ref[...] + gate2 * acc_ref[...]
    out_ref[...] = _rms(x2) * fg_ref[...]


def _combine(starts, x1, slot_te, mod_rows, blocks_per_mod_row, final_g, y):
    t = x1.shape[0]
    cap = y.shape[1]
    nblk = t // TOKEN_BLOCK
    if blocks_per_mod_row is None:
        mod_map = lambda b, *_: (0, 0, 0)
    else:
        mod_map = lambda b, *_: (b // blocks_per_mod_row, 0, 0)
    grid_spec = pltpu.PrefetchScalarGridSpec(
        num_scalar_prefetch=1,
        grid=(nblk,),
        in_specs=[
            pl.BlockSpec((TOKEN_BLOCK, D_MODEL), lambda b, *_: (b, 0)),
            pl.BlockSpec((TOKEN_BLOCK, N_EXPERTS), lambda b, *_: (b, 0)),
            pl.BlockSpec((1, 1, 6 * D_MODEL), mod_map),
            pl.BlockSpec((1, D_MODEL), lambda b, *_: (0, 0)),
            pl.BlockSpec(memory_space=pltpu.VMEM),
        ],
        out_specs=pl.BlockSpec((TOKEN_BLOCK, D_MODEL), lambda b, *_: (b, 0)),
        scratch_shapes=[pltpu.VMEM((TOKEN_BLOCK, D_MODEL), F32)],
    )
    return pl.pallas_call(
        functools.partial(_combine_kernel, cap=cap),
        out_shape=jax.ShapeDtypeStruct((t, D_MODEL), F32),
        grid_spec=grid_spec,
        compiler_params=pltpu.CompilerParams(dimension_semantics=("arbitrary",),
                                             vmem_limit_bytes=VMEM_LIMIT),
        name="combine",
    )(starts, x1, slot_te, mod_rows, final_g.reshape(1, D_MODEL), y)


def kernel(x_prompt, x_sample, state_ret, c, c_ctx, norm1_g, norm2_g, final_g, w_mod, b_mod, w_in,
           w_fmix, w_out, w_router, w_gate, w_up, w_down):
    bp, seq, _ = x_prompt.shape
    bs, dec_seq, _ = x_sample.shape
    assert w_mod.shape[0] == 1, "single-layer trunk"
    tp, ts = bp * seq, bs * dec_seq

    cond = jnp.concatenate([c_ctx[None, :], c], axis=0)
    mod = _modulation(cond, w_mod[0], b_mod[0])
    mod_ctx = mod[0:1].reshape(1, 1, 6 * D_MODEL)
    mod_lat = mod[1:1 + bs].reshape(bs, 1, 6 * D_MODEL)

    w_in_bf = w_in[0].astype(BF16)
    w_out_bf = w_out[0].astype(BF16)
    mix_p, states = _mixer(x_prompt, mod_ctx, False, None, True, False, norm1_g[0], w_in_bf, w_fmix[0])
    (mix_s,) = _mixer(x_sample, mod_lat, True, state_ret[:, 0], False, True, norm1_g[0], w_in_bf,
                      w_fmix[0])
    blocks_per_seq = dec_seq // TOKEN_BLOCK
    x1p, h2p, affp = _post(x_prompt.reshape(tp, D_MODEL), mix_p.reshape(tp, D_MODEL), mod_ctx, None,
                           norm2_g[0], w_out_bf, w_router[0])
    x1s, h2s, affs = _post(x_sample.reshape(ts, D_MODEL), mix_s.reshape(ts, D_MODEL), mod_lat,
                           blocks_per_seq, norm2_g[0], w_out_bf, w_router[0])

    affp, affs = affp.T, affs.T
    slot_p, starts_p = _route(affp)
    slot_s, starts_s = _route(affs)

    yp, ys = _experts(starts_p, starts_s, h2p, h2s, slot_p, slot_s, affp, affs,
                      w_gate[0], w_up[0], w_down[0])

    out_p = _combine(starts_p, x1p, slot_p.T, mod_ctx, None, final_g, yp)
    out_s = _combine(starts_s, x1s, slot_s.T, mod_lat, blocks_per_seq, final_g, ys)

    y_prompt = out_p.reshape(bp, seq, D_MODEL)
    y_sample = out_s.reshape(bs, dec_seq, D_MODEL)
    state_new = states.reshape(bp, 1, 2, N_RET_HEADS, HEAD_DIM, HEAD_DIM).astype(x_prompt.dtype)
    return (y_prompt, y_sample, state_new)
```

```python
import functools
import math

import jax
import jax.numpy as jnp
import numpy as np
from jax import lax
from jax.experimental import pallas as pl
from jax.experimental.pallas import tpu as pltpu

D_MODEL = 1024
D_FOURIER = 512
N_FOURIER_GROUPS = 4
FOURIER_GROUP_W = 128
D_RET = 512
N_RET_HEADS = 4
HEAD_DIM = 128
CHUNK = 128
GRID_W = 64
N_EXPERTS = 16
EC_CAPACITY_FACTOR = 2
D_EXPERT_FF = 2816
ROPE_BASE = 10000.0
EPS = 1e-6
D_IN_PROJ = D_FOURIER + 5 * D_RET
LOG_GAMMA_FWD = np.log(1.0 - 2.0 ** (-5.0 - np.arange(N_RET_HEADS))).astype(np.float32)
LOG_GAMMA_BWD = np.log(1.0 - 2.0 ** (-5.5 - np.arange(N_RET_HEADS))).astype(np.float32)

TOKEN_BLOCK = 256
SLOT_WINDOW = 64
PACK = TOKEN_BLOCK // SLOT_WINDOW
FF_TILE = 256
N_FF_TILES = D_EXPERT_FF // FF_TILE
MOD_TILE = 512
VMEM_LIMIT = 56 * 1024 * 1024

F32 = jnp.float32
BF16 = jnp.bfloat16


def _dot(a, b):
    return jnp.dot(a, b, preferred_element_type=F32)


def _dot_nt(a, b):
    return lax.dot_general(a, b, (((1,), (1,)), ((), ())), preferred_element_type=F32)


def _silu(x):
    return x * jax.nn.sigmoid(x)


def _mod_kernel(condt_ref, w_ref, b_ref, out_ref, *, n_cond):
    s = _silu(condt_ref[...])
    w = w_ref[...]
    out_ref[...] = jnp.zeros(out_ref.shape, F32)
    for r in range(n_cond):
        out_ref[r:r + 1, :] = jnp.sum(w * s[:, r:r + 1], axis=0, keepdims=True) + b_ref[...]


def _modulation(cond_rows, w_mod, b_mod):
    n_cond = cond_rows.shape[0]
    condt = jnp.zeros((D_MODEL, 8), F32).at[:, :n_cond].set(cond_rows.T)
    n_out = w_mod.shape[1]
    return pl.pallas_call(
        functools.partial(_mod_kernel, n_cond=n_cond),
        out_shape=jax.ShapeDtypeStruct((8, n_out), F32),
        grid=(n_out // MOD_TILE,),
        in_specs=[
            pl.BlockSpec((D_MODEL, 8), lambda j: (0, 0)),
            pl.BlockSpec((D_MODEL, MOD_TILE), lambda j: (0, j)),
            pl.BlockSpec((1, MOD_TILE), lambda j: (0, j)),
        ],
        out_specs=pl.BlockSpec((8, MOD_TILE), lambda j: (0, j)),
        compiler_params=pltpu.CompilerParams(dimension_semantics=("arbitrary",)),
        name="mod",
    )(condt, w_mod, b_mod.reshape(1, n_out))


def _rms(x):
    return x * lax.rsqrt(jnp.mean(x * x, axis=-1, keepdims=True) + EPS)


def _groupnorm(o):
    mu = jnp.mean(o, axis=-1, keepdims=True)
    c = o - mu
    return c * lax.rsqrt(jnp.mean(c * c, axis=-1, keepdims=True) + EPS)


def _split_hi_lo(x):
    hi = x.astype(BF16)
    lo = (x - hi.astype(F32)).astype(BF16)
    return hi, lo


def _mixer_kernel(*refs, n, use_rope, has_state_in, emit_state):
    it = iter(refs)
    x_ref, mod_ref, g1_ref, win_ref, wfmix_ref = (next(it) for _ in range(5))
    cw_ref, cn_ref, sn_ref, dmat_ref, qdec_ref, kdec_ref, sdec_ref = (next(it) for _ in range(7))
    cos_ref = sin_ref = s0_ref = st_ref = None
    if use_rope:
        cos_ref, sin_ref = next(it), next(it)
    if has_state_in:
        s0_ref = next(it)
    mix_ref = next(it)
    if emit_state:
        st_ref = next(it)
    p_ref, of_ref, ob_ref = next(it), next(it), next(it)

    nc = n // CHUNK
    mod = mod_ref[0]
    shift1 = mod[:, 0:D_MODEL]
    scale1 = mod[:, D_MODEL:2 * D_MODEL]

    h = (_rms(x_ref[0]) * g1_ref[...] * (1.0 + scale1) + shift1).astype(BF16)
    for j in range(D_IN_PROJ // 512):
        p_ref[:, j * 512:(j + 1) * 512] = _dot(h, win_ref[:, j * 512:(j + 1) * 512])

    xf = p_ref[:, 0:D_FOURIER].astype(BF16)
    xc, xs = [], []
    cw = cw_ref[...].astype(BF16)
    for g in range(N_FOURIER_GROUPS):
        t = _dot(xf[:, g * FOURIER_GROUP_W:(g + 1) * FOURIER_GROUP_W], cw)
        xc.append(t[:, :FOURIER_GROUP_W].astype(BF16))
        xs.append(t[:, FOURIER_GROUP_W:].astype(BF16))
    xc = jnp.concatenate(xc, axis=1)
    xs = jnp.concatenate(xs, axis=1)
    fre = _dot(cn_ref[...].astype(BF16), xc) - _dot(sn_ref[...].astype(BF16), xs)
    fre = fre * (1.0 / math.sqrt(n * FOURIER_GROUP_W))
    fre = fre.astype(BF16)
    for g in range(N_FOURIER_GROUPS):
        sl = slice(g * FOURIER_GROUP_W, (g + 1) * FOURIER_GROUP_W)
        mix_ref[0, :, sl] = _dot(fre[:, sl], wfmix_ref[g].astype(BF16)).astype(BF16)

    for hh in range(N_RET_HEADS):
        base = D_FOURIER + hh * HEAD_DIM
        q = p_ref[:, base:base + HEAD_DIM]
        k = p_ref[:, base + D_RET:base + D_RET + HEAD_DIM]
        v = p_ref[:, base + 2 * D_RET:base + 2 * D_RET + HEAD_DIM]
        if use_rope:
            lane = lax.broadcasted_iota(jnp.int32, (n, HEAD_DIM), 1)
            first = (lane % 64) < 32

            def rope(t):
                swapped = jnp.where(first, pltpu.roll(t, HEAD_DIM - 32, 1), pltpu.roll(t, 32, 1))
                return t * cos_ref[...] + swapped * sin_ref[...]

            q, k = rope(q), rope(k)
        k = k * (HEAD_DIM ** -0.5)
        qb, vb = q.astype(BF16), v.astype(BF16)
        kb = k.astype(BF16)

        if has_state_in:
            sf = s0_ref[0, 0, hh]
            sb = s0_ref[0, 1, hh]
        else:
            sf = jnp.zeros((HEAD_DIM, HEAD_DIM), F32)
            sb = jnp.zeros((HEAD_DIM, HEAD_DIM), F32)

        for c in range(nc):
            rs = slice(c * CHUNK, (c + 1) * CHUNK)
            qc, kc, vc = qb[rs], kb[rs], vb[rs]
            qk = _dot_nt(qc, kc)
            o = _dot((qk * dmat_ref[0, hh]).astype(BF16), vc)
            if has_state_in or c > 0:
                o = o + qdec_ref[0, hh] * _dot(qc, sf.astype(BF16))
            of_ref[rs, :] = o
            ob_ref[rs, :] = _dot((qk * dmat_ref[1, hh]).astype(BF16), vc)
            kd = (k[rs] * kdec_ref[0, hh]).T.astype(BF16)
            sf = sf * sdec_ref[0, hh] + _dot(kd, vc)
        for c in reversed(range(nc)):
            rs = slice(c * CHUNK, (c + 1) * CHUNK)
            qc, vc = qb[rs], vb[rs]
            if has_state_in or c < nc - 1:
                ob_ref[rs, :] = ob_ref[rs, :] + qdec_ref[1, hh] * _dot(qc, sb.astype(BF16))
            kd = (k[rs] * kdec_ref[1, hh]).T.astype(BF16)
            sb = sb * sdec_ref[1, hh] + _dot(kd, vc)
        if emit_state:
            st_ref[0, 0, hh] = sf
            st_ref[0, 1, hh] = sb

        gf = p_ref[:, base + 3 * D_RET:base + 3 * D_RET + HEAD_DIM]
        gb = p_ref[:, base + 4 * D_RET:base + 4 * D_RET + HEAD_DIM]
        y = _silu(gf) * _groupnorm(of_ref[...]) + _silu(gb) * _groupnorm(ob_ref[...])
        mix_ref[0, :, base:base + HEAD_DIM] = y.astype(BF16)


def _post_kernel(x_ref, mix_ref, mod_ref, g2_ref, wout_ref, wr_ref, x1_ref, h2_ref, aff_ref):
    mod = mod_ref[0]
    gate1 = mod[:, 2 * D_MODEL:3 * D_MODEL]
    shift2 = mod[:, 3 * D_MODEL:4 * D_MODEL]
    scale2 = mod[:, 4 * D_MODEL:5 * D_MODEL]
    x1 = x_ref[...] + gate1 * _dot(mix_ref[...], wout_ref[...])
    x1_ref[...] = x1
    h2 = _rms(x1) * g2_ref[...] * (1.0 + scale2) + shift2
    h2_hi, h2_lo = _split_hi_lo(h2)
    h2_ref[...] = h2_hi
    wr_hi, wr_lo = _split_hi_lo(wr_ref[...])
    logits = _dot(h2_hi, wr_hi) + (_dot(h2_lo, wr_hi) + _dot(h2_hi, wr_lo))
    z = jnp.exp(logits - jnp.max(logits, axis=-1, keepdims=True))
    aff_ref[...] = z / jnp.sum(z, axis=-1, keepdims=True)


def _dft_consts(n):
    w = FOURIER_GROUP_W
    jw = np.arange(w)
    angw = 2.0 * np.pi * np.outer(jw, jw) / w
    cw = np.concatenate([np.cos(angw), np.sin(angw)], axis=1)
    jn = np.arange(n)
    angn = 2.0 * np.pi * (np.outer(jn, jn) % n) / n
    return (jnp.asarray(cw, F32), jnp.asarray(np.cos(angn), F32), jnp.asarray(np.sin(angn), F32))


def _retention_consts():
    i = np.arange(CHUNK, dtype=np.float64)
    diff = i[:, None] - i[None, :]
    dmat = np.zeros((2, N_RET_HEADS, CHUNK, CHUNK))
    qdec = np.zeros((2, N_RET_HEADS, CHUNK, HEAD_DIM))
    kdec = np.zeros((2, N_RET_HEADS, CHUNK, HEAD_DIM))
    sdec = np.zeros((2, N_RET_HEADS, CHUNK, HEAD_DIM))
    for hh in range(N_RET_HEADS):
        lf = float(LOG_GAMMA_FWD[hh])
        lb = float(LOG_GAMMA_BWD[hh])
        dmat[0, hh] = np.where(diff >= 0, np.exp(lf * np.maximum(diff, 0.0)), 0.0)
        dmat[1, hh] = np.where(diff <= 0, np.exp(lb * np.maximum(-diff, 0.0)), 0.0)
        qdec[0, hh] = np.exp(lf * (i + 1.0))[:, None]
        qdec[1, hh] = np.exp(lb * (CHUNK - i))[:, None]
        kdec[0, hh] = np.exp(lf * (CHUNK - 1.0 - i))[:, None]
        kdec[1, hh] = np.exp(lb * i)[:, None]
        sdec[0, hh] = math.exp(lf * CHUNK)
        sdec[1, hh] = math.exp(lb * CHUNK)
    return tuple(jnp.asarray(a, F32) for a in (dmat, qdec, kdec, sdec))


def _rope_consts(n):
    rows_n = n // GRID_W
    row = np.repeat(np.arange(rows_n, dtype=np.float64), GRID_W)
    col = np.tile(np.arange(GRID_W, dtype=np.float64), rows_n)
    n_pairs = HEAD_DIM // 4
    freqs = (np.float32(ROPE_BASE) ** (-np.arange(n_pairs, dtype=np.float32) / n_pairs)).astype(np.float64)
    ar = row[:, None] * freqs[None, :]
    ac = col[:, None] * freqs[None, :]
    cos = np.concatenate([np.cos(ar), np.cos(ar), np.cos(ac), np.cos(ac)], axis=1)
    sin = np.concatenate([-np.sin(ar), np.sin(ar), -np.sin(ac), np.sin(ac)], axis=1)
    return jnp.asarray(cos, F32), jnp.asarray(sin, F32)


def _const_spec(shape):
    nd = len(shape)
    return pl.BlockSpec(shape, lambda b, _nd=nd: (0,) * _nd, pipeline_mode=pl.Buffered(1))


def _mixer(x, mod_rows, mod_per_batch, state_in, emit_state, use_rope, g1, w_in_bf, w_fmix):
    nb, n, _ = x.shape
    has_state_in = state_in is not None
    cw, cn, sn = _dft_consts(n)
    dmat, qdec, kdec, sdec = _retention_consts()
    consts = [cw, cn, sn, dmat, qdec, kdec, sdec]
    if use_rope:
        consts += list(_rope_consts(n))
    weights = [g1.reshape(1, D_MODEL), w_in_bf, w_fmix]

    mod_map = (lambda b: (b, 0, 0)) if mod_per_batch else (lambda b: (0, 0, 0))
    state_spec = pl.BlockSpec((1, 2, N_RET_HEADS, HEAD_DIM, HEAD_DIM), lambda b: (b, 0, 0, 0, 0))
    in_specs = [pl.BlockSpec((1, n, D_MODEL), lambda b: (b, 0, 0)),
                pl.BlockSpec((1, 1, 6 * D_MODEL), mod_map)]
    in_specs += [_const_spec(a.shape) for a in weights + consts]
    args = [x, mod_rows] + weights + consts
    if has_state_in:
        in_specs.append(state_spec)
        args.append(state_in)

    out_shape = [jax.ShapeDtypeStruct((nb, n, D_MODEL), BF16)]
    out_specs = [pl.BlockSpec((1, n, D_MODEL), lambda b: (b, 0, 0))]
    if emit_state:
        out_shape.append(jax.ShapeDtypeStruct((nb, 2, N_RET_HEADS, HEAD_DIM, HEAD_DIM), F32))
        out_specs.append(state_spec)

    return pl.pallas_call(
        functools.partial(_mixer_kernel, n=n, use_rope=use_rope, has_state_in=has_state_in,
                          emit_state=emit_state),
        out_shape=out_shape,
        grid=(nb,),
        in_specs=in_specs,
        out_specs=out_specs,
        scratch_shapes=[pltpu.VMEM((n, D_IN_PROJ), F32),
                        pltpu.VMEM((n, HEAD_DIM), F32), pltpu.VMEM((n, HEAD_DIM), F32)],
        compiler_params=pltpu.CompilerParams(dimension_semantics=("arbitrary",),
                                             vmem_limit_bytes=VMEM_LIMIT),
        name="mixer_rope" if use_rope else "mixer",
    )(*args)


def _post(x, mix, mod_rows, blocks_per_mod_row, g2, w_out_bf, w_router):
    t = x.shape[0]
    if blocks_per_mod_row is None:
        mod_map = lambda b: (0, 0, 0)
    else:
        mod_map = lambda b: (b // blocks_per_mod_row, 0, 0)
    row_spec = pl.BlockSpec((TOKEN_BLOCK, D_MODEL), lambda b: (b, 0))
    return pl.pallas_call(
        _post_kernel,
        out_shape=[jax.ShapeDtypeStruct((t, D_MODEL), F32),
                   jax.ShapeDtypeStruct((t, D_MODEL), BF16),
                   jax.ShapeDtypeStruct((t, N_EXPERTS), F32)],
        grid=(t // TOKEN_BLOCK,),
        in_specs=[row_spec, row_spec,
                  pl.BlockSpec((1, 1, 6 * D_MODEL), mod_map),
                  _const_spec((1, D_MODEL)), _const_spec((D_MODEL, D_MODEL)),
                  _const_spec((D_MODEL, N_EXPERTS))],
        out_specs=[row_spec, row_spec, pl.BlockSpec((TOKEN_BLOCK, N_EXPERTS), lambda b: (b, 0))],
        compiler_params=pltpu.CompilerParams(dimension_semantics=("arbitrary",)),
        name="post",
    )(x, mix, mod_rows, g2.reshape(1, D_MODEL), w_out_bf, w_router)


def _route_kernel(aff_ref, u_ref, slot_ref, starts_ref, *, t, cap):
    aff = aff_ref[...]

    def count(mask):
        return jnp.sum(mask.astype(jnp.int32), axis=1, keepdims=True)

    def as_float(word):
        return lax.bitcast_convert_type(word, F32)

    def value_step(i, cur):
        cand = cur | jnp.left_shift(jnp.int32(1), 30 - i)
        return jnp.where(count(aff >= as_float(cand)) >= cap, cand, cur)

    thr = lax.fori_loop(0, 31, value_step, jnp.zeros((N_EXPERTS, 1), jnp.int32))
    gt = aff >= as_float(thr + 1)
    eq = (aff >= as_float(thr)) & jnp.logical_not(gt)
    need = cap - count(gt)
    tok = lax.broadcasted_iota(jnp.int32, (N_EXPERTS, t), 1)
    nbits = t.bit_length() - 1

    def index_step(i, cur):
        cand = cur | jnp.left_shift(jnp.int32(1), nbits - 1 - i)
        return jnp.where(count(eq & (tok < cand)) < need, cand, cur)

    last = lax.fori_loop(0, nbits, index_step, jnp.zeros((N_EXPERTS, 1), jnp.int32))
    self = jnp.where(gt | (eq & (tok <= last)), 1.0, 0.0).astype(F32)

    carry = jnp.zeros((N_EXPERTS, 1), F32)
    for b in range(t // TOKEN_BLOCK):
        sl = slice(b * TOKEN_BLOCK, (b + 1) * TOKEN_BLOCK)
        sbf = self[:, sl]
        pre = _dot(sbf.astype(BF16), u_ref[...]) + carry
        slot_ref[:, sl] = jnp.where(sbf > 0.5, pre.astype(jnp.int32), -1)
        starts_ref[b] = jnp.broadcast_to(carry, (N_EXPERTS, 128)).astype(jnp.int32)
        carry = carry + jnp.sum(sbf, axis=1, keepdims=True)
    starts_ref[t // TOKEN_BLOCK] = jnp.broadcast_to(carry, (N_EXPERTS, 128)).astype(jnp.int32)


def _route(aff_et):
    t = aff_et.shape[1]
    cap = EC_CAPACITY_FACTOR * t // N_EXPERTS
    nblk = t // TOKEN_BLOCK
    upper = jnp.asarray(np.triu(np.ones((TOKEN_BLOCK, TOKEN_BLOCK)), 1), BF16)
    slot_et, starts = pl.pallas_call(
        functools.partial(_route_kernel, t=t, cap=cap),
        out_shape=[jax.ShapeDtypeStruct((N_EXPERTS, t), jnp.int32),
                   jax.ShapeDtypeStruct((nblk + 1, N_EXPERTS, 128), jnp.int32)],
        grid=(1,),
        in_specs=[pl.BlockSpec((N_EXPERTS, t), lambda i: (0, 0)),
                  pl.BlockSpec((TOKEN_BLOCK, TOKEN_BLOCK), lambda i: (0, 0))],
        out_specs=[pl.BlockSpec((N_EXPERTS, t), lambda i: (0, 0)),
                   pl.BlockSpec((nblk + 1, N_EXPERTS, 128), lambda i: (0, 0, 0))],
        compiler_params=pltpu.CompilerParams(dimension_semantics=("arbitrary",)),
        name="route",
    )(aff_et, upper)
    return slot_et, starts[:, :, 0]


def _pack_windows(starts_ref, b, experts, cap):
    first = [jnp.minimum((starts_ref[b, e] // 16) * 16, cap - SLOT_WINDOW) for e in experts]
    rows = [jnp.where(starts_ref[b + 1, e] > starts_ref[b, e], starts_ref[b + 1, e] - w, 0)
            for e, w in zip(experts, first)]
    return first, pl.cdiv(functools.reduce(jnp.maximum, rows), SLOT_WINDOW)


def _gather_group(g, starts_ref, slot_ref, gate_ref, h2_ref, xs_ref, gs_ref, row0, t, cap):
    sub = lax.broadcasted_iota(jnp.int32, (SLOT_WINDOW, TOKEN_BLOCK), 0)
    experts = [g * PACK + j for j in range(PACK)]

    def block(b, carry):
        first, n_windows = _pack_windows(starts_ref, b, experts, cap)
        hb = pl.ds(pl.multiple_of(b * TOKEN_BLOCK, TOKEN_BLOCK), TOKEN_BLOCK)

        def window(i, carry):
            hits, dst = [], []
            for j in range(PACK):
                lo = first[j] + i * SLOT_WINDOW
                w = jnp.minimum(lo, cap - SLOT_WINDOW)
                srow = slot_ref[j, pl.ds(b, 1), :]
                hits.append((srow == w + sub) & (srow >= lo))
                dst.append(pl.ds(pl.multiple_of(row0 + w, 16), SLOT_WINDOW))
            onehot = jnp.concatenate([jnp.where(h, 1.0, 0.0) for h in hits], axis=0).astype(BF16)
            got = _dot(onehot, h2_ref[hb, :])
            for j in range(PACK):
                piece = got[j * SLOT_WINDOW:(j + 1) * SLOT_WINDOW]
                xs_ref[j, dst[j], :] = (xs_ref[j, dst[j], :].astype(F32) + piece).astype(BF16)
                grow = gate_ref[j, pl.ds(b, 1), :]
                gs_ref[j, dst[j], :] += jnp.sum(jnp.where(hits[j], grow, 0.0), axis=1, keepdims=True)
            return carry

        lax.fori_loop(0, n_windows, window, 0)
        return carry

    lax.fori_loop(0, t // TOKEN_BLOCK, block, 0)


def _experts_kernel(sp_ref, ss_ref, h2p_ref, h2s_ref, slotp_ref, slots_ref, gatep_ref, gates_ref,
                    wg_ref, wu_ref, wd_ref, yp_ref, ys_ref, xs_ref, gs_ref, acc_ref,
                    *, tp, ts, capp, caps):
    g = pl.program_id(0)
    step = pl.program_id(1)
    j = step // N_FF_TILES
    f = step % N_FF_TILES

    @pl.when(step == 0)
    def _():
        xs_ref[...] = jnp.zeros(xs_ref.shape, BF16)
        gs_ref[...] = jnp.zeros(gs_ref.shape, F32)
        _gather_group(g, sp_ref, slotp_ref, gatep_ref, h2p_ref, xs_ref, gs_ref, 0, tp, capp)
        _gather_group(g, ss_ref, slots_ref, gates_ref, h2s_ref, xs_ref, gs_ref, capp, ts, caps)

    @pl.when(f == 0)
    def _():
        acc_ref[...] = jnp.zeros(acc_ref.shape, F32)

    x = xs_ref[j]
    a = _dot(x, wg_ref[0].astype(BF16))
    u = _dot(x, wu_ref[0].astype(BF16))
    acc_ref[...] += _dot((_silu(a) * u).astype(BF16), wd_ref[0].astype(BF16))

    @pl.when(f == N_FF_TILES - 1)
    def _():
        yp_ref[0] = (acc_ref[0:capp, :] * gs_ref[j, 0:capp, :]).astype(BF16)
        ys_ref[0] = (acc_ref[capp:capp + caps, :] * gs_ref[j, capp:capp + caps, :]).astype(BF16)


def _experts(starts_p, starts_s, h2p, h2s, slot_p, slot_s, gate_p, gate_s, w_gate, w_up, w_down):
    tp, ts = h2p.shape[0], h2s.shape[0]
    capp = EC_CAPACITY_FACTOR * tp // N_EXPERTS
    caps = EC_CAPACITY_FACTOR * ts // N_EXPERTS
    rows = capp + caps
    nbp, nbs = tp // TOKEN_BLOCK, ts // TOKEN_BLOCK
    expert = lambda g, s: g * PACK + s // N_FF_TILES
    grid_spec = pltpu.PrefetchScalarGridSpec(
        num_scalar_prefetch=2,
        grid=(N_EXPERTS // PACK, PACK * N_FF_TILES),
        in_specs=[
            pl.BlockSpec(memory_space=pltpu.VMEM),
            pl.BlockSpec(memory_space=pltpu.VMEM),
            pl.BlockSpec((PACK, nbp, TOKEN_BLOCK), lambda g, s, *_: (g, 0, 0)),
            pl.BlockSpec((PACK, nbs, TOKEN_BLOCK), lambda g, s, *_: (g, 0, 0)),
            pl.BlockSpec((PACK, nbp, TOKEN_BLOCK), lambda g, s, *_: (g, 0, 0)),
            pl.BlockSpec((PACK, nbs, TOKEN_BLOCK), lambda g, s, *_: (g, 0, 0)),
            pl.BlockSpec((1, D_MODEL, FF_TILE), lambda g, s, *_: (expert(g, s), 0, s % N_FF_TILES)),
            pl.BlockSpec((1, D_MODEL, FF_TILE), lambda g, s, *_: (expert(g, s), 0, s % N_FF_TILES)),
            pl.BlockSpec((1, FF_TILE, D_MODEL), lambda g, s, *_: (expert(g, s), s % N_FF_TILES, 0)),
        ],
        out_specs=[
            pl.BlockSpec((1, capp, D_MODEL), lambda g, s, *_: (expert(g, s), 0, 0)),
            pl.BlockSpec((1, caps, D_MODEL), lambda g, s, *_: (expert(g, s), 0, 0)),
        ],
        scratch_shapes=[pltpu.VMEM((PACK, rows, D_MODEL), BF16), pltpu.VMEM((PACK, rows, 1), F32),
                        pltpu.VMEM((rows, D_MODEL), F32)],
    )
    return pl.pallas_call(
        functools.partial(_experts_kernel, tp=tp, ts=ts, capp=capp, caps=caps),
        out_shape=[jax.ShapeDtypeStruct((N_EXPERTS, capp, D_MODEL), BF16),
                   jax.ShapeDtypeStruct((N_EXPERTS, caps, D_MODEL), BF16)],
        grid_spec=grid_spec,
        compiler_params=pltpu.CompilerParams(dimension_semantics=("arbitrary", "arbitrary"),
                                             vmem_limit_bytes=VMEM_LIMIT),
        name="experts",
    )(starts_p, starts_s, h2p, h2s,
      slot_p.reshape(N_EXPERTS, nbp, TOKEN_BLOCK), slot_s.reshape(N_EXPERTS, nbs, TOKEN_BLOCK),
      gate_p.reshape(N_EXPERTS, nbp, TOKEN_BLOCK), gate_s.reshape(N_EXPERTS, nbs, TOKEN_BLOCK),
      w_gate, w_up, w_down)


def _combine_kernel(st_ref, x1_ref, slot_ref, mod_ref, fg_ref, y_ref, out_ref, acc_ref, *, cap):
    b = pl.program_id(0)
    slot = slot_ref[...]
    lane = lax.broadcasted_iota(jnp.int32, (TOKEN_BLOCK, PACK * SLOT_WINDOW), 1)

    def scatter(experts, first, i):
        target = None
        windows = []
        for j, e in enumerate(experts):
            lo = first[j] + i * SLOT_WINDOW
            w = jnp.minimum(lo, cap - SLOT_WINDOW)
            sc = slot[:, e:e + 1]
            col = jnp.where(sc >= lo, sc - w + j * SLOT_WINDOW, -1)
            target = col if target is None else jnp.where(lane < j * SLOT_WINDOW, target, col)
            windows.append(y_ref[e, pl.ds(pl.multiple_of(w, 16), SLOT_WINDOW), :])
        onehot = jnp.where(target == lane, 1.0, 0.0).astype(BF16)
        return _dot(onehot, jnp.concatenate(windows, axis=0))

    groups = []
    total = None
    for g in range(N_EXPERTS // PACK):
        experts = list(range(g * PACK, (g + 1) * PACK))
        first, n_windows = _pack_windows(st_ref, b, experts, cap)
        groups.append((experts, first, n_windows))
        part = scatter(experts, first, 0)
        total = part if total is None else total + part
    acc_ref[...] = total

    for experts, first, n_windows in groups:
        def more(i, carry, experts=experts, first=first):
            acc_ref[...] += scatter(experts, first, i)
            return carry

        lax.fori_loop(1, n_windows, more, 0)

    gate2 = mod_ref[0][:, 5 * D_MODEL:6 * D_MODEL]
    x2 = x1_ref[...] + gate2 * acc_ref[...]
    out_ref[...] = _rms(x2) * fg_ref[...]


def _combine(starts, x1, slot_te, mod_rows, blocks_per_mod_row, final_g, y):
    t = x1.shape[0]
    cap = y.shape[1]
    nblk = t // TOKEN_BLOCK
    if blocks_per_mod_row is None:
        mod_map = lambda b, *_: (0, 0, 0)
    else:
        mod_map = lambda b, *_: (b // blocks_per_mod_row, 0, 0)
    grid_spec = pltpu.PrefetchScalarGridSpec(
        num_scalar_prefetch=1,
        grid=(nblk,),
        in_specs=[
            pl.BlockSpec((TOKEN_BLOCK, D_MODEL), lambda b, *_: (b, 0)),
            pl.BlockSpec((TOKEN_BLOCK, N_EXPERTS), lambda b, *_: (b, 0)),
            pl.BlockSpec((1, 1, 6 * D_MODEL), mod_map),
            pl.BlockSpec((1, D_MODEL), lambda b, *_: (0, 0)),
            pl.BlockSpec(memory_space=pltpu.VMEM),
        ],
        out_specs=pl.BlockSpec((TOKEN_BLOCK, D_MODEL), lambda b, *_: (b, 0)),
        scratch_shapes=[pltpu.VMEM((TOKEN_BLOCK, D_MODEL), F32)],
    )
    return pl.pallas_call(
        functools.partial(_combine_kernel, cap=cap),
        out_shape=jax.ShapeDtypeStruct((t, D_MODEL), F32),
        grid_spec=grid_spec,
        compiler_params=pltpu.CompilerParams(dimension_semantics=("arbitrary",),
                                             vmem_limit_bytes=VMEM_LIMIT),
        name="combine",
    )(starts, x1, slot_te, mod_rows, final_g.reshape(1, D_MODEL), y)


def kernel(x_prompt, x_sample, state_ret, c, c_ctx, norm1_g, norm2_g, final_g, w_mod, b_mod, w_in,
           w_fmix, w_out, w_router, w_gate, w_up, w_down):
    bp, seq, _ = x_prompt.shape
    bs, dec_seq, _ = x_sample.shape
    assert w_mod.shape[0] == 1, "single-layer trunk"
    tp, ts = bp * seq, bs * dec_seq

    cond = jnp.concatenate([c_ctx[None, :], c], axis=0)
    mod = _modulation(cond, w_mod[0], b_mod[0])
    mod_ctx = mod[0:1].reshape(1, 1, 6 * D_MODEL)
    mod_lat = mod[1:1 + bs].reshape(bs, 1, 6 * D_MODEL)

    w_in_bf = w_in[0].astype(BF16)
    w_out_bf = w_out[0].astype(BF16)
    mix_p, states = _mixer(x_prompt, mod_ctx, False, None, True, False, norm1_g[0], w_in_bf, w_fmix[0])
    (mix_s,) = _mixer(x_sample, mod_lat, True, state_ret[:, 0], False, True, norm1_g[0], w_in_bf,
                      w_fmix[0])
    blocks_per_seq = dec_seq // TOKEN_BLOCK
    x1p, h2p, affp = _post(x_prompt.reshape(tp, D_MODEL), mix_p.reshape(tp, D_MODEL), mod_ctx, None,
                           norm2_g[0], w_out_bf, w_router[0])
    x1s, h2s, affs = _post(x_sample.reshape(ts, D_MODEL), mix_s.reshape(ts, D_MODEL), mod_lat,
                           blocks_per_seq, norm2_g[0], w_out_bf, w_router[0])

    affp, affs = affp.T, affs.T
    slot_p, starts_p = _route(affp)
    slot_s, starts_s = _route(affs)

    yp, ys = _experts(starts_p, starts_s, h2p, h2s, slot_p, slot_s, affp, affs,
                      w_gate[0], w_up[0], w_down[0])

    out_p = _combine(starts_p, x1p, slot_p.T, mod_ctx, None, final_g, yp)
    out_s = _combine(starts_s, x1s, slot_s.T, mod_lat, blocks_per_seq, final_g, ys)

    y_prompt = out_p.reshape(bp, seq, D_MODEL)
    y_sample = out_s.reshape(bs, dec_seq, D_MODEL)
    state_new = states.reshape(bp, 1, 2, N_RET_HEADS, HEAD_DIM, HEAD_DIM).astype(x_prompt.dtype)
    return (y_prompt, y_sample, state_new)
```

```python
import functools
import math

import jax
import jax.numpy as jnp
import numpy as np
from jax import lax
from jax.experimental import pallas as pl
from jax.experimental.pallas import tpu as pltpu

D_MODEL = 1024
D_FOURIER = 512
N_FOURIER_GROUPS = 4
FOURIER_GROUP_W = 128
D_RET = 512
N_RET_HEADS = 4
HEAD_DIM = 128
CHUNK = 128
GRID_W = 64
N_EXPERTS = 16
EC_CAPACITY_FACTOR = 2
D_EXPERT_FF = 2816
ROPE_BASE = 10000.0
EPS = 1e-6
D_IN_PROJ = D_FOURIER + 5 * D_RET
LOG_GAMMA_FWD = np.log(1.0 - 2.0 ** (-5.0 - np.arange(N_RET_HEADS))).astype(np.float32)
LOG_GAMMA_BWD = np.log(1.0 - 2.0 ** (-5.5 - np.arange(N_RET_HEADS))).astype(np.float32)

TOKEN_BLOCK = 256
SLOT_WINDOW = 64
PACK = TOKEN_BLOCK // SLOT_WINDOW
GATHER_UNROLL = 4
FF_TILE = 256
N_FF_TILES = D_EXPERT_FF // FF_TILE
MOD_TILE = 512
MIXER_ROWS = 1024
POST_ROWS = 256
VMEM_LIMIT = 56 * 1024 * 1024

F32 = jnp.float32
BF16 = jnp.bfloat16


def _dot(a, b):
    return jnp.dot(a, b, preferred_element_type=F32)


def _dot_nt(a, b):
    return lax.dot_general(a, b, (((1,), (1,)), ((), ())), preferred_element_type=F32)


def _silu(x):
    return x * jax.nn.sigmoid(x)


def _mod_kernel(condt_ref, w_ref, b_ref, out_ref, *, n_cond):
    s = _silu(condt_ref[...])
    w = w_ref[...]
    out_ref[...] = jnp.zeros(out_ref.shape, F32)
    for r in range(n_cond):
        out_ref[r:r + 1, :] = jnp.sum(w * s[:, r:r + 1], axis=0, keepdims=True) + b_ref[...]


def _modulation(cond_rows, w_mod, b_mod):
    n_cond = cond_rows.shape[0]
    condt = jnp.zeros((D_MODEL, 8), F32).at[:, :n_cond].set(cond_rows.T)
    n_out = w_mod.shape[1]
    return pl.pallas_call(
        functools.partial(_mod_kernel, n_cond=n_cond),
        out_shape=jax.ShapeDtypeStruct((8, n_out), F32),
        grid=(n_out // MOD_TILE,),
        in_specs=[
            pl.BlockSpec((D_MODEL, 8), lambda j: (0, 0)),
            pl.BlockSpec((D_MODEL, MOD_TILE), lambda j: (0, j)),
            pl.BlockSpec((1, MOD_TILE), lambda j: (0, j)),
        ],
        out_specs=pl.BlockSpec((8, MOD_TILE), lambda j: (0, j)),
        compiler_params=pltpu.CompilerParams(dimension_semantics=("arbitrary",)),
        name="mod",
    )(condt, w_mod, b_mod.reshape(1, n_out))


def _rms(x):
    return x * lax.rsqrt(jnp.mean(x * x, axis=-1, keepdims=True) + EPS)


def _groupnorm(o):
    mu = jnp.mean(o, axis=-1, keepdims=True)
    c = o - mu
    return c * lax.rsqrt(jnp.mean(c * c, axis=-1, keepdims=True) + EPS)


def _split_hi_lo(x):
    hi = x.astype(BF16)
    lo = (x - hi.astype(F32)).astype(BF16)
    return hi, lo


def _mixer_kernel(*refs, n, use_rope, has_state_in, emit_state):
    it = iter(refs)
    x_ref, mod_ref, g1_ref, win_ref, wfmix_ref = (next(it) for _ in range(5))
    cw_ref, cn_ref, sn_ref, dmat_ref, qdec_ref, kdec_ref, sdec_ref = (next(it) for _ in range(7))
    cos_ref = sin_ref = s0_ref = st_ref = None
    if use_rope:
        cos_ref, sin_ref = next(it), next(it)
    if has_state_in:
        s0_ref = next(it)
    mix_ref = next(it)
    if emit_state:
        st_ref = next(it)
    p_ref, of_ref, ob_ref = next(it), next(it), next(it)

    n_seq = MIXER_ROWS // n
    chunks_per_seq = n // CHUNK
    mod = mod_ref[0]
    shift1 = mod[:, 0:D_MODEL]
    scale1 = mod[:, D_MODEL:2 * D_MODEL]

    h = (_rms(x_ref[...]) * g1_ref[...] * (1.0 + scale1) + shift1).astype(BF16)
    for j in range(D_IN_PROJ // 512):
        p_ref[:, j * 512:(j + 1) * 512] = _dot(h, win_ref[:, j * 512:(j + 1) * 512])

    xf = p_ref[:, 0:D_FOURIER].astype(BF16)
    xc, xs = [], []
    cw = cw_ref[...].astype(BF16)
    for g in range(N_FOURIER_GROUPS):
        t = _dot(xf[:, g * FOURIER_GROUP_W:(g + 1) * FOURIER_GROUP_W], cw)
        xc.append(t[:, :FOURIER_GROUP_W].astype(BF16))
        xs.append(t[:, FOURIER_GROUP_W:].astype(BF16))
    xc = jnp.concatenate(xc, axis=1)
    xs = jnp.concatenate(xs, axis=1)
    cn = cn_ref[...].astype(BF16)
    sn = sn_ref[...].astype(BF16)
    for s in range(n_seq):
        rs = slice(s * n, (s + 1) * n)
        fre = (_dot(cn, xc[rs]) - _dot(sn, xs[rs])) * (1.0 / math.sqrt(n * FOURIER_GROUP_W))
        fre = fre.astype(BF16)
        for g in range(N_FOURIER_GROUPS):
            sl = slice(g * FOURIER_GROUP_W, (g + 1) * FOURIER_GROUP_W)
            mix_ref[rs, sl] = _dot(fre[:, sl], wfmix_ref[g].astype(BF16)).astype(BF16)

    for hh in range(N_RET_HEADS):
        base = D_FOURIER + hh * HEAD_DIM
        q = p_ref[:, base:base + HEAD_DIM]
        k = p_ref[:, base + D_RET:base + D_RET + HEAD_DIM]
        v = p_ref[:, base + 2 * D_RET:base + 2 * D_RET + HEAD_DIM]
        if use_rope:
            lane = lax.broadcasted_iota(jnp.int32, (MIXER_ROWS, HEAD_DIM), 1)
            first = (lane % 64) < 32

            def rope(t):
                swapped = jnp.where(first, pltpu.roll(t, HEAD_DIM - 32, 1), pltpu.roll(t, 32, 1))
                return t * cos_ref[...] + swapped * sin_ref[...]

            q, k = rope(q), rope(k)
        k = k * (HEAD_DIM ** -0.5)
        qb, vb = q.astype(BF16), v.astype(BF16)
        kb = k.astype(BF16)

        def initial(s, direction):
            if has_state_in:
                return s0_ref[s, direction, hh]
            return jnp.zeros((HEAD_DIM, HEAD_DIM), F32)

        for s in range(n_seq):
            sf = initial(s, 0)
            for ci in range(chunks_per_seq):
                c = s * chunks_per_seq + ci
                rs = slice(c * CHUNK, (c + 1) * CHUNK)
                qc, kc, vc = qb[rs], kb[rs], vb[rs]
                qk = _dot_nt(qc, kc)
                o = _dot((qk * dmat_ref[0, hh]).astype(BF16), vc)
                if has_state_in or ci > 0:
                    o = o + qdec_ref[0, hh] * _dot(qc, sf.astype(BF16))
                of_ref[rs, :] = o
                ob_ref[rs, :] = _dot((qk * dmat_ref[1, hh]).astype(BF16), vc)
                kd = (k[rs] * kdec_ref[0, hh]).T.astype(BF16)
                sf = sf * sdec_ref[0, hh] + _dot(kd, vc)
            sb = initial(s, 1)
            for ci in reversed(range(chunks_per_seq)):
                c = s * chunks_per_seq + ci
                rs = slice(c * CHUNK, (c + 1) * CHUNK)
                qc, vc = qb[rs], vb[rs]
                if has_state_in or ci < chunks_per_seq - 1:
                    ob_ref[rs, :] = ob_ref[rs, :] + qdec_ref[1, hh] * _dot(qc, sb.astype(BF16))
                kd = (k[rs] * kdec_ref[1, hh]).T.astype(BF16)
                sb = sb * sdec_ref[1, hh] + _dot(kd, vc)
            if emit_state:
                st_ref[s, 0, hh] = sf
                st_ref[s, 1, hh] = sb

        gf = p_ref[:, base + 3 * D_RET:base + 3 * D_RET + HEAD_DIM]
        gb = p_ref[:, base + 4 * D_RET:base + 4 * D_RET + HEAD_DIM]
        y = _silu(gf) * _groupnorm(of_ref[...]) + _silu(gb) * _groupnorm(ob_ref[...])
        mix_ref[:, base:base + HEAD_DIM] = y.astype(BF16)


def _post_kernel(x_ref, mix_ref, mod_ref, g2_ref, wout_ref, wr_ref, x1_ref, h2_ref, aff_ref):
    mod = mod_ref[0]
    gate1 = mod[:, 2 * D_MODEL:3 * D_MODEL]
    shift2 = mod[:, 3 * D_MODEL:4 * D_MODEL]
    scale2 = mod[:, 4 * D_MODEL:5 * D_MODEL]
    x1 = x_ref[...] + gate1 * _dot(mix_ref[...], wout_ref[...])
    x1_ref[...] = x1
    h2 = _rms(x1) * g2_ref[...] * (1.0 + scale2) + shift2
    h2_hi, h2_lo = _split_hi_lo(h2)
    h2_ref[...] = h2_hi
    wr_hi, wr_lo = _split_hi_lo(wr_ref[...])
    by_hi = _dot(h2_hi, jnp.concatenate([wr_hi, wr_lo], axis=1))
    logits = by_hi[:, :N_EXPERTS] + (_dot(h2_lo, wr_hi) + by_hi[:, N_EXPERTS:])
    z = jnp.exp(logits - jnp.max(logits, axis=-1, keepdims=True))
    aff_ref[...] = z / jnp.sum(z, axis=-1, keepdims=True)


def _dft_consts(n):
    w = FOURIER_GROUP_W
    jw = np.arange(w)
    angw = 2.0 * np.pi * np.outer(jw, jw) / w
    cw = np.concatenate([np.cos(angw), np.sin(angw)], axis=1)
    jn = np.arange(n)
    angn = 2.0 * np.pi * (np.outer(jn, jn) % n) / n
    return (jnp.asarray(cw, F32), jnp.asarray(np.cos(angn), F32), jnp.asarray(np.sin(angn), F32))


def _retention_consts():
    i = np.arange(CHUNK, dtype=np.float64)
    diff = i[:, None] - i[None, :]
    dmat = np.zeros((2, N_RET_HEADS, CHUNK, CHUNK))
    qdec = np.zeros((2, N_RET_HEADS, CHUNK, HEAD_DIM))
    kdec = np.zeros((2, N_RET_HEADS, CHUNK, HEAD_DIM))
    sdec = np.zeros((2, N_RET_HEADS, CHUNK, HEAD_DIM))
    for hh in range(N_RET_HEADS):
        lf = float(LOG_GAMMA_FWD[hh])
        lb = float(LOG_GAMMA_BWD[hh])
        dmat[0, hh] = np.where(diff >= 0, np.exp(lf * np.maximum(diff, 0.0)), 0.0)
        dmat[1, hh] = np.where(diff <= 0, np.exp(lb * np.maximum(-diff, 0.0)), 0.0)
        qdec[0, hh] = np.exp(lf * (i + 1.0))[:, None]
        qdec[1, hh] = np.exp(lb * (CHUNK - i))[:, None]
        kdec[0, hh] = np.exp(lf * (CHUNK - 1.0 - i))[:, None]
        kdec[1, hh] = np.exp(lb * i)[:, None]
        sdec[0, hh] = math.exp(lf * CHUNK)
        sdec[1, hh] = math.exp(lb * CHUNK)
    return tuple(jnp.asarray(a, F32) for a in (dmat, qdec, kdec, sdec))


def _rope_consts(n):
    rows_n = n // GRID_W
    row = np.repeat(np.arange(rows_n, dtype=np.float64), GRID_W)
    col = np.tile(np.arange(GRID_W, dtype=np.float64), rows_n)
    n_pairs = HEAD_DIM // 4
    freqs = (np.float32(ROPE_BASE) ** (-np.arange(n_pairs, dtype=np.float32) / n_pairs)).astype(np.float64)
    ar = row[:, None] * freqs[None, :]
    ac = col[:, None] * freqs[None, :]
    cos = np.concatenate([np.cos(ar), np.cos(ar), np.cos(ac), np.cos(ac)], axis=1)
    sin = np.concatenate([-np.sin(ar), np.sin(ar), -np.sin(ac), np.sin(ac)], axis=1)
    return jnp.asarray(cos, F32), jnp.asarray(sin, F32)


def _const_spec(shape):
    nd = len(shape)
    return pl.BlockSpec(shape, lambda b, _nd=nd: (0,) * _nd, pipeline_mode=pl.Buffered(1))


def _mixer(x, mod_rows, mod_per_batch, state_in, emit_state, use_rope, g1, w_in_bf, w_fmix):
    nb, n, _ = x.shape
    assert MIXER_ROWS % n == 0 and (nb * n) % MIXER_ROWS == 0
    n_seq = MIXER_ROWS // n
    has_state_in = state_in is not None
    cw, cn, sn = _dft_consts(n)
    dmat, qdec, kdec, sdec = _retention_consts()
    consts = [cw, cn, sn, dmat, qdec, kdec, sdec]
    if use_rope:
        assert n_seq == 1
        consts += list(_rope_consts(n))
    weights = [g1.reshape(1, D_MODEL), w_in_bf, w_fmix]

    mod_map = (lambda b: (b // n_seq, 0, 0)) if mod_per_batch else (lambda b: (0, 0, 0))
    state_spec = pl.BlockSpec((n_seq, 2, N_RET_HEADS, HEAD_DIM, HEAD_DIM), lambda b: (b, 0, 0, 0, 0))
    row_spec = pl.BlockSpec((MIXER_ROWS, D_MODEL), lambda b: (b, 0))
    in_specs = [row_spec, pl.BlockSpec((1, 1, 6 * D_MODEL), mod_map)]
    in_specs += [_const_spec(a.shape) for a in weights + consts]
    args = [x.reshape(nb * n, D_MODEL), mod_rows] + weights + consts
    if has_state_in:
        in_specs.append(state_spec)
        args.append(state_in)

    out_shape = [jax.ShapeDtypeStruct((nb * n, D_MODEL), BF16)]
    out_specs = [row_spec]
    if emit_state:
        out_shape.append(jax.ShapeDtypeStruct((nb, 2, N_RET_HEADS, HEAD_DIM, HEAD_DIM), F32))
        out_specs.append(state_spec)

    return pl.pallas_call(
        functools.partial(_mixer_kernel, n=n, use_rope=use_rope, has_state_in=has_state_in,
                          emit_state=emit_state),
        out_shape=out_shape,
        grid=(nb * n // MIXER_ROWS,),
        in_specs=in_specs,
        out_specs=out_specs,
        scratch_shapes=[pltpu.VMEM((MIXER_ROWS, D_IN_PROJ), F32),
                        pltpu.VMEM((MIXER_ROWS, HEAD_DIM), F32), pltpu.VMEM((MIXER_ROWS, HEAD_DIM), F32)],
        compiler_params=pltpu.CompilerParams(dimension_semantics=("arbitrary",),
                                             vmem_limit_bytes=VMEM_LIMIT),
        name="mixer_rope" if use_rope else "mixer",
    )(*args)


def _post(x, mix, mod_rows, tokens_per_mod_row, g2, w_out_bf, w_router):
    t = x.shape[0]
    if tokens_per_mod_row is None:
        mod_map = lambda b: (0, 0, 0)
    else:
        assert tokens_per_mod_row % POST_ROWS == 0
        mod_map = lambda b: (b // (tokens_per_mod_row // POST_ROWS), 0, 0)
    row_spec = pl.BlockSpec((POST_ROWS, D_MODEL), lambda b: (b, 0))
    return pl.pallas_call(
        _post_kernel,
        out_shape=[jax.ShapeDtypeStruct((t, D_MODEL), F32),
                   jax.ShapeDtypeStruct((t, D_MODEL), BF16),
                   jax.ShapeDtypeStruct((t, N_EXPERTS), F32)],
        grid=(t // POST_ROWS,),
        in_specs=[row_spec, row_spec,
                  pl.BlockSpec((1, 1, 6 * D_MODEL), mod_map),
                  _const_spec((1, D_MODEL)), _const_spec((D_MODEL, D_MODEL)),
                  _const_spec((D_MODEL, N_EXPERTS))],
        out_specs=[row_spec, row_spec, pl.BlockSpec((POST_ROWS, N_EXPERTS), lambda b: (b, 0))],
        compiler_params=pltpu.CompilerParams(dimension_semantics=("arbitrary",),
                                             vmem_limit_bytes=VMEM_LIMIT),
        name="post",
    )(x, mix, mod_rows, g2.reshape(1, D_MODEL), w_out_bf, w_router)


def _route_kernel(aff_ref, u_ref, slot_ref, starts_ref, *, t, cap):
    aff = aff_ref[...]

    def count(mask):
        return jnp.sum(mask.astype(jnp.int32), axis=1, keepdims=True)

    def as_float(word):
        return lax.bitcast_convert_type(word, F32)

    def value_step(i, cur):
        cand = cur | jnp.left_shift(jnp.int32(1), 30 - i)
        return jnp.where(count(aff >= as_float(cand)) >= cap, cand, cur)

    thr = lax.fori_loop(0, 31, value_step, jnp.zeros((N_EXPERTS, 1), jnp.int32))
    gt = aff >= as_float(thr + 1)
    eq = (aff >= as_float(thr)) & jnp.logical_not(gt)
    need = cap - count(gt)
    tok = lax.broadcasted_iota(jnp.int32, (N_EXPERTS, t), 1)
    nbits = t.bit_length() - 1

    def index_step(i, cur):
        cand = cur | jnp.left_shift(jnp.int32(1), nbits - 1 - i)
        return jnp.where(count(eq & (tok < cand)) < need, cand, cur)

    last = lax.fori_loop(0, nbits, index_step, jnp.zeros((N_EXPERTS, 1), jnp.int32))
    self = jnp.where(gt | (eq & (tok <= last)), 1.0, 0.0).astype(F32)

    carry = jnp.zeros((N_EXPERTS, 1), F32)
    for b in range(t // TOKEN_BLOCK):
        sl = slice(b * TOKEN_BLOCK, (b + 1) * TOKEN_BLOCK)
        sbf = self[:, sl]
        pre = _dot(sbf.astype(BF16), u_ref[...]) + carry
        slot_ref[:, sl] = jnp.where(sbf > 0.5, pre.astype(jnp.int32), -1)
        starts_ref[b] = jnp.broadcast_to(carry, (N_EXPERTS, 128)).astype(jnp.int32)
        carry = carry + jnp.sum(sbf, axis=1, keepdims=True)
    starts_ref[t // TOKEN_BLOCK] = jnp.broadcast_to(carry, (N_EXPERTS, 128)).astype(jnp.int32)


def _route(aff_et):
    t = aff_et.shape[1]
    cap = EC_CAPACITY_FACTOR * t // N_EXPERTS
    nblk = t // TOKEN_BLOCK
    upper = jnp.asarray(np.triu(np.ones((TOKEN_BLOCK, TOKEN_BLOCK)), 1), BF16)
    slot_et, starts = pl.pallas_call(
        functools.partial(_route_kernel, t=t, cap=cap),
        out_shape=[jax.ShapeDtypeStruct((N_EXPERTS, t), jnp.int32),
                   jax.ShapeDtypeStruct((nblk + 1, N_EXPERTS, 128), jnp.int32)],
        grid=(1,),
        in_specs=[pl.BlockSpec((N_EXPERTS, t), lambda i: (0, 0)),
                  pl.BlockSpec((TOKEN_BLOCK, TOKEN_BLOCK), lambda i: (0, 0))],
        out_specs=[pl.BlockSpec((N_EXPERTS, t), lambda i: (0, 0)),
                   pl.BlockSpec((nblk + 1, N_EXPERTS, 128), lambda i: (0, 0, 0))],
        compiler_params=pltpu.CompilerParams(dimension_semantics=("arbitrary",)),
        name="route",
    )(aff_et, upper)
    return slot_et, starts[:, :, 0]


def _pack_windows(starts_ref, b, experts, cap):
    first = [jnp.minimum((starts_ref[b, e] // 16) * 16, cap - SLOT_WINDOW) for e in experts]
    rows = [jnp.where(starts_ref[b + 1, e] > starts_ref[b, e], starts_ref[b + 1, e] - w, 0)
            for e, w in zip(experts, first)]
    return first, pl.cdiv(functools.reduce(jnp.maximum, rows), SLOT_WINDOW)


def _gather_group(g, starts_ref, slot_ref, gate_ref, h2_ref, xs_ref, gs_ref, row0, t, cap):
    sub = lax.broadcasted_iota(jnp.int32, (SLOT_WINDOW, TOKEN_BLOCK), 0)
    experts = [g * PACK + j for j in range(PACK)]
    assert (t // TOKEN_BLOCK) % GATHER_UNROLL == 0

    def window(b, first, i):
        hb = pl.ds(pl.multiple_of(b * TOKEN_BLOCK, TOKEN_BLOCK), TOKEN_BLOCK)
        hits, dst = [], []
        for j in range(PACK):
            lo = first[j] + i * SLOT_WINDOW
            w = jnp.minimum(lo, cap - SLOT_WINDOW)
            srow = slot_ref[j, pl.ds(b, 1), :]
            hits.append((srow == w + sub) & (srow >= lo))
            dst.append(pl.ds(pl.multiple_of(row0 + w, 16), SLOT_WINDOW))
        onehot = jnp.concatenate([jnp.where(h, 1.0, 0.0) for h in hits], axis=0).astype(BF16)
        got = _dot(onehot, h2_ref[hb, :])
        for j in range(PACK):
            piece = got[j * SLOT_WINDOW:(j + 1) * SLOT_WINDOW]
            xs_ref[j, dst[j], :] = (xs_ref[j, dst[j], :].astype(F32) + piece).astype(BF16)
            grow = gate_ref[j, pl.ds(b, 1), :]
            gs_ref[j, dst[j], :] += jnp.sum(jnp.where(hits[j], grow, 0.0), axis=1, keepdims=True)

    def blocks(q, carry):
        pending = []
        for u in range(GATHER_UNROLL):
            b = q * GATHER_UNROLL + u
            first, n_windows = _pack_windows(starts_ref, b, experts, cap)
            window(b, first, 0)
            pending.append((b, first, n_windows))
        for b, first, n_windows in pending:
            def more(i, carry, b=b, first=first):
                window(b, first, i)
                return carry

            lax.fori_loop(1, n_windows, more, 0)
        return carry

    lax.fori_loop(0, t // TOKEN_BLOCK // GATHER_UNROLL, blocks, 0)


def _experts_kernel(sp_ref, ss_ref, h2p_ref, h2s_ref, slotp_ref, slots_ref, gatep_ref, gates_ref,
                    wg_ref, wu_ref, wd_ref, yp_ref, ys_ref, xs_ref, gs_ref, acc_ref,
                    *, tp, ts, capp, caps):
    g = pl.program_id(0)
    step = pl.program_id(1)
    j = step // N_FF_TILES
    f = step % N_FF_TILES

    @pl.when(step == 0)
    def _():
        xs_ref[...] = jnp.zeros(xs_ref.shape, BF16)
        gs_ref[...] = jnp.zeros(gs_ref.shape, F32)
        _gather_group(g, sp_ref, slotp_ref, gatep_ref, h2p_ref, xs_ref, gs_ref, 0, tp, capp)
        _gather_group(g, ss_ref, slots_ref, gates_ref, h2s_ref, xs_ref, gs_ref, capp, ts, caps)

    @pl.when(f == 0)
    def _():
        acc_ref[...] = jnp.zeros(acc_ref.shape, F32)

    x = xs_ref[j]
    a = _dot(x, wg_ref[0].astype(BF16))
    u = _dot(x, wu_ref[0].astype(BF16))
    acc_ref[...] += _dot((_silu(a) * u).astype(BF16), wd_ref[0].astype(BF16))

    @pl.when(f == N_FF_TILES - 1)
    def _():
        yp_ref[0] = (acc_ref[0:capp, :] * gs_ref[j, 0:capp, :]).astype(BF16)
        ys_ref[0] = (acc_ref[capp:capp + caps, :] * gs_ref[j, capp:capp + caps, :]).astype(BF16)


def _experts(starts_p, starts_s, h2p, h2s, slot_p, slot_s, gate_p, gate_s, w_gate, w_up, w_down):
    tp, ts = h2p.shape[0], h2s.shape[0]
    capp = EC_CAPACITY_FACTOR * tp // N_EXPERTS
    caps = EC_CAPACITY_FACTOR * ts // N_EXPERTS
    rows = capp + caps
    nbp, nbs = tp // TOKEN_BLOCK, ts // TOKEN_BLOCK
    expert = lambda g, s: g * PACK + s // N_FF_TILES
    grid_spec = pltpu.PrefetchScalarGridSpec(
        num_scalar_prefetch=2,
        grid=(N_EXPERTS // PACK, PACK * N_FF_TILES),
        in_specs=[
            pl.BlockSpec(memory_space=pltpu.VMEM),
            pl.BlockSpec(memory_space=pltpu.VMEM),
            pl.BlockSpec((PACK, nbp, TOKEN_BLOCK), lambda g, s, *_: (g, 0, 0)),
            pl.BlockSpec((PACK, nbs, TOKEN_BLOCK), lambda g, s, *_: (g, 0, 0)),
            pl.BlockSpec((PACK, nbp, TOKEN_BLOCK), lambda g, s, *_: (g, 0, 0)),
            pl.BlockSpec((PACK, nbs, TOKEN_BLOCK), lambda g, s, *_: (g, 0, 0)),
            pl.BlockSpec((1, D_MODEL, FF_TILE), lambda g, s, *_: (expert(g, s), 0, s % N_FF_TILES)),
            pl.BlockSpec((1, D_MODEL, FF_TILE), lambda g, s, *_: (expert(g, s), 0, s % N_FF_TILES)),
            pl.BlockSpec((1, FF_TILE, D_MODEL), lambda g, s, *_: (expert(g, s), s % N_FF_TILES, 0)),
        ],
        out_specs=[
            pl.BlockSpec((1, capp, D_MODEL), lambda g, s, *_: (expert(g, s), 0, 0)),
            pl.BlockSpec((1, caps, D_MODEL), lambda g, s, *_: (expert(g, s), 0, 0)),
        ],
        scratch_shapes=[pltpu.VMEM((PACK, rows, D_MODEL), BF16), pltpu.VMEM((PACK, rows, 1), F32),
                        pltpu.VMEM((rows, D_MODEL), F32)],
    )
    return pl.pallas_call(
        functools.partial(_experts_kernel, tp=tp, ts=ts, capp=capp, caps=caps),
        out_shape=[jax.ShapeDtypeStruct((N_EXPERTS, capp, D_MODEL), BF16),
                   jax.ShapeDtypeStruct((N_EXPERTS, caps, D_MODEL), BF16)],
        grid_spec=grid_spec,
        compiler_params=pltpu.CompilerParams(dimension_semantics=("arbitrary", "arbitrary"),
                                             vmem_limit_bytes=VMEM_LIMIT),
        name="experts",
    )(starts_p, starts_s, h2p, h2s,
      slot_p.reshape(N_EXPERTS, nbp, TOKEN_BLOCK), slot_s.reshape(N_EXPERTS, nbs, TOKEN_BLOCK),
      gate_p.reshape(N_EXPERTS, nbp, TOKEN_BLOCK), gate_s.reshape(N_EXPERTS, nbs, TOKEN_BLOCK),
      w_gate, w_up, w_down)


def _combine_kernel(st_ref, x1_ref, slot_ref, mod_ref, fg_ref, y_ref, out_ref, acc_ref, *, cap):
    b = pl.program_id(0)
    slot = slot_ref[...]
    lane = lax.broadcasted_iota(jnp.int32, (TOKEN_BLOCK, PACK * SLOT_WINDOW), 1)

    def scatter(experts, first, i):
        target = None
        windows = []
        for j, e in enumerate(experts):
            lo = first[j] + i * SLOT_WINDOW
            w = jnp.minimum(lo, cap - SLOT_WINDOW)
            sc = slot[:, e:e + 1]
            col = jnp.where(sc >= lo, sc - w + j * SLOT_WINDOW, -1)
            target = col if target is None else jnp.where(lane < j * SLOT_WINDOW, target, col)
            windows.append(y_ref[e, pl.ds(pl.multiple_of(w, 16), SLOT_WINDOW), :])
        onehot = jnp.where(target == lane, 1.0, 0.0).astype(BF16)
        return _dot(onehot, jnp.concatenate(windows, axis=0))

    groups = []
    total = None
    for g in range(N_EXPERTS // PACK):
        experts = list(range(g * PACK, (g + 1) * PACK))
        first, n_windows = _pack_windows(st_ref, b, experts, cap)
        groups.append((experts, first, n_windows))
        part = scatter(experts, first, 0)
        total = part if total is None else total + part
    acc_ref[...] = total

    for experts, first, n_windows in groups:
        def more(i, carry, experts=experts, first=first):
            acc_ref[...] += scatter(experts, first, i)
            return carry

        lax.fori_loop(1, n_windows, more, 0)

    gate2 = mod_ref[0][:, 5 * D_MODEL:6 * D_MODEL]
    x2 = x1_ref[...] + gate2 * acc_ref[...]
    out_ref[...] = _rms(x2) * fg_ref[...]


def _combine(starts, x1, slot_te, mod_rows, blocks_per_mod_row, final_g, y):
    t = x1.shape[0]
    cap = y.shape[1]
    nblk = t // TOKEN_BLOCK
    if blocks_per_mod_row is None:
        mod_map = lambda b, *_: (0, 0, 0)
    else:
        mod_map = lambda b, *_: (b // blocks_per_mod_row, 0, 0)
    grid_spec = pltpu.PrefetchScalarGridSpec(
        num_scalar_prefetch=1,
        grid=(nblk,),
        in_specs=[
            pl.BlockSpec((TOKEN_BLOCK, D_MODEL), lambda b, *_: (b, 0)),
            pl.BlockSpec((TOKEN_BLOCK, N_EXPERTS), lambda b, *_: (b, 0)),
            pl.BlockSpec((1, 1, 6 * D_MODEL), mod_map),
            pl.BlockSpec((1, D_MODEL), lambda b, *_: (0, 0)),
            pl.BlockSpec(memory_space=pltpu.VMEM),
        ],
        out_specs=pl.BlockSpec((TOKEN_BLOCK, D_MODEL), lambda b, *_: (b, 0)),
        scratch_shapes=[pltpu.VMEM((TOKEN_BLOCK, D_MODEL), F32)],
    )
    return pl.pallas_call(
        functools.partial(_combine_kernel, cap=cap),
        out_shape=jax.ShapeDtypeStruct((t, D_MODEL), F32),
        grid_spec=grid_spec,
        compiler_params=pltpu.CompilerParams(dimension_semantics=("arbitrary",),
                                             vmem_limit_bytes=VMEM_LIMIT),
        name="combine",
    )(starts, x1, slot_te, mod_rows, final_g.reshape(1, D_MODEL), y)


def kernel(x_prompt, x_sample, state_ret, c, c_ctx, norm1_g, norm2_g, final_g, w_mod, b_mod, w_in,
           w_fmix, w_out, w_router, w_gate, w_up, w_down):
    bp, seq, _ = x_prompt.shape
    bs, dec_seq, _ = x_sample.shape
    assert w_mod.shape[0] == 1, "single-layer trunk"
    tp, ts = bp * seq, bs * dec_seq

    cond = jnp.concatenate([c_ctx[None, :], c], axis=0)
    mod = _modulation(cond, w_mod[0], b_mod[0])
    mod_ctx = mod[0:1].reshape(1, 1, 6 * D_MODEL)
    mod_lat = mod[1:1 + bs].reshape(bs, 1, 6 * D_MODEL)

    w_in_bf = w_in[0].astype(BF16)
    w_out_bf = w_out[0].astype(BF16)
    mix_p, states = _mixer(x_prompt, mod_ctx, False, None, True, False, norm1_g[0], w_in_bf, w_fmix[0])
    (mix_s,) = _mixer(x_sample, mod_lat, True, state_ret[:, 0], False, True, norm1_g[0], w_in_bf,
                      w_fmix[0])
    blocks_per_seq = dec_seq // TOKEN_BLOCK
    x1p, h2p, affp = _post(x_prompt.reshape(tp, D_MODEL), mix_p, mod_ctx, None,
                           norm2_g[0], w_out_bf, w_router[0])
    x1s, h2s, affs = _post(x_sample.reshape(ts, D_MODEL), mix_s, mod_lat, dec_seq,
                           norm2_g[0], w_out_bf, w_router[0])

    affp, affs = affp.T, affs.T
    slot_p, starts_p = _route(affp)
    slot_s, starts_s = _route(affs)

    yp, ys = _experts(starts_p, starts_s, h2p, h2s, slot_p, slot_s, affp, affs,
                      w_gate[0], w_up[0], w_down[0])

    out_p = _combine(starts_p, x1p, slot_p.T, mod_ctx, None, final_g, yp)
    out_s = _combine(starts_s, x1s, slot_s.T, mod_lat, blocks_per_seq, final_g, ys)

    y_prompt = out_p.reshape(bp, seq, D_MODEL)
    y_sample = out_s.reshape(bs, dec_seq, D_MODEL)
    state_new = states.reshape(bp, 1, 2, N_RET_HEADS, HEAD_DIM, HEAD_DIM).astype(x_prompt.dtype)
    return (y_prompt, y_sample, state_new)
```

```python
import functools
import math

import jax
import jax.numpy as jnp
import numpy as np
from jax import lax
from jax.experimental import pallas as pl
from jax.experimental.pallas import tpu as pltpu

D_MODEL = 1024
D_FOURIER = 512
N_FOURIER_GROUPS = 4
FOURIER_GROUP_W = 128
D_RET = 512
N_RET_HEADS = 4
HEAD_DIM = 128
CHUNK = 256
GRID_W = 64
N_EXPERTS = 16
EC_CAPACITY_FACTOR = 2
D_EXPERT_FF = 2816
ROPE_BASE = 10000.0
EPS = 1e-6
D_IN_PROJ = D_FOURIER + 5 * D_RET
LOG_GAMMA_FWD = np.log(1.0 - 2.0 ** (-5.0 - np.arange(N_RET_HEADS))).astype(np.float32)
LOG_GAMMA_BWD = np.log(1.0 - 2.0 ** (-5.5 - np.arange(N_RET_HEADS))).astype(np.float32)

TOKEN_BLOCK = 256
SLOT_WINDOW = 64
PACK = TOKEN_BLOCK // SLOT_WINDOW
GATHER_UNROLL = 4
FF_TILE = 256
N_FF_TILES = D_EXPERT_FF // FF_TILE
MOD_TILE = 512
MIXER_ROWS = 1024
POST_ROWS = 256
VMEM_LIMIT = 56 * 1024 * 1024

F32 = jnp.float32
BF16 = jnp.bfloat16


def _dot(a, b):
    return jnp.dot(a, b, preferred_element_type=F32)


def _dot_nt(a, b):
    return lax.dot_general(a, b, (((1,), (1,)), ((), ())), preferred_element_type=F32)


def _silu(x):
    return x * jax.nn.sigmoid(x)


def _mod_kernel(condt_ref, w_ref, b_ref, out_ref, *, n_cond):
    s = _silu(condt_ref[...])
    w = w_ref[...]
    out_ref[...] = jnp.zeros(out_ref.shape, F32)
    for r in range(n_cond):
        out_ref[r:r + 1, :] = jnp.sum(w * s[:, r:r + 1], axis=0, keepdims=True) + b_ref[...]


def _modulation(cond_rows, w_mod, b_mod):
    n_cond = cond_rows.shape[0]
    condt = jnp.zeros((D_MODEL, 8), F32).at[:, :n_cond].set(cond_rows.T)
    n_out = w_mod.shape[1]
    return pl.pallas_call(
        functools.partial(_mod_kernel, n_cond=n_cond),
        out_shape=jax.ShapeDtypeStruct((8, n_out), F32),
        grid=(n_out // MOD_TILE,),
        in_specs=[
            pl.BlockSpec((D_MODEL, 8), lambda j: (0, 0)),
            pl.BlockSpec((D_MODEL, MOD_TILE), lambda j: (0, j)),
            pl.BlockSpec((1, MOD_TILE), lambda j: (0, j)),
        ],
        out_specs=pl.BlockSpec((8, MOD_TILE), lambda j: (0, j)),
        compiler_params=pltpu.CompilerParams(dimension_semantics=("arbitrary",)),
        name="mod",
    )(condt, w_mod, b_mod.reshape(1, n_out))


def _rms(x):
    return x * lax.rsqrt(jnp.mean(x * x, axis=-1, keepdims=True) + EPS)


def _groupnorm(o):
    mu = jnp.mean(o, axis=-1, keepdims=True)
    c = o - mu
    return c * lax.rsqrt(jnp.mean(c * c, axis=-1, keepdims=True) + EPS)


def _split_hi_lo(x):
    hi = x.astype(BF16)
    lo = (x - hi.astype(F32)).astype(BF16)
    return hi, lo


def _mixer_kernel(*refs, n, use_rope, has_state_in, emit_state):
    it = iter(refs)
    x_ref, mod_ref, g1_ref, win_ref, wfmix_ref = (next(it) for _ in range(5))
    cw_ref, cn_ref, sn_ref, dmat_ref, qdec_ref, kdec_ref, sdec_ref = (next(it) for _ in range(7))
    cos_ref = sin_ref = s0_ref = st_ref = None
    if use_rope:
        cos_ref, sin_ref = next(it), next(it)
    if has_state_in:
        s0_ref = next(it)
    mix_ref = next(it)
    if emit_state:
        st_ref = next(it)
    p_ref, of_ref, ob_ref = next(it), next(it), next(it)

    n_seq = MIXER_ROWS // n
    chunks_per_seq = n // CHUNK
    mod = mod_ref[0]
    shift1 = mod[:, 0:D_MODEL]
    scale1 = mod[:, D_MODEL:2 * D_MODEL]

    h = (_rms(x_ref[...]) * g1_ref[...] * (1.0 + scale1) + shift1).astype(BF16)
    for j in range(D_IN_PROJ // 512):
        p_ref[:, j * 512:(j + 1) * 512] = _dot(h, win_ref[:, j * 512:(j + 1) * 512])

    xf = p_ref[:, 0:D_FOURIER].astype(BF16)
    xc, xs = [], []
    cw = cw_ref[...].astype(BF16)
    for g in range(N_FOURIER_GROUPS):
        t = _dot(xf[:, g * FOURIER_GROUP_W:(g + 1) * FOURIER_GROUP_W], cw)
        xc.append(t[:, :FOURIER_GROUP_W].astype(BF16))
        xs.append(t[:, FOURIER_GROUP_W:].astype(BF16))
    xc = jnp.concatenate(xc, axis=1)
    xs = jnp.concatenate(xs, axis=1)
    cn = cn_ref[...].astype(BF16)
    sn = sn_ref[...].astype(BF16)
    for s in range(n_seq):
        rs = slice(s * n, (s + 1) * n)
        fre = (_dot(cn, xc[rs]) - _dot(sn, xs[rs])) * (1.0 / math.sqrt(n * FOURIER_GROUP_W))
        fre = fre.astype(BF16)
        for g in range(N_FOURIER_GROUPS):
            sl = slice(g * FOURIER_GROUP_W, (g + 1) * FOURIER_GROUP_W)
            mix_ref[rs, sl] = _dot(fre[:, sl], wfmix_ref[g].astype(BF16)).astype(BF16)

    for hh in range(N_RET_HEADS):
        base = D_FOURIER + hh * HEAD_DIM
        q = p_ref[:, base:base + HEAD_DIM]
        k = p_ref[:, base + D_RET:base + D_RET + HEAD_DIM]
        v = p_ref[:, base + 2 * D_RET:base + 2 * D_RET + HEAD_DIM]
        if use_rope:
            lane = lax.broadcasted_iota(jnp.int32, (MIXER_ROWS, HEAD_DIM), 1)
            first = (lane % 64) < 32

            def rope(t):
                swapped = jnp.where(first, pltpu.roll(t, HEAD_DIM - 32, 1), pltpu.roll(t, 32, 1))
                return t * cos_ref[...] + swapped * sin_ref[...]

            q, k = rope(q), rope(k)
        k = k * (HEAD_DIM ** -0.5)
        qb, vb = q.astype(BF16), v.astype(BF16)
        kb = k.astype(BF16)

        def initial(s, direction):
            if has_state_in:
                return s0_ref[s, direction, hh]
            return jnp.zeros((HEAD_DIM, HEAD_DIM), F32)

        for s in range(n_seq):
            parts = []
            for ci in range(chunks_per_seq):
                c = s * chunks_per_seq + ci
                rs = slice(c * CHUNK, (c + 1) * CHUNK)
                qc, kc, vc = qb[rs], kb[rs], vb[rs]
                qk = _dot_nt(qc, kc)
                lhs = jnp.concatenate([(qk * dmat_ref[0, hh]).astype(BF16),
                                       (qk * dmat_ref[1, hh]).astype(BF16),
                                       (k[rs] * kdec_ref[0, hh]).T.astype(BF16),
                                       (k[rs] * kdec_ref[1, hh]).T.astype(BF16)], axis=0)
                parts.append((rs, qc, _dot(lhs, vc)))
            sf = initial(s, 0)
            for ci in range(chunks_per_seq):
                rs, qc, r = parts[ci]
                o = r[0:CHUNK]
                if has_state_in or ci > 0:
                    o = o + qdec_ref[0, hh] * _dot(qc, sf.astype(BF16))
                of_ref[rs, :] = o
                sf = sf * sdec_ref[0, hh] + r[2 * CHUNK:2 * CHUNK + HEAD_DIM]
            sb = initial(s, 1)
            for ci in reversed(range(chunks_per_seq)):
                rs, qc, r = parts[ci]
                o = r[CHUNK:2 * CHUNK]
                if has_state_in or ci < chunks_per_seq - 1:
                    o = o + qdec_ref[1, hh] * _dot(qc, sb.astype(BF16))
                ob_ref[rs, :] = o
                sb = sb * sdec_ref[1, hh] + r[2 * CHUNK + HEAD_DIM:]
            if emit_state:
                st_ref[s, 0, hh] = sf
                st_ref[s, 1, hh] = sb

        gf = p_ref[:, base + 3 * D_RET:base + 3 * D_RET + HEAD_DIM]
        gb = p_ref[:, base + 4 * D_RET:base + 4 * D_RET + HEAD_DIM]
        y = _silu(gf) * _groupnorm(of_ref[...]) + _silu(gb) * _groupnorm(ob_ref[...])
        mix_ref[:, base:base + HEAD_DIM] = y.astype(BF16)


def _post_kernel(x_ref, mix_ref, mod_ref, g2_ref, wout_ref, wr_ref, x1_ref, h2_ref, aff_ref):
    mod = mod_ref[0]
    gate1 = mod[:, 2 * D_MODEL:3 * D_MODEL]
    shift2 = mod[:, 3 * D_MODEL:4 * D_MODEL]
    scale2 = mod[:, 4 * D_MODEL:5 * D_MODEL]
    x1 = x_ref[...] + gate1 * _dot(mix_ref[...], wout_ref[...])
    x1_ref[...] = x1
    h2 = _rms(x1) * g2_ref[...] * (1.0 + scale2) + shift2
    h2_hi, h2_lo = _split_hi_lo(h2)
    h2_ref[...] = h2_hi
    wr_hi, wr_lo = _split_hi_lo(wr_ref[...])
    by_hi = _dot(h2_hi, jnp.concatenate([wr_hi, wr_lo], axis=1))
    logits = by_hi[:, :N_EXPERTS] + (_dot(h2_lo, wr_hi) + by_hi[:, N_EXPERTS:])
    z = jnp.exp(logits - jnp.max(logits, axis=-1, keepdims=True))
    aff_ref[...] = z / jnp.sum(z, axis=-1, keepdims=True)


def _dft_consts(n):
    w = FOURIER_GROUP_W
    jw = np.arange(w)
    angw = 2.0 * np.pi * np.outer(jw, jw) / w
    cw = np.concatenate([np.cos(angw), np.sin(angw)], axis=1)
    jn = np.arange(n)
    angn = 2.0 * np.pi * (np.outer(jn, jn) % n) / n
    return (jnp.asarray(cw, F32), jnp.asarray(np.cos(angn), F32), jnp.asarray(np.sin(angn), F32))


def _retention_consts():
    i = np.arange(CHUNK, dtype=np.float64)
    diff = i[:, None] - i[None, :]
    dmat = np.zeros((2, N_RET_HEADS, CHUNK, CHUNK))
    qdec = np.zeros((2, N_RET_HEADS, CHUNK, HEAD_DIM))
    kdec = np.zeros((2, N_RET_HEADS, CHUNK, HEAD_DIM))
    sdec = np.zeros((2, N_RET_HEADS, HEAD_DIM, HEAD_DIM))
    for hh in range(N_RET_HEADS):
        lf = float(LOG_GAMMA_FWD[hh])
        lb = float(LOG_GAMMA_BWD[hh])
        dmat[0, hh] = np.where(diff >= 0, np.exp(lf * np.maximum(diff, 0.0)), 0.0)
        dmat[1, hh] = np.where(diff <= 0, np.exp(lb * np.maximum(-diff, 0.0)), 0.0)
        qdec[0, hh] = np.exp(lf * (i + 1.0))[:, None]
        qdec[1, hh] = np.exp(lb * (CHUNK - i))[:, None]
        kdec[0, hh] = np.exp(lf * (CHUNK - 1.0 - i))[:, None]
        kdec[1, hh] = np.exp(lb * i)[:, None]
        sdec[0, hh] = math.exp(lf * CHUNK)
        sdec[1, hh] = math.exp(lb * CHUNK)
    return tuple(jnp.asarray(a, F32) for a in (dmat, qdec, kdec, sdec))


def _rope_consts(n):
    rows_n = n // GRID_W
    row = np.repeat(np.arange(rows_n, dtype=np.float64), GRID_W)
    col = np.tile(np.arange(GRID_W, dtype=np.float64), rows_n)
    n_pairs = HEAD_DIM // 4
    freqs = (np.float32(ROPE_BASE) ** (-np.arange(n_pairs, dtype=np.float32) / n_pairs)).astype(np.float64)
    ar = row[:, None] * freqs[None, :]
    ac = col[:, None] * freqs[None, :]
    cos = np.concatenate([np.cos(ar), np.cos(ar), np.cos(ac), np.cos(ac)], axis=1)
    sin = np.concatenate([-np.sin(ar), np.sin(ar), -np.sin(ac), np.sin(ac)], axis=1)
    return jnp.asarray(cos, F32), jnp.asarray(sin, F32)


def _const_spec(shape):
    nd = len(shape)
    return pl.BlockSpec(shape, lambda b, _nd=nd: (0,) * _nd, pipeline_mode=pl.Buffered(1))


def _mixer(x, mod_rows, mod_per_batch, state_in, emit_state, use_rope, g1, w_in_bf, w_fmix):
    nb, n, _ = x.shape
    assert MIXER_ROWS % n == 0 and (nb * n) % MIXER_ROWS == 0
    n_seq = MIXER_ROWS // n
    has_state_in = state_in is not None
    cw, cn, sn = _dft_consts(n)
    dmat, qdec, kdec, sdec = _retention_consts()
    consts = [cw, cn, sn, dmat, qdec, kdec, sdec]
    if use_rope:
        assert n_seq == 1
        consts += list(_rope_consts(n))
    weights = [g1.reshape(1, D_MODEL), w_in_bf, w_fmix]

    mod_map = (lambda b: (b // n_seq, 0, 0)) if mod_per_batch else (lambda b: (0, 0, 0))
    state_spec = pl.BlockSpec((n_seq, 2, N_RET_HEADS, HEAD_DIM, HEAD_DIM), lambda b: (b, 0, 0, 0, 0))
    row_spec = pl.BlockSpec((MIXER_ROWS, D_MODEL), lambda b: (b, 0))
    in_specs = [row_spec, pl.BlockSpec((1, 1, 6 * D_MODEL), mod_map)]
    in_specs += [_const_spec(a.shape) for a in weights + consts]
    args = [x.reshape(nb * n, D_MODEL), mod_rows] + weights + consts
    if has_state_in:
        in_specs.append(state_spec)
        args.append(state_in)

    out_shape = [jax.ShapeDtypeStruct((nb * n, D_MODEL), BF16)]
    out_specs = [row_spec]
    if emit_state:
        out_shape.append(jax.ShapeDtypeStruct((nb, 2, N_RET_HEADS, HEAD_DIM, HEAD_DIM), F32))
        out_specs.append(state_spec)

    return pl.pallas_call(
        functools.partial(_mixer_kernel, n=n, use_rope=use_rope, has_state_in=has_state_in,
                          emit_state=emit_state),
        out_shape=out_shape,
        grid=(nb * n // MIXER_ROWS,),
        in_specs=in_specs,
        out_specs=out_specs,
        scratch_shapes=[pltpu.VMEM((MIXER_ROWS, D_IN_PROJ), F32),
                        pltpu.VMEM((MIXER_ROWS, HEAD_DIM), F32), pltpu.VMEM((MIXER_ROWS, HEAD_DIM), F32)],
        compiler_params=pltpu.CompilerParams(dimension_semantics=("arbitrary",),
                                             vmem_limit_bytes=VMEM_LIMIT),
        name="mixer_rope" if use_rope else "mixer",
    )(*args)


def _post(x, mix, mod_rows, tokens_per_mod_row, g2, w_out_bf, w_router):
    t = x.shape[0]
    if tokens_per_mod_row is None:
        mod_map = lambda b: (0, 0, 0)
    else:
        assert tokens_per_mod_row % POST_ROWS == 0
        mod_map = lambda b: (b // (tokens_per_mod_row // POST_ROWS), 0, 0)
    row_spec = pl.BlockSpec((POST_ROWS, D_MODEL), lambda b: (b, 0))
    return pl.pallas_call(
        _post_kernel,
        out_shape=[jax.ShapeDtypeStruct((t, D_MODEL), F32),
                   jax.ShapeDtypeStruct((t, D_MODEL), BF16),
                   jax.ShapeDtypeStruct((t, N_EXPERTS), F32)],
        grid=(t // POST_ROWS,),
        in_specs=[row_spec, row_spec,
                  pl.BlockSpec((1, 1, 6 * D_MODEL), mod_map),
                  _const_spec((1, D_MODEL)), _const_spec((D_MODEL, D_MODEL)),
                  _const_spec((D_MODEL, N_EXPERTS))],
        out_specs=[row_spec, row_spec, pl.BlockSpec((POST_ROWS, N_EXPERTS), lambda b: (b, 0))],
        compiler_params=pltpu.CompilerParams(dimension_semantics=("arbitrary",),
                                             vmem_limit_bytes=VMEM_LIMIT),
        name="post",
    )(x, mix, mod_rows, g2.reshape(1, D_MODEL), w_out_bf, w_router)


def _route_kernel(aff_ref, u_ref, slot_ref, starts_ref, *, t, cap):
    aff = aff_ref[...]

    def count(mask):
        return jnp.sum(mask.astype(jnp.int32), axis=1, keepdims=True)

    def as_float(word):
        return lax.bitcast_convert_type(word, F32)

    def value_step(i, cur):
        cand = cur | jnp.left_shift(jnp.int32(1), 30 - i)
        return jnp.where(count(aff >= as_float(cand)) >= cap, cand, cur)

    thr = lax.fori_loop(0, 31, value_step, jnp.zeros((N_EXPERTS, 1), jnp.int32))
    gt = aff >= as_float(thr + 1)
    eq = (aff >= as_float(thr)) & jnp.logical_not(gt)
    need = cap - count(gt)
    tok = lax.broadcasted_iota(jnp.int32, (N_EXPERTS, t), 1)
    nbits = t.bit_length() - 1

    def index_step(i, cur):
        cand = cur | jnp.left_shift(jnp.int32(1), nbits - 1 - i)
        return jnp.where(count(eq & (tok < cand)) < need, cand, cur)

    last = lax.fori_loop(0, nbits, index_step, jnp.zeros((N_EXPERTS, 1), jnp.int32))
    self = jnp.where(gt | (eq & (tok <= last)), 1.0, 0.0).astype(F32)

    carry = jnp.zeros((N_EXPERTS, 1), F32)
    for b in range(t // TOKEN_BLOCK):
        sl = slice(b * TOKEN_BLOCK, (b + 1) * TOKEN_BLOCK)
        sbf = self[:, sl]
        pre = _dot(sbf.astype(BF16), u_ref[...]) + carry
        slot_ref[:, sl] = jnp.where(sbf > 0.5, pre.astype(jnp.int32), -1)
        starts_ref[b] = jnp.broadcast_to(carry, (N_EXPERTS, 128)).astype(jnp.int32)
        carry = carry + jnp.sum(sbf, axis=1, keepdims=True)
    starts_ref[t // TOKEN_BLOCK] = jnp.broadcast_to(carry, (N_EXPERTS, 128)).astype(jnp.int32)


def _route(aff_et):
    t = aff_et.shape[1]
    cap = EC_CAPACITY_FACTOR * t // N_EXPERTS
    nblk = t // TOKEN_BLOCK
    upper = jnp.asarray(np.triu(np.ones((TOKEN_BLOCK, TOKEN_BLOCK)), 1), BF16)
    slot_et, starts = pl.pallas_call(
        functools.partial(_route_kernel, t=t, cap=cap),
        out_shape=[jax.ShapeDtypeStruct((N_EXPERTS, t), jnp.int32),
                   jax.ShapeDtypeStruct((nblk + 1, N_EXPERTS, 128), jnp.int32)],
        grid=(1,),
        in_specs=[pl.BlockSpec((N_EXPERTS, t), lambda i: (0, 0)),
                  pl.BlockSpec((TOKEN_BLOCK, TOKEN_BLOCK), lambda i: (0, 0))],
        out_specs=[pl.BlockSpec((N_EXPERTS, t), lambda i: (0, 0)),
                   pl.BlockSpec((nblk + 1, N_EXPERTS, 128), lambda i: (0, 0, 0))],
        compiler_params=pltpu.CompilerParams(dimension_semantics=("arbitrary",)),
        name="route",
    )(aff_et, upper)
    return slot_et, starts[:, :, 0]


def _pack_windows(starts_ref, b, experts, cap):
    first = [jnp.minimum((starts_ref[b, e] // 16) * 16, cap - SLOT_WINDOW) for e in experts]
    rows = [jnp.where(starts_ref[b + 1, e] > starts_ref[b, e], starts_ref[b + 1, e] - w, 0)
            for e, w in zip(experts, first)]
    return first, pl.cdiv(functools.reduce(jnp.maximum, rows), SLOT_WINDOW)


def _gather_group(g, starts_ref, slot_ref, gate_ref, h2_ref, xs_ref, gs_ref, row0, t, cap):
    sub = lax.broadcasted_iota(jnp.int32, (SLOT_WINDOW, TOKEN_BLOCK), 0)
    experts = [g * PACK + j for j in range(PACK)]
    assert (t // TOKEN_BLOCK) % GATHER_UNROLL == 0

    def window(b, first, i):
        hb = pl.ds(pl.multiple_of(b * TOKEN_BLOCK, TOKEN_BLOCK), TOKEN_BLOCK)
        hits, dst = [], []
        for j in range(PACK):
            lo = first[j] + i * SLOT_WINDOW
            w = jnp.minimum(lo, cap - SLOT_WINDOW)
            srow = slot_ref[j, pl.ds(b, 1), :]
            hits.append((srow == w + sub) & (srow >= lo))
            dst.append(pl.ds(pl.multiple_of(row0 + w, 16), SLOT_WINDOW))
        onehot = jnp.concatenate([jnp.where(h, 1.0, 0.0) for h in hits], axis=0).astype(BF16)
        got = _dot(onehot, h2_ref[hb, :])
        for j in range(PACK):
            piece = got[j * SLOT_WINDOW:(j + 1) * SLOT_WINDOW]
            xs_ref[j, dst[j], :] = (xs_ref[j, dst[j], :].astype(F32) + piece).astype(BF16)
            grow = gate_ref[j, pl.ds(b, 1), :]
            gs_ref[j, dst[j], :] += jnp.sum(jnp.where(hits[j], grow, 0.0), axis=1, keepdims=True)

    def blocks(q, carry):
        pending = []
        for u in range(GATHER_UNROLL):
            b = q * GATHER_UNROLL + u
            first, n_windows = _pack_windows(starts_ref, b, experts, cap)
            window(b, first, 0)
            pending.append((b, first, n_windows))
        for b, first, n_windows in pending:
            def more(i, carry, b=b, first=first):
                window(b, first, i)
                return carry

            lax.fori_loop(1, n_windows, more, 0)
        return carry

    lax.fori_loop(0, t // TOKEN_BLOCK // GATHER_UNROLL, blocks, 0)


def _experts_kernel(sp_ref, ss_ref, h2p_ref, h2s_ref, slotp_ref, slots_ref, gatep_ref, gates_ref,
                    wg_ref, wu_ref, wd_ref, yp_ref, ys_ref, xs_ref, gs_ref, acc_ref,
                    *, tp, ts, capp, caps):
    g = pl.program_id(0)
    step = pl.program_id(1)
    j = step // N_FF_TILES
    f = step % N_FF_TILES

    @pl.when(step == 0)
    def _():
        xs_ref[...] = jnp.zeros(xs_ref.shape, BF16)
        gs_ref[...] = jnp.zeros(gs_ref.shape, F32)
        _gather_group(g, sp_ref, slotp_ref, gatep_ref, h2p_ref, xs_ref, gs_ref, 0, tp, capp)
        _gather_group(g, ss_ref, slots_ref, gates_ref, h2s_ref, xs_ref, gs_ref, capp, ts, caps)

    @pl.when(f == 0)
    def _():
        acc_ref[...] = jnp.zeros(acc_ref.shape, F32)

    x = xs_ref[j]
    a = _dot(x, wg_ref[0].astype(BF16))
    u = _dot(x, wu_ref[0].astype(BF16))
    acc_ref[...] += _dot((_silu(a) * u).astype(BF16), wd_ref[0].astype(BF16))

    @pl.when(f == N_FF_TILES - 1)
    def _():
        yp_ref[0] = (acc_ref[0:capp, :] * gs_ref[j, 0:capp, :]).astype(BF16)
        ys_ref[0] = (acc_ref[capp:capp + caps, :] * gs_ref[j, capp:capp + caps, :]).astype(BF16)


def _experts(starts_p, starts_s, h2p, h2s, slot_p, slot_s, gate_p, gate_s, w_gate, w_up, w_down):
    tp, ts = h2p.shape[0], h2s.shape[0]
    capp = EC_CAPACITY_FACTOR * tp // N_EXPERTS
    caps = EC_CAPACITY_FACTOR * ts // N_EXPERTS
    rows = capp + caps
    nbp, nbs = tp // TOKEN_BLOCK, ts // TOKEN_BLOCK
    expert = lambda g, s: g * PACK + s // N_FF_TILES
    grid_spec = pltpu.PrefetchScalarGridSpec(
        num_scalar_prefetch=2,
        grid=(N_EXPERTS // PACK, PACK * N_FF_TILES),
        in_specs=[
            pl.BlockSpec(memory_space=pltpu.VMEM),
            pl.BlockSpec(memory_space=pltpu.VMEM),
            pl.BlockSpec((PACK, nbp, TOKEN_BLOCK), lambda g, s, *_: (g, 0, 0)),
            pl.BlockSpec((PACK, nbs, TOKEN_BLOCK), lambda g, s, *_: (g, 0, 0)),
            pl.BlockSpec((PACK, nbp, TOKEN_BLOCK), lambda g, s, *_: (g, 0, 0)),
            pl.BlockSpec((PACK, nbs, TOKEN_BLOCK), lambda g, s, *_: (g, 0, 0)),
            pl.BlockSpec((1, D_MODEL, FF_TILE), lambda g, s, *_: (expert(g, s), 0, s % N_FF_TILES)),
            pl.BlockSpec((1, D_MODEL, FF_TILE), lambda g, s, *_: (expert(g, s), 0, s % N_FF_TILES)),
            pl.BlockSpec((1, FF_TILE, D_MODEL), lambda g, s, *_: (expert(g, s), s % N_FF_TILES, 0)),
        ],
        out_specs=[
            pl.BlockSpec((1, capp, D_MODEL), lambda g, s, *_: (expert(g, s), 0, 0)),
            pl.BlockSpec((1, caps, D_MODEL), lambda g, s, *_: (expert(g, s), 0, 0)),
        ],
        scratch_shapes=[pltpu.VMEM((PACK, rows, D_MODEL), BF16), pltpu.VMEM((PACK, rows, 1), F32),
                        pltpu.VMEM((rows, D_MODEL), F32)],
    )
    return pl.pallas_call(
        functools.partial(_experts_kernel, tp=tp, ts=ts, capp=capp, caps=caps),
        out_shape=[jax.ShapeDtypeStruct((N_EXPERTS, capp, D_MODEL), BF16),
                   jax.ShapeDtypeStruct((N_EXPERTS, caps, D_MODEL), BF16)],
        grid_spec=grid_spec,
        compiler_params=pltpu.CompilerParams(dimension_semantics=("arbitrary", "arbitrary"),
                                             vmem_limit_bytes=VMEM_LIMIT),
        name="experts",
    )(starts_p, starts_s, h2p, h2s,
      slot_p.reshape(N_EXPERTS, nbp, TOKEN_BLOCK), slot_s.reshape(N_EXPERTS, nbs, TOKEN_BLOCK),
      gate_p.reshape(N_EXPERTS, nbp, TOKEN_BLOCK), gate_s.reshape(N_EXPERTS, nbs, TOKEN_BLOCK),
      w_gate, w_up, w_down)


def _combine_kernel(st_ref, x1_ref, slot_ref, mod_ref, fg_ref, y_ref, out_ref, acc_ref, *, cap):
    b = pl.program_id(0)
    slot = slot_ref[...]
    lane = lax.broadcasted_iota(jnp.int32, (TOKEN_BLOCK, PACK * SLOT_WINDOW), 1)

    def scatter(experts, first, i):
        target = None
        windows = []
        for j, e in enumerate(experts):
            lo = first[j] + i * SLOT_WINDOW
            w = jnp.minimum(lo, cap - SLOT_WINDOW)
            sc = slot[:, e:e + 1]
            col = jnp.where(sc >= lo, sc - w + j * SLOT_WINDOW, -1)
            target = col if target is None else jnp.where(lane < j * SLOT_WINDOW, target, col)
            windows.append(y_ref[e, pl.ds(pl.multiple_of(w, 16), SLOT_WINDOW), :])
        onehot = jnp.where(target == lane, 1.0, 0.0).astype(BF16)
        return _dot(onehot, jnp.concatenate(windows, axis=0))

    groups = []
    total = None
    for g in range(N_EXPERTS // PACK):
        experts = list(range(g * PACK, (g + 1) * PACK))
        first, n_windows = _pack_windows(st_ref, b, experts, cap)
        groups.append((experts, first, n_windows))
        part = scatter(experts, first, 0)
        total = part if total is None else total + part
    acc_ref[...] = total

    for experts, first, n_windows in groups:
        def more(i, carry, experts=experts, first=first):
            acc_ref[...] += scatter(experts, first, i)
            return carry

        lax.fori_loop(1, n_windows, more, 0)

    gate2 = mod_ref[0][:, 5 * D_MODEL:6 * D_MODEL]
    x2 = x1_ref[...] + gate2 * acc_ref[...]
    out_ref[...] = _rms(x2) * fg_ref[...]


def _combine(starts, x1, slot_te, mod_rows, blocks_per_mod_row, final_g, y):
    t = x1.shape[0]
    cap = y.shape[1]
    nblk = t // TOKEN_BLOCK
    if blocks_per_mod_row is None:
        mod_map = lambda b, *_: (0, 0, 0)
    else:
        mod_map = lambda b, *_: (b // blocks_per_mod_row, 0, 0)
    grid_spec = pltpu.PrefetchScalarGridSpec(
        num_scalar_prefetch=1,
        grid=(nblk,),
        in_specs=[
            pl.BlockSpec((TOKEN_BLOCK, D_MODEL), lambda b, *_: (b, 0)),
            pl.BlockSpec((TOKEN_BLOCK, N_EXPERTS), lambda b, *_: (b, 0)),
            pl.BlockSpec((1, 1, 6 * D_MODEL), mod_map),
            pl.BlockSpec((1, D_MODEL), lambda b, *_: (0, 0)),
            pl.BlockSpec(memory_space=pltpu.VMEM),
        ],
        out_specs=pl.BlockSpec((TOKEN_BLOCK, D_MODEL), lambda b, *_: (b, 0)),
        scratch_shapes=[pltpu.VMEM((TOKEN_BLOCK, D_MODEL), F32)],
    )
    return pl.pallas_call(
        functools.partial(_combine_kernel, cap=cap),
        out_shape=jax.ShapeDtypeStruct((t, D_MODEL), F32),
        grid_spec=grid_spec,
        compiler_params=pltpu.CompilerParams(dimension_semantics=("arbitrary",),
                                             vmem_limit_bytes=VMEM_LIMIT),
        name="combine",
    )(starts, x1, slot_te, mod_rows, final_g.reshape(1, D_MODEL), y)


def kernel(x_prompt, x_sample, state_ret, c, c_ctx, norm1_g, norm2_g, final_g, w_mod, b_mod, w_in,
           w_fmix, w_out, w_router, w_gate, w_up, w_down):
    bp, seq, _ = x_prompt.shape
    bs, dec_seq, _ = x_sample.shape
    assert w_mod.shape[0] == 1, "single-layer trunk"
    tp, ts = bp * seq, bs * dec_seq

    cond = jnp.concatenate([c_ctx[None, :], c], axis=0)
    mod = _modulation(cond, w_mod[0], b_mod[0])
    mod_ctx = mod[0:1].reshape(1, 1, 6 * D_MODEL)
    mod_lat = mod[1:1 + bs].reshape(bs, 1, 6 * D_MODEL)

    w_in_bf = w_in[0].astype(BF16)
    w_out_bf = w_out[0].astype(BF16)
    mix_p, states = _mixer(x_prompt, mod_ctx, False, None, True, False, norm1_g[0], w_in_bf, w_fmix[0])
    (mix_s,) = _mixer(x_sample, mod_lat, True, state_ret[:, 0], False, True, norm1_g[0], w_in_bf,
                      w_fmix[0])
    blocks_per_seq = dec_seq // TOKEN_BLOCK
    x1p, h2p, affp = _post(x_prompt.reshape(tp, D_MODEL), mix_p, mod_ctx, None,
                           norm2_g[0], w_out_bf, w_router[0])
    x1s, h2s, affs = _post(x_sample.reshape(ts, D_MODEL), mix_s, mod_lat, dec_seq,
                           norm2_g[0], w_out_bf, w_router[0])

    affp, affs = affp.T, affs.T
    slot_p, starts_p = _route(affp)
    slot_s, starts_s = _route(affs)

    yp, ys = _experts(starts_p, starts_s, h2p, h2s, slot_p, slot_s, affp, affs,
                      w_gate[0], w_up[0], w_down[0])

    out_p = _combine(starts_p, x1p, slot_p.T, mod_ctx, None, final_g, yp)
    out_s = _combine(starts_s, x1s, slot_s.T, mod_lat, blocks_per_seq, final_g, ys)

    y_prompt = out_p.reshape(bp, seq, D_MODEL)
    y_sample = out_s.reshape(bs, dec_seq, D_MODEL)
    state_new = states.reshape(bp, 1, 2, N_RET_HEADS, HEAD_DIM, HEAD_DIM).astype(x_prompt.dtype)
    return (y_prompt, y_sample, state_new)
```

```python
import functools
import math

import jax
import jax.numpy as jnp
import numpy as np
from jax import lax
from jax.experimental import pallas as pl
from jax.experimental.pallas import tpu as pltpu

D_MODEL = 1024
D_FOURIER = 512
N_FOURIER_GROUPS = 4
FOURIER_GROUP_W = 128
D_RET = 512
N_RET_HEADS = 4
HEAD_DIM = 128
CHUNK = 256
GRID_W = 64
N_EXPERTS = 16
EC_CAPACITY_FACTOR = 2
D_EXPERT_FF = 2816
ROPE_BASE = 10000.0
EPS = 1e-6
D_IN_PROJ = D_FOURIER + 5 * D_RET
LOG_GAMMA_FWD = np.log(1.0 - 2.0 ** (-5.0 - np.arange(N_RET_HEADS))).astype(np.float32)
LOG_GAMMA_BWD = np.log(1.0 - 2.0 ** (-5.5 - np.arange(N_RET_HEADS))).astype(np.float32)

TOKEN_BLOCK = 256
SLOT_WINDOW = 64
PACK = TOKEN_BLOCK // SLOT_WINDOW
GATHER_UNROLL = 4
Y_CHUNK = 256
FF_TILE = 256
N_FF_TILES = D_EXPERT_FF // FF_TILE
MOD_TILE = 512
MIXER_ROWS = 1024
POST_ROWS = 256
VMEM_LIMIT = 56 * 1024 * 1024

F32 = jnp.float32
BF16 = jnp.bfloat16


def _dot(a, b):
    return jnp.dot(a, b, preferred_element_type=F32)


def _dot_nt(a, b):
    return lax.dot_general(a, b, (((1,), (1,)), ((), ())), preferred_element_type=F32)


def _silu(x):
    return x * jax.nn.sigmoid(x)


def _mod_kernel(condt_ref, w_ref, b_ref, out_ref, *, n_cond):
    s = _silu(condt_ref[...])
    w = w_ref[...]
    out_ref[...] = jnp.zeros(out_ref.shape, F32)
    for r in range(n_cond):
        out_ref[r:r + 1, :] = jnp.sum(w * s[:, r:r + 1], axis=0, keepdims=True) + b_ref[...]


def _modulation(cond_rows, w_mod, b_mod):
    n_cond = cond_rows.shape[0]
    condt = jnp.zeros((D_MODEL, 8), F32).at[:, :n_cond].set(cond_rows.T)
    n_out = w_mod.shape[1]
    return pl.pallas_call(
        functools.partial(_mod_kernel, n_cond=n_cond),
        out_shape=jax.ShapeDtypeStruct((8, n_out), F32),
        grid=(n_out // MOD_TILE,),
        in_specs=[
            pl.BlockSpec((D_MODEL, 8), lambda j: (0, 0)),
            pl.BlockSpec((D_MODEL, MOD_TILE), lambda j: (0, j)),
            pl.BlockSpec((1, MOD_TILE), lambda j: (0, j)),
        ],
        out_specs=pl.BlockSpec((8, MOD_TILE), lambda j: (0, j)),
        compiler_params=pltpu.CompilerParams(dimension_semantics=("arbitrary",)),
        name="mod",
    )(condt, w_mod, b_mod.reshape(1, n_out))


def _rms(x):
    return x * lax.rsqrt(jnp.mean(x * x, axis=-1, keepdims=True) + EPS)


def _groupnorm(o):
    mu = jnp.mean(o, axis=-1, keepdims=True)
    c = o - mu
    return c * lax.rsqrt(jnp.mean(c * c, axis=-1, keepdims=True) + EPS)


def _split_hi_lo(x):
    hi = x.astype(BF16)
    lo = (x - hi.astype(F32)).astype(BF16)
    return hi, lo


def _mixer_kernel(*refs, n, use_rope, has_state_in, emit_state):
    it = iter(refs)
    x_ref, mod_ref, g1_ref, win_ref, wfmix_ref = (next(it) for _ in range(5))
    cw_ref, cn_ref, sn_ref, dmat_ref, qdec_ref, kdec_ref, sdec_ref = (next(it) for _ in range(7))
    cos_ref = sin_ref = s0_ref = st_ref = None
    if use_rope:
        cos_ref, sin_ref = next(it), next(it)
    if has_state_in:
        s0_ref = next(it)
    mix_ref = next(it)
    if emit_state:
        st_ref = next(it)
    p_ref, of_ref, ob_ref = next(it), next(it), next(it)

    n_seq = MIXER_ROWS // n
    chunks_per_seq = n // CHUNK
    mod = mod_ref[0]
    shift1 = mod[:, 0:D_MODEL]
    scale1 = mod[:, D_MODEL:2 * D_MODEL]

    h = (_rms(x_ref[...]) * g1_ref[...] * (1.0 + scale1) + shift1).astype(BF16)
    for j in range(D_IN_PROJ // 512):
        p_ref[:, j * 512:(j + 1) * 512] = _dot(h, win_ref[:, j * 512:(j + 1) * 512])

    xf = p_ref[:, 0:D_FOURIER].astype(BF16)
    xc, xs = [], []
    cw = cw_ref[...].astype(BF16)
    for g in range(N_FOURIER_GROUPS):
        t = _dot(xf[:, g * FOURIER_GROUP_W:(g + 1) * FOURIER_GROUP_W], cw)
        xc.append(t[:, :FOURIER_GROUP_W].astype(BF16))
        xs.append(t[:, FOURIER_GROUP_W:].astype(BF16))
    xc = jnp.concatenate(xc, axis=1)
    xs = jnp.concatenate(xs, axis=1)
    cn = cn_ref[...].astype(BF16)
    sn = sn_ref[...].astype(BF16)
    for s in range(n_seq):
        rs = slice(s * n, (s + 1) * n)
        fre = (_dot(cn, xc[rs]) - _dot(sn, xs[rs])) * (1.0 / math.sqrt(n * FOURIER_GROUP_W))
        fre = fre.astype(BF16)
        for g in range(N_FOURIER_GROUPS):
            sl = slice(g * FOURIER_GROUP_W, (g + 1) * FOURIER_GROUP_W)
            mix_ref[rs, sl] = _dot(fre[:, sl], wfmix_ref[g].astype(BF16)).astype(BF16)

    for hh in range(N_RET_HEADS):
        base = D_FOURIER + hh * HEAD_DIM
        q = p_ref[:, base:base + HEAD_DIM]
        k = p_ref[:, base + D_RET:base + D_RET + HEAD_DIM]
        v = p_ref[:, base + 2 * D_RET:base + 2 * D_RET + HEAD_DIM]
        if use_rope:
            lane = lax.broadcasted_iota(jnp.int32, (MIXER_ROWS, HEAD_DIM), 1)
            first = (lane % 64) < 32

            def rope(t):
                swapped = jnp.where(first, pltpu.roll(t, HEAD_DIM - 32, 1), pltpu.roll(t, 32, 1))
                return t * cos_ref[...] + swapped * sin_ref[...]

            q, k = rope(q), rope(k)
        k = k * (HEAD_DIM ** -0.5)
        qb, vb = q.astype(BF16), v.astype(BF16)
        kb = k.astype(BF16)

        def initial(s, direction):
            if has_state_in:
                return s0_ref[s, direction, hh]
            return jnp.zeros((HEAD_DIM, HEAD_DIM), F32)

        for s in range(n_seq):
            parts = []
            for ci in range(chunks_per_seq):
                c = s * chunks_per_seq + ci
                rs = slice(c * CHUNK, (c + 1) * CHUNK)
                qc, kc, vc = qb[rs], kb[rs], vb[rs]
                qk = _dot_nt(qc, kc)
                lhs = jnp.concatenate([(qk * dmat_ref[0, hh]).astype(BF16),
                                       (qk * dmat_ref[1, hh]).astype(BF16),
                                       (k[rs] * kdec_ref[0, hh]).T.astype(BF16),
                                       (k[rs] * kdec_ref[1, hh]).T.astype(BF16)], axis=0)
                parts.append((rs, qc, _dot(lhs, vc)))
            sf = initial(s, 0)
            for ci in range(chunks_per_seq):
                rs, qc, r = parts[ci]
                o = r[0:CHUNK]
                if has_state_in or ci > 0:
                    o = o + qdec_ref[0, hh] * _dot(qc, sf.astype(BF16))
                of_ref[rs, :] = o
                sf = sf * sdec_ref[0, hh] + r[2 * CHUNK:2 * CHUNK + HEAD_DIM]
            sb = initial(s, 1)
            for ci in reversed(range(chunks_per_seq)):
                rs, qc, r = parts[ci]
                o = r[CHUNK:2 * CHUNK]
                if has_state_in or ci < chunks_per_seq - 1:
                    o = o + qdec_ref[1, hh] * _dot(qc, sb.astype(BF16))
                ob_ref[rs, :] = o
                sb = sb * sdec_ref[1, hh] + r[2 * CHUNK + HEAD_DIM:]
            if emit_state:
                st_ref[s, 0, hh] = sf
                st_ref[s, 1, hh] = sb

        gf = p_ref[:, base + 3 * D_RET:base + 3 * D_RET + HEAD_DIM]
        gb = p_ref[:, base + 4 * D_RET:base + 4 * D_RET + HEAD_DIM]
        y = _silu(gf) * _groupnorm(of_ref[...]) + _silu(gb) * _groupnorm(ob_ref[...])
        mix_ref[:, base:base + HEAD_DIM] = y.astype(BF16)


def _post_kernel(x_ref, mix_ref, mod_ref, g2_ref, wout_ref, wr_ref, x1_ref, h2_ref, aff_ref):
    mod = mod_ref[0]
    gate1 = mod[:, 2 * D_MODEL:3 * D_MODEL]
    shift2 = mod[:, 3 * D_MODEL:4 * D_MODEL]
    scale2 = mod[:, 4 * D_MODEL:5 * D_MODEL]
    x1 = x_ref[...] + gate1 * _dot(mix_ref[...], wout_ref[...])
    x1_ref[...] = x1
    h2 = _rms(x1) * g2_ref[...] * (1.0 + scale2) + shift2
    h2_hi, h2_lo = _split_hi_lo(h2)
    h2_ref[...] = h2_hi
    wr_hi, wr_lo = _split_hi_lo(wr_ref[...])
    by_hi = _dot(h2_hi, jnp.concatenate([wr_hi, wr_lo], axis=1))
    logits = by_hi[:, :N_EXPERTS] + (_dot(h2_lo, wr_hi) + by_hi[:, N_EXPERTS:])
    z = jnp.exp(logits - jnp.max(logits, axis=-1, keepdims=True))
    aff_ref[...] = z / jnp.sum(z, axis=-1, keepdims=True)


def _dft_consts(n):
    w = FOURIER_GROUP_W
    jw = np.arange(w)
    angw = 2.0 * np.pi * np.outer(jw, jw) / w
    cw = np.concatenate([np.cos(angw), np.sin(angw)], axis=1)
    jn = np.arange(n)
    angn = 2.0 * np.pi * (np.outer(jn, jn) % n) / n
    return (jnp.asarray(cw, F32), jnp.asarray(np.cos(angn), F32), jnp.asarray(np.sin(angn), F32))


def _retention_consts():
    i = np.arange(CHUNK, dtype=np.float64)
    diff = i[:, None] - i[None, :]
    dmat = np.zeros((2, N_RET_HEADS, CHUNK, CHUNK))
    qdec = np.zeros((2, N_RET_HEADS, CHUNK, HEAD_DIM))
    kdec = np.zeros((2, N_RET_HEADS, CHUNK, HEAD_DIM))
    sdec = np.zeros((2, N_RET_HEADS, HEAD_DIM, HEAD_DIM))
    for hh in range(N_RET_HEADS):
        lf = float(LOG_GAMMA_FWD[hh])
        lb = float(LOG_GAMMA_BWD[hh])
        dmat[0, hh] = np.where(diff >= 0, np.exp(lf * np.maximum(diff, 0.0)), 0.0)
        dmat[1, hh] = np.where(diff <= 0, np.exp(lb * np.maximum(-diff, 0.0)), 0.0)
        qdec[0, hh] = np.exp(lf * (i + 1.0))[:, None]
        qdec[1, hh] = np.exp(lb * (CHUNK - i))[:, None]
        kdec[0, hh] = np.exp(lf * (CHUNK - 1.0 - i))[:, None]
        kdec[1, hh] = np.exp(lb * i)[:, None]
        sdec[0, hh] = math.exp(lf * CHUNK)
        sdec[1, hh] = math.exp(lb * CHUNK)
    return tuple(jnp.asarray(a, F32) for a in (dmat, qdec, kdec, sdec))


def _rope_consts(n):
    rows_n = n // GRID_W
    row = np.repeat(np.arange(rows_n, dtype=np.float64), GRID_W)
    col = np.tile(np.arange(GRID_W, dtype=np.float64), rows_n)
    n_pairs = HEAD_DIM // 4
    freqs = (np.float32(ROPE_BASE) ** (-np.arange(n_pairs, dtype=np.float32) / n_pairs)).astype(np.float64)
    ar = row[:, None] * freqs[None, :]
    ac = col[:, None] * freqs[None, :]
    cos = np.concatenate([np.cos(ar), np.cos(ar), np.cos(ac), np.cos(ac)], axis=1)
    sin = np.concatenate([-np.sin(ar), np.sin(ar), -np.sin(ac), np.sin(ac)], axis=1)
    return jnp.asarray(cos, F32), jnp.asarray(sin, F32)


def _const_spec(shape):
    nd = len(shape)
    return pl.BlockSpec(shape, lambda b, _nd=nd: (0,) * _nd, pipeline_mode=pl.Buffered(1))


def _mixer(x, mod_rows, mod_per_batch, state_in, emit_state, use_rope, g1, w_in_bf, w_fmix):
    nb, n, _ = x.shape
    assert MIXER_ROWS % n == 0 and (nb * n) % MIXER_ROWS == 0
    n_seq = MIXER_ROWS // n
    has_state_in = state_in is not None
    cw, cn, sn = _dft_consts(n)
    dmat, qdec, kdec, sdec = _retention_consts()
    consts = [cw, cn, sn, dmat, qdec, kdec, sdec]
    if use_rope:
        assert n_seq == 1
        consts += list(_rope_consts(n))
    weights = [g1.reshape(1, D_MODEL), w_in_bf, w_fmix]

    mod_map = (lambda b: (b // n_seq, 0, 0)) if mod_per_batch else (lambda b: (0, 0, 0))
    state_spec = pl.BlockSpec((n_seq, 2, N_RET_HEADS, HEAD_DIM, HEAD_DIM), lambda b: (b, 0, 0, 0, 0))
    row_spec = pl.BlockSpec((MIXER_ROWS, D_MODEL), lambda b: (b, 0))
    in_specs = [row_spec, pl.BlockSpec((1, 1, 6 * D_MODEL), mod_map)]
    in_specs += [_const_spec(a.shape) for a in weights + consts]
    args = [x.reshape(nb * n, D_MODEL), mod_rows] + weights + consts
    if has_state_in:
        in_specs.append(state_spec)
        args.append(state_in)

    out_shape = [jax.ShapeDtypeStruct((nb * n, D_MODEL), BF16)]
    out_specs = [row_spec]
    if emit_state:
        out_shape.append(jax.ShapeDtypeStruct((nb, 2, N_RET_HEADS, HEAD_DIM, HEAD_DIM), F32))
        out_specs.append(state_spec)

    return pl.pallas_call(
        functools.partial(_mixer_kernel, n=n, use_rope=use_rope, has_state_in=has_state_in,
                          emit_state=emit_state),
        out_shape=out_shape,
        grid=(nb * n // MIXER_ROWS,),
        in_specs=in_specs,
        out_specs=out_specs,
        scratch_shapes=[pltpu.VMEM((MIXER_ROWS, D_IN_PROJ), F32),
                        pltpu.VMEM((MIXER_ROWS, HEAD_DIM), F32), pltpu.VMEM((MIXER_ROWS, HEAD_DIM), F32)],
        compiler_params=pltpu.CompilerParams(dimension_semantics=("arbitrary",),
                                             vmem_limit_bytes=VMEM_LIMIT),
        name="mixer_rope" if use_rope else "mixer",
    )(*args)


def _post(x, mix, mod_rows, tokens_per_mod_row, g2, w_out_bf, w_router):
    t = x.shape[0]
    if tokens_per_mod_row is None:
        mod_map = lambda b: (0, 0, 0)
    else:
        assert tokens_per_mod_row % POST_ROWS == 0
        mod_map = lambda b: (b // (tokens_per_mod_row // POST_ROWS), 0, 0)
    row_spec = pl.BlockSpec((POST_ROWS, D_MODEL), lambda b: (b, 0))
    return pl.pallas_call(
        _post_kernel,
        out_shape=[jax.ShapeDtypeStruct((t, D_MODEL), F32),
                   jax.ShapeDtypeStruct((t, D_MODEL), BF16),
                   jax.ShapeDtypeStruct((t, N_EXPERTS), F32)],
        grid=(t // POST_ROWS,),
        in_specs=[row_spec, row_spec,
                  pl.BlockSpec((1, 1, 6 * D_MODEL), mod_map),
                  _const_spec((1, D_MODEL)), _const_spec((D_MODEL, D_MODEL)),
                  _const_spec((D_MODEL, N_EXPERTS))],
        out_specs=[row_spec, row_spec, pl.BlockSpec((POST_ROWS, N_EXPERTS), lambda b: (b, 0))],
        compiler_params=pltpu.CompilerParams(dimension_semantics=("arbitrary",),
                                             vmem_limit_bytes=VMEM_LIMIT),
        name="post",
    )(x, mix, mod_rows, g2.reshape(1, D_MODEL), w_out_bf, w_router)


def _route_kernel(aff_ref, u_ref, slot_ref, starts_ref, *, t, cap):
    aff = aff_ref[...]

    def count(mask):
        return jnp.sum(mask.astype(jnp.int32), axis=1, keepdims=True)

    def as_float(word):
        return lax.bitcast_convert_type(word, F32)

    def value_step(i, cur):
        cand = cur | jnp.left_shift(jnp.int32(1), 30 - i)
        return jnp.where(count(aff >= as_float(cand)) >= cap, cand, cur)

    thr = lax.fori_loop(0, 31, value_step, jnp.zeros((N_EXPERTS, 1), jnp.int32))
    gt = aff >= as_float(thr + 1)
    eq = (aff >= as_float(thr)) & jnp.logical_not(gt)
    need = cap - count(gt)
    tok = lax.broadcasted_iota(jnp.int32, (N_EXPERTS, t), 1)
    nbits = t.bit_length() - 1

    def index_step(i, cur):
        cand = cur | jnp.left_shift(jnp.int32(1), nbits - 1 - i)
        return jnp.where(count(eq & (tok < cand)) < need, cand, cur)

    last = lax.fori_loop(0, nbits, index_step, jnp.zeros((N_EXPERTS, 1), jnp.int32))
    self = jnp.where(gt | (eq & (tok <= last)), 1.0, 0.0).astype(F32)

    carry = jnp.zeros((N_EXPERTS, 1), F32)
    for b in range(t // TOKEN_BLOCK):
        sl = slice(b * TOKEN_BLOCK, (b + 1) * TOKEN_BLOCK)
        sbf = self[:, sl]
        pre = _dot(sbf.astype(BF16), u_ref[...]) + carry
        slot_ref[:, sl] = jnp.where(sbf > 0.5, pre.astype(jnp.int32), -1)
        starts_ref[b] = jnp.broadcast_to(carry, (N_EXPERTS, 128)).astype(jnp.int32)
        carry = carry + jnp.sum(sbf, axis=1, keepdims=True)
    starts_ref[t // TOKEN_BLOCK] = jnp.broadcast_to(carry, (N_EXPERTS, 128)).astype(jnp.int32)


def _route(aff_et):
    t = aff_et.shape[1]
    cap = EC_CAPACITY_FACTOR * t // N_EXPERTS
    nblk = t // TOKEN_BLOCK
    upper = jnp.asarray(np.triu(np.ones((TOKEN_BLOCK, TOKEN_BLOCK)), 1), BF16)
    slot_et, starts = pl.pallas_call(
        functools.partial(_route_kernel, t=t, cap=cap),
        out_shape=[jax.ShapeDtypeStruct((N_EXPERTS, t), jnp.int32),
                   jax.ShapeDtypeStruct((nblk + 1, N_EXPERTS, 128), jnp.int32)],
        grid=(1,),
        in_specs=[pl.BlockSpec((N_EXPERTS, t), lambda i: (0, 0)),
                  pl.BlockSpec((TOKEN_BLOCK, TOKEN_BLOCK), lambda i: (0, 0))],
        out_specs=[pl.BlockSpec((N_EXPERTS, t), lambda i: (0, 0)),
                   pl.BlockSpec((nblk + 1, N_EXPERTS, 128), lambda i: (0, 0, 0))],
        compiler_params=pltpu.CompilerParams(dimension_semantics=("arbitrary",)),
        name="route",
    )(aff_et, upper)
    return slot_et, starts[:, :, 0]


def _pack_windows(starts_ref, b, experts, cap):
    first = [jnp.minimum((starts_ref[b, e] // 16) * 16, cap - SLOT_WINDOW) for e in experts]
    rows = [jnp.where(starts_ref[b + 1, e] > starts_ref[b, e], starts_ref[b + 1, e] - w, 0)
            for e, w in zip(experts, first)]
    return first, pl.cdiv(functools.reduce(jnp.maximum, rows), SLOT_WINDOW)


def _block_copy(hbm_ref, vmem_ref, sem_ref, b):
    rows = pl.ds(pl.multiple_of(b * TOKEN_BLOCK, TOKEN_BLOCK), TOKEN_BLOCK)
    return pltpu.make_async_copy(hbm_ref.at[rows], vmem_ref.at[rows], sem_ref.at[b])


def _gather_group(g, starts_ref, slot_ref, gate_ref, h2_ref, xs_ref, gs_ref, row0, t, cap, arrive):
    sub = lax.broadcasted_iota(jnp.int32, (SLOT_WINDOW, TOKEN_BLOCK), 0)
    experts = [g * PACK + j for j in range(PACK)]
    assert (t // TOKEN_BLOCK) % GATHER_UNROLL == 0

    def window(b, first, i):
        hb = pl.ds(pl.multiple_of(b * TOKEN_BLOCK, TOKEN_BLOCK), TOKEN_BLOCK)
        hits, dst = [], []
        for j in range(PACK):
            lo = first[j] + i * SLOT_WINDOW
            w = jnp.minimum(lo, cap - SLOT_WINDOW)
            srow = slot_ref[j, pl.ds(b, 1), :]
            hits.append((srow == w + sub) & (srow >= lo))
            dst.append(pl.ds(pl.multiple_of(row0 + w, 16), SLOT_WINDOW))
        onehot = jnp.concatenate([jnp.where(h, 1.0, 0.0) for h in hits], axis=0).astype(BF16)
        got = _dot(onehot, h2_ref[hb, :])
        for j in range(PACK):
            piece = got[j * SLOT_WINDOW:(j + 1) * SLOT_WINDOW].astype(BF16)
            xs_ref[j, dst[j], :] = xs_ref[j, dst[j], :] + piece
            grow = gate_ref[j, pl.ds(b, 1), :]
            gs_ref[j, dst[j], :] += jnp.sum(jnp.where(hits[j], grow, 0.0), axis=1, keepdims=True)

    def blocks(q, carry):
        pending = []
        for u in range(GATHER_UNROLL):
            arrive(q * GATHER_UNROLL + u)
        for u in range(GATHER_UNROLL):
            b = q * GATHER_UNROLL + u
            first, n_windows = _pack_windows(starts_ref, b, experts, cap)
            window(b, first, 0)
            pending.append((b, first, n_windows))
        for b, first, n_windows in pending:
            def more(i, carry, b=b, first=first):
                window(b, first, i)
                return carry

            lax.fori_loop(1, n_windows, more, 0)
        return carry

    lax.fori_loop(0, t // TOKEN_BLOCK // GATHER_UNROLL, blocks, 0)


def _experts_kernel(sp_ref, ss_ref, h2p_hbm, h2s_hbm, slotp_ref, slots_ref, gatep_ref, gates_ref,
                    wg_ref, wu_ref, wd_ref, yp_ref, ys_ref, xs_ref, gs_ref, acc_ref,
                    h2p_ref, h2s_ref, semp_ref, sems_ref, *, tp, ts, capp, caps):
    g = pl.program_id(0)
    step = pl.program_id(1)
    j = step // N_FF_TILES
    f = step % N_FF_TILES
    loading = [(h2p_hbm, h2p_ref, semp_ref, tp // TOKEN_BLOCK),
               (h2s_hbm, h2s_ref, sems_ref, ts // TOKEN_BLOCK)]

    @pl.when((step == 0) & (g == 0))
    def _():
        for hbm_ref, vmem_ref, sem_ref, n_blocks in loading:
            for b in range(n_blocks):
                _block_copy(hbm_ref, vmem_ref, sem_ref, b).start()

    def arrive(hbm_ref, vmem_ref, sem_ref, _):
        def wait(b):
            @pl.when(g == 0)
            def _():
                _block_copy(hbm_ref, vmem_ref, sem_ref, b).wait()

        return wait

    @pl.when(step == 0)
    def _():
        xs_ref[...] = jnp.zeros(xs_ref.shape, BF16)
        gs_ref[...] = jnp.zeros(gs_ref.shape, F32)
        _gather_group(g, sp_ref, slotp_ref, gatep_ref, h2p_ref, xs_ref, gs_ref, 0, tp, capp,
                      arrive(*loading[0]))
        _gather_group(g, ss_ref, slots_ref, gates_ref, h2s_ref, xs_ref, gs_ref, capp, ts, caps,
                      arrive(*loading[1]))

    @pl.when(f == 0)
    def _():
        acc_ref[...] = jnp.zeros(acc_ref.shape, F32)

    x = xs_ref[j]
    a = _dot(x, wg_ref[0].astype(BF16))
    u = _dot(x, wu_ref[0].astype(BF16))
    acc_ref[...] += _dot((_silu(a) * u).astype(BF16), wd_ref[0].astype(BF16))

    @pl.when(f == N_FF_TILES - 1)
    def _():
        yp_ref[0] = (acc_ref[0:capp, :] * gs_ref[j, 0:capp, :]).astype(BF16)
        ys_ref[0] = (acc_ref[capp:capp + caps, :] * gs_ref[j, capp:capp + caps, :]).astype(BF16)


def _experts(starts_p, starts_s, h2p, h2s, slot_p, slot_s, gate_p, gate_s, w_gate, w_up, w_down):
    tp, ts = h2p.shape[0], h2s.shape[0]
    capp = EC_CAPACITY_FACTOR * tp // N_EXPERTS
    caps = EC_CAPACITY_FACTOR * ts // N_EXPERTS
    rows = capp + caps
    nbp, nbs = tp // TOKEN_BLOCK, ts // TOKEN_BLOCK
    expert = lambda g, s: g * PACK + s // N_FF_TILES
    grid_spec = pltpu.PrefetchScalarGridSpec(
        num_scalar_prefetch=2,
        grid=(N_EXPERTS // PACK, PACK * N_FF_TILES),
        in_specs=[
            pl.BlockSpec(memory_space=pl.ANY),
            pl.BlockSpec(memory_space=pl.ANY),
            pl.BlockSpec((PACK, nbp, TOKEN_BLOCK), lambda g, s, *_: (g, 0, 0)),
            pl.BlockSpec((PACK, nbs, TOKEN_BLOCK), lambda g, s, *_: (g, 0, 0)),
            pl.BlockSpec((PACK, nbp, TOKEN_BLOCK), lambda g, s, *_: (g, 0, 0)),
            pl.BlockSpec((PACK, nbs, TOKEN_BLOCK), lambda g, s, *_: (g, 0, 0)),
            pl.BlockSpec((1, D_MODEL, FF_TILE), lambda g, s, *_: (expert(g, s), 0, s % N_FF_TILES)),
            pl.BlockSpec((1, D_MODEL, FF_TILE), lambda g, s, *_: (expert(g, s), 0, s % N_FF_TILES)),
            pl.BlockSpec((1, FF_TILE, D_MODEL), lambda g, s, *_: (expert(g, s), s % N_FF_TILES, 0)),
        ],
        out_specs=[
            pl.BlockSpec((1, capp, D_MODEL), lambda g, s, *_: (expert(g, s), 0, 0)),
            pl.BlockSpec((1, caps, D_MODEL), lambda g, s, *_: (expert(g, s), 0, 0)),
        ],
        scratch_shapes=[pltpu.VMEM((PACK, rows, D_MODEL), BF16), pltpu.VMEM((PACK, rows, 1), F32),
                        pltpu.VMEM((rows, D_MODEL), F32),
                        pltpu.VMEM((tp, D_MODEL), BF16), pltpu.VMEM((ts, D_MODEL), BF16),
                        pltpu.SemaphoreType.DMA((nbp,)), pltpu.SemaphoreType.DMA((nbs,))],
    )
    return pl.pallas_call(
        functools.partial(_experts_kernel, tp=tp, ts=ts, capp=capp, caps=caps),
        out_shape=[jax.ShapeDtypeStruct((N_EXPERTS, capp, D_MODEL), BF16),
                   jax.ShapeDtypeStruct((N_EXPERTS, caps, D_MODEL), BF16)],
        grid_spec=grid_spec,
        compiler_params=pltpu.CompilerParams(dimension_semantics=("arbitrary", "arbitrary"),
                                             vmem_limit_bytes=VMEM_LIMIT),
        name="experts",
    )(starts_p, starts_s, h2p, h2s,
      slot_p.reshape(N_EXPERTS, nbp, TOKEN_BLOCK), slot_s.reshape(N_EXPERTS, nbs, TOKEN_BLOCK),
      gate_p.reshape(N_EXPERTS, nbp, TOKEN_BLOCK), gate_s.reshape(N_EXPERTS, nbs, TOKEN_BLOCK),
      w_gate, w_up, w_down)


def _combine_kernel(st_ref, x1_ref, slot_ref, mod_ref, fg_ref, y_hbm, out_ref, acc_ref, y_ref,
                    sem_ref, arrived_ref, *, cap):
    b = pl.program_id(0)
    n_chunks = cap // Y_CHUNK

    def chunk_copy(c):
        rows = pl.ds(c * Y_CHUNK, Y_CHUNK)
        return pltpu.make_async_copy(y_hbm.at[:, rows, :], y_ref.at[:, rows, :], sem_ref.at[c])

    @pl.when(b == 0)
    def _():
        arrived_ref[0] = 0
        for c in range(n_chunks):
            chunk_copy(c).start()

    top = functools.reduce(jnp.maximum, [st_ref[b + 1, e] for e in range(N_EXPERTS)])
    want = pl.cdiv(jnp.minimum(top + SLOT_WINDOW, cap), Y_CHUNK)
    have = arrived_ref[0]
    for c in range(n_chunks):
        @pl.when((c >= have) & (c < want))
        def _(c=c):
            chunk_copy(c).wait()
    arrived_ref[0] = jnp.maximum(have, want)

    slot = slot_ref[...]
    lane = lax.broadcasted_iota(jnp.int32, (TOKEN_BLOCK, PACK * SLOT_WINDOW), 1)

    def scatter(experts, first, i):
        target = None
        windows = []
        for j, e in enumerate(experts):
            lo = first[j] + i * SLOT_WINDOW
            w = jnp.minimum(lo, cap - SLOT_WINDOW)
            sc = slot[:, e:e + 1]
            col = jnp.where(sc >= lo, sc - w + j * SLOT_WINDOW, -1)
            target = col if target is None else jnp.where(lane < j * SLOT_WINDOW, target, col)
            windows.append(y_ref[e, pl.ds(pl.multiple_of(w, 16), SLOT_WINDOW), :])
        onehot = jnp.where(target == lane, 1.0, 0.0).astype(BF16)
        return _dot(onehot, jnp.concatenate(windows, axis=0))

    groups = []
    total = None
    for g in range(N_EXPERTS // PACK):
        experts = list(range(g * PACK, (g + 1) * PACK))
        first, n_windows = _pack_windows(st_ref, b, experts, cap)
        groups.append((experts, first, n_windows))
        part = scatter(experts, first, 0)
        total = part if total is None else total + part
    acc_ref[...] = total

    for experts, first, n_windows in groups:
        def more(i, carry, experts=experts, first=first):
            acc_ref[...] += scatter(experts, first, i)
            return carry

        lax.fori_loop(1, n_windows, more, 0)

    gate2 = mod_ref[0][:, 5 * D_MODEL:6 * D_MODEL]
    x2 = x1_ref[...] + gate2 * acc_ref[...]
    out_ref[...] = _rms(x2) * fg_ref[...]


def _combine(starts, x1, slot_te, mod_rows, blocks_per_mod_row, final_g, y):
    t = x1.shape[0]
    cap = y.shape[1]
    nblk = t // TOKEN_BLOCK
    if blocks_per_mod_row is None:
        mod_map = lambda b, *_: (0, 0, 0)
    else:
        mod_map = lambda b, *_: (b // blocks_per_mod_row, 0, 0)
    grid_spec = pltpu.PrefetchScalarGridSpec(
        num_scalar_prefetch=1,
        grid=(nblk,),
        in_specs=[
            pl.BlockSpec((TOKEN_BLOCK, D_MODEL), lambda b, *_: (b, 0)),
            pl.BlockSpec((TOKEN_BLOCK, N_EXPERTS), lambda b, *_: (b, 0)),
            pl.BlockSpec((1, 1, 6 * D_MODEL), mod_map),
            pl.BlockSpec((1, D_MODEL), lambda b, *_: (0, 0)),
            pl.BlockSpec(memory_space=pl.ANY),
        ],
        out_specs=pl.BlockSpec((TOKEN_BLOCK, D_MODEL), lambda b, *_: (b, 0)),
        scratch_shapes=[pltpu.VMEM((TOKEN_BLOCK, D_MODEL), F32), pltpu.VMEM(y.shape, BF16),
                        pltpu.SemaphoreType.DMA((cap // Y_CHUNK,)), pltpu.SMEM((1,), jnp.int32)],
    )
    return pl.pallas_call(
        functools.partial(_combine_kernel, cap=cap),
        out_shape=jax.ShapeDtypeStruct((t, D_MODEL), F32),
        grid_spec=grid_spec,
        compiler_params=pltpu.CompilerParams(dimension_semantics=("arbitrary",),
                                             vmem_limit_bytes=VMEM_LIMIT),
        name="combine",
    )(starts, x1, slot_te, mod_rows, final_g.reshape(1, D_MODEL), y)


def kernel(x_prompt, x_sample, state_ret, c, c_ctx, norm1_g, norm2_g, final_g, w_mod, b_mod, w_in,
           w_fmix, w_out, w_router, w_gate, w_up, w_down):
    bp, seq, _ = x_prompt.shape
    bs, dec_seq, _ = x_sample.shape
    assert w_mod.shape[0] == 1, "single-layer trunk"
    tp, ts = bp * seq, bs * dec_seq

    cond = jnp.concatenate([c_ctx[None, :], c], axis=0)
    mod = _modulation(cond, w_mod[0], b_mod[0])
    mod_ctx = mod[0:1].reshape(1, 1, 6 * D_MODEL)
    mod_lat = mod[1:1 + bs].reshape(bs, 1, 6 * D_MODEL)

    w_in_bf = w_in[0].astype(BF16)
    w_out_bf = w_out[0].astype(BF16)
    mix_p, states = _mixer(x_prompt, mod_ctx, False, None, True, False, norm1_g[0], w_in_bf, w_fmix[0])
    (mix_s,) = _mixer(x_sample, mod_lat, True, state_ret[:, 0], False, True, norm1_g[0], w_in_bf,
                      w_fmix[0])
    blocks_per_seq = dec_seq // TOKEN_BLOCK
    x1p, h2p, affp = _post(x_prompt.reshape(tp, D_MODEL), mix_p, mod_ctx, None,
                           norm2_g[0], w_out_bf, w_router[0])
    x1s, h2s, affs = _post(x_sample.reshape(ts, D_MODEL), mix_s, mod_lat, dec_seq,
                           norm2_g[0], w_out_bf, w_router[0])

    affp, affs = affp.T, affs.T
    slot_p, starts_p = _route(affp)
    slot_s, starts_s = _route(affs)

    yp, ys = _experts(starts_p, starts_s, h2p, h2s, slot_p, slot_s, affp, affs,
                      w_gate[0], w_up[0], w_down[0])

    out_p = _combine(starts_p, x1p, slot_p.T, mod_ctx, None, final_g, yp)
    out_s = _combine(starts_s, x1s, slot_s.T, mod_lat, blocks_per_seq, final_g, ys)

    y_prompt = out_p.reshape(bp, seq, D_MODEL)
    y_sample = out_s.reshape(bs, dec_seq, D_MODEL)
    state_new = states.reshape(bp, 1, 2, N_RET_HEADS, HEAD_DIM, HEAD_DIM).astype(x_prompt.dtype)
    return (y_prompt, y_sample, state_new)
```

```python
import functools
import math

import jax
import jax.numpy as jnp
import numpy as np
from jax import lax
from jax.experimental import pallas as pl
from jax.experimental.pallas import tpu as pltpu

D_MODEL = 1024
D_FOURIER = 512
N_FOURIER_GROUPS = 4
FOURIER_GROUP_W = 128
D_RET = 512
N_RET_HEADS = 4
HEAD_DIM = 128
CHUNK = 256
GRID_W = 64
N_EXPERTS = 16
EC_CAPACITY_FACTOR = 2
D_EXPERT_FF = 2816
ROPE_BASE = 10000.0
EPS = 1e-6
D_IN_PROJ = D_FOURIER + 5 * D_RET
LOG_GAMMA_FWD = np.log(1.0 - 2.0 ** (-5.0 - np.arange(N_RET_HEADS))).astype(np.float32)
LOG_GAMMA_BWD = np.log(1.0 - 2.0 ** (-5.5 - np.arange(N_RET_HEADS))).astype(np.float32)

TOKEN_BLOCK = 256
SLOT_WINDOW = 64
PACK = TOKEN_BLOCK // SLOT_WINDOW
GATHER_UNROLL = 4
Y_CHUNK = 256
FF_TILE = 256
N_FF_TILES = D_EXPERT_FF // FF_TILE
MOD_TILE = 512
MIXER_ROWS = 1024
POST_ROWS = 256
VMEM_LIMIT = 56 * 1024 * 1024

F32 = jnp.float32
BF16 = jnp.bfloat16


def _dot(a, b):
    return jnp.dot(a, b, preferred_element_type=F32)


def _dot_nt(a, b):
    return lax.dot_general(a, b, (((1,), (1,)), ((), ())), preferred_element_type=F32)


def _silu(x):
    return x * jax.nn.sigmoid(x)


def _mod_kernel(condt_ref, w_ref, b_ref, out_ref, *, n_cond):
    s = _silu(condt_ref[...])
    w = w_ref[...]
    out_ref[...] = jnp.zeros(out_ref.shape, F32)
    for r in range(n_cond):
        out_ref[r] = jnp.sum(w * s[:, r:r + 1], axis=0, keepdims=True) + b_ref[...]


def _modulation(cond_rows, w_mod, b_mod):
    n_cond = cond_rows.shape[0]
    condt = jnp.zeros((D_MODEL, 8), F32).at[:, :n_cond].set(cond_rows.T)
    n_out = w_mod.shape[1]
    return pl.pallas_call(
        functools.partial(_mod_kernel, n_cond=n_cond),
        out_shape=jax.ShapeDtypeStruct((8, 1, n_out), F32),
        grid=(n_out // MOD_TILE,),
        in_specs=[
            pl.BlockSpec((D_MODEL, 8), lambda j: (0, 0)),
            pl.BlockSpec((D_MODEL, MOD_TILE), lambda j: (0, j)),
            pl.BlockSpec((1, MOD_TILE), lambda j: (0, j)),
        ],
        out_specs=pl.BlockSpec((8, 1, MOD_TILE), lambda j: (0, 0, j)),
        compiler_params=pltpu.CompilerParams(dimension_semantics=("arbitrary",)),
        name="mod",
    )(condt, w_mod, b_mod.reshape(1, n_out))


def _rms(x):
    return x * lax.rsqrt(jnp.mean(x * x, axis=-1, keepdims=True) + EPS)


def _groupnorm(o):
    mu = jnp.mean(o, axis=-1, keepdims=True)
    c = o - mu
    return c * lax.rsqrt(jnp.mean(c * c, axis=-1, keepdims=True) + EPS)


def _split_hi_lo(x):
    hi = x.astype(BF16)
    lo = (x - hi.astype(F32)).astype(BF16)
    return hi, lo


def _mixer_kernel(*refs, n, use_rope, has_state_in, emit_state):
    it = iter(refs)
    x_ref, mod_ref, g1_ref, win_ref, wfmix_ref = (next(it) for _ in range(5))
    cw_ref, cn_ref, sn_ref, dmat_ref, qdec_ref, kdec_ref, sdec_ref = (next(it) for _ in range(7))
    cos_ref = sin_ref = s0_ref = st_ref = None
    if use_rope:
        cos_ref, sin_ref = next(it), next(it)
    if has_state_in:
        s0_ref = next(it)
    mix_ref = next(it)
    if emit_state:
        st_ref = next(it)
    p_ref, of_ref, ob_ref = next(it), next(it), next(it)

    n_seq = MIXER_ROWS // n
    chunks_per_seq = n // CHUNK
    mod = mod_ref[0]
    shift1 = mod[:, 0:D_MODEL]
    scale1 = mod[:, D_MODEL:2 * D_MODEL]

    h = (_rms(x_ref[...]) * g1_ref[...] * (1.0 + scale1) + shift1).astype(BF16)
    for j in range(D_IN_PROJ // 512):
        p_ref[:, j * 512:(j + 1) * 512] = _dot(h, win_ref[:, j * 512:(j + 1) * 512])

    xf = p_ref[:, 0:D_FOURIER].astype(BF16)
    xc, xs = [], []
    cw = cw_ref[...].astype(BF16)
    for g in range(N_FOURIER_GROUPS):
        t = _dot(xf[:, g * FOURIER_GROUP_W:(g + 1) * FOURIER_GROUP_W], cw)
        xc.append(t[:, :FOURIER_GROUP_W].astype(BF16))
        xs.append(t[:, FOURIER_GROUP_W:].astype(BF16))
    xc = jnp.concatenate(xc, axis=1)
    xs = jnp.concatenate(xs, axis=1)
    cn = cn_ref[...].astype(BF16)
    sn = sn_ref[...].astype(BF16)
    for s in range(n_seq):
        rs = slice(s * n, (s + 1) * n)
        fre = (_dot(cn, xc[rs]) - _dot(sn, xs[rs])) * (1.0 / math.sqrt(n * FOURIER_GROUP_W))
        fre = fre.astype(BF16)
        for g in range(N_FOURIER_GROUPS):
            sl = slice(g * FOURIER_GROUP_W, (g + 1) * FOURIER_GROUP_W)
            mix_ref[rs, sl] = _dot(fre[:, sl], wfmix_ref[g].astype(BF16)).astype(BF16)

    for hh in range(N_RET_HEADS):
        base = D_FOURIER + hh * HEAD_DIM
        q = p_ref[:, base:base + HEAD_DIM]
        k = p_ref[:, base + D_RET:base + D_RET + HEAD_DIM]
        v = p_ref[:, base + 2 * D_RET:base + 2 * D_RET + HEAD_DIM]
        if use_rope:
            lane = lax.broadcasted_iota(jnp.int32, (MIXER_ROWS, HEAD_DIM), 1)
            first = (lane % 64) < 32

            def rope(t):
                swapped = jnp.where(first, pltpu.roll(t, HEAD_DIM - 32, 1), pltpu.roll(t, 32, 1))
                return t * cos_ref[...] + swapped * sin_ref[...]

            q, k = rope(q), rope(k)
        k = k * (HEAD_DIM ** -0.5)
        qb, vb = q.astype(BF16), v.astype(BF16)
        kb = k.astype(BF16)

        def initial(s, direction):
            if has_state_in:
                return s0_ref[s, direction, hh]
            return jnp.zeros((HEAD_DIM, HEAD_DIM), F32)

        for s in range(n_seq):
            parts = []
            for ci in range(chunks_per_seq):
                c = s * chunks_per_seq + ci
                rs = slice(c * CHUNK, (c + 1) * CHUNK)
                qc, kc, vc = qb[rs], kb[rs], vb[rs]
                qk = _dot_nt(qc, kc)
                lhs = jnp.concatenate([(qk * dmat_ref[0, hh]).astype(BF16),
                                       (qk * dmat_ref[1, hh]).astype(BF16),
                                       (k[rs] * kdec_ref[0, hh]).T.astype(BF16),
                                       (k[rs] * kdec_ref[1, hh]).T.astype(BF16)], axis=0)
                parts.append((rs, qc, _dot(lhs, vc)))
            sf = initial(s, 0)
            for ci in range(chunks_per_seq):
                rs, qc, r = parts[ci]
                o = r[0:CHUNK]
                if has_state_in or ci > 0:
                    o = o + qdec_ref[0, hh] * _dot(qc, sf.astype(BF16))
                of_ref[rs, :] = o
                sf = sf * sdec_ref[0, hh] + r[2 * CHUNK:2 * CHUNK + HEAD_DIM]
            sb = initial(s, 1)
            for ci in reversed(range(chunks_per_seq)):
                rs, qc, r = parts[ci]
                o = r[CHUNK:2 * CHUNK]
                if has_state_in or ci < chunks_per_seq - 1:
                    o = o + qdec_ref[1, hh] * _dot(qc, sb.astype(BF16))
                ob_ref[rs, :] = o
                sb = sb * sdec_ref[1, hh] + r[2 * CHUNK + HEAD_DIM:]
            if emit_state:
                st_ref[s, 0, hh] = sf
                st_ref[s, 1, hh] = sb

        gf = p_ref[:, base + 3 * D_RET:base + 3 * D_RET + HEAD_DIM]
        gb = p_ref[:, base + 4 * D_RET:base + 4 * D_RET + HEAD_DIM]
        y = _silu(gf) * _groupnorm(of_ref[...]) + _silu(gb) * _groupnorm(ob_ref[...])
        mix_ref[:, base:base + HEAD_DIM] = y.astype(BF16)


def _post_kernel(x_ref, mix_ref, mod_ref, g2_ref, wout_ref, wr_ref, x1_ref, h2_ref, aff_ref):
    mod = mod_ref[0]
    gate1 = mod[:, 2 * D_MODEL:3 * D_MODEL]
    shift2 = mod[:, 3 * D_MODEL:4 * D_MODEL]
    scale2 = mod[:, 4 * D_MODEL:5 * D_MODEL]
    x1 = x_ref[...] + gate1 * _dot(mix_ref[...], wout_ref[...])
    x1_ref[...] = x1
    h2 = _rms(x1) * g2_ref[...] * (1.0 + scale2) + shift2
    h2_hi, h2_lo = _split_hi_lo(h2)
    h2_ref[...] = h2_hi
    wr_hi, wr_lo = _split_hi_lo(wr_ref[...])
    by_hi = _dot(h2_hi, jnp.concatenate([wr_hi, wr_lo], axis=1))
    logits = by_hi[:, :N_EXPERTS] + (_dot(h2_lo, wr_hi) + by_hi[:, N_EXPERTS:])
    z = jnp.exp(logits - jnp.max(logits, axis=-1, keepdims=True))
    aff_ref[...] = z / jnp.sum(z, axis=-1, keepdims=True)


def _dft_consts(n):
    w = FOURIER_GROUP_W
    jw = np.arange(w)
    angw = 2.0 * np.pi * np.outer(jw, jw) / w
    cw = np.concatenate([np.cos(angw), np.sin(angw)], axis=1)
    jn = np.arange(n)
    angn = 2.0 * np.pi * (np.outer(jn, jn) % n) / n
    return (jnp.asarray(cw, F32), jnp.asarray(np.cos(angn), F32), jnp.asarray(np.sin(angn), F32))


def _retention_consts():
    i = np.arange(CHUNK, dtype=np.float64)
    diff = i[:, None] - i[None, :]
    dmat = np.zeros((2, N_RET_HEADS, CHUNK, CHUNK))
    qdec = np.zeros((2, N_RET_HEADS, CHUNK, HEAD_DIM))
    kdec = np.zeros((2, N_RET_HEADS, CHUNK, HEAD_DIM))
    sdec = np.zeros((2, N_RET_HEADS, HEAD_DIM, HEAD_DIM))
    for hh in range(N_RET_HEADS):
        lf = float(LOG_GAMMA_FWD[hh])
        lb = float(LOG_GAMMA_BWD[hh])
        dmat[0, hh] = np.where(diff >= 0, np.exp(lf * np.maximum(diff, 0.0)), 0.0)
        dmat[1, hh] = np.where(diff <= 0, np.exp(lb * np.maximum(-diff, 0.0)), 0.0)
        qdec[0, hh] = np.exp(lf * (i + 1.0))[:, None]
        qdec[1, hh] = np.exp(lb * (CHUNK - i))[:, None]
        kdec[0, hh] = np.exp(lf * (CHUNK - 1.0 - i))[:, None]
        kdec[1, hh] = np.exp(lb * i)[:, None]
        sdec[0, hh] = math.exp(lf * CHUNK)
        sdec[1, hh] = math.exp(lb * CHUNK)
    return tuple(jnp.asarray(a, F32) for a in (dmat, qdec, kdec, sdec))


def _rope_consts(n):
    rows_n = n // GRID_W
    row = np.repeat(np.arange(rows_n, dtype=np.float64), GRID_W)
    col = np.tile(np.arange(GRID_W, dtype=np.float64), rows_n)
    n_pairs = HEAD_DIM // 4
    freqs = (np.float32(ROPE_BASE) ** (-np.arange(n_pairs, dtype=np.float32) / n_pairs)).astype(np.float64)
    ar = row[:, None] * freqs[None, :]
    ac = col[:, None] * freqs[None, :]
    cos = np.concatenate([np.cos(ar), np.cos(ar), np.cos(ac), np.cos(ac)], axis=1)
    sin = np.concatenate([-np.sin(ar), np.sin(ar), -np.sin(ac), np.sin(ac)], axis=1)
    return jnp.asarray(cos, F32), jnp.asarray(sin, F32)


def _const_spec(shape):
    nd = len(shape)
    return pl.BlockSpec(shape, lambda b, _nd=nd: (0,) * _nd, pipeline_mode=pl.Buffered(1))


def _mod_row_spec(first_row, blocks_per_row):
    if blocks_per_row is None:
        return pl.BlockSpec((1, 1, 6 * D_MODEL), lambda b, *_: (first_row, 0, 0))
    return pl.BlockSpec((1, 1, 6 * D_MODEL), lambda b, *_: (first_row + b // blocks_per_row, 0, 0))


def _mixer(x, mod_rows, mod_first_row, mod_per_batch, state_in, emit_state, use_rope, g1, w_in_bf,
           w_fmix):
    nb, n, _ = x.shape
    assert MIXER_ROWS % n == 0 and (nb * n) % MIXER_ROWS == 0
    n_seq = MIXER_ROWS // n
    has_state_in = state_in is not None
    cw, cn, sn = _dft_consts(n)
    dmat, qdec, kdec, sdec = _retention_consts()
    consts = [cw, cn, sn, dmat, qdec, kdec, sdec]
    if use_rope:
        assert n_seq == 1
        consts += list(_rope_consts(n))
    weights = [g1.reshape(1, D_MODEL), w_in_bf, w_fmix]

    if mod_per_batch:
        assert n % MIXER_ROWS == 0
    state_spec = pl.BlockSpec((n_seq, 2, N_RET_HEADS, HEAD_DIM, HEAD_DIM), lambda b: (b, 0, 0, 0, 0))
    row_spec = pl.BlockSpec((MIXER_ROWS, D_MODEL), lambda b: (b, 0))
    in_specs = [row_spec, _mod_row_spec(mod_first_row, n // MIXER_ROWS if mod_per_batch else None)]
    in_specs += [_const_spec(a.shape) for a in weights + consts]
    args = [x.reshape(nb * n, D_MODEL), mod_rows] + weights + consts
    if has_state_in:
        in_specs.append(state_spec)
        args.append(state_in)

    out_shape = [jax.ShapeDtypeStruct((nb * n, D_MODEL), BF16)]
    out_specs = [row_spec]
    if emit_state:
        out_shape.append(jax.ShapeDtypeStruct((nb, 2, N_RET_HEADS, HEAD_DIM, HEAD_DIM), F32))
        out_specs.append(state_spec)

    return pl.pallas_call(
        functools.partial(_mixer_kernel, n=n, use_rope=use_rope, has_state_in=has_state_in,
                          emit_state=emit_state),
        out_shape=out_shape,
        grid=(nb * n // MIXER_ROWS,),
        in_specs=in_specs,
        out_specs=out_specs,
        scratch_shapes=[pltpu.VMEM((MIXER_ROWS, D_IN_PROJ), F32),
                        pltpu.VMEM((MIXER_ROWS, HEAD_DIM), F32), pltpu.VMEM((MIXER_ROWS, HEAD_DIM), F32)],
        compiler_params=pltpu.CompilerParams(dimension_semantics=("arbitrary",),
                                             vmem_limit_bytes=VMEM_LIMIT),
        name="mixer_rope" if use_rope else "mixer",
    )(*args)


def _post(x, mix, mod_rows, mod_first_row, tokens_per_mod_row, g2, w_out_bf, w_router):
    t = x.shape[0]
    assert tokens_per_mod_row is None or tokens_per_mod_row % POST_ROWS == 0
    blocks_per_row = None if tokens_per_mod_row is None else tokens_per_mod_row // POST_ROWS
    row_spec = pl.BlockSpec((POST_ROWS, D_MODEL), lambda b: (b, 0))
    return pl.pallas_call(
        _post_kernel,
        out_shape=[jax.ShapeDtypeStruct((t, D_MODEL), F32),
                   jax.ShapeDtypeStruct((t, D_MODEL), BF16),
                   jax.ShapeDtypeStruct((t, N_EXPERTS), F32)],
        grid=(t // POST_ROWS,),
        in_specs=[row_spec, row_spec,
                  _mod_row_spec(mod_first_row, blocks_per_row),
                  _const_spec((1, D_MODEL)), _const_spec((D_MODEL, D_MODEL)),
                  _const_spec((D_MODEL, N_EXPERTS))],
        out_specs=[row_spec, row_spec, pl.BlockSpec((POST_ROWS, N_EXPERTS), lambda b: (b, 0))],
        compiler_params=pltpu.CompilerParams(dimension_semantics=("arbitrary",),
                                             vmem_limit_bytes=VMEM_LIMIT),
        name="post",
    )(x, mix, mod_rows, g2.reshape(1, D_MODEL), w_out_bf, w_router)


def _route_kernel(aff_ref, u_ref, slot_ref, gate_ref, starts_ref, *, t, cap):
    aff = aff_ref[...]

    def count(mask):
        return jnp.sum(mask.astype(jnp.int32), axis=1, keepdims=True)

    def as_float(word):
        return lax.bitcast_convert_type(word, F32)

    def value_step(i, cur):
        cand = cur | jnp.left_shift(jnp.int32(1), 30 - i)
        return jnp.where(count(aff >= as_float(cand)) >= cap, cand, cur)

    thr = lax.fori_loop(0, 31, value_step, jnp.zeros((N_EXPERTS, 1), jnp.int32))
    gt = aff >= as_float(thr + 1)
    eq = (aff >= as_float(thr)) & jnp.logical_not(gt)
    need = cap - count(gt)
    tok = lax.broadcasted_iota(jnp.int32, (N_EXPERTS, t), 1)
    nbits = t.bit_length() - 1

    def index_step(i, cur):
        cand = cur | jnp.left_shift(jnp.int32(1), nbits - 1 - i)
        return jnp.where(count(eq & (tok < cand)) < need, cand, cur)

    last = lax.fori_loop(0, nbits, index_step, jnp.zeros((N_EXPERTS, 1), jnp.int32))
    self = jnp.where(gt | (eq & (tok <= last)), 1.0, 0.0).astype(F32)

    carry = jnp.zeros((N_EXPERTS, 1), F32)
    starts_ref[...] = jnp.zeros(starts_ref.shape, jnp.int32)
    for b in range(t // TOKEN_BLOCK):
        sl = slice(b * TOKEN_BLOCK, (b + 1) * TOKEN_BLOCK)
        sbf = self[:, sl]
        pre = _dot(sbf.astype(BF16), u_ref[...]) + carry
        slot_ref[:, b, :] = jnp.where(sbf > 0.5, pre.astype(jnp.int32), -1)
        gate_ref[:, b, :] = aff[:, sl]
        starts_ref[:, b:b + 1] = carry.astype(jnp.int32)
        carry = carry + jnp.sum(sbf, axis=1, keepdims=True)
    nblk = t // TOKEN_BLOCK
    starts_ref[:, nblk:nblk + 1] = carry.astype(jnp.int32)


def _route(aff_et):
    t = aff_et.shape[1]
    cap = EC_CAPACITY_FACTOR * t // N_EXPERTS
    nblk = t // TOKEN_BLOCK
    upper = jnp.asarray(np.triu(np.ones((TOKEN_BLOCK, TOKEN_BLOCK)), 1), BF16)
    assert nblk + 1 <= 128
    blocked = (N_EXPERTS, nblk, TOKEN_BLOCK)
    return pl.pallas_call(
        functools.partial(_route_kernel, t=t, cap=cap),
        out_shape=[jax.ShapeDtypeStruct(blocked, jnp.int32), jax.ShapeDtypeStruct(blocked, F32),
                   jax.ShapeDtypeStruct((N_EXPERTS, 128), jnp.int32)],
        grid=(1,),
        in_specs=[pl.BlockSpec((N_EXPERTS, t), lambda i: (0, 0)),
                  pl.BlockSpec((TOKEN_BLOCK, TOKEN_BLOCK), lambda i: (0, 0))],
        out_specs=[pl.BlockSpec(blocked, lambda i: (0, 0, 0)), pl.BlockSpec(blocked, lambda i: (0, 0, 0)),
                   pl.BlockSpec((N_EXPERTS, 128), lambda i: (0, 0))],
        compiler_params=pltpu.CompilerParams(dimension_semantics=("arbitrary",)),
        name="route",
    )(aff_et, upper)


def _pack_windows(starts_ref, b, experts, cap):
    first = [jnp.minimum((starts_ref[e, b] // 16) * 16, cap - SLOT_WINDOW) for e in experts]
    rows = [jnp.where(starts_ref[e, b + 1] > starts_ref[e, b], starts_ref[e, b + 1] - w, 0)
            for e, w in zip(experts, first)]
    return first, pl.cdiv(functools.reduce(jnp.maximum, rows), SLOT_WINDOW)


def _block_copy(hbm_ref, vmem_ref, sem_ref, b):
    rows = pl.ds(pl.multiple_of(b * TOKEN_BLOCK, TOKEN_BLOCK), TOKEN_BLOCK)
    return pltpu.make_async_copy(hbm_ref.at[rows], vmem_ref.at[rows], sem_ref.at[b])


def _gather_group(g, starts_ref, slot_ref, gate_ref, h2_ref, xs_ref, gs_ref, row0, t, cap, arrive):
    sub = lax.broadcasted_iota(jnp.int32, (SLOT_WINDOW, TOKEN_BLOCK), 0)
    experts = [g * PACK + j for j in range(PACK)]
    assert (t // TOKEN_BLOCK) % GATHER_UNROLL == 0

    def window(b, first, i):
        hb = pl.ds(pl.multiple_of(b * TOKEN_BLOCK, TOKEN_BLOCK), TOKEN_BLOCK)
        hits, dst = [], []
        for j in range(PACK):
            lo = first[j] + i * SLOT_WINDOW
            w = jnp.minimum(lo, cap - SLOT_WINDOW)
            srow = slot_ref[j, pl.ds(b, 1), :]
            hits.append((srow == w + sub) & (srow >= lo))
            dst.append(pl.ds(pl.multiple_of(row0 + w, 16), SLOT_WINDOW))
        onehot = jnp.concatenate([jnp.where(h, 1.0, 0.0) for h in hits], axis=0).astype(BF16)
        got = _dot(onehot, h2_ref[hb, :])
        for j in range(PACK):
            piece = got[j * SLOT_WINDOW:(j + 1) * SLOT_WINDOW].astype(BF16)
            xs_ref[j, dst[j], :] = xs_ref[j, dst[j], :] + piece
            grow = gate_ref[j, pl.ds(b, 1), :]
            gs_ref[j, dst[j], :] += jnp.sum(jnp.where(hits[j], grow, 0.0), axis=1, keepdims=True)

    def blocks(q, carry):
        pending = []
        for u in range(GATHER_UNROLL):
            arrive(q * GATHER_UNROLL + u)
        for u in range(GATHER_UNROLL):
            b = q * GATHER_UNROLL + u
            first, n_windows = _pack_windows(starts_ref, b, experts, cap)
            window(b, first, 0)
            pending.append((b, first, n_windows))
        for b, first, n_windows in pending:
            def more(i, carry, b=b, first=first):
                window(b, first, i)
                return carry

            lax.fori_loop(1, n_windows, more, 0)
        return carry

    lax.fori_loop(0, t // TOKEN_BLOCK // GATHER_UNROLL, blocks, 0)


def _experts_kernel(sp_ref, ss_ref, h2p_hbm, h2s_hbm, slotp_ref, slots_ref, gatep_ref, gates_ref,
                    wg_ref, wu_ref, wd_ref, yp_ref, ys_ref, xs_ref, gs_ref, acc_ref,
                    h2p_ref, h2s_ref, semp_ref, sems_ref, *, tp, ts, capp, caps):
    g = pl.program_id(0)
    step = pl.program_id(1)
    j = step // N_FF_TILES
    f = step % N_FF_TILES
    loading = [(h2p_hbm, h2p_ref, semp_ref, tp // TOKEN_BLOCK),
               (h2s_hbm, h2s_ref, sems_ref, ts // TOKEN_BLOCK)]

    @pl.when((step == 0) & (g == 0))
    def _():
        for hbm_ref, vmem_ref, sem_ref, n_blocks in loading:
            for b in range(n_blocks):
                _block_copy(hbm_ref, vmem_ref, sem_ref, b).start()

    def arrive(hbm_ref, vmem_ref, sem_ref, _):
        def wait(b):
            @pl.when(g == 0)
            def _():
                _block_copy(hbm_ref, vmem_ref, sem_ref, b).wait()

        return wait

    @pl.when(step == 0)
    def _():
        xs_ref[...] = jnp.zeros(xs_ref.shape, BF16)
        gs_ref[...] = jnp.zeros(gs_ref.shape, F32)
        _gather_group(g, sp_ref, slotp_ref, gatep_ref, h2p_ref, xs_ref, gs_ref, 0, tp, capp,
                      arrive(*loading[0]))
        _gather_group(g, ss_ref, slots_ref, gates_ref, h2s_ref, xs_ref, gs_ref, capp, ts, caps,
                      arrive(*loading[1]))

    @pl.when(f == 0)
    def _():
        acc_ref[...] = jnp.zeros(acc_ref.shape, F32)

    x = xs_ref[j]
    a = _dot(x, wg_ref[0].astype(BF16))
    u = _dot(x, wu_ref[0].astype(BF16))
    acc_ref[...] += _dot((_silu(a) * u).astype(BF16), wd_ref[0].astype(BF16))

    @pl.when(f == N_FF_TILES - 1)
    def _():
        yp_ref[0] = (acc_ref[0:capp, :] * gs_ref[j, 0:capp, :]).astype(BF16)
        ys_ref[0] = (acc_ref[capp:capp + caps, :] * gs_ref[j, capp:capp + caps, :]).astype(BF16)


def _experts(starts_p, starts_s, h2p, h2s, slot_p, slot_s, gate_p, gate_s, w_gate, w_up, w_down):
    tp, ts = h2p.shape[0], h2s.shape[0]
    capp = EC_CAPACITY_FACTOR * tp // N_EXPERTS
    caps = EC_CAPACITY_FACTOR * ts // N_EXPERTS
    rows = capp + caps
    nbp, nbs = tp // TOKEN_BLOCK, ts // TOKEN_BLOCK
    expert = lambda g, s: g * PACK + s // N_FF_TILES
    grid_spec = pltpu.PrefetchScalarGridSpec(
        num_scalar_prefetch=2,
        grid=(N_EXPERTS // PACK, PACK * N_FF_TILES),
        in_specs=[
            pl.BlockSpec(memory_space=pl.ANY),
            pl.BlockSpec(memory_space=pl.ANY),
            pl.BlockSpec((PACK, nbp, TOKEN_BLOCK), lambda g, s, *_: (g, 0, 0)),
            pl.BlockSpec((PACK, nbs, TOKEN_BLOCK), lambda g, s, *_: (g, 0, 0)),
            pl.BlockSpec((PACK, nbp, TOKEN_BLOCK), lambda g, s, *_: (g, 0, 0)),
            pl.BlockSpec((PACK, nbs, TOKEN_BLOCK), lambda g, s, *_: (g, 0, 0)),
            pl.BlockSpec((1, D_MODEL, FF_TILE), lambda g, s, *_: (expert(g, s), 0, s % N_FF_TILES)),
            pl.BlockSpec((1, D_MODEL, FF_TILE), lambda g, s, *_: (expert(g, s), 0, s % N_FF_TILES)),
            pl.BlockSpec((1, FF_TILE, D_MODEL), lambda g, s, *_: (expert(g, s), s % N_FF_TILES, 0)),
        ],
        out_specs=[
            pl.BlockSpec((1, capp, D_MODEL), lambda g, s, *_: (expert(g, s), 0, 0)),
            pl.BlockSpec((1, caps, D_MODEL), lambda g, s, *_: (expert(g, s), 0, 0)),
        ],
        scratch_shapes=[pltpu.VMEM((PACK, rows, D_MODEL), BF16), pltpu.VMEM((PACK, rows, 1), F32),
                        pltpu.VMEM((rows, D_MODEL), F32),
                        pltpu.VMEM((tp, D_MODEL), BF16), pltpu.VMEM((ts, D_MODEL), BF16),
                        pltpu.SemaphoreType.DMA((nbp,)), pltpu.SemaphoreType.DMA((nbs,))],
    )
    return pl.pallas_call(
        functools.partial(_experts_kernel, tp=tp, ts=ts, capp=capp, caps=caps),
        out_shape=[jax.ShapeDtypeStruct((N_EXPERTS, capp, D_MODEL), BF16),
                   jax.ShapeDtypeStruct((N_EXPERTS, caps, D_MODEL), BF16)],
        grid_spec=grid_spec,
        compiler_params=pltpu.CompilerParams(dimension_semantics=("arbitrary", "arbitrary"),
                                             vmem_limit_bytes=VMEM_LIMIT),
        name="experts",
    )(starts_p, starts_s, h2p, h2s, slot_p, slot_s, gate_p, gate_s, w_gate, w_up, w_down)


def _combine_kernel(st_ref, x1_ref, slot_ref, mod_ref, fg_ref, y_hbm, out_ref, acc_ref, y_ref,
                    sem_ref, arrived_ref, *, cap):
    b = pl.program_id(0)
    n_chunks = cap // Y_CHUNK

    def chunk_copy(c):
        rows = pl.ds(c * Y_CHUNK, Y_CHUNK)
        return pltpu.make_async_copy(y_hbm.at[:, rows, :], y_ref.at[:, rows, :], sem_ref.at[c])

    @pl.when(b == 0)
    def _():
        arrived_ref[0] = 0
        for c in range(n_chunks):
            chunk_copy(c).start()

    top = functools.reduce(jnp.maximum, [st_ref[e, b + 1] for e in range(N_EXPERTS)])
    want = pl.cdiv(jnp.minimum(top + SLOT_WINDOW, cap), Y_CHUNK)
    have = arrived_ref[0]
    for c in range(n_chunks):
        @pl.when((c >= have) & (c < want))
        def _(c=c):
            chunk_copy(c).wait()
    arrived_ref[0] = jnp.maximum(have, want)

    slot = slot_ref[...]
    lane = lax.broadcasted_iota(jnp.int32, (TOKEN_BLOCK, PACK * SLOT_WINDOW), 1)

    def scatter(experts, first, i):
        target = None
        windows = []
        for j, e in enumerate(experts):
            lo = first[j] + i * SLOT_WINDOW
            w = jnp.minimum(lo, cap - SLOT_WINDOW)
            sc = slot[:, e:e + 1]
            col = jnp.where(sc >= lo, sc - w + j * SLOT_WINDOW, -1)
            target = col if target is None else jnp.where(lane < j * SLOT_WINDOW, target, col)
            windows.append(y_ref[e, pl.ds(pl.multiple_of(w, 16), SLOT_WINDOW), :])
        onehot = jnp.where(target == lane, 1.0, 0.0).astype(BF16)
        return _dot(onehot, jnp.concatenate(windows, axis=0))

    groups = []
    total = None
    for g in range(N_EXPERTS // PACK):
        experts = list(range(g * PACK, (g + 1) * PACK))
        first, n_windows = _pack_windows(st_ref, b, experts, cap)
        groups.append((experts, first, n_windows))
        part = scatter(experts, first, 0)
        total = part if total is None else total + part
    acc_ref[...] = total

    for experts, first, n_windows in groups:
        def more(i, carry, experts=experts, first=first):
            acc_ref[...] += scatter(experts, first, i)
            return carry

        lax.fori_loop(1, n_windows, more, 0)

    gate2 = mod_ref[0][:, 5 * D_MODEL:6 * D_MODEL]
    x2 = x1_ref[...] + gate2 * acc_ref[...]
    out_ref[...] = _rms(x2) * fg_ref[...]


def _combine(starts, x1, slot_te, mod_rows, mod_first_row, blocks_per_mod_row, final_g, y):
    t = x1.shape[0]
    cap = y.shape[1]
    nblk = t // TOKEN_BLOCK
    grid_spec = pltpu.PrefetchScalarGridSpec(
        num_scalar_prefetch=1,
        grid=(nblk,),
        in_specs=[
            pl.BlockSpec((TOKEN_BLOCK, D_MODEL), lambda b, *_: (b, 0)),
            pl.BlockSpec((TOKEN_BLOCK, N_EXPERTS), lambda b, *_: (b, 0)),
            _mod_row_spec(mod_first_row, blocks_per_mod_row),
            pl.BlockSpec((1, D_MODEL), lambda b, *_: (0, 0)),
            pl.BlockSpec(memory_space=pl.ANY),
        ],
        out_specs=pl.BlockSpec((TOKEN_BLOCK, D_MODEL), lambda b, *_: (b, 0)),
        scratch_shapes=[pltpu.VMEM((TOKEN_BLOCK, D_MODEL), F32), pltpu.VMEM(y.shape, BF16),
                        pltpu.SemaphoreType.DMA((cap // Y_CHUNK,)), pltpu.SMEM((1,), jnp.int32)],
    )
    return pl.pallas_call(
        functools.partial(_combine_kernel, cap=cap),
        out_shape=jax.ShapeDtypeStruct((t, D_MODEL), F32),
        grid_spec=grid_spec,
        compiler_params=pltpu.CompilerParams(dimension_semantics=("arbitrary",),
                                             vmem_limit_bytes=VMEM_LIMIT),
        name="combine",
    )(starts, x1, slot_te, mod_rows, final_g.reshape(1, D_MODEL), y)


def kernel(x_prompt, x_sample, state_ret, c, c_ctx, norm1_g, norm2_g, final_g, w_mod, b_mod, w_in,
           w_fmix, w_out, w_router, w_gate, w_up, w_down):
    bp, seq, _ = x_prompt.shape
    bs, dec_seq, _ = x_sample.shape
    assert w_mod.shape[0] == 1, "single-layer trunk"
    tp, ts = bp * seq, bs * dec_seq

    cond = jnp.concatenate([c_ctx[None, :], c], axis=0)
    mod = _modulation(cond, w_mod[0], b_mod[0])
    ctx_row, lat_row = 0, 1

    w_in_bf = w_in[0].astype(BF16)
    w_out_bf = w_out[0].astype(BF16)
    mix_p, states = _mixer(x_prompt, mod, ctx_row, False, None, True, False, norm1_g[0], w_in_bf,
                           w_fmix[0])
    (mix_s,) = _mixer(x_sample, mod, lat_row, True, state_ret[:, 0], False, True, norm1_g[0],
                      w_in_bf, w_fmix[0])
    x1p, h2p, affp = _post(x_prompt.reshape(tp, D_MODEL), mix_p, mod, ctx_row, None,
                           norm2_g[0], w_out_bf, w_router[0])
    x1s, h2s, affs = _post(x_sample.reshape(ts, D_MODEL), mix_s, mod, lat_row, dec_seq,
                           norm2_g[0], w_out_bf, w_router[0])

    slot_p, gate_p, starts_p = _route(affp.T)
    slot_s, gate_s, starts_s = _route(affs.T)

    yp, ys = _experts(starts_p, starts_s, h2p, h2s, slot_p, slot_s, gate_p, gate_s,
                      w_gate[0], w_up[0], w_down[0])

    out_p = _combine(starts_p, x1p, slot_p.reshape(N_EXPERTS, tp).T, mod, ctx_row, None, final_g, yp)
    out_s = _combine(starts_s, x1s, slot_s.reshape(N_EXPERTS, ts).T, mod, lat_row,
                     dec_seq // TOKEN_BLOCK, final_g, ys)

    y_prompt = out_p.reshape(bp, seq, D_MODEL)
    y_sample = out_s.reshape(bs, dec_seq, D_MODEL)
    state_new = states.reshape(bp, 1, 2, N_RET_HEADS, HEAD_DIM, HEAD_DIM).astype(x_prompt.dtype)
    return (y_prompt, y_sample, state_new)
```

```python
import functools
import math

import jax
import jax.numpy as jnp
import numpy as np
from jax import lax
from jax.experimental import pallas as pl
from jax.experimental.pallas import tpu as pltpu

D_MODEL = 1024
D_FOURIER = 512
N_FOURIER_GROUPS = 4
FOURIER_GROUP_W = 128
D_RET = 512
N_RET_HEADS = 4
HEAD_DIM = 128
CHUNK = 256
GRID_W = 64
N_EXPERTS = 16
EC_CAPACITY_FACTOR = 2
D_EXPERT_FF = 2816
ROPE_BASE = 10000.0
EPS = 1e-6
D_IN_PROJ = D_FOURIER + 5 * D_RET
LOG_GAMMA_FWD = np.log(1.0 - 2.0 ** (-5.0 - np.arange(N_RET_HEADS))).astype(np.float32)
LOG_GAMMA_BWD = np.log(1.0 - 2.0 ** (-5.5 - np.arange(N_RET_HEADS))).astype(np.float32)

TOKEN_BLOCK = 256
SLOT_WINDOW = 64
PACK = TOKEN_BLOCK // SLOT_WINDOW
GATHER_UNROLL = 4
Y_CHUNK = 256
FF_TILE = 256
N_FF_TILES = D_EXPERT_FF // FF_TILE
MOD_TILE = 512
MIXER_ROWS = 1024
POST_ROWS = 256
VMEM_LIMIT = 56 * 1024 * 1024

F32 = jnp.float32
BF16 = jnp.bfloat16


def _dot(a, b):
    return jnp.dot(a, b, preferred_element_type=F32)


def _dot_nt(a, b):
    return lax.dot_general(a, b, (((1,), (1,)), ((), ())), preferred_element_type=F32)


def _silu(x):
    return x * jax.nn.sigmoid(x)


def _mod_kernel(condt_ref, w_ref, b_ref, out_ref, *, n_cond):
    s = _silu(condt_ref[...])
    w = w_ref[...]
    out_ref[...] = jnp.zeros(out_ref.shape, F32)
    for r in range(n_cond):
        out_ref[r] = jnp.sum(w * s[:, r:r + 1], axis=0, keepdims=True) + b_ref[...]


def _modulation(cond_rows, w_mod, b_mod):
    n_cond = cond_rows.shape[0]
    condt = jnp.zeros((D_MODEL, 8), F32).at[:, :n_cond].set(cond_rows.T)
    n_out = w_mod.shape[1]
    return pl.pallas_call(
        functools.partial(_mod_kernel, n_cond=n_cond),
        out_shape=jax.ShapeDtypeStruct((8, 1, n_out), F32),
        grid=(n_out // MOD_TILE,),
        in_specs=[
            pl.BlockSpec((D_MODEL, 8), lambda j: (0, 0)),
            pl.BlockSpec((D_MODEL, MOD_TILE), lambda j: (0, j)),
            pl.BlockSpec((1, MOD_TILE), lambda j: (0, j)),
        ],
        out_specs=pl.BlockSpec((8, 1, MOD_TILE), lambda j: (0, 0, j)),
        compiler_params=pltpu.CompilerParams(dimension_semantics=("arbitrary",)),
        name="mod",
    )(condt, w_mod, b_mod.reshape(1, n_out))


def _rms(x):
    return x * lax.rsqrt(jnp.mean(x * x, axis=-1, keepdims=True) + EPS)


def _groupnorm(o):
    mu = jnp.mean(o, axis=-1, keepdims=True)
    c = o - mu
    return c * lax.rsqrt(jnp.mean(c * c, axis=-1, keepdims=True) + EPS)


def _split_hi_lo(x):
    hi = x.astype(BF16)
    lo = (x - hi.astype(F32)).astype(BF16)
    return hi, lo


def _mixer_kernel(*refs, n, use_rope, has_state_in, emit_state):
    it = iter(refs)
    x_ref, mod_ref, g1_ref, win_ref, wfmix_ref = (next(it) for _ in range(5))
    cw_ref, cn_ref, sn_ref, dmat_ref, qdec_ref, kdec_ref, sdec_ref = (next(it) for _ in range(7))
    cos_ref = sin_ref = s0_ref = st_ref = None
    if use_rope:
        cos_ref, sin_ref = next(it), next(it)
    if has_state_in:
        s0_ref = next(it)
    mix_ref = next(it)
    if emit_state:
        st_ref = next(it)
    p_ref, of_ref, ob_ref = next(it), next(it), next(it)

    n_seq = MIXER_ROWS // n
    chunks_per_seq = n // CHUNK
    mod = mod_ref[0]
    shift1 = mod[:, 0:D_MODEL]
    scale1 = mod[:, D_MODEL:2 * D_MODEL]

    h = (_rms(x_ref[...]) * g1_ref[...] * (1.0 + scale1) + shift1).astype(BF16)
    for j in range(D_IN_PROJ // 512):
        p_ref[:, j * 512:(j + 1) * 512] = _dot(h, win_ref[:, j * 512:(j + 1) * 512])

    xf = p_ref[:, 0:D_FOURIER].astype(BF16)
    xc, xs = [], []
    cw = cw_ref[...].astype(BF16)
    for g in range(N_FOURIER_GROUPS):
        t = _dot(xf[:, g * FOURIER_GROUP_W:(g + 1) * FOURIER_GROUP_W], cw)
        xc.append(t[:, :FOURIER_GROUP_W].astype(BF16))
        xs.append(t[:, FOURIER_GROUP_W:].astype(BF16))
    xc = jnp.concatenate(xc, axis=1)
    xs = jnp.concatenate(xs, axis=1)
    cn = cn_ref[...].astype(BF16)
    sn = sn_ref[...].astype(BF16)
    for s in range(n_seq):
        rs = slice(s * n, (s + 1) * n)
        fre = (_dot(cn, xc[rs]) - _dot(sn, xs[rs])) * (1.0 / math.sqrt(n * FOURIER_GROUP_W))
        fre = fre.astype(BF16)
        for g in range(N_FOURIER_GROUPS):
            sl = slice(g * FOURIER_GROUP_W, (g + 1) * FOURIER_GROUP_W)
            mix_ref[rs, sl] = _dot(fre[:, sl], wfmix_ref[g].astype(BF16)).astype(BF16)

    for hh in range(N_RET_HEADS):
        base = D_FOURIER + hh * HEAD_DIM
        q = p_ref[:, base:base + HEAD_DIM]
        k = p_ref[:, base + D_RET:base + D_RET + HEAD_DIM]
        v = p_ref[:, base + 2 * D_RET:base + 2 * D_RET + HEAD_DIM]
        if use_rope:
            lane = lax.broadcasted_iota(jnp.int32, (MIXER_ROWS, HEAD_DIM), 1)
            first = (lane % 64) < 32

            def rope(t):
                swapped = jnp.where(first, pltpu.roll(t, HEAD_DIM - 32, 1), pltpu.roll(t, 32, 1))
                return t * cos_ref[...] + swapped * sin_ref[...]

            q, k = rope(q), rope(k)
        k = k * (HEAD_DIM ** -0.5)
        qb, vb = q.astype(BF16), v.astype(BF16)
        kb = k.astype(BF16)

        def initial(s, direction):
            if has_state_in:
                return s0_ref[s, direction, hh]
            return jnp.zeros((HEAD_DIM, HEAD_DIM), F32)

        for s in range(n_seq):
            parts = []
            for ci in range(chunks_per_seq):
                c = s * chunks_per_seq + ci
                rs = slice(c * CHUNK, (c + 1) * CHUNK)
                qc, kc, vc = qb[rs], kb[rs], vb[rs]
                qk = _dot_nt(qc, kc)
                lhs = jnp.concatenate([(qk * dmat_ref[0, hh]).astype(BF16),
                                       (qk * dmat_ref[1, hh]).astype(BF16),
                                       (k[rs] * kdec_ref[0, hh]).T.astype(BF16),
                                       (k[rs] * kdec_ref[1, hh]).T.astype(BF16)], axis=0)
                parts.append((rs, qc, _dot(lhs, vc)))
            sf = initial(s, 0)
            for ci in range(chunks_per_seq):
                rs, qc, r = parts[ci]
                o = r[0:CHUNK]
                if has_state_in or ci > 0:
                    o = o + qdec_ref[0, hh] * _dot(qc, sf.astype(BF16))
                of_ref[rs, :] = o
                sf = sf * sdec_ref[0, hh] + r[2 * CHUNK:2 * CHUNK + HEAD_DIM]
            sb = initial(s, 1)
            for ci in reversed(range(chunks_per_seq)):
                rs, qc, r = parts[ci]
                o = r[CHUNK:2 * CHUNK]
                if has_state_in or ci < chunks_per_seq - 1:
                    o = o + qdec_ref[1, hh] * _dot(qc, sb.astype(BF16))
                ob_ref[rs, :] = o
                sb = sb * sdec_ref[1, hh] + r[2 * CHUNK + HEAD_DIM:]
            if emit_state:
                st_ref[s, 0, hh] = sf
                st_ref[s, 1, hh] = sb

        gf = p_ref[:, base + 3 * D_RET:base + 3 * D_RET + HEAD_DIM]
        gb = p_ref[:, base + 4 * D_RET:base + 4 * D_RET + HEAD_DIM]
        y = _silu(gf) * _groupnorm(of_ref[...]) + _silu(gb) * _groupnorm(ob_ref[...])
        mix_ref[:, base:base + HEAD_DIM] = y.astype(BF16)


def _post_kernel(x_ref, mix_ref, mod_ref, g2_ref, wout_ref, wr_ref, x1_ref, h2_ref, aff_ref):
    mod = mod_ref[0]
    gate1 = mod[:, 2 * D_MODEL:3 * D_MODEL]
    shift2 = mod[:, 3 * D_MODEL:4 * D_MODEL]
    scale2 = mod[:, 4 * D_MODEL:5 * D_MODEL]
    x1 = x_ref[...] + gate1 * _dot(mix_ref[...], wout_ref[...])
    x1_ref[...] = x1
    h2 = _rms(x1) * g2_ref[...] * (1.0 + scale2) + shift2
    h2_hi, h2_lo = _split_hi_lo(h2)
    h2_ref[...] = h2_hi
    wr_hi, wr_lo = _split_hi_lo(wr_ref[...])
    by_hi = _dot(h2_hi, jnp.concatenate([wr_hi, wr_lo], axis=1))
    logits = by_hi[:, :N_EXPERTS] + (_dot(h2_lo, wr_hi) + by_hi[:, N_EXPERTS:])
    z = jnp.exp(logits - jnp.max(logits, axis=-1, keepdims=True))
    aff_ref[...] = z / jnp.sum(z, axis=-1, keepdims=True)


def _dft_consts(n):
    w = FOURIER_GROUP_W
    jw = np.arange(w)
    angw = 2.0 * np.pi * np.outer(jw, jw) / w
    cw = np.concatenate([np.cos(angw), np.sin(angw)], axis=1)
    jn = np.arange(n)
    angn = 2.0 * np.pi * (np.outer(jn, jn) % n) / n
    return (jnp.asarray(cw, F32), jnp.asarray(np.cos(angn), F32), jnp.asarray(np.sin(angn), F32))


def _retention_consts():
    i = np.arange(CHUNK, dtype=np.float64)
    diff = i[:, None] - i[None, :]
    dmat = np.zeros((2, N_RET_HEADS, CHUNK, CHUNK))
    qdec = np.zeros((2, N_RET_HEADS, CHUNK, HEAD_DIM))
    kdec = np.zeros((2, N_RET_HEADS, CHUNK, HEAD_DIM))
    sdec = np.zeros((2, N_RET_HEADS, HEAD_DIM, HEAD_DIM))
    for hh in range(N_RET_HEADS):
        lf = float(LOG_GAMMA_FWD[hh])
        lb = float(LOG_GAMMA_BWD[hh])
        dmat[0, hh] = np.where(diff >= 0, np.exp(lf * np.maximum(diff, 0.0)), 0.0)
        dmat[1, hh] = np.where(diff <= 0, np.exp(lb * np.maximum(-diff, 0.0)), 0.0)
        qdec[0, hh] = np.exp(lf * (i + 1.0))[:, None]
        qdec[1, hh] = np.exp(lb * (CHUNK - i))[:, None]
        kdec[0, hh] = np.exp(lf * (CHUNK - 1.0 - i))[:, None]
        kdec[1, hh] = np.exp(lb * i)[:, None]
        sdec[0, hh] = math.exp(lf * CHUNK)
        sdec[1, hh] = math.exp(lb * CHUNK)
    return tuple(jnp.asarray(a, F32) for a in (dmat, qdec, kdec, sdec))


def _rope_consts(n):
    rows_n = n // GRID_W
    row = np.repeat(np.arange(rows_n, dtype=np.float64), GRID_W)
    col = np.tile(np.arange(GRID_W, dtype=np.float64), rows_n)
    n_pairs = HEAD_DIM // 4
    freqs = (np.float32(ROPE_BASE) ** (-np.arange(n_pairs, dtype=np.float32) / n_pairs)).astype(np.float64)
    ar = row[:, None] * freqs[None, :]
    ac = col[:, None] * freqs[None, :]
    cos = np.concatenate([np.cos(ar), np.cos(ar), np.cos(ac), np.cos(ac)], axis=1)
    sin = np.concatenate([-np.sin(ar), np.sin(ar), -np.sin(ac), np.sin(ac)], axis=1)
    return jnp.asarray(cos, F32), jnp.asarray(sin, F32)


def _const_spec(shape):
    nd = len(shape)
    return pl.BlockSpec(shape, lambda b, _nd=nd: (0,) * _nd, pipeline_mode=pl.Buffered(1))


def _mod_row_spec(first_row, blocks_per_row):
    if blocks_per_row is None:
        return pl.BlockSpec((1, 1, 6 * D_MODEL), lambda b, *_: (first_row, 0, 0))
    return pl.BlockSpec((1, 1, 6 * D_MODEL), lambda b, *_: (first_row + b // blocks_per_row, 0, 0))


def _mixer(x, mod_rows, mod_first_row, mod_per_batch, state_in, emit_state, use_rope, g1, w_in_bf,
           w_fmix):
    nb, n, _ = x.shape
    assert MIXER_ROWS % n == 0 and (nb * n) % MIXER_ROWS == 0
    n_seq = MIXER_ROWS // n
    has_state_in = state_in is not None
    cw, cn, sn = _dft_consts(n)
    dmat, qdec, kdec, sdec = _retention_consts()
    consts = [cw, cn, sn, dmat, qdec, kdec, sdec]
    if use_rope:
        assert n_seq == 1
        consts += list(_rope_consts(n))
    weights = [g1.reshape(1, D_MODEL), w_in_bf, w_fmix]

    if mod_per_batch:
        assert n % MIXER_ROWS == 0
    state_spec = pl.BlockSpec((n_seq, 2, N_RET_HEADS, HEAD_DIM, HEAD_DIM), lambda b: (b, 0, 0, 0, 0))
    row_spec = pl.BlockSpec((MIXER_ROWS, D_MODEL), lambda b: (b, 0))
    in_specs = [row_spec, _mod_row_spec(mod_first_row, n // MIXER_ROWS if mod_per_batch else None)]
    in_specs += [_const_spec(a.shape) for a in weights + consts]
    args = [x.reshape(nb * n, D_MODEL), mod_rows] + weights + consts
    if has_state_in:
        in_specs.append(state_spec)
        args.append(state_in)

    out_shape = [jax.ShapeDtypeStruct((nb * n, D_MODEL), BF16)]
    out_specs = [row_spec]
    if emit_state:
        out_shape.append(jax.ShapeDtypeStruct((nb, 2, N_RET_HEADS, HEAD_DIM, HEAD_DIM), F32))
        out_specs.append(state_spec)

    return pl.pallas_call(
        functools.partial(_mixer_kernel, n=n, use_rope=use_rope, has_state_in=has_state_in,
                          emit_state=emit_state),
        out_shape=out_shape,
        grid=(nb * n // MIXER_ROWS,),
        in_specs=in_specs,
        out_specs=out_specs,
        scratch_shapes=[pltpu.VMEM((MIXER_ROWS, D_IN_PROJ), F32),
                        pltpu.VMEM((MIXER_ROWS, HEAD_DIM), F32), pltpu.VMEM((MIXER_ROWS, HEAD_DIM), F32)],
        compiler_params=pltpu.CompilerParams(dimension_semantics=("arbitrary",),
                                             vmem_limit_bytes=VMEM_LIMIT),
        name="mixer_rope" if use_rope else "mixer",
    )(*args)


def _post(x, mix, mod_rows, mod_first_row, tokens_per_mod_row, g2, w_out_bf, w_router):
    t = x.shape[0]
    assert tokens_per_mod_row is None or tokens_per_mod_row % POST_ROWS == 0
    blocks_per_row = None if tokens_per_mod_row is None else tokens_per_mod_row // POST_ROWS
    row_spec = pl.BlockSpec((POST_ROWS, D_MODEL), lambda b: (b, 0))
    return pl.pallas_call(
        _post_kernel,
        out_shape=[jax.ShapeDtypeStruct((t, D_MODEL), F32),
                   jax.ShapeDtypeStruct((t, D_MODEL), BF16),
                   jax.ShapeDtypeStruct((t, N_EXPERTS), F32)],
        grid=(t // POST_ROWS,),
        in_specs=[row_spec, row_spec,
                  _mod_row_spec(mod_first_row, blocks_per_row),
                  _const_spec((1, D_MODEL)), _const_spec((D_MODEL, D_MODEL)),
                  _const_spec((D_MODEL, N_EXPERTS))],
        out_specs=[row_spec, row_spec, pl.BlockSpec((POST_ROWS, N_EXPERTS), lambda b: (b, 0))],
        compiler_params=pltpu.CompilerParams(dimension_semantics=("arbitrary",),
                                             vmem_limit_bytes=VMEM_LIMIT),
        name="post",
    )(x, mix, mod_rows, g2.reshape(1, D_MODEL), w_out_bf, w_router)


def _route_kernel(aff_ref, u_ref, slot_ref, gate_ref, starts_ref, *, t, cap):
    aff = aff_ref[...]

    def count(mask):
        return jnp.sum(mask.astype(jnp.int32), axis=1, keepdims=True)

    def as_float(word):
        return lax.bitcast_convert_type(word, F32)

    def value_step(i, cur):
        cand = cur | jnp.left_shift(jnp.int32(1), 30 - i)
        return jnp.where(count(aff >= as_float(cand)) >= cap, cand, cur)

    thr = lax.fori_loop(0, 31, value_step, jnp.zeros((N_EXPERTS, 1), jnp.int32))
    gt = aff >= as_float(thr + 1)
    eq = (aff >= as_float(thr)) & jnp.logical_not(gt)
    need = cap - count(gt)
    tok = lax.broadcasted_iota(jnp.int32, (N_EXPERTS, t), 1)
    nbits = t.bit_length() - 1

    def index_step(i, cur):
        cand = cur | jnp.left_shift(jnp.int32(1), nbits - 1 - i)
        return jnp.where(count(eq & (tok < cand)) < need, cand, cur)

    last = lax.fori_loop(0, nbits, index_step, jnp.zeros((N_EXPERTS, 1), jnp.int32))
    self = jnp.where(gt | (eq & (tok <= last)), 1.0, 0.0).astype(F32)

    carry = jnp.zeros((N_EXPERTS, 1), F32)
    starts_ref[...] = jnp.zeros(starts_ref.shape, jnp.int32)
    for b in range(t // TOKEN_BLOCK):
        sl = slice(b * TOKEN_BLOCK, (b + 1) * TOKEN_BLOCK)
        sbf = self[:, sl]
        pre = _dot(sbf.astype(BF16), u_ref[...]) + carry
        slot_ref[:, b, :] = jnp.where(sbf > 0.5, pre.astype(jnp.int32), -1)
        gate_ref[:, b, :] = aff[:, sl]
        starts_ref[:, b:b + 1] = carry.astype(jnp.int32)
        carry = carry + jnp.sum(sbf, axis=1, keepdims=True)
    nblk = t // TOKEN_BLOCK
    starts_ref[:, nblk:nblk + 1] = carry.astype(jnp.int32)


def _route(aff_et):
    t = aff_et.shape[1]
    cap = EC_CAPACITY_FACTOR * t // N_EXPERTS
    nblk = t // TOKEN_BLOCK
    upper = jnp.asarray(np.triu(np.ones((TOKEN_BLOCK, TOKEN_BLOCK)), 1), BF16)
    assert nblk + 1 <= 128
    blocked = (N_EXPERTS, nblk, TOKEN_BLOCK)
    return pl.pallas_call(
        functools.partial(_route_kernel, t=t, cap=cap),
        out_shape=[jax.ShapeDtypeStruct(blocked, jnp.int32), jax.ShapeDtypeStruct(blocked, F32),
                   jax.ShapeDtypeStruct((N_EXPERTS, 128), jnp.int32)],
        grid=(1,),
        in_specs=[pl.BlockSpec((N_EXPERTS, t), lambda i: (0, 0)),
                  pl.BlockSpec((TOKEN_BLOCK, TOKEN_BLOCK), lambda i: (0, 0))],
        out_specs=[pl.BlockSpec(blocked, lambda i: (0, 0, 0)), pl.BlockSpec(blocked, lambda i: (0, 0, 0)),
                   pl.BlockSpec((N_EXPERTS, 128), lambda i: (0, 0))],
        compiler_params=pltpu.CompilerParams(dimension_semantics=("arbitrary",)),
        name="route",
    )(aff_et, upper)


def _pack_windows(starts_ref, b, experts, cap):
    first = [jnp.minimum((starts_ref[e, b] // 16) * 16, cap - SLOT_WINDOW) for e in experts]
    rows = [jnp.where(starts_ref[e, b + 1] > starts_ref[e, b], starts_ref[e, b + 1] - w, 0)
            for e, w in zip(experts, first)]
    return first, pl.cdiv(functools.reduce(jnp.maximum, rows), SLOT_WINDOW)


def _block_copy(hbm_ref, vmem_ref, sem_ref, b):
    rows = pl.ds(pl.multiple_of(b * TOKEN_BLOCK, TOKEN_BLOCK), TOKEN_BLOCK)
    return pltpu.make_async_copy(hbm_ref.at[rows], vmem_ref.at[rows], sem_ref.at[b])


def _gather_group(g, starts_ref, slot_ref, gate_ref, h2_ref, xs_ref, gs_ref, row0, t, cap, arrive):
    sub = lax.broadcasted_iota(jnp.int32, (SLOT_WINDOW, TOKEN_BLOCK), 0)
    experts = [g * PACK + j for j in range(PACK)]
    assert (t // TOKEN_BLOCK) % GATHER_UNROLL == 0

    def window(b, first, i):
        hb = pl.ds(pl.multiple_of(b * TOKEN_BLOCK, TOKEN_BLOCK), TOKEN_BLOCK)
        hits, dst = [], []
        for j in range(PACK):
            lo = first[j] + i * SLOT_WINDOW
            w = jnp.minimum(lo, cap - SLOT_WINDOW)
            srow = slot_ref[j, pl.ds(b, 1), :]
            hits.append((srow == w + sub) & (srow >= lo))
            dst.append(pl.ds(pl.multiple_of(row0 + w, 16), SLOT_WINDOW))
        onehot = jnp.concatenate([jnp.where(h, 1.0, 0.0) for h in hits], axis=0).astype(BF16)
        got = _dot(onehot, h2_ref[hb, :])
        for j in range(PACK):
            piece = got[j * SLOT_WINDOW:(j + 1) * SLOT_WINDOW].astype(BF16)
            xs_ref[j, dst[j], :] = xs_ref[j, dst[j], :] + piece
            grow = gate_ref[j, pl.ds(b, 1), :]
            gs_ref[j, dst[j], :] += jnp.sum(jnp.where(hits[j], grow, 0.0), axis=1, keepdims=True)

    def blocks(q, carry):
        pending = []
        for u in range(GATHER_UNROLL):
            arrive(q * GATHER_UNROLL + u)
        for u in range(GATHER_UNROLL):
            b = q * GATHER_UNROLL + u
            first, n_windows = _pack_windows(starts_ref, b, experts, cap)
            window(b, first, 0)
            pending.append((b, first, n_windows))
        for b, first, n_windows in pending:
            def more(i, carry, b=b, first=first):
                window(b, first, i)
                return carry

            lax.fori_loop(1, n_windows, more, 0)
        return carry

    lax.fori_loop(0, t // TOKEN_BLOCK // GATHER_UNROLL, blocks, 0)


def _experts_kernel(sp_ref, ss_ref, h2p_hbm, h2s_hbm, slotp_ref, slots_ref, gatep_ref, gates_ref,
                    wg_ref, wu_ref, wd_ref, yp_ref, ys_ref, xs_ref, gs_ref, acc_ref,
                    h2p_ref, h2s_ref, semp_ref, sems_ref, *, tp, ts, capp, caps):
    g = pl.program_id(0)
    step = pl.program_id(1)
    j = step // N_FF_TILES
    f = step % N_FF_TILES
    loading = [(h2p_hbm, h2p_ref, semp_ref, tp // TOKEN_BLOCK),
               (h2s_hbm, h2s_ref, sems_ref, ts // TOKEN_BLOCK)]

    @pl.when((step == 0) & (g == 0))
    def _():
        for hbm_ref, vmem_ref, sem_ref, n_blocks in loading:
            for b in range(n_blocks):
                _block_copy(hbm_ref, vmem_ref, sem_ref, b).start()

    def arrive(hbm_ref, vmem_ref, sem_ref, _):
        def wait(b):
            @pl.when(g == 0)
            def _():
                _block_copy(hbm_ref, vmem_ref, sem_ref, b).wait()

        return wait

    @pl.when(step == 0)
    def _():
        xs_ref[...] = jnp.zeros(xs_ref.shape, BF16)
        gs_ref[...] = jnp.zeros(gs_ref.shape, F32)
        _gather_group(g, sp_ref, slotp_ref, gatep_ref, h2p_ref, xs_ref, gs_ref, 0, tp, capp,
                      arrive(*loading[0]))
        _gather_group(g, ss_ref, slots_ref, gates_ref, h2s_ref, xs_ref, gs_ref, capp, ts, caps,
                      arrive(*loading[1]))

    def tile():
        x = xs_ref[j]
        a = _dot(x, wg_ref[0].astype(BF16))
        u = _dot(x, wu_ref[0].astype(BF16))
        return _dot((_silu(a) * u).astype(BF16), wd_ref[0].astype(BF16))

    @pl.when(f == 0)
    def _():
        acc_ref[...] = tile()

    @pl.when((f > 0) & (f < N_FF_TILES - 1))
    def _():
        acc_ref[...] += tile()

    @pl.when(f == N_FF_TILES - 1)
    def _():
        y = (acc_ref[...] + tile()) * gs_ref[j]
        yp_ref[0] = y[0:capp].astype(BF16)
        ys_ref[0] = y[capp:capp + caps].astype(BF16)


def _experts(starts_p, starts_s, h2p, h2s, slot_p, slot_s, gate_p, gate_s, w_gate, w_up, w_down):
    tp, ts = h2p.shape[0], h2s.shape[0]
    capp = EC_CAPACITY_FACTOR * tp // N_EXPERTS
    caps = EC_CAPACITY_FACTOR * ts // N_EXPERTS
    rows = capp + caps
    nbp, nbs = tp // TOKEN_BLOCK, ts // TOKEN_BLOCK
    expert = lambda g, s: g * PACK + s // N_FF_TILES
    grid_spec = pltpu.PrefetchScalarGridSpec(
        num_scalar_prefetch=2,
        grid=(N_EXPERTS // PACK, PACK * N_FF_TILES),
        in_specs=[
            pl.BlockSpec(memory_space=pl.ANY),
            pl.BlockSpec(memory_space=pl.ANY),
            pl.BlockSpec((PACK, nbp, TOKEN_BLOCK), lambda g, s, *_: (g, 0, 0)),
            pl.BlockSpec((PACK, nbs, TOKEN_BLOCK), lambda g, s, *_: (g, 0, 0)),
            pl.BlockSpec((PACK, nbp, TOKEN_BLOCK), lambda g, s, *_: (g, 0, 0)),
            pl.BlockSpec((PACK, nbs, TOKEN_BLOCK), lambda g, s, *_: (g, 0, 0)),
            pl.BlockSpec((1, D_MODEL, FF_TILE), lambda g, s, *_: (expert(g, s), 0, s % N_FF_TILES)),
            pl.BlockSpec((1, D_MODEL, FF_TILE), lambda g, s, *_: (expert(g, s), 0, s % N_FF_TILES)),
            pl.BlockSpec((1, FF_TILE, D_MODEL), lambda g, s, *_: (expert(g, s), s % N_FF_TILES, 0)),
        ],
        out_specs=[
            pl.BlockSpec((1, capp, D_MODEL), lambda g, s, *_: (expert(g, s), 0, 0)),
            pl.BlockSpec((1, caps, D_MODEL), lambda g, s, *_: (expert(g, s), 0, 0)),
        ],
        scratch_shapes=[pltpu.VMEM((PACK, rows, D_MODEL), BF16), pltpu.VMEM((PACK, rows, 1), F32),
                        pltpu.VMEM((rows, D_MODEL), F32),
                        pltpu.VMEM((tp, D_MODEL), BF16), pltpu.VMEM((ts, D_MODEL), BF16),
                        pltpu.SemaphoreType.DMA((nbp,)), pltpu.SemaphoreType.DMA((nbs,))],
    )
    return pl.pallas_call(
        functools.partial(_experts_kernel, tp=tp, ts=ts, capp=capp, caps=caps),
        out_shape=[jax.ShapeDtypeStruct((N_EXPERTS, capp, D_MODEL), BF16),
                   jax.ShapeDtypeStruct((N_EXPERTS, caps, D_MODEL), BF16)],
        grid_spec=grid_spec,
        compiler_params=pltpu.CompilerParams(dimension_semantics=("arbitrary", "arbitrary"),
                                             vmem_limit_bytes=VMEM_LIMIT),
        name="experts",
    )(starts_p, starts_s, h2p, h2s, slot_p, slot_s, gate_p, gate_s, w_gate, w_up, w_down)


def _combine_kernel(st_ref, x1_ref, slot_ref, mod_ref, fg_ref, y_hbm, out_ref, acc_ref, y_ref,
                    sem_ref, arrived_ref, *, cap):
    b = pl.program_id(0)
    n_chunks = cap // Y_CHUNK

    def chunk_copy(c):
        rows = pl.ds(c * Y_CHUNK, Y_CHUNK)
        return pltpu.make_async_copy(y_hbm.at[:, rows, :], y_ref.at[:, rows, :], sem_ref.at[c])

    @pl.when(b == 0)
    def _():
        arrived_ref[0] = 0
        for c in range(n_chunks):
            chunk_copy(c).start()

    top = functools.reduce(jnp.maximum, [st_ref[e, b + 1] for e in range(N_EXPERTS)])
    want = pl.cdiv(jnp.minimum(top + SLOT_WINDOW, cap), Y_CHUNK)
    have = arrived_ref[0]
    for c in range(n_chunks):
        @pl.when((c >= have) & (c < want))
        def _(c=c):
            chunk_copy(c).wait()
    arrived_ref[0] = jnp.maximum(have, want)

    slot = slot_ref[...]
    lane = lax.broadcasted_iota(jnp.int32, (TOKEN_BLOCK, PACK * SLOT_WINDOW), 1)

    def scatter(experts, first, i):
        target = None
        windows = []
        for j, e in enumerate(experts):
            lo = first[j] + i * SLOT_WINDOW
            w = jnp.minimum(lo, cap - SLOT_WINDOW)
            sc = slot[:, e:e + 1]
            col = jnp.where(sc >= lo, sc - w + j * SLOT_WINDOW, -1)
            target = col if target is None else jnp.where(lane < j * SLOT_WINDOW, target, col)
            windows.append(y_ref[e, pl.ds(pl.multiple_of(w, 16), SLOT_WINDOW), :])
        onehot = jnp.where(target == lane, 1.0, 0.0).astype(BF16)
        return _dot(onehot, jnp.concatenate(windows, axis=0))

    groups = []
    total = None
    for g in range(N_EXPERTS // PACK):
        experts = list(range(g * PACK, (g + 1) * PACK))
        first, n_windows = _pack_windows(st_ref, b, experts, cap)
        groups.append((experts, first, n_windows))
        part = scatter(experts, first, 0)
        total = part if total is None else total + part
    acc_ref[...] = total

    for experts, first, n_windows in groups:
        def more(i, carry, experts=experts, first=first):
            acc_ref[...] += scatter(experts, first, i)
            return carry

        lax.fori_loop(1, n_windows, more, 0)

    gate2 = mod_ref[0][:, 5 * D_MODEL:6 * D_MODEL]
    x2 = x1_ref[...] + gate2 * acc_ref[...]
    out_ref[...] = _rms(x2) * fg_ref[...]


def _combine(starts, x1, slot_te, mod_rows, mod_first_row, blocks_per_mod_row, final_g, y):
    t = x1.shape[0]
    cap = y.shape[1]
    nblk = t // TOKEN_BLOCK
    grid_spec = pltpu.PrefetchScalarGridSpec(
        num_scalar_prefetch=1,
        grid=(nblk,),
        in_specs=[
            pl.BlockSpec((TOKEN_BLOCK, D_MODEL), lambda b, *_: (b, 0)),
            pl.BlockSpec((TOKEN_BLOCK, N_EXPERTS), lambda b, *_: (b, 0)),
            _mod_row_spec(mod_first_row, blocks_per_mod_row),
            pl.BlockSpec((1, D_MODEL), lambda b, *_: (0, 0)),
            pl.BlockSpec(memory_space=pl.ANY),
        ],
        out_specs=pl.BlockSpec((TOKEN_BLOCK, D_MODEL), lambda b, *_: (b, 0)),
        scratch_shapes=[pltpu.VMEM((TOKEN_BLOCK, D_MODEL), F32), pltpu.VMEM(y.shape, BF16),
                        pltpu.SemaphoreType.DMA((cap // Y_CHUNK,)), pltpu.SMEM((1,), jnp.int32)],
    )
    return pl.pallas_call(
        functools.partial(_combine_kernel, cap=cap),
        out_shape=jax.ShapeDtypeStruct((t, D_MODEL), F32),
        grid_spec=grid_spec,
        compiler_params=pltpu.CompilerParams(dimension_semantics=("arbitrary",),
                                             vmem_limit_bytes=VMEM_LIMIT),
        name="combine",
    )(starts, x1, slot_te, mod_rows, final_g.reshape(1, D_MODEL), y)


def kernel(x_prompt, x_sample, state_ret, c, c_ctx, norm1_g, norm2_g, final_g, w_mod, b_mod, w_in,
           w_fmix, w_out, w_router, w_gate, w_up, w_down):
    bp, seq, _ = x_prompt.shape
    bs, dec_seq, _ = x_sample.shape
    assert w_mod.shape[0] == 1, "single-layer trunk"
    tp, ts = bp * seq, bs * dec_seq

    cond = jnp.concatenate([c_ctx[None, :], c], axis=0)
    mod = _modulation(cond, w_mod[0], b_mod[0])
    ctx_row, lat_row = 0, 1

    w_in_bf = w_in[0].astype(BF16)
    w_out_bf = w_out[0].astype(BF16)
    mix_p, states = _mixer(x_prompt, mod, ctx_row, False, None, True, False, norm1_g[0], w_in_bf,
                           w_fmix[0])
    (mix_s,) = _mixer(x_sample, mod, lat_row, True, state_ret[:, 0], False, True, norm1_g[0],
                      w_in_bf, w_fmix[0])
    x1p, h2p, affp = _post(x_prompt.reshape(tp, D_MODEL), mix_p, mod, ctx_row, None,
                           norm2_g[0], w_out_bf, w_router[0])
    x1s, h2s, affs = _post(x_sample.reshape(ts, D_MODEL), mix_s, mod, lat_row, dec_seq,
                           norm2_g[0], w_out_bf, w_router[0])

    slot_p, gate_p, starts_p = _route(affp.T)
    slot_s, gate_s, starts_s = _route(affs.T)

    yp, ys = _experts(starts_p, starts_s, h2p, h2s, slot_p, slot_s, gate_p, gate_s,
                      w_gate[0], w_up[0], w_down[0])

    out_p = _combine(starts_p, x1p, slot_p.reshape(N_EXPERTS, tp).T, mod, ctx_row, None, final_g, yp)
    out_s = _combine(starts_s, x1s, slot_s.reshape(N_EXPERTS, ts).T, mod, lat_row,
                     dec_seq // TOKEN_BLOCK, final_g, ys)

    y_prompt = out_p.reshape(bp, seq, D_MODEL)
    y_sample = out_s.reshape(bs, dec_seq, D_MODEL)
    state_new = states.reshape(bp, 1, 2, N_RET_HEADS, HEAD_DIM, HEAD_DIM).astype(x_prompt.dtype)
    return (y_prompt, y_sample, state_new)
```

```python
import functools
import math

import jax
import jax.numpy as jnp
import numpy as np
from jax import lax
from jax.experimental import pallas as pl
from jax.experimental.pallas import tpu as pltpu

D_MODEL = 1024
D_FOURIER = 512
N_FOURIER_GROUPS = 4
FOURIER_GROUP_W = 128
D_RET = 512
N_RET_HEADS = 4
HEAD_DIM = 128
CHUNK = 256
GRID_W = 64
N_EXPERTS = 16
EC_CAPACITY_FACTOR = 2
D_EXPERT_FF = 2816
ROPE_BASE = 10000.0
EPS = 1e-6
D_IN_PROJ = D_FOURIER + 5 * D_RET
LOG_GAMMA_FWD = np.log(1.0 - 2.0 ** (-5.0 - np.arange(N_RET_HEADS))).astype(np.float32)
LOG_GAMMA_BWD = np.log(1.0 - 2.0 ** (-5.5 - np.arange(N_RET_HEADS))).astype(np.float32)

TOKEN_BLOCK = 256
SLOT_WINDOW = 64
PACK = TOKEN_BLOCK // SLOT_WINDOW
GATHER_UNROLL = 4
Y_CHUNK = 256
FF_TILE = 256
N_FF_TILES = D_EXPERT_FF // FF_TILE
MOD_TILE = 512
MIXER_ROWS = 1024
POST_ROWS = 256
VMEM_LIMIT = 56 * 1024 * 1024

F32 = jnp.float32
BF16 = jnp.bfloat16


def _dot(a, b):
    return jnp.dot(a, b, preferred_element_type=F32)


def _dot_nt(a, b):
    return lax.dot_general(a, b, (((1,), (1,)), ((), ())), preferred_element_type=F32)


def _silu(x):
    return x * jax.nn.sigmoid(x)


def _mod_kernel(condt_ref, w_ref, b_ref, out_ref, *, n_cond):
    s = _silu(condt_ref[...])
    w = w_ref[...]
    out_ref[...] = jnp.zeros(out_ref.shape, F32)
    for r in range(n_cond):
        out_ref[r] = jnp.sum(w * s[:, r:r + 1], axis=0, keepdims=True) + b_ref[...]


def _modulation(cond_rows, w_mod, b_mod):
    n_cond = cond_rows.shape[0]
    condt = jnp.zeros((D_MODEL, 8), F32).at[:, :n_cond].set(cond_rows.T)
    n_out = w_mod.shape[1]
    return pl.pallas_call(
        functools.partial(_mod_kernel, n_cond=n_cond),
        out_shape=jax.ShapeDtypeStruct((8, 1, n_out), F32),
        grid=(n_out // MOD_TILE,),
        in_specs=[
            pl.BlockSpec((D_MODEL, 8), lambda j: (0, 0)),
            pl.BlockSpec((D_MODEL, MOD_TILE), lambda j: (0, j)),
            pl.BlockSpec((1, MOD_TILE), lambda j: (0, j)),
        ],
        out_specs=pl.BlockSpec((8, 1, MOD_TILE), lambda j: (0, 0, j)),
        compiler_params=pltpu.CompilerParams(dimension_semantics=("arbitrary",)),
        name="mod",
    )(condt, w_mod, b_mod.reshape(1, n_out))


def _rms(x):
    return x * lax.rsqrt(jnp.mean(x * x, axis=-1, keepdims=True) + EPS)


def _groupnorm(o):
    mu = jnp.mean(o, axis=-1, keepdims=True)
    c = o - mu
    return c * lax.rsqrt(jnp.mean(c * c, axis=-1, keepdims=True) + EPS)


def _split_hi_lo(x):
    hi = x.astype(BF16)
    lo = (x - hi.astype(F32)).astype(BF16)
    return hi, lo


def _mixer_kernel(*refs, n, use_rope, has_state_in, emit_state):
    it = iter(refs)
    x_ref, mod_ref, g1_ref, win_ref, wfmix_ref = (next(it) for _ in range(5))
    cw_ref, cn_ref, sn_ref, dmat_ref, qdec_ref, kdec_ref, sdec_ref = (next(it) for _ in range(7))
    cos_ref = sin_ref = s0_ref = st_ref = None
    if use_rope:
        cos_ref, sin_ref = next(it), next(it)
    if has_state_in:
        s0_ref = next(it)
    mix_ref = next(it)
    if emit_state:
        st_ref = next(it)
    p_ref, of_ref, ob_ref = next(it), next(it), next(it)

    n_seq = MIXER_ROWS // n
    chunks_per_seq = n // CHUNK
    mod = mod_ref[0]
    shift1 = mod[:, 0:D_MODEL]
    scale1 = mod[:, D_MODEL:2 * D_MODEL]

    h = (_rms(x_ref[...]) * g1_ref[...] * (1.0 + scale1) + shift1).astype(BF16)
    for j in range(D_IN_PROJ // 512):
        p_ref[:, j * 512:(j + 1) * 512] = _dot(h, win_ref[:, j * 512:(j + 1) * 512])

    xf = p_ref[:, 0:D_FOURIER].astype(BF16)
    xc, xs = [], []
    cw = cw_ref[...].astype(BF16)
    for g in range(N_FOURIER_GROUPS):
        t = _dot(xf[:, g * FOURIER_GROUP_W:(g + 1) * FOURIER_GROUP_W], cw)
        xc.append(t[:, :FOURIER_GROUP_W].astype(BF16))
        xs.append(t[:, FOURIER_GROUP_W:].astype(BF16))
    xc = jnp.concatenate(xc, axis=1)
    xs = jnp.concatenate(xs, axis=1)
    cn = cn_ref[...].astype(BF16)
    sn = sn_ref[...].astype(BF16)
    for s in range(n_seq):
        rs = slice(s * n, (s + 1) * n)
        fre = (_dot(cn, xc[rs]) - _dot(sn, xs[rs])) * (1.0 / math.sqrt(n * FOURIER_GROUP_W))
        fre = fre.astype(BF16)
        for g in range(N_FOURIER_GROUPS):
            sl = slice(g * FOURIER_GROUP_W, (g + 1) * FOURIER_GROUP_W)
            mix_ref[rs, sl] = _dot(fre[:, sl], wfmix_ref[g].astype(BF16)).astype(BF16)

    for hh in range(N_RET_HEADS):
        base = D_FOURIER + hh * HEAD_DIM
        q = p_ref[:, base:base + HEAD_DIM]
        k = p_ref[:, base + D_RET:base + D_RET + HEAD_DIM]
        v = p_ref[:, base + 2 * D_RET:base + 2 * D_RET + HEAD_DIM]
        if use_rope:
            lane = lax.broadcasted_iota(jnp.int32, (MIXER_ROWS, HEAD_DIM), 1)
            first = (lane % 64) < 32

            def rope(t):
                swapped = jnp.where(first, pltpu.roll(t, HEAD_DIM - 32, 1), pltpu.roll(t, 32, 1))
                return t * cos_ref[...] + swapped * sin_ref[...]

            q, k = rope(q), rope(k)
        k = k * (HEAD_DIM ** -0.5)
        qb, vb = q.astype(BF16), v.astype(BF16)
        kb = k.astype(BF16)

        def initial(s, direction):
            if has_state_in:
                return s0_ref[s, direction, hh]
            return jnp.zeros((HEAD_DIM, HEAD_DIM), F32)

        for s in range(n_seq):
            parts = []
            for ci in range(chunks_per_seq):
                c = s * chunks_per_seq + ci
                rs = slice(c * CHUNK, (c + 1) * CHUNK)
                qc, kc, vc = qb[rs], kb[rs], vb[rs]
                qk = _dot_nt(qc, kc)
                lhs = jnp.concatenate([(qk * dmat_ref[0, hh]).astype(BF16),
                                       (qk * dmat_ref[1, hh]).astype(BF16),
                                       (k[rs] * kdec_ref[0, hh]).T.astype(BF16),
                                       (k[rs] * kdec_ref[1, hh]).T.astype(BF16)], axis=0)
                parts.append((rs, qc, _dot(lhs, vc)))
            sf = initial(s, 0)
            for ci in range(chunks_per_seq):
                rs, qc, r = parts[ci]
                o = r[0:CHUNK]
                if has_state_in or ci > 0:
                    o = o + qdec_ref[0, hh] * _dot(qc, sf.astype(BF16))
                of_ref[rs, :] = o
                sf = sf * sdec_ref[0, hh] + r[2 * CHUNK:2 * CHUNK + HEAD_DIM]
            sb = initial(s, 1)
            for ci in reversed(range(chunks_per_seq)):
                rs, qc, r = parts[ci]
                o = r[CHUNK:2 * CHUNK]
                if has_state_in or ci < chunks_per_seq - 1:
                    o = o + qdec_ref[1, hh] * _dot(qc, sb.astype(BF16))
                ob_ref[rs, :] = o
                sb = sb * sdec_ref[1, hh] + r[2 * CHUNK + HEAD_DIM:]
            if emit_state:
                st_ref[s, 0, hh] = sf
                st_ref[s, 1, hh] = sb

        gf = p_ref[:, base + 3 * D_RET:base + 3 * D_RET + HEAD_DIM]
        gb = p_ref[:, base + 4 * D_RET:base + 4 * D_RET + HEAD_DIM]
        y = _silu(gf) * _groupnorm(of_ref[...]) + _silu(gb) * _groupnorm(ob_ref[...])
        mix_ref[:, base:base + HEAD_DIM] = y.astype(BF16)


def _residual1(x_ref, mix_ref, mod, wout_ref):
    gate1 = mod[:, 2 * D_MODEL:3 * D_MODEL]
    return x_ref[...] + gate1 * _dot(mix_ref[...], wout_ref[...])


def _post_kernel(x_ref, mix_ref, mod_ref, g2_ref, wout_ref, wr_ref, h2_ref, aff_ref):
    mod = mod_ref[0]
    shift2 = mod[:, 3 * D_MODEL:4 * D_MODEL]
    scale2 = mod[:, 4 * D_MODEL:5 * D_MODEL]
    x1 = _residual1(x_ref, mix_ref, mod, wout_ref)
    h2 = _rms(x1) * g2_ref[...] * (1.0 + scale2) + shift2
    h2_hi, h2_lo = _split_hi_lo(h2)
    h2_ref[...] = h2_hi
    wr_hi, wr_lo = _split_hi_lo(wr_ref[...])
    by_hi = _dot(h2_hi, jnp.concatenate([wr_hi, wr_lo], axis=1))
    logits = by_hi[:, :N_EXPERTS] + (_dot(h2_lo, wr_hi) + by_hi[:, N_EXPERTS:])
    z = jnp.exp(logits - jnp.max(logits, axis=-1, keepdims=True))
    aff_ref[...] = z / jnp.sum(z, axis=-1, keepdims=True)


def _dft_consts(n):
    w = FOURIER_GROUP_W
    jw = np.arange(w)
    angw = 2.0 * np.pi * np.outer(jw, jw) / w
    cw = np.concatenate([np.cos(angw), np.sin(angw)], axis=1)
    jn = np.arange(n)
    angn = 2.0 * np.pi * (np.outer(jn, jn) % n) / n
    return (jnp.asarray(cw, F32), jnp.asarray(np.cos(angn), F32), jnp.asarray(np.sin(angn), F32))


def _retention_consts():
    i = np.arange(CHUNK, dtype=np.float64)
    diff = i[:, None] - i[None, :]
    dmat = np.zeros((2, N_RET_HEADS, CHUNK, CHUNK))
    qdec = np.zeros((2, N_RET_HEADS, CHUNK, HEAD_DIM))
    kdec = np.zeros((2, N_RET_HEADS, CHUNK, HEAD_DIM))
    sdec = np.zeros((2, N_RET_HEADS, HEAD_DIM, HEAD_DIM))
    for hh in range(N_RET_HEADS):
        lf = float(LOG_GAMMA_FWD[hh])
        lb = float(LOG_GAMMA_BWD[hh])
        dmat[0, hh] = np.where(diff >= 0, np.exp(lf * np.maximum(diff, 0.0)), 0.0)
        dmat[1, hh] = np.where(diff <= 0, np.exp(lb * np.maximum(-diff, 0.0)), 0.0)
        qdec[0, hh] = np.exp(lf * (i + 1.0))[:, None]
        qdec[1, hh] = np.exp(lb * (CHUNK - i))[:, None]
        kdec[0, hh] = np.exp(lf * (CHUNK - 1.0 - i))[:, None]
        kdec[1, hh] = np.exp(lb * i)[:, None]
        sdec[0, hh] = math.exp(lf * CHUNK)
        sdec[1, hh] = math.exp(lb * CHUNK)
    return tuple(jnp.asarray(a, F32) for a in (dmat, qdec, kdec, sdec))


def _rope_consts(n):
    rows_n = n // GRID_W
    row = np.repeat(np.arange(rows_n, dtype=np.float64), GRID_W)
    col = np.tile(np.arange(GRID_W, dtype=np.float64), rows_n)
    n_pairs = HEAD_DIM // 4
    freqs = (np.float32(ROPE_BASE) ** (-np.arange(n_pairs, dtype=np.float32) / n_pairs)).astype(np.float64)
    ar = row[:, None] * freqs[None, :]
    ac = col[:, None] * freqs[None, :]
    cos = np.concatenate([np.cos(ar), np.cos(ar), np.cos(ac), np.cos(ac)], axis=1)
    sin = np.concatenate([-np.sin(ar), np.sin(ar), -np.sin(ac), np.sin(ac)], axis=1)
    return jnp.asarray(cos, F32), jnp.asarray(sin, F32)


def _const_spec(shape):
    nd = len(shape)
    return pl.BlockSpec(shape, lambda b, _nd=nd: (0,) * _nd, pipeline_mode=pl.Buffered(1))


def _mod_row_spec(first_row, blocks_per_row):
    if blocks_per_row is None:
        return pl.BlockSpec((1, 1, 6 * D_MODEL), lambda b, *_: (first_row, 0, 0))
    return pl.BlockSpec((1, 1, 6 * D_MODEL), lambda b, *_: (first_row + b // blocks_per_row, 0, 0))


def _mixer(x, mod_rows, mod_first_row, mod_per_batch, state_in, emit_state, use_rope, g1, w_in_bf,
           w_fmix):
    nb, n, _ = x.shape
    assert MIXER_ROWS % n == 0 and (nb * n) % MIXER_ROWS == 0
    n_seq = MIXER_ROWS // n
    has_state_in = state_in is not None
    cw, cn, sn = _dft_consts(n)
    dmat, qdec, kdec, sdec = _retention_consts()
    consts = [cw, cn, sn, dmat, qdec, kdec, sdec]
    if use_rope:
        assert n_seq == 1
        consts += list(_rope_consts(n))
    weights = [g1.reshape(1, D_MODEL), w_in_bf, w_fmix]

    if mod_per_batch:
        assert n % MIXER_ROWS == 0
    state_spec = pl.BlockSpec((n_seq, 2, N_RET_HEADS, HEAD_DIM, HEAD_DIM), lambda b: (b, 0, 0, 0, 0))
    row_spec = pl.BlockSpec((MIXER_ROWS, D_MODEL), lambda b: (b, 0))
    in_specs = [row_spec, _mod_row_spec(mod_first_row, n // MIXER_ROWS if mod_per_batch else None)]
    in_specs += [_const_spec(a.shape) for a in weights + consts]
    args = [x.reshape(nb * n, D_MODEL), mod_rows] + weights + consts
    if has_state_in:
        in_specs.append(state_spec)
        args.append(state_in)

    out_shape = [jax.ShapeDtypeStruct((nb * n, D_MODEL), BF16)]
    out_specs = [row_spec]
    if emit_state:
        out_shape.append(jax.ShapeDtypeStruct((nb, 2, N_RET_HEADS, HEAD_DIM, HEAD_DIM), F32))
        out_specs.append(state_spec)

    return pl.pallas_call(
        functools.partial(_mixer_kernel, n=n, use_rope=use_rope, has_state_in=has_state_in,
                          emit_state=emit_state),
        out_shape=out_shape,
        grid=(nb * n // MIXER_ROWS,),
        in_specs=in_specs,
        out_specs=out_specs,
        scratch_shapes=[pltpu.VMEM((MIXER_ROWS, D_IN_PROJ), F32),
                        pltpu.VMEM((MIXER_ROWS, HEAD_DIM), F32), pltpu.VMEM((MIXER_ROWS, HEAD_DIM), F32)],
        compiler_params=pltpu.CompilerParams(dimension_semantics=("arbitrary",),
                                             vmem_limit_bytes=VMEM_LIMIT),
        name="mixer_rope" if use_rope else "mixer",
    )(*args)


def _post(x, mix, mod_rows, mod_first_row, tokens_per_mod_row, g2, w_out_bf, w_router):
    t = x.shape[0]
    assert tokens_per_mod_row is None or tokens_per_mod_row % POST_ROWS == 0
    blocks_per_row = None if tokens_per_mod_row is None else tokens_per_mod_row // POST_ROWS
    row_spec = pl.BlockSpec((POST_ROWS, D_MODEL), lambda b: (b, 0))
    return pl.pallas_call(
        _post_kernel,
        out_shape=[jax.ShapeDtypeStruct((t, D_MODEL), BF16),
                   jax.ShapeDtypeStruct((t, N_EXPERTS), F32)],
        grid=(t // POST_ROWS,),
        in_specs=[row_spec, row_spec,
                  _mod_row_spec(mod_first_row, blocks_per_row),
                  _const_spec((1, D_MODEL)), _const_spec((D_MODEL, D_MODEL)),
                  _const_spec((D_MODEL, N_EXPERTS))],
        out_specs=[row_spec, pl.BlockSpec((POST_ROWS, N_EXPERTS), lambda b: (b, 0))],
        compiler_params=pltpu.CompilerParams(dimension_semantics=("arbitrary",),
                                             vmem_limit_bytes=VMEM_LIMIT),
        name="post",
    )(x, mix, mod_rows, g2.reshape(1, D_MODEL), w_out_bf, w_router)


def _route_kernel(aff_ref, u_ref, slot_ref, gate_ref, starts_ref, *, t, cap):
    aff = aff_ref[...]

    def count(mask):
        return jnp.sum(mask.astype(jnp.int32), axis=1, keepdims=True)

    def as_float(word):
        return lax.bitcast_convert_type(word, F32)

    def value_step(i, cur):
        cand = cur | jnp.left_shift(jnp.int32(1), 30 - i)
        return jnp.where(count(aff >= as_float(cand)) >= cap, cand, cur)

    thr = lax.fori_loop(0, 31, value_step, jnp.zeros((N_EXPERTS, 1), jnp.int32))
    gt = aff >= as_float(thr + 1)
    eq = (aff >= as_float(thr)) & jnp.logical_not(gt)
    need = cap - count(gt)
    tok = lax.broadcasted_iota(jnp.int32, (N_EXPERTS, t), 1)
    nbits = t.bit_length() - 1

    def index_step(i, cur):
        cand = cur | jnp.left_shift(jnp.int32(1), nbits - 1 - i)
        return jnp.where(count(eq & (tok < cand)) < need, cand, cur)

    last = lax.fori_loop(0, nbits, index_step, jnp.zeros((N_EXPERTS, 1), jnp.int32))
    self = jnp.where(gt | (eq & (tok <= last)), 1.0, 0.0).astype(F32)

    carry = jnp.zeros((N_EXPERTS, 1), F32)
    starts_ref[...] = jnp.zeros(starts_ref.shape, jnp.int32)
    for b in range(t // TOKEN_BLOCK):
        sl = slice(b * TOKEN_BLOCK, (b + 1) * TOKEN_BLOCK)
        sbf = self[:, sl]
        pre = _dot(sbf.astype(BF16), u_ref[...]) + carry
        slot_ref[:, b, :] = jnp.where(sbf > 0.5, pre.astype(jnp.int32), -1)
        gate_ref[:, b, :] = aff[:, sl]
        starts_ref[:, b:b + 1] = carry.astype(jnp.int32)
        carry = carry + jnp.sum(sbf, axis=1, keepdims=True)
    nblk = t // TOKEN_BLOCK
    starts_ref[:, nblk:nblk + 1] = carry.astype(jnp.int32)


def _route(aff_et):
    t = aff_et.shape[1]
    cap = EC_CAPACITY_FACTOR * t // N_EXPERTS
    nblk = t // TOKEN_BLOCK
    upper = jnp.asarray(np.triu(np.ones((TOKEN_BLOCK, TOKEN_BLOCK)), 1), BF16)
    assert nblk + 1 <= 128
    blocked = (N_EXPERTS, nblk, TOKEN_BLOCK)
    return pl.pallas_call(
        functools.partial(_route_kernel, t=t, cap=cap),
        out_shape=[jax.ShapeDtypeStruct(blocked, jnp.int32), jax.ShapeDtypeStruct(blocked, F32),
                   jax.ShapeDtypeStruct((N_EXPERTS, 128), jnp.int32)],
        grid=(1,),
        in_specs=[pl.BlockSpec((N_EXPERTS, t), lambda i: (0, 0)),
                  pl.BlockSpec((TOKEN_BLOCK, TOKEN_BLOCK), lambda i: (0, 0))],
        out_specs=[pl.BlockSpec(blocked, lambda i: (0, 0, 0)), pl.BlockSpec(blocked, lambda i: (0, 0, 0)),
                   pl.BlockSpec((N_EXPERTS, 128), lambda i: (0, 0))],
        compiler_params=pltpu.CompilerParams(dimension_semantics=("arbitrary",)),
        name="route",
    )(aff_et, upper)


def _pack_windows(starts_ref, b, experts, cap):
    first = [jnp.minimum((starts_ref[e, b] // 16) * 16, cap - SLOT_WINDOW) for e in experts]
    rows = [jnp.where(starts_ref[e, b + 1] > starts_ref[e, b], starts_ref[e, b + 1] - w, 0)
            for e, w in zip(experts, first)]
    return first, pl.cdiv(functools.reduce(jnp.maximum, rows), SLOT_WINDOW)


def _block_copy(hbm_ref, vmem_ref, sem_ref, b):
    rows = pl.ds(pl.multiple_of(b * TOKEN_BLOCK, TOKEN_BLOCK), TOKEN_BLOCK)
    return pltpu.make_async_copy(hbm_ref.at[rows], vmem_ref.at[rows], sem_ref.at[b])


def _gather_group(g, starts_ref, slot_ref, gate_ref, h2_ref, xs_ref, gs_ref, row0, t, cap, arrive):
    sub = lax.broadcasted_iota(jnp.int32, (SLOT_WINDOW, TOKEN_BLOCK), 0)
    experts = [g * PACK + j for j in range(PACK)]
    assert (t // TOKEN_BLOCK) % GATHER_UNROLL == 0

    def window(b, first, i):
        hb = pl.ds(pl.multiple_of(b * TOKEN_BLOCK, TOKEN_BLOCK), TOKEN_BLOCK)
        hits, dst = [], []
        for j in range(PACK):
            lo = first[j] + i * SLOT_WINDOW
            w = jnp.minimum(lo, cap - SLOT_WINDOW)
            srow = slot_ref[j, pl.ds(b, 1), :]
            hits.append((srow == w + sub) & (srow >= lo))
            dst.append(pl.ds(pl.multiple_of(row0 + w, 16), SLOT_WINDOW))
        onehot = jnp.concatenate([jnp.where(h, 1.0, 0.0) for h in hits], axis=0).astype(BF16)
        got = _dot(onehot, h2_ref[hb, :])
        for j in range(PACK):
            piece = got[j * SLOT_WINDOW:(j + 1) * SLOT_WINDOW].astype(BF16)
            xs_ref[j, dst[j], :] = xs_ref[j, dst[j], :] + piece
            grow = gate_ref[j, pl.ds(b, 1), :]
            gs_ref[j, dst[j], :] += jnp.sum(jnp.where(hits[j], grow, 0.0), axis=1, keepdims=True)

    def blocks(q, carry):
        pending = []
        for u in range(GATHER_UNROLL):
            arrive(q * GATHER_UNROLL + u)
        for u in range(GATHER_UNROLL):
            b = q * GATHER_UNROLL + u
            first, n_windows = _pack_windows(starts_ref, b, experts, cap)
            window(b, first, 0)
            pending.append((b, first, n_windows))
        for b, first, n_windows in pending:
            def more(i, carry, b=b, first=first):
                window(b, first, i)
                return carry

            lax.fori_loop(1, n_windows, more, 0)
        return carry

    lax.fori_loop(0, t // TOKEN_BLOCK // GATHER_UNROLL, blocks, 0)


def _experts_kernel(sp_ref, ss_ref, h2p_hbm, h2s_hbm, slotp_ref, slots_ref, gatep_ref, gates_ref,
                    wg_ref, wu_ref, wd_ref, yp_ref, ys_ref, xs_ref, gs_ref, acc_ref,
                    h2p_ref, h2s_ref, semp_ref, sems_ref, *, tp, ts, capp, caps):
    g = pl.program_id(0)
    step = pl.program_id(1)
    j = step // N_FF_TILES
    f = step % N_FF_TILES
    loading = [(h2p_hbm, h2p_ref, semp_ref, tp // TOKEN_BLOCK),
               (h2s_hbm, h2s_ref, sems_ref, ts // TOKEN_BLOCK)]

    @pl.when((step == 0) & (g == 0))
    def _():
        for hbm_ref, vmem_ref, sem_ref, n_blocks in loading:
            for b in range(n_blocks):
                _block_copy(hbm_ref, vmem_ref, sem_ref, b).start()

    def arrive(hbm_ref, vmem_ref, sem_ref, _):
        def wait(b):
            @pl.when(g == 0)
            def _():
                _block_copy(hbm_ref, vmem_ref, sem_ref, b).wait()

        return wait

    @pl.when(step == 0)
    def _():
        xs_ref[...] = jnp.zeros(xs_ref.shape, BF16)
        gs_ref[...] = jnp.zeros(gs_ref.shape, F32)
        _gather_group(g, sp_ref, slotp_ref, gatep_ref, h2p_ref, xs_ref, gs_ref, 0, tp, capp,
                      arrive(*loading[0]))
        _gather_group(g, ss_ref, slots_ref, gates_ref, h2s_ref, xs_ref, gs_ref, capp, ts, caps,
                      arrive(*loading[1]))

    def tile():
        x = xs_ref[j]
        a = _dot(x, wg_ref[0].astype(BF16))
        u = _dot(x, wu_ref[0].astype(BF16))
        return _dot((_silu(a) * u).astype(BF16), wd_ref[0].astype(BF16))

    @pl.when(f == 0)
    def _():
        acc_ref[...] = tile()

    @pl.when((f > 0) & (f < N_FF_TILES - 1))
    def _():
        acc_ref[...] += tile()

    @pl.when(f == N_FF_TILES - 1)
    def _():
        y = (acc_ref[...] + tile()) * gs_ref[j]
        yp_ref[0] = y[0:capp].astype(BF16)
        ys_ref[0] = y[capp:capp + caps].astype(BF16)


def _experts(starts_p, starts_s, h2p, h2s, slot_p, slot_s, gate_p, gate_s, w_gate, w_up, w_down):
    tp, ts = h2p.shape[0], h2s.shape[0]
    capp = EC_CAPACITY_FACTOR * tp // N_EXPERTS
    caps = EC_CAPACITY_FACTOR * ts // N_EXPERTS
    rows = capp + caps
    nbp, nbs = tp // TOKEN_BLOCK, ts // TOKEN_BLOCK
    expert = lambda g, s: g * PACK + s // N_FF_TILES
    grid_spec = pltpu.PrefetchScalarGridSpec(
        num_scalar_prefetch=2,
        grid=(N_EXPERTS // PACK, PACK * N_FF_TILES),
        in_specs=[
            pl.BlockSpec(memory_space=pl.ANY),
            pl.BlockSpec(memory_space=pl.ANY),
            pl.BlockSpec((PACK, nbp, TOKEN_BLOCK), lambda g, s, *_: (g, 0, 0)),
            pl.BlockSpec((PACK, nbs, TOKEN_BLOCK), lambda g, s, *_: (g, 0, 0)),
            pl.BlockSpec((PACK, nbp, TOKEN_BLOCK), lambda g, s, *_: (g, 0, 0)),
            pl.BlockSpec((PACK, nbs, TOKEN_BLOCK), lambda g, s, *_: (g, 0, 0)),
            pl.BlockSpec((1, D_MODEL, FF_TILE), lambda g, s, *_: (expert(g, s), 0, s % N_FF_TILES)),
            pl.BlockSpec((1, D_MODEL, FF_TILE), lambda g, s, *_: (expert(g, s), 0, s % N_FF_TILES)),
            pl.BlockSpec((1, FF_TILE, D_MODEL), lambda g, s, *_: (expert(g, s), s % N_FF_TILES, 0)),
        ],
        out_specs=[
            pl.BlockSpec((1, capp, D_MODEL), lambda g, s, *_: (expert(g, s), 0, 0)),
            pl.BlockSpec((1, caps, D_MODEL), lambda g, s, *_: (expert(g, s), 0, 0)),
        ],
        scratch_shapes=[pltpu.VMEM((PACK, rows, D_MODEL), BF16), pltpu.VMEM((PACK, rows, 1), F32),
                        pltpu.VMEM((rows, D_MODEL), F32),
                        pltpu.VMEM((tp, D_MODEL), BF16), pltpu.VMEM((ts, D_MODEL), BF16),
                        pltpu.SemaphoreType.DMA((nbp,)), pltpu.SemaphoreType.DMA((nbs,))],
    )
    return pl.pallas_call(
        functools.partial(_experts_kernel, tp=tp, ts=ts, capp=capp, caps=caps),
        out_shape=[jax.ShapeDtypeStruct((N_EXPERTS, capp, D_MODEL), BF16),
                   jax.ShapeDtypeStruct((N_EXPERTS, caps, D_MODEL), BF16)],
        grid_spec=grid_spec,
        compiler_params=pltpu.CompilerParams(dimension_semantics=("arbitrary", "arbitrary"),
                                             vmem_limit_bytes=VMEM_LIMIT),
        name="experts",
    )(starts_p, starts_s, h2p, h2s, slot_p, slot_s, gate_p, gate_s, w_gate, w_up, w_down)


def _combine_kernel(st_ref, x_ref, mix_ref, slot_ref, mod_ref, fg_ref, wout_ref, y_hbm, out_ref,
                    acc_ref, y_ref, sem_ref, arrived_ref, *, cap):
    b = pl.program_id(0)
    n_chunks = cap // Y_CHUNK

    def chunk_copy(c):
        rows = pl.ds(c * Y_CHUNK, Y_CHUNK)
        return pltpu.make_async_copy(y_hbm.at[:, rows, :], y_ref.at[:, rows, :], sem_ref.at[c])

    @pl.when(b == 0)
    def _():
        arrived_ref[0] = 0
        for c in range(n_chunks):
            chunk_copy(c).start()

    top = functools.reduce(jnp.maximum, [st_ref[e, b + 1] for e in range(N_EXPERTS)])
    want = pl.cdiv(jnp.minimum(top + SLOT_WINDOW, cap), Y_CHUNK)
    have = arrived_ref[0]
    for c in range(n_chunks):
        @pl.when((c >= have) & (c < want))
        def _(c=c):
            chunk_copy(c).wait()
    arrived_ref[0] = jnp.maximum(have, want)

    slot = slot_ref[...]
    lane = lax.broadcasted_iota(jnp.int32, (TOKEN_BLOCK, PACK * SLOT_WINDOW), 1)

    def scatter(experts, first, i):
        target = None
        windows = []
        for j, e in enumerate(experts):
            lo = first[j] + i * SLOT_WINDOW
            w = jnp.minimum(lo, cap - SLOT_WINDOW)
            sc = slot[:, e:e + 1]
            col = jnp.where(sc >= lo, sc - w + j * SLOT_WINDOW, -1)
            target = col if target is None else jnp.where(lane < j * SLOT_WINDOW, target, col)
            windows.append(y_ref[e, pl.ds(pl.multiple_of(w, 16), SLOT_WINDOW), :])
        onehot = jnp.where(target == lane, 1.0, 0.0).astype(BF16)
        return _dot(onehot, jnp.concatenate(windows, axis=0))

    groups = []
    total = None
    for g in range(N_EXPERTS // PACK):
        experts = list(range(g * PACK, (g + 1) * PACK))
        first, n_windows = _pack_windows(st_ref, b, experts, cap)
        groups.append((experts, first, n_windows))
        part = scatter(experts, first, 0)
        total = part if total is None else total + part
    acc_ref[...] = total

    for experts, first, n_windows in groups:
        def more(i, carry, experts=experts, first=first):
            acc_ref[...] += scatter(experts, first, i)
            return carry

        lax.fori_loop(1, n_windows, more, 0)

    mod = mod_ref[0]
    gate2 = mod[:, 5 * D_MODEL:6 * D_MODEL]
    x2 = _residual1(x_ref, mix_ref, mod, wout_ref) + gate2 * acc_ref[...]
    out_ref[...] = _rms(x2) * fg_ref[...]


def _combine(starts, x, mix, slot_te, mod_rows, mod_first_row, blocks_per_mod_row, final_g,
             w_out_bf, y):
    t = x.shape[0]
    cap = y.shape[1]
    nblk = t // TOKEN_BLOCK
    row_spec = pl.BlockSpec((TOKEN_BLOCK, D_MODEL), lambda b, *_: (b, 0))
    grid_spec = pltpu.PrefetchScalarGridSpec(
        num_scalar_prefetch=1,
        grid=(nblk,),
        in_specs=[
            row_spec, row_spec,
            pl.BlockSpec((TOKEN_BLOCK, N_EXPERTS), lambda b, *_: (b, 0)),
            _mod_row_spec(mod_first_row, blocks_per_mod_row),
            pl.BlockSpec((1, D_MODEL), lambda b, *_: (0, 0)),
            pl.BlockSpec((D_MODEL, D_MODEL), lambda b, *_: (0, 0), pipeline_mode=pl.Buffered(1)),
            pl.BlockSpec(memory_space=pl.ANY),
        ],
        out_specs=pl.BlockSpec((TOKEN_BLOCK, D_MODEL), lambda b, *_: (b, 0)),
        scratch_shapes=[pltpu.VMEM((TOKEN_BLOCK, D_MODEL), F32), pltpu.VMEM(y.shape, BF16),
                        pltpu.SemaphoreType.DMA((cap // Y_CHUNK,)), pltpu.SMEM((1,), jnp.int32)],
    )
    return pl.pallas_call(
        functools.partial(_combine_kernel, cap=cap),
        out_shape=jax.ShapeDtypeStruct((t, D_MODEL), F32),
        grid_spec=grid_spec,
        compiler_params=pltpu.CompilerParams(dimension_semantics=("arbitrary",),
                                             vmem_limit_bytes=VMEM_LIMIT),
        name="combine",
    )(starts, x, mix, slot_te, mod_rows, final_g.reshape(1, D_MODEL), w_out_bf, y)


def kernel(x_prompt, x_sample, state_ret, c, c_ctx, norm1_g, norm2_g, final_g, w_mod, b_mod, w_in,
           w_fmix, w_out, w_router, w_gate, w_up, w_down):
    bp, seq, _ = x_prompt.shape
    bs, dec_seq, _ = x_sample.shape
    assert w_mod.shape[0] == 1, "single-layer trunk"
    tp, ts = bp * seq, bs * dec_seq

    cond = jnp.concatenate([c_ctx[None, :], c], axis=0)
    mod = _modulation(cond, w_mod[0], b_mod[0])
    ctx_row, lat_row = 0, 1

    w_in_bf = w_in[0].astype(BF16)
    w_out_bf = w_out[0].astype(BF16)
    mix_p, states = _mixer(x_prompt, mod, ctx_row, False, None, True, False, norm1_g[0], w_in_bf,
                           w_fmix[0])
    (mix_s,) = _mixer(x_sample, mod, lat_row, True, state_ret[:, 0], False, True, norm1_g[0],
                      w_in_bf, w_fmix[0])
    xp, xs = x_prompt.reshape(tp, D_MODEL), x_sample.reshape(ts, D_MODEL)
    h2p, affp = _post(xp, mix_p, mod, ctx_row, None, norm2_g[0], w_out_bf, w_router[0])
    h2s, affs = _post(xs, mix_s, mod, lat_row, dec_seq, norm2_g[0], w_out_bf, w_router[0])

    slot_p, gate_p, starts_p = _route(affp.T)
    slot_s, gate_s, starts_s = _route(affs.T)

    yp, ys = _experts(starts_p, starts_s, h2p, h2s, slot_p, slot_s, gate_p, gate_s,
                      w_gate[0], w_up[0], w_down[0])

    out_p = _combine(starts_p, xp, mix_p, slot_p.reshape(N_EXPERTS, tp).T, mod, ctx_row, None,
                     final_g, w_out_bf, yp)
    out_s = _combine(starts_s, xs, mix_s, slot_s.reshape(N_EXPERTS, ts).T, mod, lat_row,
                     dec_seq // TOKEN_BLOCK, final_g, w_out_bf, ys)

    y_prompt = out_p.reshape(bp, seq, D_MODEL)
    y_sample = out_s.reshape(bs, dec_seq, D_MODEL)
    state_new = states.reshape(bp, 1, 2, N_RET_HEADS, HEAD_DIM, HEAD_DIM).astype(x_prompt.dtype)
    return (y_prompt, y_sample, state_new)
```

```python
import functools
import math

import jax
import jax.numpy as jnp
import numpy as np
from jax import lax
from jax.experimental import pallas as pl
from jax.experimental.pallas import tpu as pltpu

D_MODEL = 1024
D_FOURIER = 512
N_FOURIER_GROUPS = 4
FOURIER_GROUP_W = 128
D_RET = 512
N_RET_HEADS = 4
HEAD_DIM = 128
CHUNK = 256
GRID_W = 64
N_EXPERTS = 16
EC_CAPACITY_FACTOR = 2
D_EXPERT_FF = 2816
ROPE_BASE = 10000.0
EPS = 1e-6
D_IN_PROJ = D_FOURIER + 5 * D_RET
LOG_GAMMA_FWD = np.log(1.0 - 2.0 ** (-5.0 - np.arange(N_RET_HEADS))).astype(np.float32)
LOG_GAMMA_BWD = np.log(1.0 - 2.0 ** (-5.5 - np.arange(N_RET_HEADS))).astype(np.float32)

TOKEN_BLOCK = 256
SLOT_WINDOW = 64
PACK = TOKEN_BLOCK // SLOT_WINDOW
GATHER_UNROLL = 4
Y_CHUNK = 256
FF_TILE = 256
N_FF_TILES = D_EXPERT_FF // FF_TILE
MOD_TILE = 512
MIXER_ROWS = 1024
POST_ROWS = 512
POST_PART = 256
VMEM_LIMIT = 56 * 1024 * 1024

F32 = jnp.float32
BF16 = jnp.bfloat16


def _dot(a, b):
    return jnp.dot(a, b, preferred_element_type=F32)


def _dot_nt(a, b):
    return lax.dot_general(a, b, (((1,), (1,)), ((), ())), preferred_element_type=F32)


def _silu(x):
    return x * jax.nn.sigmoid(x)


def _mod_kernel(condt_ref, w_ref, b_ref, out_ref, *, n_cond):
    s = _silu(condt_ref[...])
    w = w_ref[...]
    out_ref[...] = jnp.zeros(out_ref.shape, F32)
    for r in range(n_cond):
        out_ref[r] = jnp.sum(w * s[:, r:r + 1], axis=0, keepdims=True) + b_ref[...]


def _modulation(cond_rows, w_mod, b_mod):
    n_cond = cond_rows.shape[0]
    condt = jnp.zeros((D_MODEL, 8), F32).at[:, :n_cond].set(cond_rows.T)
    n_out = w_mod.shape[1]
    return pl.pallas_call(
        functools.partial(_mod_kernel, n_cond=n_cond),
        out_shape=jax.ShapeDtypeStruct((8, 1, n_out), F32),
        grid=(n_out // MOD_TILE,),
        in_specs=[
            pl.BlockSpec((D_MODEL, 8), lambda j: (0, 0)),
            pl.BlockSpec((D_MODEL, MOD_TILE), lambda j: (0, j)),
            pl.BlockSpec((1, MOD_TILE), lambda j: (0, j)),
        ],
        out_specs=pl.BlockSpec((8, 1, MOD_TILE), lambda j: (0, 0, j)),
        compiler_params=pltpu.CompilerParams(dimension_semantics=("arbitrary",)),
        name="mod",
    )(condt, w_mod, b_mod.reshape(1, n_out))


def _rms(x):
    return x * lax.rsqrt(jnp.mean(x * x, axis=-1, keepdims=True) + EPS)


def _groupnorm(o):
    mu = jnp.mean(o, axis=-1, keepdims=True)
    c = o - mu
    return c * lax.rsqrt(jnp.mean(c * c, axis=-1, keepdims=True) + EPS)


def _split_hi_lo(x):
    hi = x.astype(BF16)
    lo = (x - hi.astype(F32)).astype(BF16)
    return hi, lo


def _mixer_kernel(*refs, n, use_rope, has_state_in, emit_state):
    it = iter(refs)
    x_ref, mod_ref, g1_ref, win_ref, wfmix_ref = (next(it) for _ in range(5))
    cw_ref, cn_ref, sn_ref, dmat_ref, qdec_ref, kdec_ref, sdec_ref = (next(it) for _ in range(7))
    cos_ref = sin_ref = s0_ref = st_ref = None
    if use_rope:
        cos_ref, sin_ref = next(it), next(it)
    if has_state_in:
        s0_ref = next(it)
    mix_ref = next(it)
    if emit_state:
        st_ref = next(it)
    p_ref, of_ref, ob_ref = next(it), next(it), next(it)

    n_seq = MIXER_ROWS // n
    chunks_per_seq = n // CHUNK
    mod = mod_ref[0]
    shift1 = mod[:, 0:D_MODEL]
    scale1 = mod[:, D_MODEL:2 * D_MODEL]

    h = (_rms(x_ref[...]) * g1_ref[...] * (1.0 + scale1) + shift1).astype(BF16)
    for j in range(D_IN_PROJ // 512):
        p_ref[:, j * 512:(j + 1) * 512] = _dot(h, win_ref[:, j * 512:(j + 1) * 512])

    xf = p_ref[:, 0:D_FOURIER].astype(BF16)
    xc, xs = [], []
    cw = cw_ref[...].astype(BF16)
    for g in range(N_FOURIER_GROUPS):
        t = _dot(xf[:, g * FOURIER_GROUP_W:(g + 1) * FOURIER_GROUP_W], cw)
        xc.append(t[:, :FOURIER_GROUP_W].astype(BF16))
        xs.append(t[:, FOURIER_GROUP_W:].astype(BF16))
    xc = jnp.concatenate(xc, axis=1)
    xs = jnp.concatenate(xs, axis=1)
    cn = cn_ref[...].astype(BF16)
    sn = sn_ref[...].astype(BF16)
    for s in range(n_seq):
        rs = slice(s * n, (s + 1) * n)
        fre = (_dot(cn, xc[rs]) - _dot(sn, xs[rs])) * (1.0 / math.sqrt(n * FOURIER_GROUP_W))
        fre = fre.astype(BF16)
        for g in range(N_FOURIER_GROUPS):
            sl = slice(g * FOURIER_GROUP_W, (g + 1) * FOURIER_GROUP_W)
            mix_ref[rs, sl] = _dot(fre[:, sl], wfmix_ref[g].astype(BF16)).astype(BF16)

    for hh in range(N_RET_HEADS):
        base = D_FOURIER + hh * HEAD_DIM
        q = p_ref[:, base:base + HEAD_DIM]
        k = p_ref[:, base + D_RET:base + D_RET + HEAD_DIM]
        v = p_ref[:, base + 2 * D_RET:base + 2 * D_RET + HEAD_DIM]
        if use_rope:
            lane = lax.broadcasted_iota(jnp.int32, (MIXER_ROWS, HEAD_DIM), 1)
            first = (lane % 64) < 32

            def rope(t):
                swapped = jnp.where(first, pltpu.roll(t, HEAD_DIM - 32, 1), pltpu.roll(t, 32, 1))
                return t * cos_ref[...] + swapped * sin_ref[...]

            q, k = rope(q), rope(k)
        k = k * (HEAD_DIM ** -0.5)
        qb, vb = q.astype(BF16), v.astype(BF16)
        kb = k.astype(BF16)

        def initial(s, direction):
            if has_state_in:
                return s0_ref[s, direction, hh]
            return jnp.zeros((HEAD_DIM, HEAD_DIM), F32)

        for s in range(n_seq):
            parts = []
            for ci in range(chunks_per_seq):
                c = s * chunks_per_seq + ci
                rs = slice(c * CHUNK, (c + 1) * CHUNK)
                qc, kc, vc = qb[rs], kb[rs], vb[rs]
                qk = _dot_nt(qc, kc)
                lhs = jnp.concatenate([(qk * dmat_ref[0, hh]).astype(BF16),
                                       (qk * dmat_ref[1, hh]).astype(BF16),
                                       (k[rs] * kdec_ref[0, hh]).T.astype(BF16),
                                       (k[rs] * kdec_ref[1, hh]).T.astype(BF16)], axis=0)
                parts.append((rs, qc, _dot(lhs, vc)))
            sf = initial(s, 0)
            for ci in range(chunks_per_seq):
                rs, qc, r = parts[ci]
                o = r[0:CHUNK]
                if has_state_in or ci > 0:
                    o = o + qdec_ref[0, hh] * _dot(qc, sf.astype(BF16))
                of_ref[rs, :] = o
                sf = sf * sdec_ref[0, hh] + r[2 * CHUNK:2 * CHUNK + HEAD_DIM]
            sb = initial(s, 1)
            for ci in reversed(range(chunks_per_seq)):
                rs, qc, r = parts[ci]
                o = r[CHUNK:2 * CHUNK]
                if has_state_in or ci < chunks_per_seq - 1:
                    o = o + qdec_ref[1, hh] * _dot(qc, sb.astype(BF16))
                ob_ref[rs, :] = o
                sb = sb * sdec_ref[1, hh] + r[2 * CHUNK + HEAD_DIM:]
            if emit_state:
                st_ref[s, 0, hh] = sf
                st_ref[s, 1, hh] = sb

        gf = p_ref[:, base + 3 * D_RET:base + 3 * D_RET + HEAD_DIM]
        gb = p_ref[:, base + 4 * D_RET:base + 4 * D_RET + HEAD_DIM]
        y = _silu(gf) * _groupnorm(of_ref[...]) + _silu(gb) * _groupnorm(ob_ref[...])
        mix_ref[:, base:base + HEAD_DIM] = y.astype(BF16)


def _post_kernel(x_ref, mix_ref, mod_ref, g2_ref, wout_ref, wr_ref, x1_ref, h2_ref, aff_ref):
    mod = mod_ref[0]
    gate1 = mod[:, 2 * D_MODEL:3 * D_MODEL]
    shift2 = mod[:, 3 * D_MODEL:4 * D_MODEL]
    scale2 = mod[:, 4 * D_MODEL:5 * D_MODEL]
    wr_hi, wr_lo = _split_hi_lo(wr_ref[...])
    wr_both = jnp.concatenate([wr_hi, wr_lo], axis=1)
    parts = [slice(p * POST_PART, (p + 1) * POST_PART) for p in range(POST_ROWS // POST_PART)]
    x1 = []
    for rows in parts:
        x1.append(x_ref[rows, :] + gate1 * _dot(mix_ref[rows, :], wout_ref[...]))
        x1_ref[rows, :] = x1[-1]
    for rows, x1_part in zip(parts, x1):
        h2 = _rms(x1_part) * g2_ref[...] * (1.0 + scale2) + shift2
        h2_hi, h2_lo = _split_hi_lo(h2)
        h2_ref[rows, :] = h2_hi
        by_hi = _dot(h2_hi, wr_both)
        logits = by_hi[:, :N_EXPERTS] + (_dot(h2_lo, wr_hi) + by_hi[:, N_EXPERTS:])
        z = jnp.exp(logits - jnp.max(logits, axis=-1, keepdims=True))
        aff_ref[rows, :] = z / jnp.sum(z, axis=-1, keepdims=True)


def _dft_consts(n):
    w = FOURIER_GROUP_W
    jw = np.arange(w)
    angw = 2.0 * np.pi * np.outer(jw, jw) / w
    cw = np.concatenate([np.cos(angw), np.sin(angw)], axis=1)
    jn = np.arange(n)
    angn = 2.0 * np.pi * (np.outer(jn, jn) % n) / n
    return (jnp.asarray(cw, F32), jnp.asarray(np.cos(angn), F32), jnp.asarray(np.sin(angn), F32))


def _retention_consts():
    i = np.arange(CHUNK, dtype=np.float64)
    diff = i[:, None] - i[None, :]
    dmat = np.zeros((2, N_RET_HEADS, CHUNK, CHUNK))
    qdec = np.zeros((2, N_RET_HEADS, CHUNK, HEAD_DIM))
    kdec = np.zeros((2, N_RET_HEADS, CHUNK, HEAD_DIM))
    sdec = np.zeros((2, N_RET_HEADS, HEAD_DIM, HEAD_DIM))
    for hh in range(N_RET_HEADS):
        lf = float(LOG_GAMMA_FWD[hh])
        lb = float(LOG_GAMMA_BWD[hh])
        dmat[0, hh] = np.where(diff >= 0, np.exp(lf * np.maximum(diff, 0.0)), 0.0)
        dmat[1, hh] = np.where(diff <= 0, np.exp(lb * np.maximum(-diff, 0.0)), 0.0)
        qdec[0, hh] = np.exp(lf * (i + 1.0))[:, None]
        qdec[1, hh] = np.exp(lb * (CHUNK - i))[:, None]
        kdec[0, hh] = np.exp(lf * (CHUNK - 1.0 - i))[:, None]
        kdec[1, hh] = np.exp(lb * i)[:, None]
        sdec[0, hh] = math.exp(lf * CHUNK)
        sdec[1, hh] = math.exp(lb * CHUNK)
    return tuple(jnp.asarray(a, F32) for a in (dmat, qdec, kdec, sdec))


def _rope_consts(n):
    rows_n = n // GRID_W
    row = np.repeat(np.arange(rows_n, dtype=np.float64), GRID_W)
    col = np.tile(np.arange(GRID_W, dtype=np.float64), rows_n)
    n_pairs = HEAD_DIM // 4
    freqs = (np.float32(ROPE_BASE) ** (-np.arange(n_pairs, dtype=np.float32) / n_pairs)).astype(np.float64)
    ar = row[:, None] * freqs[None, :]
    ac = col[:, None] * freqs[None, :]
    cos = np.concatenate([np.cos(ar), np.cos(ar), np.cos(ac), np.cos(ac)], axis=1)
    sin = np.concatenate([-np.sin(ar), np.sin(ar), -np.sin(ac), np.sin(ac)], axis=1)
    return jnp.asarray(cos, F32), jnp.asarray(sin, F32)


def _const_spec(shape):
    nd = len(shape)
    return pl.BlockSpec(shape, lambda b, _nd=nd: (0,) * _nd, pipeline_mode=pl.Buffered(1))


def _mod_row_spec(first_row, blocks_per_row):
    if blocks_per_row is None:
        return pl.BlockSpec((1, 1, 6 * D_MODEL), lambda b, *_: (first_row, 0, 0))
    return pl.BlockSpec((1, 1, 6 * D_MODEL), lambda b, *_: (first_row + b // blocks_per_row, 0, 0))


def _mixer(x, mod_rows, mod_first_row, mod_per_batch, state_in, emit_state, use_rope, g1, w_in_bf,
           w_fmix):
    nb, n, _ = x.shape
    assert MIXER_ROWS % n == 0 and (nb * n) % MIXER_ROWS == 0
    n_seq = MIXER_ROWS // n
    has_state_in = state_in is not None
    cw, cn, sn = _dft_consts(n)
    dmat, qdec, kdec, sdec = _retention_consts()
    consts = [cw, cn, sn, dmat, qdec, kdec, sdec]
    if use_rope:
        assert n_seq == 1
        consts += list(_rope_consts(n))
    weights = [g1.reshape(1, D_MODEL), w_in_bf, w_fmix]

    if mod_per_batch:
        assert n % MIXER_ROWS == 0
    state_spec = pl.BlockSpec((n_seq, 2, N_RET_HEADS, HEAD_DIM, HEAD_DIM), lambda b: (b, 0, 0, 0, 0))
    row_spec = pl.BlockSpec((MIXER_ROWS, D_MODEL), lambda b: (b, 0))
    in_specs = [row_spec, _mod_row_spec(mod_first_row, n // MIXER_ROWS if mod_per_batch else None)]
    in_specs += [_const_spec(a.shape) for a in weights + consts]
    args = [x.reshape(nb * n, D_MODEL), mod_rows] + weights + consts
    if has_state_in:
        in_specs.append(state_spec)
        args.append(state_in)

    out_shape = [jax.ShapeDtypeStruct((nb * n, D_MODEL), BF16)]
    out_specs = [row_spec]
    if emit_state:
        out_shape.append(jax.ShapeDtypeStruct((nb, 2, N_RET_HEADS, HEAD_DIM, HEAD_DIM), F32))
        out_specs.append(state_spec)

    return pl.pallas_call(
        functools.partial(_mixer_kernel, n=n, use_rope=use_rope, has_state_in=has_state_in,
                          emit_state=emit_state),
        out_shape=out_shape,
        grid=(nb * n // MIXER_ROWS,),
        in_specs=in_specs,
        out_specs=out_specs,
        scratch_shapes=[pltpu.VMEM((MIXER_ROWS, D_IN_PROJ), F32),
                        pltpu.VMEM((MIXER_ROWS, HEAD_DIM), F32), pltpu.VMEM((MIXER_ROWS, HEAD_DIM), F32)],
        compiler_params=pltpu.CompilerParams(dimension_semantics=("arbitrary",),
                                             vmem_limit_bytes=VMEM_LIMIT),
        name="mixer_rope" if use_rope else "mixer",
    )(*args)


def _post(x, mix, mod_rows, mod_first_row, tokens_per_mod_row, g2, w_out_bf, w_router):
    t = x.shape[0]
    assert tokens_per_mod_row is None or tokens_per_mod_row % POST_ROWS == 0
    blocks_per_row = None if tokens_per_mod_row is None else tokens_per_mod_row // POST_ROWS
    row_spec = pl.BlockSpec((POST_ROWS, D_MODEL), lambda b: (b, 0))
    return pl.pallas_call(
        _post_kernel,
        out_shape=[jax.ShapeDtypeStruct((t, D_MODEL), F32),
                   jax.ShapeDtypeStruct((t, D_MODEL), BF16),
                   jax.ShapeDtypeStruct((t, N_EXPERTS), F32)],
        grid=(t // POST_ROWS,),
        in_specs=[row_spec, row_spec,
                  _mod_row_spec(mod_first_row, blocks_per_row),
                  _const_spec((1, D_MODEL)), _const_spec((D_MODEL, D_MODEL)),
                  _const_spec((D_MODEL, N_EXPERTS))],
        out_specs=[row_spec, row_spec, pl.BlockSpec((POST_ROWS, N_EXPERTS), lambda b: (b, 0))],
        compiler_params=pltpu.CompilerParams(dimension_semantics=("arbitrary",),
                                             vmem_limit_bytes=VMEM_LIMIT),
        name="post",
    )(x, mix, mod_rows, g2.reshape(1, D_MODEL), w_out_bf, w_router)


def _route_kernel(aff_ref, u_ref, slot_ref, gate_ref, starts_ref, *, t, cap):
    aff = aff_ref[...]

    def count(mask):
        return jnp.sum(mask.astype(jnp.int32), axis=1, keepdims=True)

    def as_float(word):
        return lax.bitcast_convert_type(word, F32)

    def value_step(i, cur):
        cand = cur | jnp.left_shift(jnp.int32(1), 30 - i)
        return jnp.where(count(aff >= as_float(cand)) >= cap, cand, cur)

    thr = lax.fori_loop(0, 31, value_step, jnp.zeros((N_EXPERTS, 1), jnp.int32))
    gt = aff >= as_float(thr + 1)
    eq = (aff >= as_float(thr)) & jnp.logical_not(gt)
    need = cap - count(gt)
    tok = lax.broadcasted_iota(jnp.int32, (N_EXPERTS, t), 1)
    nbits = t.bit_length() - 1

    def index_step(i, cur):
        cand = cur | jnp.left_shift(jnp.int32(1), nbits - 1 - i)
        return jnp.where(count(eq & (tok < cand)) < need, cand, cur)

    last = lax.fori_loop(0, nbits, index_step, jnp.zeros((N_EXPERTS, 1), jnp.int32))
    self = jnp.where(gt | (eq & (tok <= last)), 1.0, 0.0).astype(F32)

    carry = jnp.zeros((N_EXPERTS, 1), F32)
    starts_ref[...] = jnp.zeros(starts_ref.shape, jnp.int32)
    for b in range(t // TOKEN_BLOCK):
        sl = slice(b * TOKEN_BLOCK, (b + 1) * TOKEN_BLOCK)
        sbf = self[:, sl]
        pre = _dot(sbf.astype(BF16), u_ref[...]) + carry
        slot_ref[:, b, :] = jnp.where(sbf > 0.5, pre.astype(jnp.int32), -1)
        gate_ref[:, b, :] = aff[:, sl]
        starts_ref[:, b:b + 1] = carry.astype(jnp.int32)
        carry = carry + jnp.sum(sbf, axis=1, keepdims=True)
    nblk = t // TOKEN_BLOCK
    starts_ref[:, nblk:nblk + 1] = carry.astype(jnp.int32)


def _route(aff_et):
    t = aff_et.shape[1]
    cap = EC_CAPACITY_FACTOR * t // N_EXPERTS
    nblk = t // TOKEN_BLOCK
    upper = jnp.asarray(np.triu(np.ones((TOKEN_BLOCK, TOKEN_BLOCK)), 1), BF16)
    assert nblk + 1 <= 128
    blocked = (N_EXPERTS, nblk, TOKEN_BLOCK)
    return pl.pallas_call(
        functools.partial(_route_kernel, t=t, cap=cap),
        out_shape=[jax.ShapeDtypeStruct(blocked, jnp.int32), jax.ShapeDtypeStruct(blocked, F32),
                   jax.ShapeDtypeStruct((N_EXPERTS, 128), jnp.int32)],
        grid=(1,),
        in_specs=[pl.BlockSpec((N_EXPERTS, t), lambda i: (0, 0)),
                  pl.BlockSpec((TOKEN_BLOCK, TOKEN_BLOCK), lambda i: (0, 0))],
        out_specs=[pl.BlockSpec(blocked, lambda i: (0, 0, 0)), pl.BlockSpec(blocked, lambda i: (0, 0, 0)),
                   pl.BlockSpec((N_EXPERTS, 128), lambda i: (0, 0))],
        compiler_params=pltpu.CompilerParams(dimension_semantics=("arbitrary",)),
        name="route",
    )(aff_et, upper)


def _pack_windows(starts_ref, b, experts, cap):
    first = [jnp.minimum((starts_ref[e, b] // 16) * 16, cap - SLOT_WINDOW) for e in experts]
    rows = [jnp.where(starts_ref[e, b + 1] > starts_ref[e, b], starts_ref[e, b + 1] - w, 0)
            for e, w in zip(experts, first)]
    return first, pl.cdiv(functools.reduce(jnp.maximum, rows), SLOT_WINDOW)


def _block_copy(hbm_ref, vmem_ref, sem_ref, b):
    rows = pl.ds(pl.multiple_of(b * TOKEN_BLOCK, TOKEN_BLOCK), TOKEN_BLOCK)
    return pltpu.make_async_copy(hbm_ref.at[rows], vmem_ref.at[rows], sem_ref.at[b])


def _gather_group(g, starts_ref, slot_ref, gate_ref, h2_ref, xs_ref, gs_ref, row0, t, cap, arrive):
    sub = lax.broadcasted_iota(jnp.int32, (SLOT_WINDOW, TOKEN_BLOCK), 0)
    experts = [g * PACK + j for j in range(PACK)]
    assert (t // TOKEN_BLOCK) % GATHER_UNROLL == 0

    def window(b, first, i):
        hb = pl.ds(pl.multiple_of(b * TOKEN_BLOCK, TOKEN_BLOCK), TOKEN_BLOCK)
        hits, dst = [], []
        for j in range(PACK):
            lo = first[j] + i * SLOT_WINDOW
            w = jnp.minimum(lo, cap - SLOT_WINDOW)
            srow = slot_ref[j, pl.ds(b, 1), :]
            hits.append((srow == w + sub) & (srow >= lo))
            dst.append(pl.ds(pl.multiple_of(row0 + w, 16), SLOT_WINDOW))
        onehot = jnp.concatenate([jnp.where(h, 1.0, 0.0) for h in hits], axis=0).astype(BF16)
        got = _dot(onehot, h2_ref[hb, :])
        for j in range(PACK):
            piece = got[j * SLOT_WINDOW:(j + 1) * SLOT_WINDOW].astype(BF16)
            xs_ref[j, dst[j], :] = xs_ref[j, dst[j], :] + piece
            grow = gate_ref[j, pl.ds(b, 1), :]
            gs_ref[j, dst[j], :] += jnp.sum(jnp.where(hits[j], grow, 0.0), axis=1, keepdims=True)

    def blocks(q, carry):
        pending = []
        for u in range(GATHER_UNROLL):
            arrive(q * GATHER_UNROLL + u)
        for u in range(GATHER_UNROLL):
            b = q * GATHER_UNROLL + u
            first, n_windows = _pack_windows(starts_ref, b, experts, cap)
            window(b, first, 0)
            pending.append((b, first, n_windows))
        for b, first, n_windows in pending:
            def more(i, carry, b=b, first=first):
                window(b, first, i)
                return carry

            lax.fori_loop(1, n_windows, more, 0)
        return carry

    lax.fori_loop(0, t // TOKEN_BLOCK // GATHER_UNROLL, blocks, 0)


def _experts_kernel(sp_ref, ss_ref, h2p_hbm, h2s_hbm, slotp_ref, slots_ref, gatep_ref, gates_ref,
                    wg_ref, wu_ref, wd_ref, yp_ref, ys_ref, xs_ref, gs_ref, acc_ref,
                    h2p_ref, h2s_ref, semp_ref, sems_ref, *, tp, ts, capp, caps):
    g = pl.program_id(0)
    step = pl.program_id(1)
    j = step // N_FF_TILES
    f = step % N_FF_TILES
    loading = [(h2p_hbm, h2p_ref, semp_ref, tp // TOKEN_BLOCK),
               (h2s_hbm, h2s_ref, sems_ref, ts // TOKEN_BLOCK)]

    @pl.when((step == 0) & (g == 0))
    def _():
        for hbm_ref, vmem_ref, sem_ref, n_blocks in loading:
            for b in range(n_blocks):
                _block_copy(hbm_ref, vmem_ref, sem_ref, b).start()

    def arrive(hbm_ref, vmem_ref, sem_ref, _):
        def wait(b):
            @pl.when(g == 0)
            def _():
                _block_copy(hbm_ref, vmem_ref, sem_ref, b).wait()

        return wait

    @pl.when(step == 0)
    def _():
        xs_ref[...] = jnp.zeros(xs_ref.shape, BF16)
        gs_ref[...] = jnp.zeros(gs_ref.shape, F32)
        _gather_group(g, sp_ref, slotp_ref, gatep_ref, h2p_ref, xs_ref, gs_ref, 0, tp, capp,
                      arrive(*loading[0]))
        _gather_group(g, ss_ref, slots_ref, gates_ref, h2s_ref, xs_ref, gs_ref, capp, ts, caps,
                      arrive(*loading[1]))

    @pl.when(f == 0)
    def _():
        acc_ref[...] = jnp.zeros(acc_ref.shape, F32)

    x = xs_ref[j]
    a = _dot(x, wg_ref[0].astype(BF16))
    u = _dot(x, wu_ref[0].astype(BF16))
    acc_ref[...] += _dot((_silu(a) * u).astype(BF16), wd_ref[0].astype(BF16))

    @pl.when(f == N_FF_TILES - 1)
    def _():
        yp_ref[0] = (acc_ref[0:capp, :] * gs_ref[j, 0:capp, :]).astype(BF16)
        ys_ref[0] = (acc_ref[capp:capp + caps, :] * gs_ref[j, capp:capp + caps, :]).astype(BF16)


def _experts(starts_p, starts_s, h2p, h2s, slot_p, slot_s, gate_p, gate_s, w_gate, w_up, w_down):
    tp, ts = h2p.shape[0], h2s.shape[0]
    capp = EC_CAPACITY_FACTOR * tp // N_EXPERTS
    caps = EC_CAPACITY_FACTOR * ts // N_EXPERTS
    rows = capp + caps
    nbp, nbs = tp // TOKEN_BLOCK, ts // TOKEN_BLOCK
    expert = lambda g, s: g * PACK + s // N_FF_TILES
    grid_spec = pltpu.PrefetchScalarGridSpec(
        num_scalar_prefetch=2,
        grid=(N_EXPERTS // PACK, PACK * N_FF_TILES),
        in_specs=[
            pl.BlockSpec(memory_space=pl.ANY),
            pl.BlockSpec(memory_space=pl.ANY),
            pl.BlockSpec((PACK, nbp, TOKEN_BLOCK), lambda g, s, *_: (g, 0, 0)),
            pl.BlockSpec((PACK, nbs, TOKEN_BLOCK), lambda g, s, *_: (g, 0, 0)),
            pl.BlockSpec((PACK, nbp, TOKEN_BLOCK), lambda g, s, *_: (g, 0, 0)),
            pl.BlockSpec((PACK, nbs, TOKEN_BLOCK), lambda g, s, *_: (g, 0, 0)),
            pl.BlockSpec((1, D_MODEL, FF_TILE), lambda g, s, *_: (expert(g, s), 0, s % N_FF_TILES)),
            pl.BlockSpec((1, D_MODEL, FF_TILE), lambda g, s, *_: (expert(g, s), 0, s % N_FF_TILES)),
            pl.BlockSpec((1, FF_TILE, D_MODEL), lambda g, s, *_: (expert(g, s), s % N_FF_TILES, 0)),
        ],
        out_specs=[
            pl.BlockSpec((1, capp, D_MODEL), lambda g, s, *_: (expert(g, s), 0, 0)),
            pl.BlockSpec((1, caps, D_MODEL), lambda g, s, *_: (expert(g, s), 0, 0)),
        ],
        scratch_shapes=[pltpu.VMEM((PACK, rows, D_MODEL), BF16), pltpu.VMEM((PACK, rows, 1), F32),
                        pltpu.VMEM((rows, D_MODEL), F32),
                        pltpu.VMEM((tp, D_MODEL), BF16), pltpu.VMEM((ts, D_MODEL), BF16),
                        pltpu.SemaphoreType.DMA((nbp,)), pltpu.SemaphoreType.DMA((nbs,))],
    )
    return pl.pallas_call(
        functools.partial(_experts_kernel, tp=tp, ts=ts, capp=capp, caps=caps),
        out_shape=[jax.ShapeDtypeStruct((N_EXPERTS, capp, D_MODEL), BF16),
                   jax.ShapeDtypeStruct((N_EXPERTS, caps, D_MODEL), BF16)],
        grid_spec=grid_spec,
        compiler_params=pltpu.CompilerParams(dimension_semantics=("arbitrary", "arbitrary"),
                                             vmem_limit_bytes=VMEM_LIMIT),
        name="experts",
    )(starts_p, starts_s, h2p, h2s, slot_p, slot_s, gate_p, gate_s, w_gate, w_up, w_down)


def _combine_kernel(st_ref, x1_ref, slot_ref, mod_ref, fg_ref, y_hbm, out_ref, acc_ref, y_ref,
                    sem_ref, arrived_ref, *, cap):
    b = pl.program_id(0)
    n_chunks = cap // Y_CHUNK

    def chunk_copy(c):
        rows = pl.ds(c * Y_CHUNK, Y_CHUNK)
        return pltpu.make_async_copy(y_hbm.at[:, rows, :], y_ref.at[:, rows, :], sem_ref.at[c])

    @pl.when(b == 0)
    def _():
        arrived_ref[0] = 0
        for c in range(n_chunks):
            chunk_copy(c).start()

    top = functools.reduce(jnp.maximum, [st_ref[e, b + 1] for e in range(N_EXPERTS)])
    want = pl.cdiv(jnp.minimum(top + SLOT_WINDOW, cap), Y_CHUNK)
    have = arrived_ref[0]
    for c in range(n_chunks):
        @pl.when((c >= have) & (c < want))
        def _(c=c):
            chunk_copy(c).wait()
    arrived_ref[0] = jnp.maximum(have, want)

    slot = slot_ref[...]
    lane = lax.broadcasted_iota(jnp.int32, (TOKEN_BLOCK, PACK * SLOT_WINDOW), 1)

    def scatter(experts, first, i):
        target = None
        windows = []
        for j, e in enumerate(experts):
            lo = first[j] + i * SLOT_WINDOW
            w = jnp.minimum(lo, cap - SLOT_WINDOW)
            sc = slot[:, e:e + 1]
            col = jnp.where(sc >= lo, sc - w + j * SLOT_WINDOW, -1)
            target = col if target is None else jnp.where(lane < j * SLOT_WINDOW, target, col)
            windows.append(y_ref[e, pl.ds(pl.multiple_of(w, 16), SLOT_WINDOW), :])
        onehot = jnp.where(target == lane, 1.0, 0.0).astype(BF16)
        return _dot(onehot, jnp.concatenate(windows, axis=0))

    groups = []
    total = None
    for g in range(N_EXPERTS // PACK):
        experts = list(range(g * PACK, (g + 1) * PACK))
        first, n_windows = _pack_windows(st_ref, b, experts, cap)
        groups.append((experts, first, n_windows))
        part = scatter(experts, first, 0)
        total = part if total is None else total + part
    acc_ref[...] = total

    for experts, first, n_windows in groups:
        def more(i, carry, experts=experts, first=first):
            acc_ref[...] += scatter(experts, first, i)
            return carry

        lax.fori_loop(1, n_windows, more, 0)

    gate2 = mod_ref[0][:, 5 * D_MODEL:6 * D_MODEL]
    x2 = x1_ref[...] + gate2 * acc_ref[...]
    out_ref[...] = _rms(x2) * fg_ref[...]


def _combine(starts, x1, slot_te, mod_rows, mod_first_row, blocks_per_mod_row, final_g, y):
    t = x1.shape[0]
    cap = y.shape[1]
    nblk = t // TOKEN_BLOCK
    grid_spec = pltpu.PrefetchScalarGridSpec(
        num_scalar_prefetch=1,
        grid=(nblk,),
        in_specs=[
            pl.BlockSpec((TOKEN_BLOCK, D_MODEL), lambda b, *_: (b, 0)),
            pl.BlockSpec((TOKEN_BLOCK, N_EXPERTS), lambda b, *_: (b, 0)),
            _mod_row_spec(mod_first_row, blocks_per_mod_row),
            pl.BlockSpec((1, D_MODEL), lambda b, *_: (0, 0)),
            pl.BlockSpec(memory_space=pl.ANY),
        ],
        out_specs=pl.BlockSpec((TOKEN_BLOCK, D_MODEL), lambda b, *_: (b, 0)),
        scratch_shapes=[pltpu.VMEM((TOKEN_BLOCK, D_MODEL), F32), pltpu.VMEM(y.shape, BF16),
                        pltpu.SemaphoreType.DMA((cap // Y_CHUNK,)), pltpu.SMEM((1,), jnp.int32)],
    )
    return pl.pallas_call(
        functools.partial(_combine_kernel, cap=cap),
        out_shape=jax.ShapeDtypeStruct((t, D_MODEL), F32),
        grid_spec=grid_spec,
        compiler_params=pltpu.CompilerParams(dimension_semantics=("arbitrary",),
                                             vmem_limit_bytes=VMEM_LIMIT),
        name="combine",
    )(starts, x1, slot_te, mod_rows, final_g.reshape(1, D_MODEL), y)


def kernel(x_prompt, x_sample, state_ret, c, c_ctx, norm1_g, norm2_g, final_g, w_mod, b_mod, w_in,
           w_fmix, w_out, w_router, w_gate, w_up, w_down):
    bp, seq, _ = x_prompt.shape
    bs, dec_seq, _ = x_sample.shape
    assert w_mod.shape[0] == 1, "single-layer trunk"
    tp, ts = bp * seq, bs * dec_seq

    cond = jnp.concatenate([c_ctx[None, :], c], axis=0)
    mod = _modulation(cond, w_mod[0], b_mod[0])
    ctx_row, lat_row = 0, 1

    w_in_bf = w_in[0].astype(BF16)
    w_out_bf = w_out[0].astype(BF16)
    mix_p, states = _mixer(x_prompt, mod, ctx_row, False, None, True, False, norm1_g[0], w_in_bf,
                           w_fmix[0])
    (mix_s,) = _mixer(x_sample, mod, lat_row, True, state_ret[:, 0], False, True, norm1_g[0],
                      w_in_bf, w_fmix[0])
    x1p, h2p, affp = _post(x_prompt.reshape(tp, D_MODEL), mix_p, mod, ctx_row, None,
                           norm2_g[0], w_out_bf, w_router[0])
    x1s, h2s, affs = _post(x_sample.reshape(ts, D_MODEL), mix_s, mod, lat_row, dec_seq,
                           norm2_g[0], w_out_bf, w_router[0])

    slot_p, gate_p, starts_p = _route(affp.T)
    slot_s, gate_s, starts_s = _route(affs.T)

    yp, ys = _experts(starts_p, starts_s, h2p, h2s, slot_p, slot_s, gate_p, gate_s,
                      w_gate[0], w_up[0], w_down[0])

    out_p = _combine(starts_p, x1p, slot_p.reshape(N_EXPERTS, tp).T, mod, ctx_row, None, final_g, yp)
    out_s = _combine(starts_s, x1s, slot_s.reshape(N_EXPERTS, ts).T, mod, lat_row,
                     dec_seq // TOKEN_BLOCK, final_g, ys)

    y_prompt = out_p.reshape(bp, seq, D_MODEL)
    y_sample = out_s.reshape(bs, dec_seq, D_MODEL)
    state_new = states.reshape(bp, 1, 2, N_RET_HEADS, HEAD_DIM, HEAD_DIM).astype(x_prompt.dtype)
    return (y_prompt, y_sample, state_new)
```

```python
import functools
import math

import jax
import jax.numpy as jnp
import numpy as np
from jax import lax
from jax.experimental import pallas as pl
from jax.experimental.pallas import tpu as pltpu

D_MODEL = 1024
D_FOURIER = 512
N_FOURIER_GROUPS = 4
FOURIER_GROUP_W = 128
D_RET = 512
N_RET_HEADS = 4
HEAD_DIM = 128
CHUNK = 256
GRID_W = 64
N_EXPERTS = 16
EC_CAPACITY_FACTOR = 2
D_EXPERT_FF = 2816
ROPE_BASE = 10000.0
EPS = 1e-6
D_IN_PROJ = D_FOURIER + 5 * D_RET
LOG_GAMMA_FWD = np.log(1.0 - 2.0 ** (-5.0 - np.arange(N_RET_HEADS))).astype(np.float32)
LOG_GAMMA_BWD = np.log(1.0 - 2.0 ** (-5.5 - np.arange(N_RET_HEADS))).astype(np.float32)

TOKEN_BLOCK = 256
SLOT_WINDOW = 64
PACK = TOKEN_BLOCK // SLOT_WINDOW
GATHER_UNROLL = 4
Y_CHUNKS = 4
FF_TILE = 256
N_FF_TILES = D_EXPERT_FF // FF_TILE
MOD_TILE = 1024
MIXER_ROWS = 1024
POST_ROWS = 512
POST_PART = 256
COMBINE_ROWS = 512
VMEM_LIMIT = 56 * 1024 * 1024

F32 = jnp.float32
BF16 = jnp.bfloat16


def _dot(a, b):
    return jnp.dot(a, b, preferred_element_type=F32)


def _dot_nt(a, b):
    return lax.dot_general(a, b, (((1,), (1,)), ((), ())), preferred_element_type=F32)


def _silu(x):
    return x * jax.nn.sigmoid(x)


def _mod_kernel(condt_ref, w_ref, b_ref, out_ref, *, n_cond):
    s = _silu(condt_ref[...])
    w = w_ref[...]
    out_ref[...] = jnp.zeros(out_ref.shape, F32)
    for r in range(n_cond):
        out_ref[r] = jnp.sum(w * s[:, r:r + 1], axis=0, keepdims=True) + b_ref[...]


def _modulation(cond_rows, w_mod, b_mod):
    n_cond = cond_rows.shape[0]
    condt = jnp.zeros((D_MODEL, 8), F32).at[:, :n_cond].set(cond_rows.T)
    n_out = w_mod.shape[1]
    return pl.pallas_call(
        functools.partial(_mod_kernel, n_cond=n_cond),
        out_shape=jax.ShapeDtypeStruct((8, 1, n_out), F32),
        grid=(n_out // MOD_TILE,),
        in_specs=[
            pl.BlockSpec((D_MODEL, 8), lambda j: (0, 0)),
            pl.BlockSpec((D_MODEL, MOD_TILE), lambda j: (0, j)),
            pl.BlockSpec((1, MOD_TILE), lambda j: (0, j)),
        ],
        out_specs=pl.BlockSpec((8, 1, MOD_TILE), lambda j: (0, 0, j)),
        compiler_params=pltpu.CompilerParams(dimension_semantics=("arbitrary",)),
        name="mod",
    )(condt, w_mod, b_mod.reshape(1, n_out))


def _rms(x):
    return x * lax.rsqrt(jnp.mean(x * x, axis=-1, keepdims=True) + EPS)


def _groupnorm(o):
    mu = jnp.mean(o, axis=-1, keepdims=True)
    c = o - mu
    return c * lax.rsqrt(jnp.mean(c * c, axis=-1, keepdims=True) + EPS)


def _split_hi_lo(x):
    hi = x.astype(BF16)
    lo = (x - hi.astype(F32)).astype(BF16)
    return hi, lo


def _mixer_kernel(*refs, n, use_rope, has_state_in, emit_state):
    it = iter(refs)
    x_ref, mod_ref, g1_ref, win_ref, wfmix_ref = (next(it) for _ in range(5))
    cw_ref, cn_ref, sn_ref, dmat_ref, qdec_ref, kdec_ref, sdec_ref = (next(it) for _ in range(7))
    cos_ref = sin_ref = s0_ref = st_ref = None
    if use_rope:
        cos_ref, sin_ref = next(it), next(it)
    if has_state_in:
        s0_ref = next(it)
    mix_ref = next(it)
    if emit_state:
        st_ref = next(it)
    p_ref, of_ref, ob_ref = next(it), next(it), next(it)

    n_seq = MIXER_ROWS // n
    chunks_per_seq = n // CHUNK
    mod = mod_ref[0]
    shift1 = mod[:, 0:D_MODEL]
    scale1 = mod[:, D_MODEL:2 * D_MODEL]

    h = (_rms(x_ref[...]) * g1_ref[...] * (1.0 + scale1) + shift1).astype(BF16)
    for j in range(D_IN_PROJ // 512):
        p_ref[:, j * 512:(j + 1) * 512] = _dot(h, win_ref[:, j * 512:(j + 1) * 512])

    xf = p_ref[:, 0:D_FOURIER].astype(BF16)
    xc, xs = [], []
    cw = cw_ref[...].astype(BF16)
    for g in range(N_FOURIER_GROUPS):
        t = _dot(xf[:, g * FOURIER_GROUP_W:(g + 1) * FOURIER_GROUP_W], cw)
        xc.append(t[:, :FOURIER_GROUP_W].astype(BF16))
        xs.append(t[:, FOURIER_GROUP_W:].astype(BF16))
    xc = jnp.concatenate(xc, axis=1)
    xs = jnp.concatenate(xs, axis=1)
    cn = cn_ref[...].astype(BF16)
    sn = sn_ref[...].astype(BF16)
    for s in range(n_seq):
        rs = slice(s * n, (s + 1) * n)
        fre = (_dot(cn, xc[rs]) - _dot(sn, xs[rs])) * (1.0 / math.sqrt(n * FOURIER_GROUP_W))
        fre = fre.astype(BF16)
        for g in range(N_FOURIER_GROUPS):
            sl = slice(g * FOURIER_GROUP_W, (g + 1) * FOURIER_GROUP_W)
            mix_ref[rs, sl] = _dot(fre[:, sl], wfmix_ref[g].astype(BF16)).astype(BF16)

    for hh in range(N_RET_HEADS):
        base = D_FOURIER + hh * HEAD_DIM
        q = p_ref[:, base:base + HEAD_DIM]
        k = p_ref[:, base + D_RET:base + D_RET + HEAD_DIM]
        v = p_ref[:, base + 2 * D_RET:base + 2 * D_RET + HEAD_DIM]
        if use_rope:
            lane = lax.broadcasted_iota(jnp.int32, (MIXER_ROWS, HEAD_DIM), 1)
            first = (lane % 64) < 32

            def rope(t):
                swapped = jnp.where(first, pltpu.roll(t, HEAD_DIM - 32, 1), pltpu.roll(t, 32, 1))
                return t * cos_ref[...] + swapped * sin_ref[...]

            q, k = rope(q), rope(k)
        k = k * (HEAD_DIM ** -0.5)
        qb, vb = q.astype(BF16), v.astype(BF16)
        kb = k.astype(BF16)

        def initial(s, direction):
            if has_state_in:
                return s0_ref[s, direction, hh]
            return jnp.zeros((HEAD_DIM, HEAD_DIM), F32)

        for s in range(n_seq):
            parts = []
            for ci in range(chunks_per_seq):
                c = s * chunks_per_seq + ci
                rs = slice(c * CHUNK, (c + 1) * CHUNK)
                qc, kc, vc = qb[rs], kb[rs], vb[rs]
                qk = _dot_nt(qc, kc)
                lhs = jnp.concatenate([(qk * dmat_ref[0, hh]).astype(BF16),
                                       (qk * dmat_ref[1, hh]).astype(BF16),
                                       (k[rs] * kdec_ref[0, hh]).T.astype(BF16),
                                       (k[rs] * kdec_ref[1, hh]).T.astype(BF16)], axis=0)
                parts.append((rs, qc, _dot(lhs, vc)))
            sf = initial(s, 0)
            for ci in range(chunks_per_seq):
                rs, qc, r = parts[ci]
                o = r[0:CHUNK]
                if has_state_in or ci > 0:
                    o = o + qdec_ref[0, hh] * _dot(qc, sf.astype(BF16))
                of_ref[rs, :] = o
                sf = sf * sdec_ref[0, hh] + r[2 * CHUNK:2 * CHUNK + HEAD_DIM]
            sb = initial(s, 1)
            for ci in reversed(range(chunks_per_seq)):
                rs, qc, r = parts[ci]
                o = r[CHUNK:2 * CHUNK]
                if has_state_in or ci < chunks_per_seq - 1:
                    o = o + qdec_ref[1, hh] * _dot(qc, sb.astype(BF16))
                ob_ref[rs, :] = o
                sb = sb * sdec_ref[1, hh] + r[2 * CHUNK + HEAD_DIM:]
            if emit_state:
                st_ref[s, 0, hh] = sf
                st_ref[s, 1, hh] = sb

        gf = p_ref[:, base + 3 * D_RET:base + 3 * D_RET + HEAD_DIM]
        gb = p_ref[:, base + 4 * D_RET:base + 4 * D_RET + HEAD_DIM]
        y = _silu(gf) * _groupnorm(of_ref[...]) + _silu(gb) * _groupnorm(ob_ref[...])
        mix_ref[:, base:base + HEAD_DIM] = y.astype(BF16)


def _post_kernel(x_ref, mix_ref, mod_ref, g2_ref, wout_ref, wr_ref, x1_ref, h2_ref, aff_ref):
    mod = mod_ref[0]
    gate1 = mod[:, 2 * D_MODEL:3 * D_MODEL]
    shift2 = mod[:, 3 * D_MODEL:4 * D_MODEL]
    scale2 = mod[:, 4 * D_MODEL:5 * D_MODEL]
    wr_hi, wr_lo = _split_hi_lo(wr_ref[...])
    wr_both = jnp.concatenate([wr_hi, wr_lo], axis=1)
    parts = [slice(p * POST_PART, (p + 1) * POST_PART) for p in range(POST_ROWS // POST_PART)]
    x1 = []
    for rows in parts:
        x1.append(x_ref[rows, :] + gate1 * _dot(mix_ref[rows, :], wout_ref[...]))
        x1_ref[rows, :] = x1[-1]
    for rows, x1_part in zip(parts, x1):
        h2 = _rms(x1_part) * g2_ref[...] * (1.0 + scale2) + shift2
        h2_hi, h2_lo = _split_hi_lo(h2)
        h2_ref[rows, :] = h2_hi
        by_hi = _dot(h2_hi, wr_both)
        logits = by_hi[:, :N_EXPERTS] + (_dot(h2_lo, wr_hi) + by_hi[:, N_EXPERTS:])
        z = jnp.exp(logits - jnp.max(logits, axis=-1, keepdims=True))
        aff_ref[rows, :] = z / jnp.sum(z, axis=-1, keepdims=True)


def _dft_consts(n):
    w = FOURIER_GROUP_W
    jw = np.arange(w)
    angw = 2.0 * np.pi * np.outer(jw, jw) / w
    cw = np.concatenate([np.cos(angw), np.sin(angw)], axis=1)
    jn = np.arange(n)
    angn = 2.0 * np.pi * (np.outer(jn, jn) % n) / n
    return (jnp.asarray(cw, F32), jnp.asarray(np.cos(angn), F32), jnp.asarray(np.sin(angn), F32))


def _retention_consts():
    i = np.arange(CHUNK, dtype=np.float64)
    diff = i[:, None] - i[None, :]
    dmat = np.zeros((2, N_RET_HEADS, CHUNK, CHUNK))
    qdec = np.zeros((2, N_RET_HEADS, CHUNK, HEAD_DIM))
    kdec = np.zeros((2, N_RET_HEADS, CHUNK, HEAD_DIM))
    sdec = np.zeros((2, N_RET_HEADS, HEAD_DIM, HEAD_DIM))
    for hh in range(N_RET_HEADS):
        lf = float(LOG_GAMMA_FWD[hh])
        lb = float(LOG_GAMMA_BWD[hh])
        dmat[0, hh] = np.where(diff >= 0, np.exp(lf * np.maximum(diff, 0.0)), 0.0)
        dmat[1, hh] = np.where(diff <= 0, np.exp(lb * np.maximum(-diff, 0.0)), 0.0)
        qdec[0, hh] = np.exp(lf * (i + 1.0))[:, None]
        qdec[1, hh] = np.exp(lb * (CHUNK - i))[:, None]
        kdec[0, hh] = np.exp(lf * (CHUNK - 1.0 - i))[:, None]
        kdec[1, hh] = np.exp(lb * i)[:, None]
        sdec[0, hh] = math.exp(lf * CHUNK)
        sdec[1, hh] = math.exp(lb * CHUNK)
    return tuple(jnp.asarray(a, F32) for a in (dmat, qdec, kdec, sdec))


def _rope_consts(n):
    rows_n = n // GRID_W
    row = np.repeat(np.arange(rows_n, dtype=np.float64), GRID_W)
    col = np.tile(np.arange(GRID_W, dtype=np.float64), rows_n)
    n_pairs = HEAD_DIM // 4
    freqs = (np.float32(ROPE_BASE) ** (-np.arange(n_pairs, dtype=np.float32) / n_pairs)).astype(np.float64)
    ar = row[:, None] * freqs[None, :]
    ac = col[:, None] * freqs[None, :]
    cos = np.concatenate([np.cos(ar), np.cos(ar), np.cos(ac), np.cos(ac)], axis=1)
    sin = np.concatenate([-np.sin(ar), np.sin(ar), -np.sin(ac), np.sin(ac)], axis=1)
    return jnp.asarray(cos, F32), jnp.asarray(sin, F32)


def _const_spec(shape):
    nd = len(shape)
    return pl.BlockSpec(shape, lambda b, _nd=nd: (0,) * _nd, pipeline_mode=pl.Buffered(1))


def _mod_row_spec(first_row, blocks_per_row):
    if blocks_per_row is None:
        return pl.BlockSpec((1, 1, 6 * D_MODEL), lambda b, *_: (first_row, 0, 0))
    return pl.BlockSpec((1, 1, 6 * D_MODEL), lambda b, *_: (first_row + b // blocks_per_row, 0, 0))


def _mixer(x, mod_rows, mod_first_row, mod_per_batch, state_in, emit_state, use_rope, g1, w_in_bf,
           w_fmix):
    nb, n, _ = x.shape
    assert MIXER_ROWS % n == 0 and (nb * n) % MIXER_ROWS == 0
    n_seq = MIXER_ROWS // n
    has_state_in = state_in is not None
    cw, cn, sn = _dft_consts(n)
    dmat, qdec, kdec, sdec = _retention_consts()
    consts = [cw, cn, sn, dmat, qdec, kdec, sdec]
    if use_rope:
        assert n_seq == 1
        consts += list(_rope_consts(n))
    weights = [g1.reshape(1, D_MODEL), w_in_bf, w_fmix]

    if mod_per_batch:
        assert n % MIXER_ROWS == 0
    state_spec = pl.BlockSpec((n_seq, 2, N_RET_HEADS, HEAD_DIM, HEAD_DIM), lambda b: (b, 0, 0, 0, 0))
    row_spec = pl.BlockSpec((MIXER_ROWS, D_MODEL), lambda b: (b, 0))
    in_specs = [row_spec, _mod_row_spec(mod_first_row, n // MIXER_ROWS if mod_per_batch else None)]
    in_specs += [_const_spec(a.shape) for a in weights + consts]
    args = [x.reshape(nb * n, D_MODEL), mod_rows] + weights + consts
    if has_state_in:
        in_specs.append(state_spec)
        args.append(state_in)

    out_shape = [jax.ShapeDtypeStruct((nb * n, D_MODEL), BF16)]
    out_specs = [row_spec]
    if emit_state:
        out_shape.append(jax.ShapeDtypeStruct((nb, 2, N_RET_HEADS, HEAD_DIM, HEAD_DIM), F32))
        out_specs.append(state_spec)

    return pl.pallas_call(
        functools.partial(_mixer_kernel, n=n, use_rope=use_rope, has_state_in=has_state_in,
                          emit_state=emit_state),
        out_shape=out_shape,
        grid=(nb * n // MIXER_ROWS,),
        in_specs=in_specs,
        out_specs=out_specs,
        scratch_shapes=[pltpu.VMEM((MIXER_ROWS, D_IN_PROJ), F32),
                        pltpu.VMEM((MIXER_ROWS, HEAD_DIM), F32), pltpu.VMEM((MIXER_ROWS, HEAD_DIM), F32)],
        compiler_params=pltpu.CompilerParams(dimension_semantics=("arbitrary",),
                                             vmem_limit_bytes=VMEM_LIMIT),
        name="mixer_rope" if use_rope else "mixer",
    )(*args)


def _post(x, mix, mod_rows, mod_first_row, tokens_per_mod_row, g2, w_out_bf, w_router):
    t = x.shape[0]
    assert tokens_per_mod_row is None or tokens_per_mod_row % POST_ROWS == 0
    blocks_per_row = None if tokens_per_mod_row is None else tokens_per_mod_row // POST_ROWS
    row_spec = pl.BlockSpec((POST_ROWS, D_MODEL), lambda b: (b, 0))
    return pl.pallas_call(
        _post_kernel,
        out_shape=[jax.ShapeDtypeStruct((t, D_MODEL), F32),
                   jax.ShapeDtypeStruct((t, D_MODEL), BF16),
                   jax.ShapeDtypeStruct((t, N_EXPERTS), F32)],
        grid=(t // POST_ROWS,),
        in_specs=[row_spec, row_spec,
                  _mod_row_spec(mod_first_row, blocks_per_row),
                  _const_spec((1, D_MODEL)), _const_spec((D_MODEL, D_MODEL)),
                  _const_spec((D_MODEL, N_EXPERTS))],
        out_specs=[row_spec, row_spec, pl.BlockSpec((POST_ROWS, N_EXPERTS), lambda b: (b, 0))],
        compiler_params=pltpu.CompilerParams(dimension_semantics=("arbitrary",),
                                             vmem_limit_bytes=VMEM_LIMIT),
        name="post",
    )(x, mix, mod_rows, g2.reshape(1, D_MODEL), w_out_bf, w_router)


def _route_kernel(aff_ref, u_ref, slot_ref, gate_ref, starts_ref, *, t, cap):
    aff = aff_ref[...]

    def count(mask):
        return jnp.sum(mask.astype(jnp.int32), axis=1, keepdims=True)

    def as_float(word):
        return lax.bitcast_convert_type(word, F32)

    def value_step(i, cur):
        cand = cur | jnp.left_shift(jnp.int32(1), 30 - i)
        return jnp.where(count(aff >= as_float(cand)) >= cap, cand, cur)

    thr = lax.fori_loop(0, 31, value_step, jnp.zeros((N_EXPERTS, 1), jnp.int32))
    gt = aff >= as_float(thr + 1)
    eq = (aff >= as_float(thr)) & jnp.logical_not(gt)
    need = cap - count(gt)
    tok = lax.broadcasted_iota(jnp.int32, (N_EXPERTS, t), 1)
    nbits = t.bit_length() - 1

    def index_step(i, cur):
        cand = cur | jnp.left_shift(jnp.int32(1), nbits - 1 - i)
        return jnp.where(count(eq & (tok < cand)) < need, cand, cur)

    last = lax.fori_loop(0, nbits, index_step, jnp.zeros((N_EXPERTS, 1), jnp.int32))
    self = jnp.where(gt | (eq & (tok <= last)), 1.0, 0.0).astype(F32)

    carry = jnp.zeros((N_EXPERTS, 1), F32)
    starts_ref[...] = jnp.zeros(starts_ref.shape, jnp.int32)
    for b in range(t // TOKEN_BLOCK):
        sl = slice(b * TOKEN_BLOCK, (b + 1) * TOKEN_BLOCK)
        sbf = self[:, sl]
        pre = _dot(sbf.astype(BF16), u_ref[...]) + carry
        slot_ref[:, b, :] = jnp.where(sbf > 0.5, pre.astype(jnp.int32), -1)
        gate_ref[:, b, :] = aff[:, sl]
        starts_ref[:, b:b + 1] = carry.astype(jnp.int32)
        carry = carry + jnp.sum(sbf, axis=1, keepdims=True)
    nblk = t // TOKEN_BLOCK
    starts_ref[:, nblk:nblk + 1] = carry.astype(jnp.int32)


def _route(aff_et):
    t = aff_et.shape[1]
    cap = EC_CAPACITY_FACTOR * t // N_EXPERTS
    nblk = t // TOKEN_BLOCK
    upper = jnp.asarray(np.triu(np.ones((TOKEN_BLOCK, TOKEN_BLOCK)), 1), BF16)
    assert nblk + 1 <= 128
    blocked = (N_EXPERTS, nblk, TOKEN_BLOCK)
    return pl.pallas_call(
        functools.partial(_route_kernel, t=t, cap=cap),
        out_shape=[jax.ShapeDtypeStruct(blocked, jnp.int32), jax.ShapeDtypeStruct(blocked, F32),
                   jax.ShapeDtypeStruct((N_EXPERTS, 128), jnp.int32)],
        grid=(1,),
        in_specs=[pl.BlockSpec((N_EXPERTS, t), lambda i: (0, 0)),
                  pl.BlockSpec((TOKEN_BLOCK, TOKEN_BLOCK), lambda i: (0, 0))],
        out_specs=[pl.BlockSpec(blocked, lambda i: (0, 0, 0)), pl.BlockSpec(blocked, lambda i: (0, 0, 0)),
                   pl.BlockSpec((N_EXPERTS, 128), lambda i: (0, 0))],
        compiler_params=pltpu.CompilerParams(dimension_semantics=("arbitrary",)),
        name="route",
    )(aff_et, upper)


def _pack_windows(starts_ref, b, experts, cap):
    first = [jnp.minimum((starts_ref[e, b] // 16) * 16, cap - SLOT_WINDOW) for e in experts]
    rows = [jnp.where(starts_ref[e, b + 1] > starts_ref[e, b], starts_ref[e, b + 1] - w, 0)
            for e, w in zip(experts, first)]
    return first, pl.cdiv(functools.reduce(jnp.maximum, rows), SLOT_WINDOW)


def _block_copy(hbm_ref, vmem_ref, sem_ref, b):
    rows = pl.ds(pl.multiple_of(b * TOKEN_BLOCK, TOKEN_BLOCK), TOKEN_BLOCK)
    return pltpu.make_async_copy(hbm_ref.at[rows], vmem_ref.at[rows], sem_ref.at[b])


def _gather_group(g, starts_ref, slot_ref, gate_ref, h2_ref, xs_ref, gs_ref, row0, t, cap, arrive):
    sub = lax.broadcasted_iota(jnp.int32, (SLOT_WINDOW, TOKEN_BLOCK), 0)
    experts = [g * PACK + j for j in range(PACK)]
    assert (t // TOKEN_BLOCK) % GATHER_UNROLL == 0

    def window(b, first, i):
        hb = pl.ds(pl.multiple_of(b * TOKEN_BLOCK, TOKEN_BLOCK), TOKEN_BLOCK)
        hits, dst = [], []
        for j in range(PACK):
            lo = first[j] + i * SLOT_WINDOW
            w = jnp.minimum(lo, cap - SLOT_WINDOW)
            srow = slot_ref[j, pl.ds(b, 1), :]
            hits.append((srow == w + sub) & (srow >= lo))
            dst.append(pl.ds(pl.multiple_of(row0 + w, 16), SLOT_WINDOW))
        onehot = jnp.concatenate([jnp.where(h, 1.0, 0.0) for h in hits], axis=0).astype(BF16)
        got = _dot(onehot, h2_ref[hb, :])
        for j in range(PACK):
            piece = got[j * SLOT_WINDOW:(j + 1) * SLOT_WINDOW].astype(BF16)
            xs_ref[j, dst[j], :] = xs_ref[j, dst[j], :] + piece
            grow = gate_ref[j, pl.ds(b, 1), :]
            gs_ref[j, dst[j], :] += jnp.sum(jnp.where(hits[j], grow, 0.0), axis=1, keepdims=True)

    def blocks(q, carry):
        pending = []
        for u in range(GATHER_UNROLL):
            arrive(q * GATHER_UNROLL + u)
        for u in range(GATHER_UNROLL):
            b = q * GATHER_UNROLL + u
            first, n_windows = _pack_windows(starts_ref, b, experts, cap)
            window(b, first, 0)
            pending.append((b, first, n_windows))
        for b, first, n_windows in pending:
            def more(i, carry, b=b, first=first):
                window(b, first, i)
                return carry

            lax.fori_loop(1, n_windows, more, 0)
        return carry

    lax.fori_loop(0, t // TOKEN_BLOCK // GATHER_UNROLL, blocks, 0)


def _experts_kernel(sp_ref, ss_ref, h2p_hbm, h2s_hbm, slotp_ref, slots_ref, gatep_ref, gates_ref,
                    wg0_ref, wg1_ref, wu0_ref, wu1_ref, wd_ref, yp_ref, ys_ref, xs_ref, gs_ref,
                    acc_ref, h2p_ref, h2s_ref, semp_ref, sems_ref, *, tp, ts, capp, caps):
    g = pl.program_id(0)
    step = pl.program_id(1)
    j = step // N_FF_TILES
    f = step % N_FF_TILES
    loading = [(h2p_hbm, h2p_ref, semp_ref, tp // TOKEN_BLOCK),
               (h2s_hbm, h2s_ref, sems_ref, ts // TOKEN_BLOCK)]

    @pl.when((step == 0) & (g == 0))
    def _():
        for hbm_ref, vmem_ref, sem_ref, n_blocks in loading:
            for b in range(n_blocks):
                _block_copy(hbm_ref, vmem_ref, sem_ref, b).start()

    def arrive(hbm_ref, vmem_ref, sem_ref, _):
        def wait(b):
            @pl.when(g == 0)
            def _():
                _block_copy(hbm_ref, vmem_ref, sem_ref, b).wait()

        return wait

    @pl.when(step == 0)
    def _():
        xs_ref[...] = jnp.zeros(xs_ref.shape, BF16)
        gs_ref[...] = jnp.zeros(gs_ref.shape, F32)
        _gather_group(g, sp_ref, slotp_ref, gatep_ref, h2p_ref, xs_ref, gs_ref, 0, tp, capp,
                      arrive(*loading[0]))
        _gather_group(g, ss_ref, slots_ref, gates_ref, h2s_ref, xs_ref, gs_ref, capp, ts, caps,
                      arrive(*loading[1]))

    @pl.when(f == 0)
    def _():
        acc_ref[...] = jnp.zeros(acc_ref.shape, F32)

    half = D_MODEL // 2
    x0, x1 = xs_ref[j, :, 0:half], xs_ref[j, :, half:D_MODEL]
    a = _dot(x0, wg0_ref[0].astype(BF16)) + _dot(x1, wg1_ref[0].astype(BF16))
    u = _dot(x0, wu0_ref[0].astype(BF16)) + _dot(x1, wu1_ref[0].astype(BF16))
    acc_ref[...] += _dot((_silu(a) * u).astype(BF16), wd_ref[0].astype(BF16))

    @pl.when(f == N_FF_TILES - 1)
    def _():
        yp_ref[0] = (acc_ref[0:capp, :] * gs_ref[j, 0:capp, :]).astype(BF16)
        ys_ref[0] = (acc_ref[capp:capp + caps, :] * gs_ref[j, capp:capp + caps, :]).astype(BF16)


def _experts(starts_p, starts_s, h2p, h2s, slot_p, slot_s, gate_p, gate_s, w_gate, w_up, w_down):
    tp, ts = h2p.shape[0], h2s.shape[0]
    capp = EC_CAPACITY_FACTOR * tp // N_EXPERTS
    caps = EC_CAPACITY_FACTOR * ts // N_EXPERTS
    rows = capp + caps
    nbp, nbs = tp // TOKEN_BLOCK, ts // TOKEN_BLOCK
    expert = lambda g, s: g * PACK + s // N_FF_TILES
    half_tile = lambda k: pl.BlockSpec((1, D_MODEL // 2, FF_TILE),
                                       lambda g, s, *_: (expert(g, s), k, s % N_FF_TILES))
    grid_spec = pltpu.PrefetchScalarGridSpec(
        num_scalar_prefetch=2,
        grid=(N_EXPERTS // PACK, PACK * N_FF_TILES),
        in_specs=[
            pl.BlockSpec(memory_space=pl.ANY),
            pl.BlockSpec(memory_space=pl.ANY),
            pl.BlockSpec((PACK, nbp, TOKEN_BLOCK), lambda g, s, *_: (g, 0, 0)),
            pl.BlockSpec((PACK, nbs, TOKEN_BLOCK), lambda g, s, *_: (g, 0, 0)),
            pl.BlockSpec((PACK, nbp, TOKEN_BLOCK), lambda g, s, *_: (g, 0, 0)),
            pl.BlockSpec((PACK, nbs, TOKEN_BLOCK), lambda g, s, *_: (g, 0, 0)),
            half_tile(0), half_tile(1), half_tile(0), half_tile(1),
            pl.BlockSpec((1, FF_TILE, D_MODEL), lambda g, s, *_: (expert(g, s), s % N_FF_TILES, 0)),
        ],
        out_specs=[
            pl.BlockSpec((1, capp, D_MODEL), lambda g, s, *_: (expert(g, s), 0, 0)),
            pl.BlockSpec((1, caps, D_MODEL), lambda g, s, *_: (expert(g, s), 0, 0)),
        ],
        scratch_shapes=[pltpu.VMEM((PACK, rows, D_MODEL), BF16), pltpu.VMEM((PACK, rows, 1), F32),
                        pltpu.VMEM((rows, D_MODEL), F32),
                        pltpu.VMEM((tp, D_MODEL), BF16), pltpu.VMEM((ts, D_MODEL), BF16),
                        pltpu.SemaphoreType.DMA((nbp,)), pltpu.SemaphoreType.DMA((nbs,))],
    )
    return pl.pallas_call(
        functools.partial(_experts_kernel, tp=tp, ts=ts, capp=capp, caps=caps),
        out_shape=[jax.ShapeDtypeStruct((N_EXPERTS, capp, D_MODEL), BF16),
                   jax.ShapeDtypeStruct((N_EXPERTS, caps, D_MODEL), BF16)],
        grid_spec=grid_spec,
        compiler_params=pltpu.CompilerParams(dimension_semantics=("arbitrary", "arbitrary"),
                                             vmem_limit_bytes=VMEM_LIMIT),
        name="experts",
    )(starts_p, starts_s, h2p, h2s, slot_p, slot_s, gate_p, gate_s, w_gate, w_gate, w_up, w_up,
      w_down)


def _combine_kernel(st_ref, x1_ref, slot_ref, mod_ref, fg_ref, y_hbm, out_ref, acc_ref, y_ref,
                    sem_ref, arrived_ref, *, cap):
    step = pl.program_id(0)
    n_chunks = Y_CHUNKS
    chunk_rows = cap // Y_CHUNKS
    n_parts = COMBINE_ROWS // TOKEN_BLOCK
    last_block = step * n_parts + n_parts - 1

    def chunk_copy(c):
        rows = pl.ds(c * chunk_rows, chunk_rows)
        return pltpu.make_async_copy(y_hbm.at[:, rows, :], y_ref.at[:, rows, :], sem_ref.at[c])

    @pl.when(step == 0)
    def _():
        arrived_ref[0] = 0
        for c in range(n_chunks):
            chunk_copy(c).start()

    top = functools.reduce(jnp.maximum, [st_ref[e, last_block + 1] for e in range(N_EXPERTS)])
    want = pl.cdiv(jnp.minimum(top + SLOT_WINDOW, cap), chunk_rows)
    have = arrived_ref[0]
    for c in range(n_chunks):
        @pl.when((c >= have) & (c < want))
        def _(c=c):
            chunk_copy(c).wait()
    arrived_ref[0] = jnp.maximum(have, want)

    lane = lax.broadcasted_iota(jnp.int32, (TOKEN_BLOCK, PACK * SLOT_WINDOW), 1)
    gate2 = mod_ref[0][:, 5 * D_MODEL:6 * D_MODEL]

    def scatter(slot, experts, first, i):
        target = None
        windows = []
        for j, e in enumerate(experts):
            lo = first[j] + i * SLOT_WINDOW
            w = jnp.minimum(lo, cap - SLOT_WINDOW)
            sc = slot[:, e:e + 1]
            col = jnp.where(sc >= lo, sc - w + j * SLOT_WINDOW, -1)
            target = col if target is None else jnp.where(lane < j * SLOT_WINDOW, target, col)
            windows.append(y_ref[e, pl.ds(pl.multiple_of(w, 16), SLOT_WINDOW), :])
        onehot = jnp.where(target == lane, 1.0, 0.0).astype(BF16)
        return _dot(onehot, jnp.concatenate(windows, axis=0))

    def finish(rows):
        x2 = x1_ref[rows, :] + gate2 * acc_ref[rows, :]
        out_ref[rows, :] = _rms(x2) * fg_ref[...]

    parts = []
    for p in range(n_parts):
        rows = slice(p * TOKEN_BLOCK, (p + 1) * TOKEN_BLOCK)
        slot = slot_ref[rows, :]
        groups = []
        total = None
        for g in range(N_EXPERTS // PACK):
            experts = list(range(g * PACK, (g + 1) * PACK))
            first, n_windows = _pack_windows(st_ref, step * n_parts + p, experts, cap)
            groups.append((experts, first, n_windows))
            part = scatter(slot, experts, first, 0)
            total = part if total is None else total + part
        acc_ref[rows, :] = total
        parts.append((rows, slot, groups))
    for rows, _, _ in parts:
        finish(rows)

    for rows, slot, groups in parts:
        most = functools.reduce(jnp.maximum, [n_windows for _, _, n_windows in groups])

        @pl.when(most > 1)
        def _(rows=rows, slot=slot, groups=groups):
            for experts, first, n_windows in groups:
                def more(i, carry, experts=experts, first=first):
                    acc_ref[rows, :] += scatter(slot, experts, first, i)
                    return carry

                lax.fori_loop(1, n_windows, more, 0)
            finish(rows)


def _combine(starts, x1, slot_te, mod_rows, mod_first_row, tokens_per_mod_row, final_g, y):
    t = x1.shape[0]
    cap = y.shape[1]
    assert tokens_per_mod_row is None or tokens_per_mod_row % COMBINE_ROWS == 0
    steps_per_row = None if tokens_per_mod_row is None else tokens_per_mod_row // COMBINE_ROWS
    grid_spec = pltpu.PrefetchScalarGridSpec(
        num_scalar_prefetch=1,
        grid=(t // COMBINE_ROWS,),
        in_specs=[
            pl.BlockSpec((COMBINE_ROWS, D_MODEL), lambda b, *_: (b, 0)),
            pl.BlockSpec((COMBINE_ROWS, N_EXPERTS), lambda b, *_: (b, 0)),
            _mod_row_spec(mod_first_row, steps_per_row),
            pl.BlockSpec((1, D_MODEL), lambda b, *_: (0, 0)),
            pl.BlockSpec(memory_space=pl.ANY),
        ],
        out_specs=pl.BlockSpec((COMBINE_ROWS, D_MODEL), lambda b, *_: (b, 0)),
        scratch_shapes=[pltpu.VMEM((COMBINE_ROWS, D_MODEL), F32), pltpu.VMEM(y.shape, BF16),
                        pltpu.SemaphoreType.DMA((Y_CHUNKS,)), pltpu.SMEM((1,), jnp.int32)],
    )
    return pl.pallas_call(
        functools.partial(_combine_kernel, cap=cap),
        out_shape=jax.ShapeDtypeStruct((t, D_MODEL), F32),
        grid_spec=grid_spec,
        compiler_params=pltpu.CompilerParams(dimension_semantics=("arbitrary",),
                                             vmem_limit_bytes=VMEM_LIMIT),
        name="combine",
    )(starts, x1, slot_te, mod_rows, final_g.reshape(1, D_MODEL), y)


def kernel(x_prompt, x_sample, state_ret, c, c_ctx, norm1_g, norm2_g, final_g, w_mod, b_mod, w_in,
           w_fmix, w_out, w_router, w_gate, w_up, w_down):
    bp, seq, _ = x_prompt.shape
    bs, dec_seq, _ = x_sample.shape
    assert w_mod.shape[0] == 1, "single-layer trunk"
    tp, ts = bp * seq, bs * dec_seq

    cond = jnp.concatenate([c_ctx[None, :], c], axis=0)
    mod = _modulation(cond, w_mod[0], b_mod[0])
    ctx_row, lat_row = 0, 1

    w_in_bf = w_in[0].astype(BF16)
    w_out_bf = w_out[0].astype(BF16)
    mix_p, states = _mixer(x_prompt, mod, ctx_row, False, None, True, False, norm1_g[0], w_in_bf,
                           w_fmix[0])
    (mix_s,) = _mixer(x_sample, mod, lat_row, True, state_ret[:, 0], False, True, norm1_g[0],
                      w_in_bf, w_fmix[0])
    x1p, h2p, affp = _post(x_prompt.reshape(tp, D_MODEL), mix_p, mod, ctx_row, None,
                           norm2_g[0], w_out_bf, w_router[0])
    x1s, h2s, affs = _post(x_sample.reshape(ts, D_MODEL), mix_s, mod, lat_row, dec_seq,
                           norm2_g[0], w_out_bf, w_router[0])

    slot_p, gate_p, starts_p = _route(affp.T)
    slot_s, gate_s, starts_s = _route(affs.T)

    yp, ys = _experts(starts_p, starts_s, h2p, h2s, slot_p, slot_s, gate_p, gate_s,
                      w_gate[0], w_up[0], w_down[0])

    out_p = _combine(starts_p, x1p, slot_p.reshape(N_EXPERTS, tp).T, mod, ctx_row, None, final_g, yp)
    out_s = _combine(starts_s, x1s, slot_s.reshape(N_EXPERTS, ts).T, mod, lat_row, dec_seq,
                     final_g, ys)

    y_prompt = out_p.reshape(bp, seq, D_MODEL)
    y_sample = out_s.reshape(bs, dec_seq, D_MODEL)
    state_new = states.reshape(bp, 1, 2, N_RET_HEADS, HEAD_DIM, HEAD_DIM).astype(x_prompt.dtype)
    return (y_prompt, y_sample, state_new)
```

```python
import functools
import math

import jax
import jax.numpy as jnp
import numpy as np
from jax import lax
from jax.experimental import pallas as pl
from jax.experimental.pallas import tpu as pltpu

D_MODEL = 1024
D_FOURIER = 512
N_FOURIER_GROUPS = 4
FOURIER_GROUP_W = 128
D_RET = 512
N_RET_HEADS = 4
HEAD_DIM = 128
CHUNK = 256
GRID_W = 64
N_EXPERTS = 16
EC_CAPACITY_FACTOR = 2
D_EXPERT_FF = 2816
ROPE_BASE = 10000.0
EPS = 1e-6
D_IN_PROJ = D_FOURIER + 5 * D_RET
LOG_GAMMA_FWD = np.log(1.0 - 2.0 ** (-5.0 - np.arange(N_RET_HEADS))).astype(np.float32)
LOG_GAMMA_BWD = np.log(1.0 - 2.0 ** (-5.5 - np.arange(N_RET_HEADS))).astype(np.float32)

TOKEN_BLOCK = 256
SLOT_WINDOW = 64
PACK = TOKEN_BLOCK // SLOT_WINDOW
GATHER_UNROLL = 4
Y_CHUNKS = 4
FF_TILE = 256
N_FF_TILES = D_EXPERT_FF // FF_TILE
MOD_TILE = 1024
MIXER_ROWS = 1024
POST_ROWS = 512
POST_PART = 256
COMBINE_ROWS = 512
VMEM_LIMIT = 56 * 1024 * 1024

F32 = jnp.float32
BF16 = jnp.bfloat16


def _dot(a, b):
    return jnp.dot(a, b, preferred_element_type=F32)


def _dot_nt(a, b):
    return lax.dot_general(a, b, (((1,), (1,)), ((), ())), preferred_element_type=F32)


def _silu(x):
    return x * jax.nn.sigmoid(x)


def _mod_kernel(condt_ref, w_ref, b_ref, out_ref, *, n_cond):
    s = _silu(condt_ref[...])
    w = w_ref[...]
    out_ref[...] = jnp.zeros(out_ref.shape, F32)
    for r in range(n_cond):
        out_ref[r] = jnp.sum(w * s[:, r:r + 1], axis=0, keepdims=True) + b_ref[...]


def _modulation(cond_rows, w_mod, b_mod):
    n_cond = cond_rows.shape[0]
    condt = jnp.zeros((D_MODEL, 8), F32).at[:, :n_cond].set(cond_rows.T)
    n_out = w_mod.shape[1]
    return pl.pallas_call(
        functools.partial(_mod_kernel, n_cond=n_cond),
        out_shape=jax.ShapeDtypeStruct((8, 1, n_out), F32),
        grid=(n_out // MOD_TILE,),
        in_specs=[
            pl.BlockSpec((D_MODEL, 8), lambda j: (0, 0)),
            pl.BlockSpec((D_MODEL, MOD_TILE), lambda j: (0, j)),
            pl.BlockSpec((1, MOD_TILE), lambda j: (0, j)),
        ],
        out_specs=pl.BlockSpec((8, 1, MOD_TILE), lambda j: (0, 0, j)),
        compiler_params=pltpu.CompilerParams(dimension_semantics=("arbitrary",)),
        name="mod",
    )(condt, w_mod, b_mod.reshape(1, n_out))


def _rms(x):
    return x * lax.rsqrt(jnp.mean(x * x, axis=-1, keepdims=True) + EPS)


def _groupnorm(o):
    mu = jnp.mean(o, axis=-1, keepdims=True)
    c = o - mu
    return c * lax.rsqrt(jnp.mean(c * c, axis=-1, keepdims=True) + EPS)


def _split_hi_lo(x):
    hi = x.astype(BF16)
    lo = (x - hi.astype(F32)).astype(BF16)
    return hi, lo


def _mixer_kernel(*refs, n, use_rope, has_state_in, emit_state):
    it = iter(refs)
    x_ref, mod_ref, g1_ref, win_ref, wfmix_ref = (next(it) for _ in range(5))
    cw_ref, cn_ref, sn_ref, dmat_ref, qdec_ref, kdec_ref, sdec_ref = (next(it) for _ in range(7))
    cos_ref = sin_ref = s0_ref = st_ref = None
    if use_rope:
        cos_ref, sin_ref = next(it), next(it)
    if has_state_in:
        s0_ref = next(it)
    mix_ref = next(it)
    if emit_state:
        st_ref = next(it)
    p_ref, of_ref, ob_ref = next(it), next(it), next(it)

    n_seq = MIXER_ROWS // n
    chunks_per_seq = n // CHUNK
    mod = mod_ref[0]
    shift1 = mod[:, 0:D_MODEL]
    scale1 = mod[:, D_MODEL:2 * D_MODEL]

    h = (_rms(x_ref[...]) * g1_ref[...] * (1.0 + scale1) + shift1).astype(BF16)
    for j in range(D_IN_PROJ // 512):
        p_ref[:, j * 512:(j + 1) * 512] = _dot(h, win_ref[:, j * 512:(j + 1) * 512].astype(BF16))

    xf = p_ref[:, 0:D_FOURIER].astype(BF16)
    xc, xs = [], []
    cw = cw_ref[...].astype(BF16)
    for g in range(N_FOURIER_GROUPS):
        t = _dot(xf[:, g * FOURIER_GROUP_W:(g + 1) * FOURIER_GROUP_W], cw)
        xc.append(t[:, :FOURIER_GROUP_W].astype(BF16))
        xs.append(t[:, FOURIER_GROUP_W:].astype(BF16))
    xc = jnp.concatenate(xc, axis=1)
    xs = jnp.concatenate(xs, axis=1)
    cn = cn_ref[...].astype(BF16)
    sn = sn_ref[...].astype(BF16)
    for s in range(n_seq):
        rs = slice(s * n, (s + 1) * n)
        fre = (_dot(cn, xc[rs]) - _dot(sn, xs[rs])) * (1.0 / math.sqrt(n * FOURIER_GROUP_W))
        fre = fre.astype(BF16)
        for g in range(N_FOURIER_GROUPS):
            sl = slice(g * FOURIER_GROUP_W, (g + 1) * FOURIER_GROUP_W)
            mix_ref[rs, sl] = _dot(fre[:, sl], wfmix_ref[g].astype(BF16)).astype(BF16)

    for hh in range(N_RET_HEADS):
        base = D_FOURIER + hh * HEAD_DIM
        q = p_ref[:, base:base + HEAD_DIM]
        k = p_ref[:, base + D_RET:base + D_RET + HEAD_DIM]
        v = p_ref[:, base + 2 * D_RET:base + 2 * D_RET + HEAD_DIM]
        if use_rope:
            lane = lax.broadcasted_iota(jnp.int32, (MIXER_ROWS, HEAD_DIM), 1)
            first = (lane % 64) < 32

            def rope(t):
                swapped = jnp.where(first, pltpu.roll(t, HEAD_DIM - 32, 1), pltpu.roll(t, 32, 1))
                return t * cos_ref[...] + swapped * sin_ref[...]

            q, k = rope(q), rope(k)
        k = k * (HEAD_DIM ** -0.5)
        qb, vb = q.astype(BF16), v.astype(BF16)
        kb = k.astype(BF16)

        def initial(s, direction):
            if has_state_in:
                return s0_ref[s, direction, hh]
            return jnp.zeros((HEAD_DIM, HEAD_DIM), F32)

        for s in range(n_seq):
            parts = []
            for ci in range(chunks_per_seq):
                c = s * chunks_per_seq + ci
                rs = slice(c * CHUNK, (c + 1) * CHUNK)
                qc, kc, vc = qb[rs], kb[rs], vb[rs]
                qk = _dot_nt(qc, kc)
                lhs = jnp.concatenate([(qk * dmat_ref[0, hh]).astype(BF16),
                                       (qk * dmat_ref[1, hh]).astype(BF16),
                                       (k[rs] * kdec_ref[0, hh]).T.astype(BF16),
                                       (k[rs] * kdec_ref[1, hh]).T.astype(BF16)], axis=0)
                parts.append((rs, qc, _dot(lhs, vc)))
            sf = initial(s, 0)
            for ci in range(chunks_per_seq):
                rs, qc, r = parts[ci]
                o = r[0:CHUNK]
                if has_state_in or ci > 0:
                    o = o + qdec_ref[0, hh] * _dot(qc, sf.astype(BF16))
                of_ref[rs, :] = o
                sf = sf * sdec_ref[0, hh] + r[2 * CHUNK:2 * CHUNK + HEAD_DIM]
            sb = initial(s, 1)
            for ci in reversed(range(chunks_per_seq)):
                rs, qc, r = parts[ci]
                o = r[CHUNK:2 * CHUNK]
                if has_state_in or ci < chunks_per_seq - 1:
                    o = o + qdec_ref[1, hh] * _dot(qc, sb.astype(BF16))
                ob_ref[rs, :] = o
                sb = sb * sdec_ref[1, hh] + r[2 * CHUNK + HEAD_DIM:]
            if emit_state:
                st_ref[s, 0, hh] = sf
                st_ref[s, 1, hh] = sb

        gf = p_ref[:, base + 3 * D_RET:base + 3 * D_RET + HEAD_DIM]
        gb = p_ref[:, base + 4 * D_RET:base + 4 * D_RET + HEAD_DIM]
        y = _silu(gf) * _groupnorm(of_ref[...]) + _silu(gb) * _groupnorm(ob_ref[...])
        mix_ref[:, base:base + HEAD_DIM] = y.astype(BF16)


def _post_kernel(x_ref, mix_ref, mod_ref, g2_ref, wout_ref, wr_ref, x1_ref, h2_ref, aff_ref):
    mod = mod_ref[0]
    gate1 = mod[:, 2 * D_MODEL:3 * D_MODEL]
    shift2 = mod[:, 3 * D_MODEL:4 * D_MODEL]
    scale2 = mod[:, 4 * D_MODEL:5 * D_MODEL]
    wr_hi, wr_lo = _split_hi_lo(wr_ref[...])
    wr_both = jnp.concatenate([wr_hi, wr_lo], axis=1)
    parts = [slice(p * POST_PART, (p + 1) * POST_PART) for p in range(POST_ROWS // POST_PART)]
    x1 = []
    for rows in parts:
        x1.append(x_ref[rows, :] + gate1 * _dot(mix_ref[rows, :], wout_ref[...]))
        x1_ref[rows, :] = x1[-1]
    for rows, x1_part in zip(parts, x1):
        h2 = _rms(x1_part) * g2_ref[...] * (1.0 + scale2) + shift2
        h2_hi, h2_lo = _split_hi_lo(h2)
        h2_ref[rows, :] = h2_hi
        by_hi = _dot(h2_hi, wr_both)
        logits = by_hi[:, :N_EXPERTS] + (_dot(h2_lo, wr_hi) + by_hi[:, N_EXPERTS:])
        z = jnp.exp(logits - jnp.max(logits, axis=-1, keepdims=True))
        aff_ref[rows, :] = z / jnp.sum(z, axis=-1, keepdims=True)


def _dft_consts(n):
    w = FOURIER_GROUP_W
    jw = np.arange(w)
    angw = 2.0 * np.pi * np.outer(jw, jw) / w
    cw = np.concatenate([np.cos(angw), np.sin(angw)], axis=1)
    jn = np.arange(n)
    angn = 2.0 * np.pi * (np.outer(jn, jn) % n) / n
    return (jnp.asarray(cw, F32), jnp.asarray(np.cos(angn), F32), jnp.asarray(np.sin(angn), F32))


def _retention_consts():
    i = np.arange(CHUNK, dtype=np.float64)
    diff = i[:, None] - i[None, :]
    dmat = np.zeros((2, N_RET_HEADS, CHUNK, CHUNK))
    qdec = np.zeros((2, N_RET_HEADS, CHUNK, HEAD_DIM))
    kdec = np.zeros((2, N_RET_HEADS, CHUNK, HEAD_DIM))
    sdec = np.zeros((2, N_RET_HEADS, HEAD_DIM, HEAD_DIM))
    for hh in range(N_RET_HEADS):
        lf = float(LOG_GAMMA_FWD[hh])
        lb = float(LOG_GAMMA_BWD[hh])
        dmat[0, hh] = np.where(diff >= 0, np.exp(lf * np.maximum(diff, 0.0)), 0.0)
        dmat[1, hh] = np.where(diff <= 0, np.exp(lb * np.maximum(-diff, 0.0)), 0.0)
        qdec[0, hh] = np.exp(lf * (i + 1.0))[:, None]
        qdec[1, hh] = np.exp(lb * (CHUNK - i))[:, None]
        kdec[0, hh] = np.exp(lf * (CHUNK - 1.0 - i))[:, None]
        kdec[1, hh] = np.exp(lb * i)[:, None]
        sdec[0, hh] = math.exp(lf * CHUNK)
        sdec[1, hh] = math.exp(lb * CHUNK)
    return tuple(jnp.asarray(a, F32) for a in (dmat, qdec, kdec, sdec))


def _rope_consts(n):
    rows_n = n // GRID_W
    row = np.repeat(np.arange(rows_n, dtype=np.float64), GRID_W)
    col = np.tile(np.arange(GRID_W, dtype=np.float64), rows_n)
    n_pairs = HEAD_DIM // 4
    freqs = (np.float32(ROPE_BASE) ** (-np.arange(n_pairs, dtype=np.float32) / n_pairs)).astype(np.float64)
    ar = row[:, None] * freqs[None, :]
    ac = col[:, None] * freqs[None, :]
    cos = np.concatenate([np.cos(ar), np.cos(ar), np.cos(ac), np.cos(ac)], axis=1)
    sin = np.concatenate([-np.sin(ar), np.sin(ar), -np.sin(ac), np.sin(ac)], axis=1)
    return jnp.asarray(cos, F32), jnp.asarray(sin, F32)


def _const_spec(shape):
    nd = len(shape)
    return pl.BlockSpec(shape, lambda b, _nd=nd: (0,) * _nd, pipeline_mode=pl.Buffered(1))


def _mod_row_spec(first_row, blocks_per_row):
    if blocks_per_row is None:
        return pl.BlockSpec((1, 1, 6 * D_MODEL), lambda b, *_: (first_row, 0, 0))
    return pl.BlockSpec((1, 1, 6 * D_MODEL), lambda b, *_: (first_row + b // blocks_per_row, 0, 0))


def _mixer(x, mod_rows, mod_first_row, mod_per_batch, state_in, emit_state, use_rope, g1, w_in_bf,
           w_fmix):
    nb, n, _ = x.shape
    assert MIXER_ROWS % n == 0 and (nb * n) % MIXER_ROWS == 0
    n_seq = MIXER_ROWS // n
    has_state_in = state_in is not None
    cw, cn, sn = _dft_consts(n)
    dmat, qdec, kdec, sdec = _retention_consts()
    consts = [cw, cn, sn, dmat, qdec, kdec, sdec]
    if use_rope:
        assert n_seq == 1
        consts += list(_rope_consts(n))
    weights = [g1.reshape(1, D_MODEL), w_in_bf, w_fmix]

    if mod_per_batch:
        assert n % MIXER_ROWS == 0
    state_spec = pl.BlockSpec((n_seq, 2, N_RET_HEADS, HEAD_DIM, HEAD_DIM), lambda b: (b, 0, 0, 0, 0))
    row_spec = pl.BlockSpec((MIXER_ROWS, D_MODEL), lambda b: (b, 0))
    in_specs = [row_spec, _mod_row_spec(mod_first_row, n // MIXER_ROWS if mod_per_batch else None)]
    in_specs += [_const_spec(a.shape) for a in weights + consts]
    args = [x.reshape(nb * n, D_MODEL), mod_rows] + weights + consts
    if has_state_in:
        in_specs.append(state_spec)
        args.append(state_in)

    out_shape = [jax.ShapeDtypeStruct((nb * n, D_MODEL), BF16)]
    out_specs = [row_spec]
    if emit_state:
        out_shape.append(jax.ShapeDtypeStruct((nb, 2, N_RET_HEADS, HEAD_DIM, HEAD_DIM), F32))
        out_specs.append(state_spec)

    return pl.pallas_call(
        functools.partial(_mixer_kernel, n=n, use_rope=use_rope, has_state_in=has_state_in,
                          emit_state=emit_state),
        out_shape=out_shape,
        grid=(nb * n // MIXER_ROWS,),
        in_specs=in_specs,
        out_specs=out_specs,
        scratch_shapes=[pltpu.VMEM((MIXER_ROWS, D_IN_PROJ), F32),
                        pltpu.VMEM((MIXER_ROWS, HEAD_DIM), F32), pltpu.VMEM((MIXER_ROWS, HEAD_DIM), F32)],
        compiler_params=pltpu.CompilerParams(dimension_semantics=("arbitrary",),
                                             vmem_limit_bytes=VMEM_LIMIT),
        name="mixer_rope" if use_rope else "mixer",
    )(*args)


def _post(x, mix, mod_rows, mod_first_row, tokens_per_mod_row, g2, w_out_bf, w_router):
    t = x.shape[0]
    assert tokens_per_mod_row is None or tokens_per_mod_row % POST_ROWS == 0
    blocks_per_row = None if tokens_per_mod_row is None else tokens_per_mod_row // POST_ROWS
    row_spec = pl.BlockSpec((POST_ROWS, D_MODEL), lambda b: (b, 0))
    return pl.pallas_call(
        _post_kernel,
        out_shape=[jax.ShapeDtypeStruct((t, D_MODEL), F32),
                   jax.ShapeDtypeStruct((t, D_MODEL), BF16),
                   jax.ShapeDtypeStruct((t, N_EXPERTS), F32)],
        grid=(t // POST_ROWS,),
        in_specs=[row_spec, row_spec,
                  _mod_row_spec(mod_first_row, blocks_per_row),
                  _const_spec((1, D_MODEL)), _const_spec((D_MODEL, D_MODEL)),
                  _const_spec((D_MODEL, N_EXPERTS))],
        out_specs=[row_spec, row_spec, pl.BlockSpec((POST_ROWS, N_EXPERTS), lambda b: (b, 0))],
        compiler_params=pltpu.CompilerParams(dimension_semantics=("arbitrary",),
                                             vmem_limit_bytes=VMEM_LIMIT),
        name="post",
    )(x, mix, mod_rows, g2.reshape(1, D_MODEL), w_out_bf, w_router)


def _route_kernel(aff_ref, u_ref, slot_ref, gate_ref, starts_ref, *, t, cap):
    aff = aff_ref[...]

    def count(mask):
        return jnp.sum(mask.astype(jnp.int32), axis=1, keepdims=True)

    def as_float(word):
        return lax.bitcast_convert_type(word, F32)

    def value_step(i, cur):
        cand = cur | jnp.left_shift(jnp.int32(1), 30 - i)
        return jnp.where(count(aff >= as_float(cand)) >= cap, cand, cur)

    thr = lax.fori_loop(0, 31, value_step, jnp.zeros((N_EXPERTS, 1), jnp.int32))
    gt = aff >= as_float(thr + 1)
    eq = (aff >= as_float(thr)) & jnp.logical_not(gt)
    need = cap - count(gt)
    tok = lax.broadcasted_iota(jnp.int32, (N_EXPERTS, t), 1)
    nbits = t.bit_length() - 1

    def index_step(i, cur):
        cand = cur | jnp.left_shift(jnp.int32(1), nbits - 1 - i)
        return jnp.where(count(eq & (tok < cand)) < need, cand, cur)

    last = lax.fori_loop(0, nbits, index_step, jnp.zeros((N_EXPERTS, 1), jnp.int32))
    self = jnp.where(gt | (eq & (tok <= last)), 1.0, 0.0).astype(F32)

    carry = jnp.zeros((N_EXPERTS, 1), F32)
    starts_ref[...] = jnp.zeros(starts_ref.shape, jnp.int32)
    for b in range(t // TOKEN_BLOCK):
        sl = slice(b * TOKEN_BLOCK, (b + 1) * TOKEN_BLOCK)
        sbf = self[:, sl]
        pre = _dot(sbf.astype(BF16), u_ref[...]) + carry
        slot_ref[:, b, :] = jnp.where(sbf > 0.5, pre.astype(jnp.int32), -1)
        gate_ref[:, b, :] = aff[:, sl]
        starts_ref[:, b:b + 1] = carry.astype(jnp.int32)
        carry = carry + jnp.sum(sbf, axis=1, keepdims=True)
    nblk = t // TOKEN_BLOCK
    starts_ref[:, nblk:nblk + 1] = carry.astype(jnp.int32)


def _route(aff_et):
    t = aff_et.shape[1]
    cap = EC_CAPACITY_FACTOR * t // N_EXPERTS
    nblk = t // TOKEN_BLOCK
    upper = jnp.asarray(np.triu(np.ones((TOKEN_BLOCK, TOKEN_BLOCK)), 1), BF16)
    assert nblk + 1 <= 128
    blocked = (N_EXPERTS, nblk, TOKEN_BLOCK)
    return pl.pallas_call(
        functools.partial(_route_kernel, t=t, cap=cap),
        out_shape=[jax.ShapeDtypeStruct(blocked, jnp.int32), jax.ShapeDtypeStruct(blocked, F32),
                   jax.ShapeDtypeStruct((N_EXPERTS, 128), jnp.int32)],
        grid=(1,),
        in_specs=[pl.BlockSpec((N_EXPERTS, t), lambda i: (0, 0)),
                  pl.BlockSpec((TOKEN_BLOCK, TOKEN_BLOCK), lambda i: (0, 0))],
        out_specs=[pl.BlockSpec(blocked, lambda i: (0, 0, 0)), pl.BlockSpec(blocked, lambda i: (0, 0, 0)),
                   pl.BlockSpec((N_EXPERTS, 128), lambda i: (0, 0))],
        compiler_params=pltpu.CompilerParams(dimension_semantics=("arbitrary",)),
        name="route",
    )(aff_et, upper)


def _pack_windows(starts_ref, b, experts, cap):
    first = [jnp.minimum((starts_ref[e, b] // 16) * 16, cap - SLOT_WINDOW) for e in experts]
    rows = [jnp.where(starts_ref[e, b + 1] > starts_ref[e, b], starts_ref[e, b + 1] - w, 0)
            for e, w in zip(experts, first)]
    return first, pl.cdiv(functools.reduce(jnp.maximum, rows), SLOT_WINDOW)


def _block_copy(hbm_ref, vmem_ref, sem_ref, b):
    rows = pl.ds(pl.multiple_of(b * TOKEN_BLOCK, TOKEN_BLOCK), TOKEN_BLOCK)
    return pltpu.make_async_copy(hbm_ref.at[rows], vmem_ref.at[rows], sem_ref.at[b])


def _gather_group(g, starts_ref, slot_ref, gate_ref, h2_ref, xs_ref, gs_ref, row0, t, cap, arrive):
    sub = lax.broadcasted_iota(jnp.int32, (SLOT_WINDOW, TOKEN_BLOCK), 0)
    experts = [g * PACK + j for j in range(PACK)]
    assert (t // TOKEN_BLOCK) % GATHER_UNROLL == 0

    def window(b, first, i):
        hb = pl.ds(pl.multiple_of(b * TOKEN_BLOCK, TOKEN_BLOCK), TOKEN_BLOCK)
        hits, dst = [], []
        for j in range(PACK):
            lo = first[j] + i * SLOT_WINDOW
            w = jnp.minimum(lo, cap - SLOT_WINDOW)
            srow = slot_ref[j, pl.ds(b, 1), :]
            hits.append((srow == w + sub) & (srow >= lo))
            dst.append(pl.ds(pl.multiple_of(row0 + w, 16), SLOT_WINDOW))
        onehot = jnp.concatenate([jnp.where(h, 1.0, 0.0) for h in hits], axis=0).astype(BF16)
        got = _dot(onehot, h2_ref[hb, :])
        for j in range(PACK):
            piece = got[j * SLOT_WINDOW:(j + 1) * SLOT_WINDOW].astype(BF16)
            xs_ref[j, dst[j], :] = xs_ref[j, dst[j], :] + piece
            grow = gate_ref[j, pl.ds(b, 1), :]
            gs_ref[j, dst[j], :] += jnp.sum(jnp.where(hits[j], grow, 0.0), axis=1, keepdims=True)

    def blocks(q, carry):
        pending = []
        for u in range(GATHER_UNROLL):
            arrive(q * GATHER_UNROLL + u)
        for u in range(GATHER_UNROLL):
            b = q * GATHER_UNROLL + u
            first, n_windows = _pack_windows(starts_ref, b, experts, cap)
            window(b, first, 0)
            pending.append((b, first, n_windows))
        for b, first, n_windows in pending:
            def more(i, carry, b=b, first=first):
                window(b, first, i)
                return carry

            lax.fori_loop(1, n_windows, more, 0)
        return carry

    lax.fori_loop(0, t // TOKEN_BLOCK // GATHER_UNROLL, blocks, 0)


def _experts_kernel(sp_ref, ss_ref, h2p_hbm, h2s_hbm, slotp_ref, slots_ref, gatep_ref, gates_ref,
                    wg_ref, wu_ref, wd_ref, yp_ref, ys_ref, xs_ref, gs_ref, acc_ref,
                    h2p_ref, h2s_ref, semp_ref, sems_ref, *, tp, ts, capp, caps):
    g = pl.program_id(0)
    step = pl.program_id(1)
    j = step // N_FF_TILES
    f = step % N_FF_TILES
    loading = [(h2p_hbm, h2p_ref, semp_ref, tp // TOKEN_BLOCK),
               (h2s_hbm, h2s_ref, sems_ref, ts // TOKEN_BLOCK)]

    @pl.when((step == 0) & (g == 0))
    def _():
        for hbm_ref, vmem_ref, sem_ref, n_blocks in loading:
            for b in range(n_blocks):
                _block_copy(hbm_ref, vmem_ref, sem_ref, b).start()

    def arrive(hbm_ref, vmem_ref, sem_ref, _):
        def wait(b):
            @pl.when(g == 0)
            def _():
                _block_copy(hbm_ref, vmem_ref, sem_ref, b).wait()

        return wait

    @pl.when(step == 0)
    def _():
        xs_ref[...] = jnp.zeros(xs_ref.shape, BF16)
        gs_ref[...] = jnp.zeros(gs_ref.shape, F32)
        _gather_group(g, sp_ref, slotp_ref, gatep_ref, h2p_ref, xs_ref, gs_ref, 0, tp, capp,
                      arrive(*loading[0]))
        _gather_group(g, ss_ref, slots_ref, gates_ref, h2s_ref, xs_ref, gs_ref, capp, ts, caps,
                      arrive(*loading[1]))

    @pl.when(f == 0)
    def _():
        acc_ref[...] = jnp.zeros(acc_ref.shape, F32)

    x = xs_ref[j]
    a = _dot(x, wg_ref[0].astype(BF16))
    u = _dot(x, wu_ref[0].astype(BF16))
    acc_ref[...] += _dot((_silu(a) * u).astype(BF16), wd_ref[0].astype(BF16))

    @pl.when(f == N_FF_TILES - 1)
    def _():
        yp_ref[0] = (acc_ref[0:capp, :] * gs_ref[j, 0:capp, :]).astype(BF16)
        ys_ref[0] = (acc_ref[capp:capp + caps, :] * gs_ref[j, capp:capp + caps, :]).astype(BF16)


def _experts(starts_p, starts_s, h2p, h2s, slot_p, slot_s, gate_p, gate_s, w_gate, w_up, w_down):
    tp, ts = h2p.shape[0], h2s.shape[0]
    capp = EC_CAPACITY_FACTOR * tp // N_EXPERTS
    caps = EC_CAPACITY_FACTOR * ts // N_EXPERTS
    rows = capp + caps
    nbp, nbs = tp // TOKEN_BLOCK, ts // TOKEN_BLOCK
    expert = lambda g, s: g * PACK + s // N_FF_TILES
    grid_spec = pltpu.PrefetchScalarGridSpec(
        num_scalar_prefetch=2,
        grid=(N_EXPERTS // PACK, PACK * N_FF_TILES),
        in_specs=[
            pl.BlockSpec(memory_space=pl.ANY),
            pl.BlockSpec(memory_space=pl.ANY),
            pl.BlockSpec((PACK, nbp, TOKEN_BLOCK), lambda g, s, *_: (g, 0, 0)),
            pl.BlockSpec((PACK, nbs, TOKEN_BLOCK), lambda g, s, *_: (g, 0, 0)),
            pl.BlockSpec((PACK, nbp, TOKEN_BLOCK), lambda g, s, *_: (g, 0, 0)),
            pl.BlockSpec((PACK, nbs, TOKEN_BLOCK), lambda g, s, *_: (g, 0, 0)),
            pl.BlockSpec((1, D_MODEL, FF_TILE), lambda g, s, *_: (expert(g, s), 0, s % N_FF_TILES)),
            pl.BlockSpec((1, D_MODEL, FF_TILE), lambda g, s, *_: (expert(g, s), 0, s % N_FF_TILES)),
            pl.BlockSpec((1, FF_TILE, D_MODEL), lambda g, s, *_: (expert(g, s), s % N_FF_TILES, 0)),
        ],
        out_specs=[
            pl.BlockSpec((1, capp, D_MODEL), lambda g, s, *_: (expert(g, s), 0, 0)),
            pl.BlockSpec((1, caps, D_MODEL), lambda g, s, *_: (expert(g, s), 0, 0)),
        ],
        scratch_shapes=[pltpu.VMEM((PACK, rows, D_MODEL), BF16), pltpu.VMEM((PACK, rows, 1), F32),
                        pltpu.VMEM((rows, D_MODEL), F32),
                        pltpu.VMEM((tp, D_MODEL), BF16), pltpu.VMEM((ts, D_MODEL), BF16),
                        pltpu.SemaphoreType.DMA((nbp,)), pltpu.SemaphoreType.DMA((nbs,))],
    )
    return pl.pallas_call(
        functools.partial(_experts_kernel, tp=tp, ts=ts, capp=capp, caps=caps),
        out_shape=[jax.ShapeDtypeStruct((N_EXPERTS, capp, D_MODEL), BF16),
                   jax.ShapeDtypeStruct((N_EXPERTS, caps, D_MODEL), BF16)],
        grid_spec=grid_spec,
        compiler_params=pltpu.CompilerParams(dimension_semantics=("arbitrary", "arbitrary"),
                                             vmem_limit_bytes=VMEM_LIMIT),
        name="experts",
    )(starts_p, starts_s, h2p, h2s, slot_p, slot_s, gate_p, gate_s, w_gate, w_up, w_down)


def _combine_kernel(st_ref, x1_ref, slot_ref, mod_ref, fg_ref, y_hbm, out_ref, acc_ref, y_ref,
                    sem_ref, arrived_ref, *, cap):
    step = pl.program_id(0)
    n_chunks = Y_CHUNKS
    chunk_rows = cap // Y_CHUNKS
    n_parts = COMBINE_ROWS // TOKEN_BLOCK
    last_block = step * n_parts + n_parts - 1

    def chunk_copy(c):
        rows = pl.ds(c * chunk_rows, chunk_rows)
        return pltpu.make_async_copy(y_hbm.at[:, rows, :], y_ref.at[:, rows, :], sem_ref.at[c])

    @pl.when(step == 0)
    def _():
        arrived_ref[0] = 0
        for c in range(n_chunks):
            chunk_copy(c).start()

    top = functools.reduce(jnp.maximum, [st_ref[e, last_block + 1] for e in range(N_EXPERTS)])
    want = pl.cdiv(jnp.minimum(top + SLOT_WINDOW, cap), chunk_rows)
    have = arrived_ref[0]
    for c in range(n_chunks):
        @pl.when((c >= have) & (c < want))
        def _(c=c):
            chunk_copy(c).wait()
    arrived_ref[0] = jnp.maximum(have, want)

    lane = lax.broadcasted_iota(jnp.int32, (TOKEN_BLOCK, PACK * SLOT_WINDOW), 1)
    gate2 = mod_ref[0][:, 5 * D_MODEL:6 * D_MODEL]

    def scatter(slot, experts, first, i):
        target = None
        windows = []
        for j, e in enumerate(experts):
            lo = first[j] + i * SLOT_WINDOW
            w = jnp.minimum(lo, cap - SLOT_WINDOW)
            sc = slot[:, e:e + 1]
            col = jnp.where(sc >= lo, sc - w + j * SLOT_WINDOW, -1)
            target = col if target is None else jnp.where(lane < j * SLOT_WINDOW, target, col)
            windows.append(y_ref[e, pl.ds(pl.multiple_of(w, 16), SLOT_WINDOW), :])
        onehot = jnp.where(target == lane, 1.0, 0.0).astype(BF16)
        return _dot(onehot, jnp.concatenate(windows, axis=0))

    def finish(rows):
        x2 = x1_ref[rows, :] + gate2 * acc_ref[rows, :]
        out_ref[rows, :] = _rms(x2) * fg_ref[...]

    parts = []
    for p in range(n_parts):
        rows = slice(p * TOKEN_BLOCK, (p + 1) * TOKEN_BLOCK)
        slot = slot_ref[rows, :]
        groups = []
        total = None
        for g in range(N_EXPERTS // PACK):
            experts = list(range(g * PACK, (g + 1) * PACK))
            first, n_windows = _pack_windows(st_ref, step * n_parts + p, experts, cap)
            groups.append((experts, first, n_windows))
            part = scatter(slot, experts, first, 0)
            total = part if total is None else total + part
        acc_ref[rows, :] = total
        parts.append((rows, slot, groups))
    for rows, _, _ in parts:
        finish(rows)

    for rows, slot, groups in parts:
        most = functools.reduce(jnp.maximum, [n_windows for _, _, n_windows in groups])

        @pl.when(most > 1)
        def _(rows=rows, slot=slot, groups=groups):
            for experts, first, n_windows in groups:
                def more(i, carry, experts=experts, first=first):
                    acc_ref[rows, :] += scatter(slot, experts, first, i)
                    return carry

                lax.fori_loop(1, n_windows, more, 0)
            finish(rows)


def _combine(starts, x1, slot_te, mod_rows, mod_first_row, tokens_per_mod_row, final_g, y):
    t = x1.shape[0]
    cap = y.shape[1]
    assert tokens_per_mod_row is None or tokens_per_mod_row % COMBINE_ROWS == 0
    steps_per_row = None if tokens_per_mod_row is None else tokens_per_mod_row // COMBINE_ROWS
    grid_spec = pltpu.PrefetchScalarGridSpec(
        num_scalar_prefetch=1,
        grid=(t // COMBINE_ROWS,),
        in_specs=[
            pl.BlockSpec((COMBINE_ROWS, D_MODEL), lambda b, *_: (b, 0)),
            pl.BlockSpec((COMBINE_ROWS, N_EXPERTS), lambda b, *_: (b, 0)),
            _mod_row_spec(mod_first_row, steps_per_row),
            pl.BlockSpec((1, D_MODEL), lambda b, *_: (0, 0)),
            pl.BlockSpec(memory_space=pl.ANY),
        ],
        out_specs=pl.BlockSpec((COMBINE_ROWS, D_MODEL), lambda b, *_: (b, 0)),
        scratch_shapes=[pltpu.VMEM((COMBINE_ROWS, D_MODEL), F32), pltpu.VMEM(y.shape, BF16),
                        pltpu.SemaphoreType.DMA((Y_CHUNKS,)), pltpu.SMEM((1,), jnp.int32)],
    )
    return pl.pallas_call(
        functools.partial(_combine_kernel, cap=cap),
        out_shape=jax.ShapeDtypeStruct((t, D_MODEL), F32),
        grid_spec=grid_spec,
        compiler_params=pltpu.CompilerParams(dimension_semantics=("arbitrary",),
                                             vmem_limit_bytes=VMEM_LIMIT),
        name="combine",
    )(starts, x1, slot_te, mod_rows, final_g.reshape(1, D_MODEL), y)


def kernel(x_prompt, x_sample, state_ret, c, c_ctx, norm1_g, norm2_g, final_g, w_mod, b_mod, w_in,
           w_fmix, w_out, w_router, w_gate, w_up, w_down):
    bp, seq, _ = x_prompt.shape
    bs, dec_seq, _ = x_sample.shape
    assert w_mod.shape[0] == 1, "single-layer trunk"
    tp, ts = bp * seq, bs * dec_seq

    cond = jnp.concatenate([c_ctx[None, :], c], axis=0)
    mod = _modulation(cond, w_mod[0], b_mod[0])
    ctx_row, lat_row = 0, 1

    w_in_bf = w_in[0]
    w_out_bf = w_out[0].astype(BF16)
    mix_p, states = _mixer(x_prompt, mod, ctx_row, False, None, True, False, norm1_g[0], w_in_bf,
                           w_fmix[0])
    (mix_s,) = _mixer(x_sample, mod, lat_row, True, state_ret[:, 0], False, True, norm1_g[0],
                      w_in_bf, w_fmix[0])
    x1p, h2p, affp = _post(x_prompt.reshape(tp, D_MODEL), mix_p, mod, ctx_row, None,
                           norm2_g[0], w_out_bf, w_router[0])
    x1s, h2s, affs = _post(x_sample.reshape(ts, D_MODEL), mix_s, mod, lat_row, dec_seq,
                           norm2_g[0], w_out_bf, w_router[0])

    slot_p, gate_p, starts_p = _route(affp.T)
    slot_s, gate_s, starts_s = _route(affs.T)

    yp, ys = _experts(starts_p, starts_s, h2p, h2s, slot_p, slot_s, gate_p, gate_s,
                      w_gate[0], w_up[0], w_down[0])

    out_p = _combine(starts_p, x1p, slot_p.reshape(N_EXPERTS, tp).T, mod, ctx_row, None, final_g, yp)
    out_s = _combine(starts_s, x1s, slot_s.reshape(N_EXPERTS, ts).T, mod, lat_row, dec_seq,
                     final_g, ys)

    y_prompt = out_p.reshape(bp, seq, D_MODEL)
    y_sample = out_s.reshape(bs, dec_seq, D_MODEL)
    state_new = states.reshape(bp, 1, 2, N_RET_HEADS, HEAD_DIM, HEAD_DIM).astype(x_prompt.dtype)
    return (y_prompt, y_sample, state_new)
```

```python
import functools
import math

import jax
import jax.numpy as jnp
import numpy as np
from jax import lax
from jax.experimental import pallas as pl
from jax.experimental.pallas import tpu as pltpu

D_MODEL = 1024
D_FOURIER = 512
N_FOURIER_GROUPS = 4
FOURIER_GROUP_W = 128
D_RET = 512
N_RET_HEADS = 4
HEAD_DIM = 128
CHUNK = 256
GRID_W = 64
N_EXPERTS = 16
EC_CAPACITY_FACTOR = 2
D_EXPERT_FF = 2816
ROPE_BASE = 10000.0
EPS = 1e-6
D_IN_PROJ = D_FOURIER + 5 * D_RET
LOG_GAMMA_FWD = np.log(1.0 - 2.0 ** (-5.0 - np.arange(N_RET_HEADS))).astype(np.float32)
LOG_GAMMA_BWD = np.log(1.0 - 2.0 ** (-5.5 - np.arange(N_RET_HEADS))).astype(np.float32)

TOKEN_BLOCK = 256
SLOT_WINDOW = 64
PACK = TOKEN_BLOCK // SLOT_WINDOW
GATHER_UNROLL = 4
Y_CHUNKS = 4
FF_TILE = 256
N_FF_TILES = D_EXPERT_FF // FF_TILE
MOD_TILE = 1024
MIXER_ROWS = 1024
POST_ROWS = 512
POST_PART = 256
COMBINE_ROWS = 512
VMEM_LIMIT = 56 * 1024 * 1024

F32 = jnp.float32
BF16 = jnp.bfloat16


def _dot(a, b):
    return jnp.dot(a, b, preferred_element_type=F32)


def _dot_nt(a, b):
    return lax.dot_general(a, b, (((1,), (1,)), ((), ())), preferred_element_type=F32)


def _silu(x):
    return x * jax.nn.sigmoid(x)


def _mod_kernel(condt_ref, w_ref, b_ref, out_ref, *, n_cond):
    s = _silu(condt_ref[...])
    w = w_ref[...]
    out_ref[...] = jnp.zeros(out_ref.shape, F32)
    for r in range(n_cond):
        out_ref[r] = jnp.sum(w * s[:, r:r + 1], axis=0, keepdims=True) + b_ref[...]


def _modulation(cond_rows, w_mod, b_mod):
    n_cond = cond_rows.shape[0]
    condt = jnp.zeros((D_MODEL, 8), F32).at[:, :n_cond].set(cond_rows.T)
    n_out = w_mod.shape[1]
    return pl.pallas_call(
        functools.partial(_mod_kernel, n_cond=n_cond),
        out_shape=jax.ShapeDtypeStruct((8, 1, n_out), F32),
        grid=(n_out // MOD_TILE,),
        in_specs=[
            pl.BlockSpec((D_MODEL, 8), lambda j: (0, 0)),
            pl.BlockSpec((D_MODEL, MOD_TILE), lambda j: (0, j)),
            pl.BlockSpec((1, MOD_TILE), lambda j: (0, j)),
        ],
        out_specs=pl.BlockSpec((8, 1, MOD_TILE), lambda j: (0, 0, j)),
        compiler_params=pltpu.CompilerParams(dimension_semantics=("arbitrary",)),
        name="mod",
    )(condt, w_mod, b_mod.reshape(1, n_out))


def _rms(x):
    return x * lax.rsqrt(jnp.mean(x * x, axis=-1, keepdims=True) + EPS)


def _groupnorm(o):
    mu = jnp.mean(o, axis=-1, keepdims=True)
    c = o - mu
    return c * lax.rsqrt(jnp.mean(c * c, axis=-1, keepdims=True) + EPS)


def _split_hi_lo(x):
    hi = x.astype(BF16)
    lo = (x - hi.astype(F32)).astype(BF16)
    return hi, lo


def _mixer_kernel(*refs, n, use_rope, has_state_in, emit_state):
    it = iter(refs)
    x_ref, mod_ref, g1_ref, win_ref, wfmix_ref = (next(it) for _ in range(5))
    cw_ref, cn_ref, sn_ref, dmat_ref, qdec_ref, kdec_ref, sdec_ref = (next(it) for _ in range(7))
    cos_ref = sin_ref = s0_ref = st_ref = None
    if use_rope:
        cos_ref, sin_ref = next(it), next(it)
    if has_state_in:
        s0_ref = next(it)
    mix_ref = next(it)
    if emit_state:
        st_ref = next(it)
    p_ref, of_ref, ob_ref = next(it), next(it), next(it)

    n_seq = MIXER_ROWS // n
    chunks_per_seq = n // CHUNK
    mod = mod_ref[0]
    shift1 = mod[:, 0:D_MODEL]
    scale1 = mod[:, D_MODEL:2 * D_MODEL]

    h = (_rms(x_ref[...]) * g1_ref[...] * (1.0 + scale1) + shift1).astype(BF16)
    for j in range(D_IN_PROJ // 512):
        p_ref[:, j * 512:(j + 1) * 512] = _dot(h, win_ref[:, j * 512:(j + 1) * 512].astype(BF16))

    xf = p_ref[:, 0:D_FOURIER].astype(BF16)
    xc, xs = [], []
    cw = cw_ref[...].astype(BF16)
    for g in range(N_FOURIER_GROUPS):
        t = _dot(xf[:, g * FOURIER_GROUP_W:(g + 1) * FOURIER_GROUP_W], cw)
        xc.append(t[:, :FOURIER_GROUP_W].astype(BF16))
        xs.append(t[:, FOURIER_GROUP_W:].astype(BF16))
    xc = jnp.concatenate(xc, axis=1)
    xs = jnp.concatenate(xs, axis=1)
    cn = cn_ref[...].astype(BF16)
    sn = sn_ref[...].astype(BF16)
    for s in range(n_seq):
        rs = slice(s * n, (s + 1) * n)
        fre = (_dot(cn, xc[rs]) - _dot(sn, xs[rs])) * (1.0 / math.sqrt(n * FOURIER_GROUP_W))
        fre = fre.astype(BF16)
        for g in range(N_FOURIER_GROUPS):
            sl = slice(g * FOURIER_GROUP_W, (g + 1) * FOURIER_GROUP_W)
            mix_ref[rs, sl] = _dot(fre[:, sl], wfmix_ref[g].astype(BF16)).astype(BF16)

    for hh in range(N_RET_HEADS):
        base = D_FOURIER + hh * HEAD_DIM
        q = p_ref[:, base:base + HEAD_DIM]
        k = p_ref[:, base + D_RET:base + D_RET + HEAD_DIM]
        v = p_ref[:, base + 2 * D_RET:base + 2 * D_RET + HEAD_DIM]
        if use_rope:
            lane = lax.broadcasted_iota(jnp.int32, (MIXER_ROWS, HEAD_DIM), 1)
            first = (lane % 64) < 32

            def rope(t):
                swapped = jnp.where(first, pltpu.roll(t, HEAD_DIM - 32, 1), pltpu.roll(t, 32, 1))
                return t * cos_ref[...] + swapped * sin_ref[...]

            q, k = rope(q), rope(k)
        k = k * (HEAD_DIM ** -0.5)
        qb, vb = q.astype(BF16), v.astype(BF16)
        kb = k.astype(BF16)

        def initial(s, direction):
            if has_state_in:
                return s0_ref[s, direction, hh]
            return jnp.zeros((HEAD_DIM, HEAD_DIM), F32)

        for s in range(n_seq):
            parts = []
            for ci in range(chunks_per_seq):
                c = s * chunks_per_seq + ci
                rs = slice(c * CHUNK, (c + 1) * CHUNK)
                qc, kc, vc = qb[rs], kb[rs], vb[rs]
                qk = _dot_nt(qc, kc)
                lhs = jnp.concatenate([(qk * dmat_ref[0, hh]).astype(BF16),
                                       (qk * dmat_ref[1, hh]).astype(BF16),
                                       (k[rs] * kdec_ref[0, hh]).T.astype(BF16),
                                       (k[rs] * kdec_ref[1, hh]).T.astype(BF16)], axis=0)
                parts.append((rs, qc, _dot(lhs, vc)))
            sf = initial(s, 0)
            for ci in range(chunks_per_seq):
                rs, qc, r = parts[ci]
                o = r[0:CHUNK]
                if has_state_in or ci > 0:
                    o = o + qdec_ref[0, hh] * _dot(qc, sf.astype(BF16))
                of_ref[rs, :] = o
                sf = sf * sdec_ref[0, hh] + r[2 * CHUNK:2 * CHUNK + HEAD_DIM]
            sb = initial(s, 1)
            for ci in reversed(range(chunks_per_seq)):
                rs, qc, r = parts[ci]
                o = r[CHUNK:2 * CHUNK]
                if has_state_in or ci < chunks_per_seq - 1:
                    o = o + qdec_ref[1, hh] * _dot(qc, sb.astype(BF16))
                ob_ref[rs, :] = o
                sb = sb * sdec_ref[1, hh] + r[2 * CHUNK + HEAD_DIM:]
            if emit_state:
                st_ref[s, 0, hh] = sf
                st_ref[s, 1, hh] = sb

        gf = p_ref[:, base + 3 * D_RET:base + 3 * D_RET + HEAD_DIM]
        gb = p_ref[:, base + 4 * D_RET:base + 4 * D_RET + HEAD_DIM]
        y = _silu(gf) * _groupnorm(of_ref[...]) + _silu(gb) * _groupnorm(ob_ref[...])
        mix_ref[:, base:base + HEAD_DIM] = y.astype(BF16)


def _post_kernel(x_ref, mix_ref, mod_ref, g2_ref, wout_ref, wr_ref, x1_ref, h2_ref, aff_ref):
    mod = mod_ref[0]
    gate1 = mod[:, 2 * D_MODEL:3 * D_MODEL]
    shift2 = mod[:, 3 * D_MODEL:4 * D_MODEL]
    scale2 = mod[:, 4 * D_MODEL:5 * D_MODEL]
    wr_hi, wr_lo = _split_hi_lo(wr_ref[...])
    wr_both = jnp.concatenate([wr_hi, wr_lo], axis=1)
    parts = [slice(p * POST_PART, (p + 1) * POST_PART) for p in range(POST_ROWS // POST_PART)]
    x1 = []
    for rows in parts:
        x1.append(x_ref[rows, :] + gate1 * _dot(mix_ref[rows, :], wout_ref[...]))
        x1_ref[rows, :] = x1[-1]
    for rows, x1_part in zip(parts, x1):
        h2 = _rms(x1_part) * g2_ref[...] * (1.0 + scale2) + shift2
        h2_hi, h2_lo = _split_hi_lo(h2)
        h2_ref[rows, :] = h2_hi
        by_hi = _dot(h2_hi, wr_both)
        logits = by_hi[:, :N_EXPERTS] + (_dot(h2_lo, wr_hi) + by_hi[:, N_EXPERTS:])
        z = jnp.exp(logits - jnp.max(logits, axis=-1, keepdims=True))
        aff = z / jnp.sum(z, axis=-1, keepdims=True)
        lanes = jnp.concatenate([aff, jnp.zeros((POST_PART, 128 - N_EXPERTS), F32)], axis=1)
        aff_ref[:, rows] = lanes.T[0:N_EXPERTS, :]


def _dft_consts(n):
    w = FOURIER_GROUP_W
    jw = np.arange(w)
    angw = 2.0 * np.pi * np.outer(jw, jw) / w
    cw = np.concatenate([np.cos(angw), np.sin(angw)], axis=1)
    jn = np.arange(n)
    angn = 2.0 * np.pi * (np.outer(jn, jn) % n) / n
    return (jnp.asarray(cw, F32), jnp.asarray(np.cos(angn), F32), jnp.asarray(np.sin(angn), F32))


def _retention_consts():
    i = np.arange(CHUNK, dtype=np.float64)
    diff = i[:, None] - i[None, :]
    dmat = np.zeros((2, N_RET_HEADS, CHUNK, CHUNK))
    qdec = np.zeros((2, N_RET_HEADS, CHUNK, HEAD_DIM))
    kdec = np.zeros((2, N_RET_HEADS, CHUNK, HEAD_DIM))
    sdec = np.zeros((2, N_RET_HEADS, HEAD_DIM, HEAD_DIM))
    for hh in range(N_RET_HEADS):
        lf = float(LOG_GAMMA_FWD[hh])
        lb = float(LOG_GAMMA_BWD[hh])
        dmat[0, hh] = np.where(diff >= 0, np.exp(lf * np.maximum(diff, 0.0)), 0.0)
        dmat[1, hh] = np.where(diff <= 0, np.exp(lb * np.maximum(-diff, 0.0)), 0.0)
        qdec[0, hh] = np.exp(lf * (i + 1.0))[:, None]
        qdec[1, hh] = np.exp(lb * (CHUNK - i))[:, None]
        kdec[0, hh] = np.exp(lf * (CHUNK - 1.0 - i))[:, None]
        kdec[1, hh] = np.exp(lb * i)[:, None]
        sdec[0, hh] = math.exp(lf * CHUNK)
        sdec[1, hh] = math.exp(lb * CHUNK)
    return tuple(jnp.asarray(a, F32) for a in (dmat, qdec, kdec, sdec))


def _rope_consts(n):
    rows_n = n // GRID_W
    row = np.repeat(np.arange(rows_n, dtype=np.float64), GRID_W)
    col = np.tile(np.arange(GRID_W, dtype=np.float64), rows_n)
    n_pairs = HEAD_DIM // 4
    freqs = (np.float32(ROPE_BASE) ** (-np.arange(n_pairs, dtype=np.float32) / n_pairs)).astype(np.float64)
    ar = row[:, None] * freqs[None, :]
    ac = col[:, None] * freqs[None, :]
    cos = np.concatenate([np.cos(ar), np.cos(ar), np.cos(ac), np.cos(ac)], axis=1)
    sin = np.concatenate([-np.sin(ar), np.sin(ar), -np.sin(ac), np.sin(ac)], axis=1)
    return jnp.asarray(cos, F32), jnp.asarray(sin, F32)


def _const_spec(shape):
    nd = len(shape)
    return pl.BlockSpec(shape, lambda b, _nd=nd: (0,) * _nd, pipeline_mode=pl.Buffered(1))


def _mod_row_spec(first_row, blocks_per_row):
    if blocks_per_row is None:
        return pl.BlockSpec((1, 1, 6 * D_MODEL), lambda b, *_: (first_row, 0, 0))
    return pl.BlockSpec((1, 1, 6 * D_MODEL), lambda b, *_: (first_row + b // blocks_per_row, 0, 0))


def _mixer(x, mod_rows, mod_first_row, mod_per_batch, state_in, emit_state, use_rope, g1, w_in_bf,
           w_fmix):
    nb, n, _ = x.shape
    assert MIXER_ROWS % n == 0 and (nb * n) % MIXER_ROWS == 0
    n_seq = MIXER_ROWS // n
    has_state_in = state_in is not None
    cw, cn, sn = _dft_consts(n)
    dmat, qdec, kdec, sdec = _retention_consts()
    consts = [cw, cn, sn, dmat, qdec, kdec, sdec]
    if use_rope:
        assert n_seq == 1
        consts += list(_rope_consts(n))
    weights = [g1.reshape(1, D_MODEL), w_in_bf, w_fmix]

    if mod_per_batch:
        assert n % MIXER_ROWS == 0
    state_spec = pl.BlockSpec((n_seq, 2, N_RET_HEADS, HEAD_DIM, HEAD_DIM), lambda b: (b, 0, 0, 0, 0))
    row_spec = pl.BlockSpec((MIXER_ROWS, D_MODEL), lambda b: (b, 0))
    in_specs = [row_spec, _mod_row_spec(mod_first_row, n // MIXER_ROWS if mod_per_batch else None)]
    in_specs += [_const_spec(a.shape) for a in weights + consts]
    args = [x.reshape(nb * n, D_MODEL), mod_rows] + weights + consts
    if has_state_in:
        in_specs.append(state_spec)
        args.append(state_in)

    out_shape = [jax.ShapeDtypeStruct((nb * n, D_MODEL), BF16)]
    out_specs = [row_spec]
    if emit_state:
        out_shape.append(jax.ShapeDtypeStruct((nb, 2, N_RET_HEADS, HEAD_DIM, HEAD_DIM), F32))
        out_specs.append(state_spec)

    return pl.pallas_call(
        functools.partial(_mixer_kernel, n=n, use_rope=use_rope, has_state_in=has_state_in,
                          emit_state=emit_state),
        out_shape=out_shape,
        grid=(nb * n // MIXER_ROWS,),
        in_specs=in_specs,
        out_specs=out_specs,
        scratch_shapes=[pltpu.VMEM((MIXER_ROWS, D_IN_PROJ), F32),
                        pltpu.VMEM((MIXER_ROWS, HEAD_DIM), F32), pltpu.VMEM((MIXER_ROWS, HEAD_DIM), F32)],
        compiler_params=pltpu.CompilerParams(dimension_semantics=("arbitrary",),
                                             vmem_limit_bytes=VMEM_LIMIT),
        name="mixer_rope" if use_rope else "mixer",
    )(*args)


def _post(x, mix, mod_rows, mod_first_row, tokens_per_mod_row, g2, w_out_bf, w_router):
    t = x.shape[0]
    assert tokens_per_mod_row is None or tokens_per_mod_row % POST_ROWS == 0
    blocks_per_row = None if tokens_per_mod_row is None else tokens_per_mod_row // POST_ROWS
    row_spec = pl.BlockSpec((POST_ROWS, D_MODEL), lambda b: (b, 0))
    return pl.pallas_call(
        _post_kernel,
        out_shape=[jax.ShapeDtypeStruct((t, D_MODEL), F32),
                   jax.ShapeDtypeStruct((t, D_MODEL), BF16),
                   jax.ShapeDtypeStruct((N_EXPERTS, t), F32)],
        grid=(t // POST_ROWS,),
        in_specs=[row_spec, row_spec,
                  _mod_row_spec(mod_first_row, blocks_per_row),
                  _const_spec((1, D_MODEL)), _const_spec((D_MODEL, D_MODEL)),
                  _const_spec((D_MODEL, N_EXPERTS))],
        out_specs=[row_spec, row_spec, pl.BlockSpec((N_EXPERTS, POST_ROWS), lambda b: (0, b))],
        compiler_params=pltpu.CompilerParams(dimension_semantics=("arbitrary",),
                                             vmem_limit_bytes=VMEM_LIMIT),
        name="post",
    )(x, mix, mod_rows, g2.reshape(1, D_MODEL), w_out_bf, w_router)


def _route_kernel(aff_ref, u_ref, slot_ref, slot_te_ref, gate_ref, starts_ref, *, t, cap):
    aff = aff_ref[...]

    def count(mask):
        return jnp.sum(mask.astype(jnp.int32), axis=1, keepdims=True)

    def as_float(word):
        return lax.bitcast_convert_type(word, F32)

    def value_step(i, cur):
        cand = cur | jnp.left_shift(jnp.int32(1), 30 - i)
        return jnp.where(count(aff >= as_float(cand)) >= cap, cand, cur)

    thr = lax.fori_loop(0, 31, value_step, jnp.zeros((N_EXPERTS, 1), jnp.int32))
    gt = aff >= as_float(thr + 1)
    eq = (aff >= as_float(thr)) & jnp.logical_not(gt)
    need = cap - count(gt)
    tok = lax.broadcasted_iota(jnp.int32, (N_EXPERTS, t), 1)
    nbits = t.bit_length() - 1

    def index_step(i, cur):
        cand = cur | jnp.left_shift(jnp.int32(1), nbits - 1 - i)
        return jnp.where(count(eq & (tok < cand)) < need, cand, cur)

    last = lax.fori_loop(0, nbits, index_step, jnp.zeros((N_EXPERTS, 1), jnp.int32))
    self = jnp.where(gt | (eq & (tok <= last)), 1.0, 0.0).astype(F32)

    carry = jnp.zeros((N_EXPERTS, 1), F32)
    starts_ref[...] = jnp.zeros(starts_ref.shape, jnp.int32)
    for b in range(t // TOKEN_BLOCK):
        sl = slice(b * TOKEN_BLOCK, (b + 1) * TOKEN_BLOCK)
        sbf = self[:, sl]
        pre = _dot(sbf.astype(BF16), u_ref[...]) + carry
        slots = jnp.where(sbf > 0.5, pre.astype(jnp.int32), -1)
        slot_ref[:, b, :] = slots
        rows = jnp.concatenate([slots, jnp.zeros((128 - N_EXPERTS, TOKEN_BLOCK), jnp.int32)], axis=0)
        slot_te_ref[sl, :] = rows.T[:, 0:N_EXPERTS]
        gate_ref[:, b, :] = aff[:, sl]
        starts_ref[:, b:b + 1] = carry.astype(jnp.int32)
        carry = carry + jnp.sum(sbf, axis=1, keepdims=True)
    nblk = t // TOKEN_BLOCK
    starts_ref[:, nblk:nblk + 1] = carry.astype(jnp.int32)


def _route(aff_et):
    t = aff_et.shape[1]
    cap = EC_CAPACITY_FACTOR * t // N_EXPERTS
    nblk = t // TOKEN_BLOCK
    upper = jnp.asarray(np.triu(np.ones((TOKEN_BLOCK, TOKEN_BLOCK)), 1), BF16)
    assert nblk + 1 <= 128
    blocked = (N_EXPERTS, nblk, TOKEN_BLOCK)
    return pl.pallas_call(
        functools.partial(_route_kernel, t=t, cap=cap),
        out_shape=[jax.ShapeDtypeStruct(blocked, jnp.int32),
                   jax.ShapeDtypeStruct((t, N_EXPERTS), jnp.int32),
                   jax.ShapeDtypeStruct(blocked, F32),
                   jax.ShapeDtypeStruct((N_EXPERTS, 128), jnp.int32)],
        grid=(1,),
        in_specs=[pl.BlockSpec((N_EXPERTS, t), lambda i: (0, 0)),
                  pl.BlockSpec((TOKEN_BLOCK, TOKEN_BLOCK), lambda i: (0, 0))],
        out_specs=[pl.BlockSpec(blocked, lambda i: (0, 0, 0)),
                   pl.BlockSpec((t, N_EXPERTS), lambda i: (0, 0)),
                   pl.BlockSpec(blocked, lambda i: (0, 0, 0)),
                   pl.BlockSpec((N_EXPERTS, 128), lambda i: (0, 0))],
        compiler_params=pltpu.CompilerParams(dimension_semantics=("arbitrary",)),
        name="route",
    )(aff_et, upper)


def _pack_windows(starts_ref, b, experts, cap):
    first = [jnp.minimum((starts_ref[e, b] // 16) * 16, cap - SLOT_WINDOW) for e in experts]
    rows = [jnp.where(starts_ref[e, b + 1] > starts_ref[e, b], starts_ref[e, b + 1] - w, 0)
            for e, w in zip(experts, first)]
    return first, pl.cdiv(functools.reduce(jnp.maximum, rows), SLOT_WINDOW)


def _block_copy(hbm_ref, vmem_ref, sem_ref, b):
    rows = pl.ds(pl.multiple_of(b * TOKEN_BLOCK, TOKEN_BLOCK), TOKEN_BLOCK)
    return pltpu.make_async_copy(hbm_ref.at[rows], vmem_ref.at[rows], sem_ref.at[b])


def _gather_group(g, starts_ref, slot_ref, gate_ref, h2_ref, xs_ref, gs_ref, row0, t, cap, arrive):
    sub = lax.broadcasted_iota(jnp.int32, (SLOT_WINDOW, TOKEN_BLOCK), 0)
    experts = [g * PACK + j for j in range(PACK)]
    assert (t // TOKEN_BLOCK) % GATHER_UNROLL == 0

    def window(b, first, i):
        hb = pl.ds(pl.multiple_of(b * TOKEN_BLOCK, TOKEN_BLOCK), TOKEN_BLOCK)
        hits, dst = [], []
        for j in range(PACK):
            lo = first[j] + i * SLOT_WINDOW
            w = jnp.minimum(lo, cap - SLOT_WINDOW)
            srow = slot_ref[j, pl.ds(b, 1), :]
            hits.append((srow == w + sub) & (srow >= lo))
            dst.append(pl.ds(pl.multiple_of(row0 + w, 16), SLOT_WINDOW))
        onehot = jnp.concatenate([jnp.where(h, 1.0, 0.0) for h in hits], axis=0).astype(BF16)
        got = _dot(onehot, h2_ref[hb, :])
        for j in range(PACK):
            piece = got[j * SLOT_WINDOW:(j + 1) * SLOT_WINDOW].astype(BF16)
            xs_ref[j, dst[j], :] = xs_ref[j, dst[j], :] + piece
            grow = gate_ref[j, pl.ds(b, 1), :]
            gs_ref[j, dst[j], :] += jnp.sum(jnp.where(hits[j], grow, 0.0), axis=1, keepdims=True)

    def blocks(q, carry):
        pending = []
        for u in range(GATHER_UNROLL):
            arrive(q * GATHER_UNROLL + u)
        for u in range(GATHER_UNROLL):
            b = q * GATHER_UNROLL + u
            first, n_windows = _pack_windows(starts_ref, b, experts, cap)
            window(b, first, 0)
            pending.append((b, first, n_windows))
        for b, first, n_windows in pending:
            def more(i, carry, b=b, first=first):
                window(b, first, i)
                return carry

            lax.fori_loop(1, n_windows, more, 0)
        return carry

    lax.fori_loop(0, t // TOKEN_BLOCK // GATHER_UNROLL, blocks, 0)


def _experts_kernel(sp_ref, ss_ref, h2p_hbm, h2s_hbm, slotp_ref, slots_ref, gatep_ref, gates_ref,
                    wg_ref, wu_ref, wd_ref, yp_ref, ys_ref, xs_ref, gs_ref, acc_ref,
                    h2p_ref, h2s_ref, semp_ref, sems_ref, *, tp, ts, capp, caps):
    g = pl.program_id(0)
    step = pl.program_id(1)
    j = step // N_FF_TILES
    f = step % N_FF_TILES
    loading = [(h2p_hbm, h2p_ref, semp_ref, tp // TOKEN_BLOCK),
               (h2s_hbm, h2s_ref, sems_ref, ts // TOKEN_BLOCK)]

    @pl.when((step == 0) & (g == 0))
    def _():
        for hbm_ref, vmem_ref, sem_ref, n_blocks in loading:
            for b in range(n_blocks):
                _block_copy(hbm_ref, vmem_ref, sem_ref, b).start()

    def arrive(hbm_ref, vmem_ref, sem_ref, _):
        def wait(b):
            @pl.when(g == 0)
            def _():
                _block_copy(hbm_ref, vmem_ref, sem_ref, b).wait()

        return wait

    @pl.when(step == 0)
    def _():
        xs_ref[...] = jnp.zeros(xs_ref.shape, BF16)
        gs_ref[...] = jnp.zeros(gs_ref.shape, F32)
        _gather_group(g, sp_ref, slotp_ref, gatep_ref, h2p_ref, xs_ref, gs_ref, 0, tp, capp,
                      arrive(*loading[0]))
        _gather_group(g, ss_ref, slots_ref, gates_ref, h2s_ref, xs_ref, gs_ref, capp, ts, caps,
                      arrive(*loading[1]))

    @pl.when(f == 0)
    def _():
        acc_ref[...] = jnp.zeros(acc_ref.shape, F32)

    x = xs_ref[j]
    a = _dot(x, wg_ref[0].astype(BF16))
    u = _dot(x, wu_ref[0].astype(BF16))
    acc_ref[...] += _dot((_silu(a) * u).astype(BF16), wd_ref[0].astype(BF16))

    @pl.when(f == N_FF_TILES - 1)
    def _():
        yp_ref[0] = (acc_ref[0:capp, :] * gs_ref[j, 0:capp, :]).astype(BF16)
        ys_ref[0] = (acc_ref[capp:capp + caps, :] * gs_ref[j, capp:capp + caps, :]).astype(BF16)


def _experts(starts_p, starts_s, h2p, h2s, slot_p, slot_s, gate_p, gate_s, w_gate, w_up, w_down):
    tp, ts = h2p.shape[0], h2s.shape[0]
    capp = EC_CAPACITY_FACTOR * tp // N_EXPERTS
    caps = EC_CAPACITY_FACTOR * ts // N_EXPERTS
    rows = capp + caps
    nbp, nbs = tp // TOKEN_BLOCK, ts // TOKEN_BLOCK
    expert = lambda g, s: g * PACK + s // N_FF_TILES
    grid_spec = pltpu.PrefetchScalarGridSpec(
        num_scalar_prefetch=2,
        grid=(N_EXPERTS // PACK, PACK * N_FF_TILES),
        in_specs=[
            pl.BlockSpec(memory_space=pl.ANY),
            pl.BlockSpec(memory_space=pl.ANY),
            pl.BlockSpec((PACK, nbp, TOKEN_BLOCK), lambda g, s, *_: (g, 0, 0)),
            pl.BlockSpec((PACK, nbs, TOKEN_BLOCK), lambda g, s, *_: (g, 0, 0)),
            pl.BlockSpec((PACK, nbp, TOKEN_BLOCK), lambda g, s, *_: (g, 0, 0)),
            pl.BlockSpec((PACK, nbs, TOKEN_BLOCK), lambda g, s, *_: (g, 0, 0)),
            pl.BlockSpec((1, D_MODEL, FF_TILE), lambda g, s, *_: (expert(g, s), 0, s % N_FF_TILES)),
            pl.BlockSpec((1, D_MODEL, FF_TILE), lambda g, s, *_: (expert(g, s), 0, s % N_FF_TILES)),
            pl.BlockSpec((1, FF_TILE, D_MODEL), lambda g, s, *_: (expert(g, s), s % N_FF_TILES, 0)),
        ],
        out_specs=[
            pl.BlockSpec((1, capp, D_MODEL), lambda g, s, *_: (expert(g, s), 0, 0)),
            pl.BlockSpec((1, caps, D_MODEL), lambda g, s, *_: (expert(g, s), 0, 0)),
        ],
        scratch_shapes=[pltpu.VMEM((PACK, rows, D_MODEL), BF16), pltpu.VMEM((PACK, rows, 1), F32),
                        pltpu.VMEM((rows, D_MODEL), F32),
                        pltpu.VMEM((tp, D_MODEL), BF16), pltpu.VMEM((ts, D_MODEL), BF16),
                        pltpu.SemaphoreType.DMA((nbp,)), pltpu.SemaphoreType.DMA((nbs,))],
    )
    return pl.pallas_call(
        functools.partial(_experts_kernel, tp=tp, ts=ts, capp=capp, caps=caps),
        out_shape=[jax.ShapeDtypeStruct((N_EXPERTS, capp, D_MODEL), BF16),
                   jax.ShapeDtypeStruct((N_EXPERTS, caps, D_MODEL), BF16)],
        grid_spec=grid_spec,
        compiler_params=pltpu.CompilerParams(dimension_semantics=("arbitrary", "arbitrary"),
                                             vmem_limit_bytes=VMEM_LIMIT),
        name="experts",
    )(starts_p, starts_s, h2p, h2s, slot_p, slot_s, gate_p, gate_s, w_gate, w_up, w_down)


def _combine_kernel(st_ref, x1_ref, slot_ref, mod_ref, fg_ref, y_hbm, out_ref, acc_ref, y_ref,
                    sem_ref, arrived_ref, *, cap):
    step = pl.program_id(0)
    n_chunks = Y_CHUNKS
    chunk_rows = cap // Y_CHUNKS
    n_parts = COMBINE_ROWS // TOKEN_BLOCK
    last_block = step * n_parts + n_parts - 1

    def chunk_copy(c):
        rows = pl.ds(c * chunk_rows, chunk_rows)
        return pltpu.make_async_copy(y_hbm.at[:, rows, :], y_ref.at[:, rows, :], sem_ref.at[c])

    @pl.when(step == 0)
    def _():
        arrived_ref[0] = 0
        for c in range(n_chunks):
            chunk_copy(c).start()

    top = functools.reduce(jnp.maximum, [st_ref[e, last_block + 1] for e in range(N_EXPERTS)])
    want = pl.cdiv(jnp.minimum(top + SLOT_WINDOW, cap), chunk_rows)
    have = arrived_ref[0]
    for c in range(n_chunks):
        @pl.when((c >= have) & (c < want))
        def _(c=c):
            chunk_copy(c).wait()
    arrived_ref[0] = jnp.maximum(have, want)

    lane = lax.broadcasted_iota(jnp.int32, (TOKEN_BLOCK, PACK * SLOT_WINDOW), 1)
    gate2 = mod_ref[0][:, 5 * D_MODEL:6 * D_MODEL]

    def scatter(slot, experts, first, i):
        target = None
        windows = []
        for j, e in enumerate(experts):
            lo = first[j] + i * SLOT_WINDOW
            w = jnp.minimum(lo, cap - SLOT_WINDOW)
            sc = slot[:, e:e + 1]
            col = jnp.where(sc >= lo, sc - w + j * SLOT_WINDOW, -1)
            target = col if target is None else jnp.where(lane < j * SLOT_WINDOW, target, col)
            windows.append(y_ref[e, pl.ds(pl.multiple_of(w, 16), SLOT_WINDOW), :])
        onehot = jnp.where(target == lane, 1.0, 0.0).astype(BF16)
        return _dot(onehot, jnp.concatenate(windows, axis=0))

    def finish(rows):
        x2 = x1_ref[rows, :] + gate2 * acc_ref[rows, :]
        out_ref[rows, :] = _rms(x2) * fg_ref[...]

    parts = []
    for p in range(n_parts):
        rows = slice(p * TOKEN_BLOCK, (p + 1) * TOKEN_BLOCK)
        slot = slot_ref[rows, :]
        groups = []
        total = None
        for g in range(N_EXPERTS // PACK):
            experts = list(range(g * PACK, (g + 1) * PACK))
            first, n_windows = _pack_windows(st_ref, step * n_parts + p, experts, cap)
            groups.append((experts, first, n_windows))
            part = scatter(slot, experts, first, 0)
            total = part if total is None else total + part
        acc_ref[rows, :] = total
        parts.append((rows, slot, groups))
    for rows, _, _ in parts:
        finish(rows)

    for rows, slot, groups in parts:
        most = functools.reduce(jnp.maximum, [n_windows for _, _, n_windows in groups])

        @pl.when(most > 1)
        def _(rows=rows, slot=slot, groups=groups):
            for experts, first, n_windows in groups:
                def more(i, carry, experts=experts, first=first):
                    acc_ref[rows, :] += scatter(slot, experts, first, i)
                    return carry

                lax.fori_loop(1, n_windows, more, 0)
            finish(rows)


def _combine(starts, x1, slot_te, mod_rows, mod_first_row, tokens_per_mod_row, final_g, y):
    t = x1.shape[0]
    cap = y.shape[1]
    assert tokens_per_mod_row is None or tokens_per_mod_row % COMBINE_ROWS == 0
    steps_per_row = None if tokens_per_mod_row is None else tokens_per_mod_row // COMBINE_ROWS
    grid_spec = pltpu.PrefetchScalarGridSpec(
        num_scalar_prefetch=1,
        grid=(t // COMBINE_ROWS,),
        in_specs=[
            pl.BlockSpec((COMBINE_ROWS, D_MODEL), lambda b, *_: (b, 0)),
            pl.BlockSpec((COMBINE_ROWS, N_EXPERTS), lambda b, *_: (b, 0)),
            _mod_row_spec(mod_first_row, steps_per_row),
            pl.BlockSpec((1, D_MODEL), lambda b, *_: (0, 0)),
            pl.BlockSpec(memory_space=pl.ANY),
        ],
        out_specs=pl.BlockSpec((COMBINE_ROWS, D_MODEL), lambda b, *_: (b, 0)),
        scratch_shapes=[pltpu.VMEM((COMBINE_ROWS, D_MODEL), F32), pltpu.VMEM(y.shape, BF16),
                        pltpu.SemaphoreType.DMA((Y_CHUNKS,)), pltpu.SMEM((1,), jnp.int32)],
    )
    return pl.pallas_call(
        functools.partial(_combine_kernel, cap=cap),
        out_shape=jax.ShapeDtypeStruct((t, D_MODEL), F32),
        grid_spec=grid_spec,
        compiler_params=pltpu.CompilerParams(dimension_semantics=("arbitrary",),
                                             vmem_limit_bytes=VMEM_LIMIT),
        name="combine",
    )(starts, x1, slot_te, mod_rows, final_g.reshape(1, D_MODEL), y)


def kernel(x_prompt, x_sample, state_ret, c, c_ctx, norm1_g, norm2_g, final_g, w_mod, b_mod, w_in,
           w_fmix, w_out, w_router, w_gate, w_up, w_down):
    bp, seq, _ = x_prompt.shape
    bs, dec_seq, _ = x_sample.shape
    assert w_mod.shape[0] == 1, "single-layer trunk"
    tp, ts = bp * seq, bs * dec_seq

    cond = jnp.concatenate([c_ctx[None, :], c], axis=0)
    mod = _modulation(cond, w_mod[0], b_mod[0])
    ctx_row, lat_row = 0, 1

    w_in_bf = w_in[0]
    w_out_bf = w_out[0].astype(BF16)
    mix_p, states = _mixer(x_prompt, mod, ctx_row, False, None, True, False, norm1_g[0], w_in_bf,
                           w_fmix[0])
    (mix_s,) = _mixer(x_sample, mod, lat_row, True, state_ret[:, 0], False, True, norm1_g[0],
                      w_in_bf, w_fmix[0])
    x1p, h2p, affp = _post(x_prompt.reshape(tp, D_MODEL), mix_p, mod, ctx_row, None,
                           norm2_g[0], w_out_bf, w_router[0])
    x1s, h2s, affs = _post(x_sample.reshape(ts, D_MODEL), mix_s, mod, lat_row, dec_seq,
                           norm2_g[0], w_out_bf, w_router[0])

    slot_p, slot_te_p, gate_p, starts_p = _route(affp)
    slot_s, slot_te_s, gate_s, starts_s = _route(affs)

    yp, ys = _experts(starts_p, starts_s, h2p, h2s, slot_p, slot_s, gate_p, gate_s,
                      w_gate[0], w_up[0], w_down[0])

    out_p = _combine(starts_p, x1p, slot_te_p, mod, ctx_row, None, final_g, yp)
    out_s = _combine(starts_s, x1s, slot_te_s, mod, lat_row, dec_seq, final_g, ys)

    y_prompt = out_p.reshape(bp, seq, D_MODEL)
    y_sample = out_s.reshape(bs, dec_seq, D_MODEL)
    state_new = states.reshape(bp, 1, 2, N_RET_HEADS, HEAD_DIM, HEAD_DIM).astype(x_prompt.dtype)
    return (y_prompt, y_sample, state_new)
```

```python
import functools
import math

import jax
import jax.numpy as jnp
import numpy as np
from jax import lax
from jax.experimental import pallas as pl
from jax.experimental.pallas import tpu as pltpu

D_MODEL = 1024
D_FOURIER = 512
N_FOURIER_GROUPS = 4
FOURIER_GROUP_W = 128
D_RET = 512
N_RET_HEADS = 4
HEAD_DIM = 128
CHUNK = 256
GRID_W = 64
N_EXPERTS = 16
EC_CAPACITY_FACTOR = 2
D_EXPERT_FF = 2816
ROPE_BASE = 10000.0
EPS = 1e-6
D_IN_PROJ = D_FOURIER + 5 * D_RET
LOG_GAMMA_FWD = np.log(1.0 - 2.0 ** (-5.0 - np.arange(N_RET_HEADS))).astype(np.float32)
LOG_GAMMA_BWD = np.log(1.0 - 2.0 ** (-5.5 - np.arange(N_RET_HEADS))).astype(np.float32)

LANES = 128
BF16_ROWS = 16
TOKEN_BLOCK = 256
SLOT_WINDOW = 64
PACK = TOKEN_BLOCK // SLOT_WINDOW
GATHER_UNROLL = 4
Y_CHUNKS = 4
FF_TILE = 256
N_FF_TILES = D_EXPERT_FF // FF_TILE
MOD_TILE = 1024
MIXER_ROWS = 1024
POST_ROWS = 512
POST_PART = 256
COMBINE_ROWS = 512
VMEM_LIMIT = 56 * 1024 * 1024

F32 = jnp.float32
BF16 = jnp.bfloat16


def _dot(a, b):
    return jnp.dot(a, b, preferred_element_type=F32)


def _dot_nt(a, b):
    return lax.dot_general(a, b, (((1,), (1,)), ((), ())), preferred_element_type=F32)


def _silu(x):
    return x * jax.nn.sigmoid(x)


def _mod_kernel(condt_ref, w_ref, b_ref, out_ref, *, n_cond):
    s = _silu(condt_ref[...])
    w = w_ref[...]
    out_ref[...] = jnp.zeros(out_ref.shape, F32)
    for r in range(n_cond):
        out_ref[r] = jnp.sum(w * s[:, r:r + 1], axis=0, keepdims=True) + b_ref[...]


def _modulation(cond_rows, w_mod, b_mod):
    n_cond = cond_rows.shape[0]
    condt = jnp.zeros((D_MODEL, 8), F32).at[:, :n_cond].set(cond_rows.T)
    n_out = w_mod.shape[1]
    return pl.pallas_call(
        functools.partial(_mod_kernel, n_cond=n_cond),
        out_shape=jax.ShapeDtypeStruct((8, 1, n_out), F32),
        grid=(n_out // MOD_TILE,),
        in_specs=[
            pl.BlockSpec((D_MODEL, 8), lambda j: (0, 0)),
            pl.BlockSpec((D_MODEL, MOD_TILE), lambda j: (0, j)),
            pl.BlockSpec((1, MOD_TILE), lambda j: (0, j)),
        ],
        out_specs=pl.BlockSpec((8, 1, MOD_TILE), lambda j: (0, 0, j)),
        compiler_params=pltpu.CompilerParams(dimension_semantics=("arbitrary",)),
        name="mod",
    )(condt, w_mod, b_mod.reshape(1, n_out))


def _rms(x):
    return x * lax.rsqrt(jnp.mean(x * x, axis=-1, keepdims=True) + EPS)


def _groupnorm(o):
    mu = jnp.mean(o, axis=-1, keepdims=True)
    c = o - mu
    return c * lax.rsqrt(jnp.mean(c * c, axis=-1, keepdims=True) + EPS)


def _split_hi_lo(x):
    hi = x.astype(BF16)
    lo = (x - hi.astype(F32)).astype(BF16)
    return hi, lo


def _mixer_kernel(*refs, n, use_rope, has_state_in, emit_state):
    it = iter(refs)
    x_ref, mod_ref, g1_ref, win_ref, wfmix_ref = (next(it) for _ in range(5))
    cw_ref, cn_ref, sn_ref, dmat_ref, qdec_ref, kdec_ref, sdec_ref = (next(it) for _ in range(7))
    cos_ref = sin_ref = s0_ref = st_ref = None
    if use_rope:
        cos_ref, sin_ref = next(it), next(it)
    if has_state_in:
        s0_ref = next(it)
    mix_ref = next(it)
    if emit_state:
        st_ref = next(it)
    p_ref, of_ref, ob_ref = next(it), next(it), next(it)

    n_seq = MIXER_ROWS // n
    chunks_per_seq = n // CHUNK
    mod = mod_ref[0]
    shift1 = mod[:, 0:D_MODEL]
    scale1 = mod[:, D_MODEL:2 * D_MODEL]

    h = (_rms(x_ref[...]) * g1_ref[...] * (1.0 + scale1) + shift1).astype(BF16)
    for j in range(D_IN_PROJ // 512):
        p_ref[:, j * 512:(j + 1) * 512] = _dot(h, win_ref[:, j * 512:(j + 1) * 512].astype(BF16))

    xf = p_ref[:, 0:D_FOURIER].astype(BF16)
    xc, xs = [], []
    cw = cw_ref[...].astype(BF16)
    for g in range(N_FOURIER_GROUPS):
        t = _dot(xf[:, g * FOURIER_GROUP_W:(g + 1) * FOURIER_GROUP_W], cw)
        xc.append(t[:, :FOURIER_GROUP_W].astype(BF16))
        xs.append(t[:, FOURIER_GROUP_W:].astype(BF16))
    xc = jnp.concatenate(xc, axis=1)
    xs = jnp.concatenate(xs, axis=1)
    cn = cn_ref[...].astype(BF16)
    sn = sn_ref[...].astype(BF16)
    for s in range(n_seq):
        rs = slice(s * n, (s + 1) * n)
        fre = (_dot(cn, xc[rs]) - _dot(sn, xs[rs])) * (1.0 / math.sqrt(n * FOURIER_GROUP_W))
        fre = fre.astype(BF16)
        for g in range(N_FOURIER_GROUPS):
            sl = slice(g * FOURIER_GROUP_W, (g + 1) * FOURIER_GROUP_W)
            mix_ref[rs, sl] = _dot(fre[:, sl], wfmix_ref[g].astype(BF16)).astype(BF16)

    for hh in range(N_RET_HEADS):
        base = D_FOURIER + hh * HEAD_DIM
        q = p_ref[:, base:base + HEAD_DIM]
        k = p_ref[:, base + D_RET:base + D_RET + HEAD_DIM]
        v = p_ref[:, base + 2 * D_RET:base + 2 * D_RET + HEAD_DIM]
        if use_rope:
            lane = lax.broadcasted_iota(jnp.int32, (MIXER_ROWS, HEAD_DIM), 1)
            first = (lane % 64) < 32

            def rope(t):
                swapped = jnp.where(first, pltpu.roll(t, HEAD_DIM - 32, 1), pltpu.roll(t, 32, 1))
                return t * cos_ref[...] + swapped * sin_ref[...]

            q, k = rope(q), rope(k)
        k = k * (HEAD_DIM ** -0.5)
        qb, vb = q.astype(BF16), v.astype(BF16)
        kb = k.astype(BF16)

        def initial(s, direction):
            if has_state_in:
                return s0_ref[s, direction, hh]
            return jnp.zeros((HEAD_DIM, HEAD_DIM), F32)

        for s in range(n_seq):
            parts = []
            for ci in range(chunks_per_seq):
                c = s * chunks_per_seq + ci
                rs = slice(c * CHUNK, (c + 1) * CHUNK)
                qc, kc, vc = qb[rs], kb[rs], vb[rs]
                qk = _dot_nt(qc, kc)
                lhs = jnp.concatenate([(qk * dmat_ref[0, hh]).astype(BF16),
                                       (qk * dmat_ref[1, hh]).astype(BF16),
                                       (k[rs] * kdec_ref[0, hh]).T.astype(BF16),
                                       (k[rs] * kdec_ref[1, hh]).T.astype(BF16)], axis=0)
                parts.append((rs, qc, _dot(lhs, vc)))
            sf = initial(s, 0)
            for ci in range(chunks_per_seq):
                rs, qc, r = parts[ci]
                o = r[0:CHUNK]
                if has_state_in or ci > 0:
                    o = o + qdec_ref[0, hh] * _dot(qc, sf.astype(BF16))
                of_ref[rs, :] = o
                sf = sf * sdec_ref[0, hh] + r[2 * CHUNK:2 * CHUNK + HEAD_DIM]
            sb = initial(s, 1)
            for ci in reversed(range(chunks_per_seq)):
                rs, qc, r = parts[ci]
                o = r[CHUNK:2 * CHUNK]
                if has_state_in or ci < chunks_per_seq - 1:
                    o = o + qdec_ref[1, hh] * _dot(qc, sb.astype(BF16))
                ob_ref[rs, :] = o
                sb = sb * sdec_ref[1, hh] + r[2 * CHUNK + HEAD_DIM:]
            if emit_state:
                st_ref[s, 0, hh] = sf
                st_ref[s, 1, hh] = sb

        gf = p_ref[:, base + 3 * D_RET:base + 3 * D_RET + HEAD_DIM]
        gb = p_ref[:, base + 4 * D_RET:base + 4 * D_RET + HEAD_DIM]
        y = _silu(gf) * _groupnorm(of_ref[...]) + _silu(gb) * _groupnorm(ob_ref[...])
        mix_ref[:, base:base + HEAD_DIM] = y.astype(BF16)


def _post_kernel(x_ref, mix_ref, mod_ref, g2_ref, wout_ref, wr_ref, x1_ref, h2_ref, aff_ref):
    mod = mod_ref[0]
    gate1 = mod[:, 2 * D_MODEL:3 * D_MODEL]
    shift2 = mod[:, 3 * D_MODEL:4 * D_MODEL]
    scale2 = mod[:, 4 * D_MODEL:5 * D_MODEL]
    wr_hi, wr_lo = _split_hi_lo(wr_ref[...])
    wr_both = jnp.concatenate([wr_hi, wr_lo], axis=1)
    parts = [slice(p * POST_PART, (p + 1) * POST_PART) for p in range(POST_ROWS // POST_PART)]
    x1 = []
    for rows in parts:
        x1.append(x_ref[rows, :] + gate1 * _dot(mix_ref[rows, :], wout_ref[...]))
        x1_ref[rows, :] = x1[-1]
    for rows, x1_part in zip(parts, x1):
        h2 = _rms(x1_part) * g2_ref[...] * (1.0 + scale2) + shift2
        h2_hi, h2_lo = _split_hi_lo(h2)
        h2_ref[rows, :] = h2_hi
        by_hi = _dot(h2_hi, wr_both)
        logits = by_hi[:, :N_EXPERTS] + (_dot(h2_lo, wr_hi) + by_hi[:, N_EXPERTS:])
        z = jnp.exp(logits - jnp.max(logits, axis=-1, keepdims=True))
        aff = z / jnp.sum(z, axis=-1, keepdims=True)
        lanes = jnp.concatenate([aff, jnp.zeros((POST_PART, LANES - N_EXPERTS), F32)], axis=1)
        aff_ref[:, rows] = lanes.T[0:N_EXPERTS, :]


def _dft_consts(n):
    w = FOURIER_GROUP_W
    jw = np.arange(w)
    angw = 2.0 * np.pi * np.outer(jw, jw) / w
    cw = np.concatenate([np.cos(angw), np.sin(angw)], axis=1)
    jn = np.arange(n)
    angn = 2.0 * np.pi * (np.outer(jn, jn) % n) / n
    return (jnp.asarray(cw, F32), jnp.asarray(np.cos(angn), F32), jnp.asarray(np.sin(angn), F32))


def _retention_consts():
    i = np.arange(CHUNK, dtype=np.float64)
    diff = i[:, None] - i[None, :]
    dmat = np.zeros((2, N_RET_HEADS, CHUNK, CHUNK))
    qdec = np.zeros((2, N_RET_HEADS, CHUNK, HEAD_DIM))
    kdec = np.zeros((2, N_RET_HEADS, CHUNK, HEAD_DIM))
    sdec = np.zeros((2, N_RET_HEADS, HEAD_DIM, HEAD_DIM))
    for hh in range(N_RET_HEADS):
        lf = float(LOG_GAMMA_FWD[hh])
        lb = float(LOG_GAMMA_BWD[hh])
        dmat[0, hh] = np.where(diff >= 0, np.exp(lf * np.maximum(diff, 0.0)), 0.0)
        dmat[1, hh] = np.where(diff <= 0, np.exp(lb * np.maximum(-diff, 0.0)), 0.0)
        qdec[0, hh] = np.exp(lf * (i + 1.0))[:, None]
        qdec[1, hh] = np.exp(lb * (CHUNK - i))[:, None]
        kdec[0, hh] = np.exp(lf * (CHUNK - 1.0 - i))[:, None]
        kdec[1, hh] = np.exp(lb * i)[:, None]
        sdec[0, hh] = math.exp(lf * CHUNK)
        sdec[1, hh] = math.exp(lb * CHUNK)
    return tuple(jnp.asarray(a, F32) for a in (dmat, qdec, kdec, sdec))


def _rope_consts(n):
    rows_n = n // GRID_W
    row = np.repeat(np.arange(rows_n, dtype=np.float64), GRID_W)
    col = np.tile(np.arange(GRID_W, dtype=np.float64), rows_n)
    n_pairs = HEAD_DIM // 4
    freqs = (np.float32(ROPE_BASE) ** (-np.arange(n_pairs, dtype=np.float32) / n_pairs)).astype(np.float64)
    ar = row[:, None] * freqs[None, :]
    ac = col[:, None] * freqs[None, :]
    cos = np.concatenate([np.cos(ar), np.cos(ar), np.cos(ac), np.cos(ac)], axis=1)
    sin = np.concatenate([-np.sin(ar), np.sin(ar), -np.sin(ac), np.sin(ac)], axis=1)
    return jnp.asarray(cos, F32), jnp.asarray(sin, F32)


def _const_spec(shape):
    nd = len(shape)
    return pl.BlockSpec(shape, lambda b, _nd=nd: (0,) * _nd, pipeline_mode=pl.Buffered(1))


def _mod_row_spec(first_row, blocks_per_row):
    if blocks_per_row is None:
        return pl.BlockSpec((1, 1, 6 * D_MODEL), lambda b, *_: (first_row, 0, 0))
    return pl.BlockSpec((1, 1, 6 * D_MODEL), lambda b, *_: (first_row + b // blocks_per_row, 0, 0))


def _mixer(x, mod_rows, mod_first_row, mod_per_batch, state_in, emit_state, use_rope, g1, w_in,
           w_fmix):
    nb, n, _ = x.shape
    assert MIXER_ROWS % n == 0 and (nb * n) % MIXER_ROWS == 0
    n_seq = MIXER_ROWS // n
    has_state_in = state_in is not None
    cw, cn, sn = _dft_consts(n)
    dmat, qdec, kdec, sdec = _retention_consts()
    consts = [cw, cn, sn, dmat, qdec, kdec, sdec]
    if use_rope:
        assert n_seq == 1
        consts += list(_rope_consts(n))
    weights = [g1.reshape(1, D_MODEL), w_in, w_fmix]

    if mod_per_batch:
        assert n % MIXER_ROWS == 0
    state_spec = pl.BlockSpec((n_seq, 2, N_RET_HEADS, HEAD_DIM, HEAD_DIM), lambda b: (b, 0, 0, 0, 0))
    row_spec = pl.BlockSpec((MIXER_ROWS, D_MODEL), lambda b: (b, 0))
    in_specs = [row_spec, _mod_row_spec(mod_first_row, n // MIXER_ROWS if mod_per_batch else None)]
    in_specs += [_const_spec(a.shape) for a in weights + consts]
    args = [x.reshape(nb * n, D_MODEL), mod_rows] + weights + consts
    if has_state_in:
        in_specs.append(state_spec)
        args.append(state_in)

    out_shape = [jax.ShapeDtypeStruct((nb * n, D_MODEL), BF16)]
    out_specs = [row_spec]
    if emit_state:
        out_shape.append(jax.ShapeDtypeStruct((nb, 2, N_RET_HEADS, HEAD_DIM, HEAD_DIM), F32))
        out_specs.append(state_spec)

    return pl.pallas_call(
        functools.partial(_mixer_kernel, n=n, use_rope=use_rope, has_state_in=has_state_in,
                          emit_state=emit_state),
        out_shape=out_shape,
        grid=(nb * n // MIXER_ROWS,),
        in_specs=in_specs,
        out_specs=out_specs,
        scratch_shapes=[pltpu.VMEM((MIXER_ROWS, D_IN_PROJ), F32),
                        pltpu.VMEM((MIXER_ROWS, HEAD_DIM), F32), pltpu.VMEM((MIXER_ROWS, HEAD_DIM), F32)],
        compiler_params=pltpu.CompilerParams(dimension_semantics=("arbitrary",),
                                             vmem_limit_bytes=VMEM_LIMIT),
        name="mixer_rope" if use_rope else "mixer",
    )(*args)


def _post(x, mix, mod_rows, mod_first_row, tokens_per_mod_row, g2, w_out_bf, w_router):
    t = x.shape[0]
    assert tokens_per_mod_row is None or tokens_per_mod_row % POST_ROWS == 0
    blocks_per_row = None if tokens_per_mod_row is None else tokens_per_mod_row // POST_ROWS
    row_spec = pl.BlockSpec((POST_ROWS, D_MODEL), lambda b: (b, 0))
    return pl.pallas_call(
        _post_kernel,
        out_shape=[jax.ShapeDtypeStruct((t, D_MODEL), F32),
                   jax.ShapeDtypeStruct((t, D_MODEL), BF16),
                   jax.ShapeDtypeStruct((N_EXPERTS, t), F32)],
        grid=(t // POST_ROWS,),
        in_specs=[row_spec, row_spec,
                  _mod_row_spec(mod_first_row, blocks_per_row),
                  _const_spec((1, D_MODEL)), _const_spec((D_MODEL, D_MODEL)),
                  _const_spec((D_MODEL, N_EXPERTS))],
        out_specs=[row_spec, row_spec, pl.BlockSpec((N_EXPERTS, POST_ROWS), lambda b: (0, b))],
        compiler_params=pltpu.CompilerParams(dimension_semantics=("arbitrary",),
                                             vmem_limit_bytes=VMEM_LIMIT),
        name="post",
    )(x, mix, mod_rows, g2.reshape(1, D_MODEL), w_out_bf, w_router)


def _route_kernel(aff_ref, u_ref, slot_ref, slot_te_ref, gate_ref, starts_ref, *, t, cap):
    aff = aff_ref[...]

    def count(mask):
        return jnp.sum(mask.astype(jnp.int32), axis=1, keepdims=True)

    def as_float(word):
        return lax.bitcast_convert_type(word, F32)

    def value_step(i, cur):
        cand = cur | jnp.left_shift(jnp.int32(1), 30 - i)
        return jnp.where(count(aff >= as_float(cand)) >= cap, cand, cur)

    thr = lax.fori_loop(0, 31, value_step, jnp.zeros((N_EXPERTS, 1), jnp.int32))
    gt = aff >= as_float(thr + 1)
    eq = (aff >= as_float(thr)) & jnp.logical_not(gt)
    need = cap - count(gt)
    tok = lax.broadcasted_iota(jnp.int32, (N_EXPERTS, t), 1)
    nbits = t.bit_length() - 1

    def index_step(i, cur):
        cand = cur | jnp.left_shift(jnp.int32(1), nbits - 1 - i)
        return jnp.where(count(eq & (tok < cand)) < need, cand, cur)

    last = lax.fori_loop(0, nbits, index_step, jnp.zeros((N_EXPERTS, 1), jnp.int32))
    self = jnp.where(gt | (eq & (tok <= last)), 1.0, 0.0).astype(F32)

    carry = jnp.zeros((N_EXPERTS, 1), F32)
    starts_ref[...] = jnp.zeros(starts_ref.shape, jnp.int32)
    for b in range(t // TOKEN_BLOCK):
        sl = slice(b * TOKEN_BLOCK, (b + 1) * TOKEN_BLOCK)
        sbf = self[:, sl]
        pre = _dot(sbf.astype(BF16), u_ref[...]) + carry
        slots = jnp.where(sbf > 0.5, pre.astype(jnp.int32), -1)
        slot_ref[:, b, :] = slots
        rows = jnp.concatenate([slots, jnp.zeros((LANES - N_EXPERTS, TOKEN_BLOCK), jnp.int32)], axis=0)
        slot_te_ref[sl, :] = rows.T[:, 0:N_EXPERTS]
        gate_ref[:, b, :] = aff[:, sl]
        starts_ref[:, b:b + 1] = carry.astype(jnp.int32)
        carry = carry + jnp.sum(sbf, axis=1, keepdims=True)
    nblk = t // TOKEN_BLOCK
    starts_ref[:, nblk:nblk + 1] = carry.astype(jnp.int32)


def _route(aff_et):
    t = aff_et.shape[1]
    cap = EC_CAPACITY_FACTOR * t // N_EXPERTS
    nblk = t // TOKEN_BLOCK
    upper = jnp.asarray(np.triu(np.ones((TOKEN_BLOCK, TOKEN_BLOCK)), 1), BF16)
    assert nblk + 1 <= LANES
    blocked = (N_EXPERTS, nblk, TOKEN_BLOCK)
    return pl.pallas_call(
        functools.partial(_route_kernel, t=t, cap=cap),
        out_shape=[jax.ShapeDtypeStruct(blocked, jnp.int32),
                   jax.ShapeDtypeStruct((t, N_EXPERTS), jnp.int32),
                   jax.ShapeDtypeStruct(blocked, F32),
                   jax.ShapeDtypeStruct((N_EXPERTS, LANES), jnp.int32)],
        grid=(1,),
        in_specs=[pl.BlockSpec((N_EXPERTS, t), lambda i: (0, 0)),
                  pl.BlockSpec((TOKEN_BLOCK, TOKEN_BLOCK), lambda i: (0, 0))],
        out_specs=[pl.BlockSpec(blocked, lambda i: (0, 0, 0)),
                   pl.BlockSpec((t, N_EXPERTS), lambda i: (0, 0)),
                   pl.BlockSpec(blocked, lambda i: (0, 0, 0)),
                   pl.BlockSpec((N_EXPERTS, LANES), lambda i: (0, 0))],
        compiler_params=pltpu.CompilerParams(dimension_semantics=("arbitrary",)),
        name="route",
    )(aff_et, upper)


def _pack_windows(starts_ref, b, experts, cap):
    first = [jnp.minimum((starts_ref[e, b] // BF16_ROWS) * BF16_ROWS, cap - SLOT_WINDOW)
             for e in experts]
    rows = [jnp.where(starts_ref[e, b + 1] > starts_ref[e, b], starts_ref[e, b + 1] - w, 0)
            for e, w in zip(experts, first)]
    return first, pl.cdiv(functools.reduce(jnp.maximum, rows), SLOT_WINDOW)


def _block_copy(hbm_ref, vmem_ref, sem_ref, b):
    rows = pl.ds(pl.multiple_of(b * TOKEN_BLOCK, TOKEN_BLOCK), TOKEN_BLOCK)
    return pltpu.make_async_copy(hbm_ref.at[rows], vmem_ref.at[rows], sem_ref.at[b])


def _gather_group(g, starts_ref, slot_ref, gate_ref, h2_ref, xs_ref, gs_ref, row0, t, cap, arrive):
    sub = lax.broadcasted_iota(jnp.int32, (SLOT_WINDOW, TOKEN_BLOCK), 0)
    experts = [g * PACK + j for j in range(PACK)]
    assert (t // TOKEN_BLOCK) % GATHER_UNROLL == 0

    def window(b, first, i):
        hb = pl.ds(pl.multiple_of(b * TOKEN_BLOCK, TOKEN_BLOCK), TOKEN_BLOCK)
        hits, dst = [], []
        for j in range(PACK):
            lo = first[j] + i * SLOT_WINDOW
            w = jnp.minimum(lo, cap - SLOT_WINDOW)
            srow = slot_ref[j, pl.ds(b, 1), :]
            hits.append((srow == w + sub) & (srow >= lo))
            dst.append(pl.ds(pl.multiple_of(row0 + w, BF16_ROWS), SLOT_WINDOW))
        onehot = jnp.concatenate([jnp.where(h, 1.0, 0.0) for h in hits], axis=0).astype(BF16)
        got = _dot(onehot, h2_ref[hb, :])
        for j in range(PACK):
            piece = got[j * SLOT_WINDOW:(j + 1) * SLOT_WINDOW].astype(BF16)
            xs_ref[j, dst[j], :] = xs_ref[j, dst[j], :] + piece
            grow = gate_ref[j, pl.ds(b, 1), :]
            gs_ref[j, dst[j], :] += jnp.sum(jnp.where(hits[j], grow, 0.0), axis=1, keepdims=True)

    def blocks(q, carry):
        pending = []
        for u in range(GATHER_UNROLL):
            arrive(q * GATHER_UNROLL + u)
        for u in range(GATHER_UNROLL):
            b = q * GATHER_UNROLL + u
            first, n_windows = _pack_windows(starts_ref, b, experts, cap)
            window(b, first, 0)
            pending.append((b, first, n_windows))
        for b, first, n_windows in pending:
            def more(i, carry, b=b, first=first):
                window(b, first, i)
                return carry

            lax.fori_loop(1, n_windows, more, 0)
        return carry

    lax.fori_loop(0, t // TOKEN_BLOCK // GATHER_UNROLL, blocks, 0)


def _experts_kernel(sp_ref, ss_ref, h2p_hbm, h2s_hbm, slotp_ref, slots_ref, gatep_ref, gates_ref,
                    wg_ref, wu_ref, wd_ref, yp_ref, ys_ref, xs_ref, gs_ref, acc_ref,
                    h2p_ref, h2s_ref, semp_ref, sems_ref, *, tp, ts, capp, caps):
    g = pl.program_id(0)
    step = pl.program_id(1)
    j = step // N_FF_TILES
    f = step % N_FF_TILES
    loading = [(h2p_hbm, h2p_ref, semp_ref, tp // TOKEN_BLOCK),
               (h2s_hbm, h2s_ref, sems_ref, ts // TOKEN_BLOCK)]

    @pl.when((step == 0) & (g == 0))
    def _():
        for hbm_ref, vmem_ref, sem_ref, n_blocks in loading:
            for b in range(n_blocks):
                _block_copy(hbm_ref, vmem_ref, sem_ref, b).start()

    def arrive(hbm_ref, vmem_ref, sem_ref, _):
        def wait(b):
            @pl.when(g == 0)
            def _():
                _block_copy(hbm_ref, vmem_ref, sem_ref, b).wait()

        return wait

    @pl.when(step == 0)
    def _():
        xs_ref[...] = jnp.zeros(xs_ref.shape, BF16)
        gs_ref[...] = jnp.zeros(gs_ref.shape, F32)
        _gather_group(g, sp_ref, slotp_ref, gatep_ref, h2p_ref, xs_ref, gs_ref, 0, tp, capp,
                      arrive(*loading[0]))
        _gather_group(g, ss_ref, slots_ref, gates_ref, h2s_ref, xs_ref, gs_ref, capp, ts, caps,
                      arrive(*loading[1]))

    @pl.when(f == 0)
    def _():
        acc_ref[...] = jnp.zeros(acc_ref.shape, F32)

    x = xs_ref[j]
    a = _dot(x, wg_ref[0].astype(BF16))
    u = _dot(x, wu_ref[0].astype(BF16))
    acc_ref[...] += _dot((_silu(a) * u).astype(BF16), wd_ref[0].astype(BF16))

    @pl.when(f == N_FF_TILES - 1)
    def _():
        yp_ref[0] = (acc_ref[0:capp, :] * gs_ref[j, 0:capp, :]).astype(BF16)
        ys_ref[0] = (acc_ref[capp:capp + caps, :] * gs_ref[j, capp:capp + caps, :]).astype(BF16)


def _experts(starts_p, starts_s, h2p, h2s, slot_p, slot_s, gate_p, gate_s, w_gate, w_up, w_down):
    tp, ts = h2p.shape[0], h2s.shape[0]
    capp = EC_CAPACITY_FACTOR * tp // N_EXPERTS
    caps = EC_CAPACITY_FACTOR * ts // N_EXPERTS
    rows = capp + caps
    nbp, nbs = tp // TOKEN_BLOCK, ts // TOKEN_BLOCK
    expert = lambda g, s: g * PACK + s // N_FF_TILES
    grid_spec = pltpu.PrefetchScalarGridSpec(
        num_scalar_prefetch=2,
        grid=(N_EXPERTS // PACK, PACK * N_FF_TILES),
        in_specs=[
            pl.BlockSpec(memory_space=pl.ANY),
            pl.BlockSpec(memory_space=pl.ANY),
            pl.BlockSpec((PACK, nbp, TOKEN_BLOCK), lambda g, s, *_: (g, 0, 0)),
            pl.BlockSpec((PACK, nbs, TOKEN_BLOCK), lambda g, s, *_: (g, 0, 0)),
            pl.BlockSpec((PACK, nbp, TOKEN_BLOCK), lambda g, s, *_: (g, 0, 0)),
            pl.BlockSpec((PACK, nbs, TOKEN_BLOCK), lambda g, s, *_: (g, 0, 0)),
            pl.BlockSpec((1, D_MODEL, FF_TILE), lambda g, s, *_: (expert(g, s), 0, s % N_FF_TILES)),
            pl.BlockSpec((1, D_MODEL, FF_TILE), lambda g, s, *_: (expert(g, s), 0, s % N_FF_TILES)),
            pl.BlockSpec((1, FF_TILE, D_MODEL), lambda g, s, *_: (expert(g, s), s % N_FF_TILES, 0)),
        ],
        out_specs=[
            pl.BlockSpec((1, capp, D_MODEL), lambda g, s, *_: (expert(g, s), 0, 0)),
            pl.BlockSpec((1, caps, D_MODEL), lambda g, s, *_: (expert(g, s), 0, 0)),
        ],
        scratch_shapes=[pltpu.VMEM((PACK, rows, D_MODEL), BF16), pltpu.VMEM((PACK, rows, 1), F32),
                        pltpu.VMEM((rows, D_MODEL), F32),
                        pltpu.VMEM((tp, D_MODEL), BF16), pltpu.VMEM((ts, D_MODEL), BF16),
                        pltpu.SemaphoreType.DMA((nbp,)), pltpu.SemaphoreType.DMA((nbs,))],
    )
    return pl.pallas_call(
        functools.partial(_experts_kernel, tp=tp, ts=ts, capp=capp, caps=caps),
        out_shape=[jax.ShapeDtypeStruct((N_EXPERTS, capp, D_MODEL), BF16),
                   jax.ShapeDtypeStruct((N_EXPERTS, caps, D_MODEL), BF16)],
        grid_spec=grid_spec,
        compiler_params=pltpu.CompilerParams(dimension_semantics=("arbitrary", "arbitrary"),
                                             vmem_limit_bytes=VMEM_LIMIT),
        name="experts",
    )(starts_p, starts_s, h2p, h2s, slot_p, slot_s, gate_p, gate_s, w_gate, w_up, w_down)


def _combine_kernel(st_ref, x1_ref, slot_ref, mod_ref, fg_ref, y_hbm, out_ref, acc_ref, y_ref,
                    sem_ref, arrived_ref, *, cap):
    step = pl.program_id(0)
    n_chunks = Y_CHUNKS
    chunk_rows = cap // Y_CHUNKS
    n_parts = COMBINE_ROWS // TOKEN_BLOCK
    last_block = step * n_parts + n_parts - 1

    def chunk_copy(c):
        rows = pl.ds(c * chunk_rows, chunk_rows)
        return pltpu.make_async_copy(y_hbm.at[:, rows, :], y_ref.at[:, rows, :], sem_ref.at[c])

    @pl.when(step == 0)
    def _():
        arrived_ref[0] = 0
        for c in range(n_chunks):
            chunk_copy(c).start()

    top = functools.reduce(jnp.maximum, [st_ref[e, last_block + 1] for e in range(N_EXPERTS)])
    want = pl.cdiv(jnp.minimum(top + SLOT_WINDOW, cap), chunk_rows)
    have = arrived_ref[0]
    for c in range(n_chunks):
        @pl.when((c >= have) & (c < want))
        def _(c=c):
            chunk_copy(c).wait()
    arrived_ref[0] = jnp.maximum(have, want)

    lane = lax.broadcasted_iota(jnp.int32, (TOKEN_BLOCK, PACK * SLOT_WINDOW), 1)
    gate2 = mod_ref[0][:, 5 * D_MODEL:6 * D_MODEL]

    def scatter(slot, experts, first, i):
        target = None
        windows = []
        for j, e in enumerate(experts):
            lo = first[j] + i * SLOT_WINDOW
            w = jnp.minimum(lo, cap - SLOT_WINDOW)
            sc = slot[:, e:e + 1]
            col = jnp.where(sc >= lo, sc - w + j * SLOT_WINDOW, -1)
            target = col if target is None else jnp.where(lane < j * SLOT_WINDOW, target, col)
            windows.append(y_ref[e, pl.ds(pl.multiple_of(w, BF16_ROWS), SLOT_WINDOW), :])
        onehot = jnp.where(target == lane, 1.0, 0.0).astype(BF16)
        return _dot(onehot, jnp.concatenate(windows, axis=0))

    def finish(rows):
        x2 = x1_ref[rows, :] + gate2 * acc_ref[rows, :]
        out_ref[rows, :] = _rms(x2) * fg_ref[...]

    parts = []
    for p in range(n_parts):
        rows = slice(p * TOKEN_BLOCK, (p + 1) * TOKEN_BLOCK)
        slot = slot_ref[rows, :]
        groups = []
        total = None
        for g in range(N_EXPERTS // PACK):
            experts = list(range(g * PACK, (g + 1) * PACK))
            first, n_windows = _pack_windows(st_ref, step * n_parts + p, experts, cap)
            groups.append((experts, first, n_windows))
            part = scatter(slot, experts, first, 0)
            total = part if total is None else total + part
        acc_ref[rows, :] = total
        parts.append((rows, slot, groups))
    for rows, _, _ in parts:
        finish(rows)

    for rows, slot, groups in parts:
        most = functools.reduce(jnp.maximum, [n_windows for _, _, n_windows in groups])

        @pl.when(most > 1)
        def _(rows=rows, slot=slot, groups=groups):
            for experts, first, n_windows in groups:
                def more(i, carry, experts=experts, first=first):
                    acc_ref[rows, :] += scatter(slot, experts, first, i)
                    return carry

                lax.fori_loop(1, n_windows, more, 0)
            finish(rows)


def _combine(starts, x1, slot_te, mod_rows, mod_first_row, tokens_per_mod_row, final_g, y):
    t = x1.shape[0]
    cap = y.shape[1]
    assert tokens_per_mod_row is None or tokens_per_mod_row % COMBINE_ROWS == 0
    steps_per_row = None if tokens_per_mod_row is None else tokens_per_mod_row // COMBINE_ROWS
    grid_spec = pltpu.PrefetchScalarGridSpec(
        num_scalar_prefetch=1,
        grid=(t // COMBINE_ROWS,),
        in_specs=[
            pl.BlockSpec((COMBINE_ROWS, D_MODEL), lambda b, *_: (b, 0)),
            pl.BlockSpec((COMBINE_ROWS, N_EXPERTS), lambda b, *_: (b, 0)),
            _mod_row_spec(mod_first_row, steps_per_row),
            pl.BlockSpec((1, D_MODEL), lambda b, *_: (0, 0)),
            pl.BlockSpec(memory_space=pl.ANY),
        ],
        out_specs=pl.BlockSpec((COMBINE_ROWS, D_MODEL), lambda b, *_: (b, 0)),
        scratch_shapes=[pltpu.VMEM((COMBINE_ROWS, D_MODEL), F32), pltpu.VMEM(y.shape, BF16),
                        pltpu.SemaphoreType.DMA((Y_CHUNKS,)), pltpu.SMEM((1,), jnp.int32)],
    )
    return pl.pallas_call(
        functools.partial(_combine_kernel, cap=cap),
        out_shape=jax.ShapeDtypeStruct((t, D_MODEL), F32),
        grid_spec=grid_spec,
        compiler_params=pltpu.CompilerParams(dimension_semantics=("arbitrary",),
                                             vmem_limit_bytes=VMEM_LIMIT),
        name="combine",
    )(starts, x1, slot_te, mod_rows, final_g.reshape(1, D_MODEL), y)


def kernel(x_prompt, x_sample, state_ret, c, c_ctx, norm1_g, norm2_g, final_g, w_mod, b_mod, w_in,
           w_fmix, w_out, w_router, w_gate, w_up, w_down):
    bp, seq, _ = x_prompt.shape
    bs, dec_seq, _ = x_sample.shape
    assert w_mod.shape[0] == 1, "single-layer trunk"
    tp, ts = bp * seq, bs * dec_seq

    cond = jnp.concatenate([c_ctx[None, :], c], axis=0)
    mod = _modulation(cond, w_mod[0], b_mod[0])
    ctx_row, lat_row = 0, 1

    w_out_bf = w_out[0].astype(BF16)
    mix_p, states = _mixer(x_prompt, mod, ctx_row, False, None, True, False, norm1_g[0], w_in[0],
                           w_fmix[0])
    (mix_s,) = _mixer(x_sample, mod, lat_row, True, state_ret[:, 0], False, True, norm1_g[0],
                      w_in[0], w_fmix[0])
    x1p, h2p, affp = _post(x_prompt.reshape(tp, D_MODEL), mix_p, mod, ctx_row, None,
                           norm2_g[0], w_out_bf, w_router[0])
    x1s, h2s, affs = _post(x_sample.reshape(ts, D_MODEL), mix_s, mod, lat_row, dec_seq,
                           norm2_g[0], w_out_bf, w_router[0])

    slot_p, slot_te_p, gate_p, starts_p = _route(affp)
    slot_s, slot_te_s, gate_s, starts_s = _route(affs)

    yp, ys = _experts(starts_p, starts_s, h2p, h2s, slot_p, slot_s, gate_p, gate_s,
                      w_gate[0], w_up[0], w_down[0])

    out_p = _combine(starts_p, x1p, slot_te_p, mod, ctx_row, None, final_g, yp)
    out_s = _combine(starts_s, x1s, slot_te_s, mod, lat_row, dec_seq, final_g, ys)

    y_prompt = out_p.reshape(bp, seq, D_MODEL)
    y_sample = out_s.reshape(bs, dec_seq, D_MODEL)
    state_new = states.reshape(bp, 1, 2, N_RET_HEADS, HEAD_DIM, HEAD_DIM).astype(x_prompt.dtype)
    return (y_prompt, y_sample, state_new)
```

```python
import functools
import math

import jax
import jax.numpy as jnp
import numpy as np
from jax import lax
from jax.experimental import pallas as pl
from jax.experimental.pallas import tpu as pltpu

D_MODEL = 1024
D_FOURIER = 512
N_FOURIER_GROUPS = 4
FOURIER_GROUP_W = 128
D_RET = 512
N_RET_HEADS = 4
HEAD_DIM = 128
CHUNK = 256
GRID_W = 64
N_EXPERTS = 16
EC_CAPACITY_FACTOR = 2
D_EXPERT_FF = 2816
ROPE_BASE = 10000.0
EPS = 1e-6
D_IN_PROJ = D_FOURIER + 5 * D_RET
LOG_GAMMA_FWD = np.log(1.0 - 2.0 ** (-5.0 - np.arange(N_RET_HEADS))).astype(np.float32)
LOG_GAMMA_BWD = np.log(1.0 - 2.0 ** (-5.5 - np.arange(N_RET_HEADS))).astype(np.float32)

LANES = 128
BF16_ROWS = 16
TOKEN_BLOCK = 256
SLOT_WINDOW = 64
PACK = TOKEN_BLOCK // SLOT_WINDOW
GATHER_UNROLL = 4
Y_CHUNKS = 4
FF_TILE = 256
N_FF_TILES = D_EXPERT_FF // FF_TILE
MOD_TILE = 1024
MIXER_ROWS = 1024
POST_ROWS = 512
POST_PART = 256
COMBINE_ROWS = 512
VMEM_LIMIT = 56 * 1024 * 1024

F32 = jnp.float32
BF16 = jnp.bfloat16


def _dot(a, b):
    return jnp.dot(a, b, preferred_element_type=F32)


def _dot_nt(a, b):
    return lax.dot_general(a, b, (((1,), (1,)), ((), ())), preferred_element_type=F32)


def _silu(x):
    return x * jax.nn.sigmoid(x)


def _mod_kernel(condt_ref, w_ref, b_ref, out_ref, *, n_cond):
    s = _silu(condt_ref[...])
    w = w_ref[...]
    out_ref[...] = jnp.zeros(out_ref.shape, F32)
    for r in range(n_cond):
        out_ref[r] = jnp.sum(w * s[:, r:r + 1], axis=0, keepdims=True) + b_ref[...]


def _modulation(cond_rows, w_mod, b_mod):
    n_cond = cond_rows.shape[0]
    condt = jnp.zeros((D_MODEL, 8), F32).at[:, :n_cond].set(cond_rows.T)
    n_out = w_mod.shape[1]
    return pl.pallas_call(
        functools.partial(_mod_kernel, n_cond=n_cond),
        out_shape=jax.ShapeDtypeStruct((8, 1, n_out), F32),
        grid=(n_out // MOD_TILE,),
        in_specs=[
            pl.BlockSpec((D_MODEL, 8), lambda j: (0, 0)),
            pl.BlockSpec((D_MODEL, MOD_TILE), lambda j: (0, j)),
            pl.BlockSpec((1, MOD_TILE), lambda j: (0, j)),
        ],
        out_specs=pl.BlockSpec((8, 1, MOD_TILE), lambda j: (0, 0, j)),
        compiler_params=pltpu.CompilerParams(dimension_semantics=("arbitrary",)),
        name="mod",
    )(condt, w_mod, b_mod.reshape(1, n_out))


def _rms(x):
    return x * lax.rsqrt(jnp.mean(x * x, axis=-1, keepdims=True) + EPS)


def _groupnorm(o):
    mu = jnp.mean(o, axis=-1, keepdims=True)
    c = o - mu
    return c * lax.rsqrt(jnp.mean(c * c, axis=-1, keepdims=True) + EPS)


def _split_hi_lo(x):
    hi = x.astype(BF16)
    lo = (x - hi.astype(F32)).astype(BF16)
    return hi, lo


def _mixer_kernel(*refs, n, use_rope, has_state_in, emit_state):
    it = iter(refs)
    x_ref, mod_ref, g1_ref, win_ref, wfmix_ref = (next(it) for _ in range(5))
    cw_ref, cn_ref, sn_ref, dmat_ref, qdec_ref, kdec_ref, sdec_ref = (next(it) for _ in range(7))
    cos_ref = sin_ref = s0_ref = st_ref = None
    if use_rope:
        cos_ref, sin_ref = next(it), next(it)
    if has_state_in:
        s0_ref = next(it)
    mix_ref = next(it)
    if emit_state:
        st_ref = next(it)
    p_ref, of_ref, ob_ref = next(it), next(it), next(it)

    n_seq = MIXER_ROWS // n
    chunks_per_seq = n // CHUNK
    mod = mod_ref[0]
    shift1 = mod[:, 0:D_MODEL]
    scale1 = mod[:, D_MODEL:2 * D_MODEL]

    h = (_rms(x_ref[...]) * g1_ref[...] * (1.0 + scale1) + shift1).astype(BF16)
    for j in range(D_IN_PROJ // 512):
        p_ref[:, j * 512:(j + 1) * 512] = _dot(h, win_ref[:, j * 512:(j + 1) * 512].astype(BF16))

    xf = p_ref[:, 0:D_FOURIER].astype(BF16)
    xc, xs = [], []
    cw = cw_ref[...].astype(BF16)
    for g in range(N_FOURIER_GROUPS):
        t = _dot(xf[:, g * FOURIER_GROUP_W:(g + 1) * FOURIER_GROUP_W], cw)
        xc.append(t[:, :FOURIER_GROUP_W].astype(BF16))
        xs.append(t[:, FOURIER_GROUP_W:].astype(BF16))
    xc = jnp.concatenate(xc, axis=1)
    xs = jnp.concatenate(xs, axis=1)
    cn = cn_ref[...].astype(BF16)
    sn = sn_ref[...].astype(BF16)
    for s in range(n_seq):
        rs = slice(s * n, (s + 1) * n)
        fre = (_dot(cn, xc[rs]) - _dot(sn, xs[rs])) * (1.0 / math.sqrt(n * FOURIER_GROUP_W))
        fre = fre.astype(BF16)
        for g in range(N_FOURIER_GROUPS):
            sl = slice(g * FOURIER_GROUP_W, (g + 1) * FOURIER_GROUP_W)
            mix_ref[rs, sl] = _dot(fre[:, sl], wfmix_ref[g].astype(BF16)).astype(BF16)

    for hh in range(N_RET_HEADS):
        base = D_FOURIER + hh * HEAD_DIM
        q = p_ref[:, base:base + HEAD_DIM]
        k = p_ref[:, base + D_RET:base + D_RET + HEAD_DIM]
        v = p_ref[:, base + 2 * D_RET:base + 2 * D_RET + HEAD_DIM]
        if use_rope:
            lane = lax.broadcasted_iota(jnp.int32, (MIXER_ROWS, HEAD_DIM), 1)
            first = (lane % 64) < 32

            def rope(t):
                swapped = jnp.where(first, pltpu.roll(t, HEAD_DIM - 32, 1), pltpu.roll(t, 32, 1))
                return t * cos_ref[...] + swapped * sin_ref[...]

            q, k = rope(q), rope(k)
        k = k * (HEAD_DIM ** -0.5)
        qb, vb = q.astype(BF16), v.astype(BF16)
        kb = k.astype(BF16)

        def initial(s, direction):
            if has_state_in:
                return s0_ref[s, direction, hh]
            return jnp.zeros((HEAD_DIM, HEAD_DIM), F32)

        for s in range(n_seq):
            parts = []
            for ci in range(chunks_per_seq):
                c = s * chunks_per_seq + ci
                rs = slice(c * CHUNK, (c + 1) * CHUNK)
                qc, kc, vc = qb[rs], kb[rs], vb[rs]
                qk = _dot_nt(qc, kc)
                lhs = jnp.concatenate([(qk * dmat_ref[0, hh]).astype(BF16),
                                       (qk * dmat_ref[1, hh]).astype(BF16),
                                       (k[rs] * kdec_ref[0, hh]).T.astype(BF16),
                                       (k[rs] * kdec_ref[1, hh]).T.astype(BF16)], axis=0)
                parts.append((rs, qc, _dot(lhs, vc)))
            sf = initial(s, 0)
            for ci in range(chunks_per_seq):
                rs, qc, r = parts[ci]
                o = r[0:CHUNK]
                if has_state_in or ci > 0:
                    o = o + qdec_ref[0, hh] * _dot(qc, sf.astype(BF16))
                of_ref[rs, :] = o
                sf = sf * sdec_ref[0, hh] + r[2 * CHUNK:2 * CHUNK + HEAD_DIM]
            sb = initial(s, 1)
            for ci in reversed(range(chunks_per_seq)):
                rs, qc, r = parts[ci]
                o = r[CHUNK:2 * CHUNK]
                if has_state_in or ci < chunks_per_seq - 1:
                    o = o + qdec_ref[1, hh] * _dot(qc, sb.astype(BF16))
                ob_ref[rs, :] = o
                sb = sb * sdec_ref[1, hh] + r[2 * CHUNK + HEAD_DIM:]
            if emit_state:
                st_ref[s, 0, hh] = sf
                st_ref[s, 1, hh] = sb

        gf = p_ref[:, base + 3 * D_RET:base + 3 * D_RET + HEAD_DIM]
        gb = p_ref[:, base + 4 * D_RET:base + 4 * D_RET + HEAD_DIM]
        y = _silu(gf) * _groupnorm(of_ref[...]) + _silu(gb) * _groupnorm(ob_ref[...])
        mix_ref[:, base:base + HEAD_DIM] = y.astype(BF16)


def _post_kernel(x_ref, mix_ref, mod_ref, g2_ref, wout_ref, wr_ref, x1_ref, h2_ref, aff_ref):
    mod = mod_ref[0]
    gate1 = mod[:, 2 * D_MODEL:3 * D_MODEL]
    shift2 = mod[:, 3 * D_MODEL:4 * D_MODEL]
    scale2 = mod[:, 4 * D_MODEL:5 * D_MODEL]
    wr_hi, wr_lo = _split_hi_lo(wr_ref[...])
    wr_both = jnp.concatenate([wr_hi, wr_lo], axis=1)
    parts = [slice(p * POST_PART, (p + 1) * POST_PART) for p in range(POST_ROWS // POST_PART)]
    x1 = []
    for rows in parts:
        x1.append(x_ref[rows, :] + gate1 * _dot(mix_ref[rows, :], wout_ref[...]))
        x1_ref[rows, :] = x1[-1]
    for rows, x1_part in zip(parts, x1):
        h2 = _rms(x1_part) * g2_ref[...] * (1.0 + scale2) + shift2
        h2_hi, h2_lo = _split_hi_lo(h2)
        h2_ref[rows, :] = h2_hi
        by_hi = _dot(h2_hi, wr_both)
        logits = by_hi[:, :N_EXPERTS] + (_dot(h2_lo, wr_hi) + by_hi[:, N_EXPERTS:])
        z = jnp.exp(logits - jnp.max(logits, axis=-1, keepdims=True))
        aff = z / jnp.sum(z, axis=-1, keepdims=True)
        lanes = jnp.concatenate([aff, jnp.zeros((POST_PART, LANES - N_EXPERTS), F32)], axis=1)
        aff_ref[:, rows] = lanes.T[0:N_EXPERTS, :]


def _dft_consts(n):
    w = FOURIER_GROUP_W
    jw = np.arange(w)
    angw = 2.0 * np.pi * np.outer(jw, jw) / w
    cw = np.concatenate([np.cos(angw), np.sin(angw)], axis=1)
    jn = np.arange(n)
    angn = 2.0 * np.pi * (np.outer(jn, jn) % n) / n
    return (jnp.asarray(cw, F32), jnp.asarray(np.cos(angn), F32), jnp.asarray(np.sin(angn), F32))


def _retention_consts():
    i = np.arange(CHUNK, dtype=np.float64)
    diff = i[:, None] - i[None, :]
    dmat = np.zeros((2, N_RET_HEADS, CHUNK, CHUNK))
    qdec = np.zeros((2, N_RET_HEADS, CHUNK, HEAD_DIM))
    kdec = np.zeros((2, N_RET_HEADS, CHUNK, HEAD_DIM))
    sdec = np.zeros((2, N_RET_HEADS, HEAD_DIM, HEAD_DIM))
    for hh in range(N_RET_HEADS):
        lf = float(LOG_GAMMA_FWD[hh])
        lb = float(LOG_GAMMA_BWD[hh])
        dmat[0, hh] = np.where(diff >= 0, np.exp(lf * np.maximum(diff, 0.0)), 0.0)
        dmat[1, hh] = np.where(diff <= 0, np.exp(lb * np.maximum(-diff, 0.0)), 0.0)
        qdec[0, hh] = np.exp(lf * (i + 1.0))[:, None]
        qdec[1, hh] = np.exp(lb * (CHUNK - i))[:, None]
        kdec[0, hh] = np.exp(lf * (CHUNK - 1.0 - i))[:, None]
        kdec[1, hh] = np.exp(lb * i)[:, None]
        sdec[0, hh] = math.exp(lf * CHUNK)
        sdec[1, hh] = math.exp(lb * CHUNK)
    return tuple(jnp.asarray(a, F32) for a in (dmat, qdec, kdec, sdec))


def _rope_consts(n):
    rows_n = n // GRID_W
    row = np.repeat(np.arange(rows_n, dtype=np.float64), GRID_W)
    col = np.tile(np.arange(GRID_W, dtype=np.float64), rows_n)
    n_pairs = HEAD_DIM // 4
    freqs = (np.float32(ROPE_BASE) ** (-np.arange(n_pairs, dtype=np.float32) / n_pairs)).astype(np.float64)
    ar = row[:, None] * freqs[None, :]
    ac = col[:, None] * freqs[None, :]
    cos = np.concatenate([np.cos(ar), np.cos(ar), np.cos(ac), np.cos(ac)], axis=1)
    sin = np.concatenate([-np.sin(ar), np.sin(ar), -np.sin(ac), np.sin(ac)], axis=1)
    return jnp.asarray(cos, F32), jnp.asarray(sin, F32)


def _const_spec(shape):
    nd = len(shape)
    return pl.BlockSpec(shape, lambda b, _nd=nd: (0,) * _nd, pipeline_mode=pl.Buffered(1))


def _mod_row_spec(first_row, blocks_per_row):
    if blocks_per_row is None:
        return pl.BlockSpec((1, 1, 6 * D_MODEL), lambda b, *_: (first_row, 0, 0))
    return pl.BlockSpec((1, 1, 6 * D_MODEL), lambda b, *_: (first_row + b // blocks_per_row, 0, 0))


def _mixer(x, mod_rows, mod_first_row, mod_per_batch, state_in, emit_state, use_rope, g1, w_in,
           w_fmix):
    nb, n, _ = x.shape
    assert MIXER_ROWS % n == 0 and (nb * n) % MIXER_ROWS == 0
    n_seq = MIXER_ROWS // n
    has_state_in = state_in is not None
    cw, cn, sn = _dft_consts(n)
    dmat, qdec, kdec, sdec = _retention_consts()
    consts = [cw, cn, sn, dmat, qdec, kdec, sdec]
    if use_rope:
        assert n_seq == 1
        consts += list(_rope_consts(n))
    weights = [g1.reshape(1, D_MODEL), w_in, w_fmix]

    if mod_per_batch:
        assert n % MIXER_ROWS == 0
    state_spec = pl.BlockSpec((n_seq, 2, N_RET_HEADS, HEAD_DIM, HEAD_DIM), lambda b: (b, 0, 0, 0, 0))
    row_spec = pl.BlockSpec((MIXER_ROWS, D_MODEL), lambda b: (b, 0))
    in_specs = [row_spec, _mod_row_spec(mod_first_row, n // MIXER_ROWS if mod_per_batch else None)]
    in_specs += [_const_spec(a.shape) for a in weights + consts]
    args = [x.reshape(nb * n, D_MODEL), mod_rows] + weights + consts
    if has_state_in:
        in_specs.append(state_spec)
        args.append(state_in)

    out_shape = [jax.ShapeDtypeStruct((nb * n, D_MODEL), BF16)]
    out_specs = [row_spec]
    if emit_state:
        out_shape.append(jax.ShapeDtypeStruct((nb, 2, N_RET_HEADS, HEAD_DIM, HEAD_DIM), F32))
        out_specs.append(state_spec)

    return pl.pallas_call(
        functools.partial(_mixer_kernel, n=n, use_rope=use_rope, has_state_in=has_state_in,
                          emit_state=emit_state),
        out_shape=out_shape,
        grid=(nb * n // MIXER_ROWS,),
        in_specs=in_specs,
        out_specs=out_specs,
        scratch_shapes=[pltpu.VMEM((MIXER_ROWS, D_IN_PROJ), F32),
                        pltpu.VMEM((MIXER_ROWS, HEAD_DIM), F32), pltpu.VMEM((MIXER_ROWS, HEAD_DIM), F32)],
        compiler_params=pltpu.CompilerParams(dimension_semantics=("arbitrary",),
                                             vmem_limit_bytes=VMEM_LIMIT),
        name="mixer_rope" if use_rope else "mixer",
    )(*args)


def _post(x, mix, mod_rows, mod_first_row, tokens_per_mod_row, g2, w_out_bf, w_router):
    t = x.shape[0]
    assert tokens_per_mod_row is None or tokens_per_mod_row % POST_ROWS == 0
    blocks_per_row = None if tokens_per_mod_row is None else tokens_per_mod_row // POST_ROWS
    row_spec = pl.BlockSpec((POST_ROWS, D_MODEL), lambda b: (b, 0))
    return pl.pallas_call(
        _post_kernel,
        out_shape=[jax.ShapeDtypeStruct((t, D_MODEL), F32),
                   jax.ShapeDtypeStruct((t, D_MODEL), BF16),
                   jax.ShapeDtypeStruct((N_EXPERTS, t), F32)],
        grid=(t // POST_ROWS,),
        in_specs=[row_spec, row_spec,
                  _mod_row_spec(mod_first_row, blocks_per_row),
                  _const_spec((1, D_MODEL)), _const_spec((D_MODEL, D_MODEL)),
                  _const_spec((D_MODEL, N_EXPERTS))],
        out_specs=[row_spec, row_spec, pl.BlockSpec((N_EXPERTS, POST_ROWS), lambda b: (0, b))],
        compiler_params=pltpu.CompilerParams(dimension_semantics=("arbitrary",),
                                             vmem_limit_bytes=VMEM_LIMIT),
        name="post",
    )(x, mix, mod_rows, g2.reshape(1, D_MODEL), w_out_bf, w_router)


def _route_kernel(*refs, sizes):
    n = len(sizes)
    affs = [ref[...] for ref in refs[:n]]
    u_ref = refs[n]
    outs = [refs[n + 1 + 4 * g:n + 5 + 4 * g] for g in range(n)]

    def count(mask):
        return jnp.sum(mask.astype(jnp.int32), axis=1, keepdims=True)

    def as_float(word):
        return lax.bitcast_convert_type(word, F32)

    def zeros():
        return tuple(jnp.zeros((N_EXPERTS, 1), jnp.int32) for _ in sizes)

    def value_step(i, curs):
        bit = jnp.left_shift(jnp.int32(1), 30 - i)
        return tuple(jnp.where(count(aff >= as_float(cur | bit)) >= cap, cur | bit, cur)
                     for aff, (_, cap), cur in zip(affs, sizes, curs))

    thrs = lax.fori_loop(0, 31, value_step, zeros())
    gts = [aff >= as_float(thr + 1) for aff, thr in zip(affs, thrs)]
    eqs = [(aff >= as_float(thr)) & jnp.logical_not(gt) for aff, thr, gt in zip(affs, thrs, gts)]
    needs = [cap - count(gt) for (_, cap), gt in zip(sizes, gts)]
    toks = [lax.broadcasted_iota(jnp.int32, (N_EXPERTS, t), 1) for t, _ in sizes]
    nbits = [t.bit_length() - 1 for t, _ in sizes]

    def index_step(i, curs):
        new = []
        for eq, need, tok, bits, cur in zip(eqs, needs, toks, nbits, curs):
            shift = bits - 1 - i
            cand = cur | jnp.where(shift >= 0, jnp.left_shift(jnp.int32(1), jnp.maximum(shift, 0)), 0)
            new.append(jnp.where(count(eq & (tok < cand)) < need, cand, cur))
        return tuple(new)

    lasts = lax.fori_loop(0, max(nbits), index_step, zeros())

    for aff, (t, _), gt, eq, tok, last, (slot_ref, slot_te_ref, gate_ref, starts_ref) in zip(
            affs, sizes, gts, eqs, toks, lasts, outs):
        self = jnp.where(gt | (eq & (tok <= last)), 1.0, 0.0).astype(F32)
        carry = jnp.zeros((N_EXPERTS, 1), F32)
        starts_ref[...] = jnp.zeros(starts_ref.shape, jnp.int32)
        for b in range(t // TOKEN_BLOCK):
            sl = slice(b * TOKEN_BLOCK, (b + 1) * TOKEN_BLOCK)
            sbf = self[:, sl]
            pre = _dot(sbf.astype(BF16), u_ref[...]) + carry
            slots = jnp.where(sbf > 0.5, pre.astype(jnp.int32), -1)
            slot_ref[:, b, :] = slots
            rows = jnp.concatenate([slots, jnp.zeros((LANES - N_EXPERTS, TOKEN_BLOCK), jnp.int32)],
                                   axis=0)
            slot_te_ref[sl, :] = rows.T[:, 0:N_EXPERTS]
            gate_ref[:, b, :] = aff[:, sl]
            starts_ref[:, b:b + 1] = carry.astype(jnp.int32)
            carry = carry + jnp.sum(sbf, axis=1, keepdims=True)
        nblk = t // TOKEN_BLOCK
        starts_ref[:, nblk:nblk + 1] = carry.astype(jnp.int32)


def _route(*affs_et):
    sizes = tuple((a.shape[1], EC_CAPACITY_FACTOR * a.shape[1] // N_EXPERTS) for a in affs_et)
    upper = jnp.asarray(np.triu(np.ones((TOKEN_BLOCK, TOKEN_BLOCK)), 1), BF16)
    whole = lambda shape: pl.BlockSpec(shape, lambda i, _n=len(shape): (0,) * _n)
    out_shape, out_specs = [], []
    for t, _ in sizes:
        nblk = t // TOKEN_BLOCK
        assert nblk + 1 <= LANES
        for shape, dtype in (((N_EXPERTS, nblk, TOKEN_BLOCK), jnp.int32), ((t, N_EXPERTS), jnp.int32),
                             ((N_EXPERTS, nblk, TOKEN_BLOCK), F32), ((N_EXPERTS, LANES), jnp.int32)):
            out_shape.append(jax.ShapeDtypeStruct(shape, dtype))
            out_specs.append(whole(shape))
    outs = pl.pallas_call(
        functools.partial(_route_kernel, sizes=sizes),
        out_shape=out_shape,
        grid=(1,),
        in_specs=[whole(a.shape) for a in affs_et] + [whole((TOKEN_BLOCK, TOKEN_BLOCK))],
        out_specs=out_specs,
        compiler_params=pltpu.CompilerParams(dimension_semantics=("arbitrary",)),
        name="route",
    )(*affs_et, upper)
    return [outs[4 * g:4 * g + 4] for g in range(len(sizes))]


def _pack_windows(starts_ref, b, experts, cap):
    first = [jnp.minimum((starts_ref[e, b] // BF16_ROWS) * BF16_ROWS, cap - SLOT_WINDOW)
             for e in experts]
    rows = [jnp.where(starts_ref[e, b + 1] > starts_ref[e, b], starts_ref[e, b + 1] - w, 0)
            for e, w in zip(experts, first)]
    return first, pl.cdiv(functools.reduce(jnp.maximum, rows), SLOT_WINDOW)


def _block_copy(hbm_ref, vmem_ref, sem_ref, b):
    rows = pl.ds(pl.multiple_of(b * TOKEN_BLOCK, TOKEN_BLOCK), TOKEN_BLOCK)
    return pltpu.make_async_copy(hbm_ref.at[rows], vmem_ref.at[rows], sem_ref.at[b])


def _gather_group(g, starts_ref, slot_ref, gate_ref, h2_ref, xs_ref, gs_ref, row0, t, cap, arrive):
    sub = lax.broadcasted_iota(jnp.int32, (SLOT_WINDOW, TOKEN_BLOCK), 0)
    experts = [g * PACK + j for j in range(PACK)]
    assert (t // TOKEN_BLOCK) % GATHER_UNROLL == 0

    def window(b, first, i):
        hb = pl.ds(pl.multiple_of(b * TOKEN_BLOCK, TOKEN_BLOCK), TOKEN_BLOCK)
        hits, dst = [], []
        for j in range(PACK):
            lo = first[j] + i * SLOT_WINDOW
            w = jnp.minimum(lo, cap - SLOT_WINDOW)
            srow = slot_ref[j, pl.ds(b, 1), :]
            hits.append((srow == w + sub) & (srow >= lo))
            dst.append(pl.ds(pl.multiple_of(row0 + w, BF16_ROWS), SLOT_WINDOW))
        onehot = jnp.concatenate([jnp.where(h, 1.0, 0.0) for h in hits], axis=0).astype(BF16)
        got = _dot(onehot, h2_ref[hb, :])
        for j in range(PACK):
            piece = got[j * SLOT_WINDOW:(j + 1) * SLOT_WINDOW].astype(BF16)
            xs_ref[j, dst[j], :] = xs_ref[j, dst[j], :] + piece
            grow = gate_ref[j, pl.ds(b, 1), :]
            gs_ref[j, dst[j], :] += jnp.sum(jnp.where(hits[j], grow, 0.0), axis=1, keepdims=True)

    def blocks(q, carry):
        pending = []
        for u in range(GATHER_UNROLL):
            arrive(q * GATHER_UNROLL + u)
        for u in range(GATHER_UNROLL):
            b = q * GATHER_UNROLL + u
            first, n_windows = _pack_windows(starts_ref, b, experts, cap)
            window(b, first, 0)
            pending.append((b, first, n_windows))
        for b, first, n_windows in pending:
            def more(i, carry, b=b, first=first):
                window(b, first, i)
                return carry

            lax.fori_loop(1, n_windows, more, 0)
        return carry

    lax.fori_loop(0, t // TOKEN_BLOCK // GATHER_UNROLL, blocks, 0)


def _experts_kernel(sp_ref, ss_ref, h2p_hbm, h2s_hbm, slotp_ref, slots_ref, gatep_ref, gates_ref,
                    wg_ref, wu_ref, wd_ref, yp_ref, ys_ref, xs_ref, gs_ref, acc_ref,
                    h2p_ref, h2s_ref, semp_ref, sems_ref, *, tp, ts, capp, caps):
    g = pl.program_id(0)
    step = pl.program_id(1)
    j = step // N_FF_TILES
    f = step % N_FF_TILES
    loading = [(h2p_hbm, h2p_ref, semp_ref, tp // TOKEN_BLOCK),
               (h2s_hbm, h2s_ref, sems_ref, ts // TOKEN_BLOCK)]

    @pl.when((step == 0) & (g == 0))
    def _():
        for hbm_ref, vmem_ref, sem_ref, n_blocks in loading:
            for b in range(n_blocks):
                _block_copy(hbm_ref, vmem_ref, sem_ref, b).start()

    def arrive(hbm_ref, vmem_ref, sem_ref, _):
        def wait(b):
            @pl.when(g == 0)
            def _():
                _block_copy(hbm_ref, vmem_ref, sem_ref, b).wait()

        return wait

    @pl.when(step == 0)
    def _():
        xs_ref[...] = jnp.zeros(xs_ref.shape, BF16)
        gs_ref[...] = jnp.zeros(gs_ref.shape, F32)
        _gather_group(g, sp_ref, slotp_ref, gatep_ref, h2p_ref, xs_ref, gs_ref, 0, tp, capp,
                      arrive(*loading[0]))
        _gather_group(g, ss_ref, slots_ref, gates_ref, h2s_ref, xs_ref, gs_ref, capp, ts, caps,
                      arrive(*loading[1]))

    @pl.when(f == 0)
    def _():
        acc_ref[...] = jnp.zeros(acc_ref.shape, F32)

    x = xs_ref[j]
    a = _dot(x, wg_ref[0].astype(BF16))
    u = _dot(x, wu_ref[0].astype(BF16))
    acc_ref[...] += _dot((_silu(a) * u).astype(BF16), wd_ref[0].astype(BF16))

    @pl.when(f == N_FF_TILES - 1)
    def _():
        yp_ref[0] = (acc_ref[0:capp, :] * gs_ref[j, 0:capp, :]).astype(BF16)
        ys_ref[0] = (acc_ref[capp:capp + caps, :] * gs_ref[j, capp:capp + caps, :]).astype(BF16)


def _experts(starts_p, starts_s, h2p, h2s, slot_p, slot_s, gate_p, gate_s, w_gate, w_up, w_down):
    tp, ts = h2p.shape[0], h2s.shape[0]
    capp = EC_CAPACITY_FACTOR * tp // N_EXPERTS
    caps = EC_CAPACITY_FACTOR * ts // N_EXPERTS
    rows = capp + caps
    nbp, nbs = tp // TOKEN_BLOCK, ts // TOKEN_BLOCK
    expert = lambda g, s: g * PACK + s // N_FF_TILES
    grid_spec = pltpu.PrefetchScalarGridSpec(
        num_scalar_prefetch=2,
        grid=(N_EXPERTS // PACK, PACK * N_FF_TILES),
        in_specs=[
            pl.BlockSpec(memory_space=pl.ANY),
            pl.BlockSpec(memory_space=pl.ANY),
            pl.BlockSpec((PACK, nbp, TOKEN_BLOCK), lambda g, s, *_: (g, 0, 0)),
            pl.BlockSpec((PACK, nbs, TOKEN_BLOCK), lambda g, s, *_: (g, 0, 0)),
            pl.BlockSpec((PACK, nbp, TOKEN_BLOCK), lambda g, s, *_: (g, 0, 0)),
            pl.BlockSpec((PACK, nbs, TOKEN_BLOCK), lambda g, s, *_: (g, 0, 0)),
            pl.BlockSpec((1, D_MODEL, FF_TILE), lambda g, s, *_: (expert(g, s), 0, s % N_FF_TILES)),
            pl.BlockSpec((1, D_MODEL, FF_TILE), lambda g, s, *_: (expert(g, s), 0, s % N_FF_TILES)),
            pl.BlockSpec((1, FF_TILE, D_MODEL), lambda g, s, *_: (expert(g, s), s % N_FF_TILES, 0)),
        ],
        out_specs=[
            pl.BlockSpec((1, capp, D_MODEL), lambda g, s, *_: (expert(g, s), 0, 0)),
            pl.BlockSpec((1, caps, D_MODEL), lambda g, s, *_: (expert(g, s), 0, 0)),
        ],
        scratch_shapes=[pltpu.VMEM((PACK, rows, D_MODEL), BF16), pltpu.VMEM((PACK, rows, 1), F32),
                        pltpu.VMEM((rows, D_MODEL), F32),
                        pltpu.VMEM((tp, D_MODEL), BF16), pltpu.VMEM((ts, D_MODEL), BF16),
                        pltpu.SemaphoreType.DMA((nbp,)), pltpu.SemaphoreType.DMA((nbs,))],
    )
    return pl.pallas_call(
        functools.partial(_experts_kernel, tp=tp, ts=ts, capp=capp, caps=caps),
        out_shape=[jax.ShapeDtypeStruct((N_EXPERTS, capp, D_MODEL), BF16),
                   jax.ShapeDtypeStruct((N_EXPERTS, caps, D_MODEL), BF16)],
        grid_spec=grid_spec,
        compiler_params=pltpu.CompilerParams(dimension_semantics=("arbitrary", "arbitrary"),
                                             vmem_limit_bytes=VMEM_LIMIT),
        name="experts",
    )(starts_p, starts_s, h2p, h2s, slot_p, slot_s, gate_p, gate_s, w_gate, w_up, w_down)


def _combine_kernel(st_ref, x1_ref, slot_ref, mod_ref, fg_ref, y_hbm, out_ref, acc_ref, y_ref,
                    sem_ref, arrived_ref, *, cap):
    step = pl.program_id(0)
    n_chunks = Y_CHUNKS
    chunk_rows = cap // Y_CHUNKS
    n_parts = COMBINE_ROWS // TOKEN_BLOCK
    last_block = step * n_parts + n_parts - 1

    def chunk_copy(c):
        rows = pl.ds(c * chunk_rows, chunk_rows)
        return pltpu.make_async_copy(y_hbm.at[:, rows, :], y_ref.at[:, rows, :], sem_ref.at[c])

    @pl.when(step == 0)
    def _():
        arrived_ref[0] = 0
        for c in range(n_chunks):
            chunk_copy(c).start()

    top = functools.reduce(jnp.maximum, [st_ref[e, last_block + 1] for e in range(N_EXPERTS)])
    want = pl.cdiv(jnp.minimum(top + SLOT_WINDOW, cap), chunk_rows)
    have = arrived_ref[0]
    for c in range(n_chunks):
        @pl.when((c >= have) & (c < want))
        def _(c=c):
            chunk_copy(c).wait()
    arrived_ref[0] = jnp.maximum(have, want)

    lane = lax.broadcasted_iota(jnp.int32, (TOKEN_BLOCK, PACK * SLOT_WINDOW), 1)
    gate2 = mod_ref[0][:, 5 * D_MODEL:6 * D_MODEL]

    def scatter(slot, experts, first, i):
        target = None
        windows = []
        for j, e in enumerate(experts):
            lo = first[j] + i * SLOT_WINDOW
            w = jnp.minimum(lo, cap - SLOT_WINDOW)
            sc = slot[:, e:e + 1]
            col = jnp.where(sc >= lo, sc - w + j * SLOT_WINDOW, -1)
            target = col if target is None else jnp.where(lane < j * SLOT_WINDOW, target, col)
            windows.append(y_ref[e, pl.ds(pl.multiple_of(w, BF16_ROWS), SLOT_WINDOW), :])
        onehot = jnp.where(target == lane, 1.0, 0.0).astype(BF16)
        return _dot(onehot, jnp.concatenate(windows, axis=0))

    def finish(rows):
        x2 = x1_ref[rows, :] + gate2 * acc_ref[rows, :]
        out_ref[rows, :] = _rms(x2) * fg_ref[...]

    parts = []
    for p in range(n_parts):
        rows = slice(p * TOKEN_BLOCK, (p + 1) * TOKEN_BLOCK)
        slot = slot_ref[rows, :]
        groups = []
        total = None
        for g in range(N_EXPERTS // PACK):
            experts = list(range(g * PACK, (g + 1) * PACK))
            first, n_windows = _pack_windows(st_ref, step * n_parts + p, experts, cap)
            groups.append((experts, first, n_windows))
            part = scatter(slot, experts, first, 0)
            total = part if total is None else total + part
        acc_ref[rows, :] = total
        parts.append((rows, slot, groups))
    for rows, _, _ in parts:
        finish(rows)

    for rows, slot, groups in parts:
        most = functools.reduce(jnp.maximum, [n_windows for _, _, n_windows in groups])

        @pl.when(most > 1)
        def _(rows=rows, slot=slot, groups=groups):
            for experts, first, n_windows in groups:
                def more(i, carry, experts=experts, first=first):
                    acc_ref[rows, :] += scatter(slot, experts, first, i)
                    return carry

                lax.fori_loop(1, n_windows, more, 0)
            finish(rows)


def _combine(starts, x1, slot_te, mod_rows, mod_first_row, tokens_per_mod_row, final_g, y):
    t = x1.shape[0]
    cap = y.shape[1]
    assert tokens_per_mod_row is None or tokens_per_mod_row % COMBINE_ROWS == 0
    steps_per_row = None if tokens_per_mod_row is None else tokens_per_mod_row // COMBINE_ROWS
    grid_spec = pltpu.PrefetchScalarGridSpec(
        num_scalar_prefetch=1,
        grid=(t // COMBINE_ROWS,),
        in_specs=[
            pl.BlockSpec((COMBINE_ROWS, D_MODEL), lambda b, *_: (b, 0)),
            pl.BlockSpec((COMBINE_ROWS, N_EXPERTS), lambda b, *_: (b, 0)),
            _mod_row_spec(mod_first_row, steps_per_row),
            pl.BlockSpec((1, D_MODEL), lambda b, *_: (0, 0)),
            pl.BlockSpec(memory_space=pl.ANY),
        ],
        out_specs=pl.BlockSpec((COMBINE_ROWS, D_MODEL), lambda b, *_: (b, 0)),
        scratch_shapes=[pltpu.VMEM((COMBINE_ROWS, D_MODEL), F32), pltpu.VMEM(y.shape, BF16),
                        pltpu.SemaphoreType.DMA((Y_CHUNKS,)), pltpu.SMEM((1,), jnp.int32)],
    )
    return pl.pallas_call(
        functools.partial(_combine_kernel, cap=cap),
        out_shape=jax.ShapeDtypeStruct((t, D_MODEL), F32),
        grid_spec=grid_spec,
        compiler_params=pltpu.CompilerParams(dimension_semantics=("arbitrary",),
                                             vmem_limit_bytes=VMEM_LIMIT),
        name="combine",
    )(starts, x1, slot_te, mod_rows, final_g.reshape(1, D_MODEL), y)


def kernel(x_prompt, x_sample, state_ret, c, c_ctx, norm1_g, norm2_g, final_g, w_mod, b_mod, w_in,
           w_fmix, w_out, w_router, w_gate, w_up, w_down):
    bp, seq, _ = x_prompt.shape
    bs, dec_seq, _ = x_sample.shape
    assert w_mod.shape[0] == 1, "single-layer trunk"
    tp, ts = bp * seq, bs * dec_seq

    cond = jnp.concatenate([c_ctx[None, :], c], axis=0)
    mod = _modulation(cond, w_mod[0], b_mod[0])
    ctx_row, lat_row = 0, 1

    w_out_bf = w_out[0].astype(BF16)
    mix_p, states = _mixer(x_prompt, mod, ctx_row, False, None, True, False, norm1_g[0], w_in[0],
                           w_fmix[0])
    (mix_s,) = _mixer(x_sample, mod, lat_row, True, state_ret[:, 0], False, True, norm1_g[0],
                      w_in[0], w_fmix[0])
    x1p, h2p, affp = _post(x_prompt.reshape(tp, D_MODEL), mix_p, mod, ctx_row, None,
                           norm2_g[0], w_out_bf, w_router[0])
    x1s, h2s, affs = _post(x_sample.reshape(ts, D_MODEL), mix_s, mod, lat_row, dec_seq,
                           norm2_g[0], w_out_bf, w_router[0])

    (slot_p, slot_te_p, gate_p, starts_p), (slot_s, slot_te_s, gate_s, starts_s) = _route(affp, affs)

    yp, ys = _experts(starts_p, starts_s, h2p, h2s, slot_p, slot_s, gate_p, gate_s,
                      w_gate[0], w_up[0], w_down[0])

    out_p = _combine(starts_p, x1p, slot_te_p, mod, ctx_row, None, final_g, yp)
    out_s = _combine(starts_s, x1s, slot_te_s, mod, lat_row, dec_seq, final_g, ys)

    y_prompt = out_p.reshape(bp, seq, D_MODEL)
    y_sample = out_s.reshape(bs, dec_seq, D_MODEL)
    state_new = states.reshape(bp, 1, 2, N_RET_HEADS, HEAD_DIM, HEAD_DIM).astype(x_prompt.dtype)
    return (y_prompt, y_sample, state_new)
```

```python
import functools
import math

import jax
import jax.numpy as jnp
import numpy as np
from jax import lax
from jax.experimental import pallas as pl
from jax.experimental.pallas import tpu as pltpu

D_MODEL = 1024
D_FOURIER = 512
N_FOURIER_GROUPS = 4
FOURIER_GROUP_W = 128
D_RET = 512
N_RET_HEADS = 4
HEAD_DIM = 128
CHUNK = 256
GRID_W = 64
N_EXPERTS = 16
EC_CAPACITY_FACTOR = 2
D_EXPERT_FF = 2816
ROPE_BASE = 10000.0
EPS = 1e-6
D_IN_PROJ = D_FOURIER + 5 * D_RET
LOG_GAMMA_FWD = np.log(1.0 - 2.0 ** (-5.0 - np.arange(N_RET_HEADS))).astype(np.float32)
LOG_GAMMA_BWD = np.log(1.0 - 2.0 ** (-5.5 - np.arange(N_RET_HEADS))).astype(np.float32)

LANES = 128
BF16_ROWS = 16
TOKEN_BLOCK = 256
SLOT_WINDOW = 64
PACK = TOKEN_BLOCK // SLOT_WINDOW
GATHER_UNROLL = 4
Y_CHUNKS = 4
FF_TILE = 256
N_FF_TILES = D_EXPERT_FF // FF_TILE
MOD_TILE = 2048
MIXER_ROWS = 1024
POST_ROWS = 1024
POST_PART = 256
COMBINE_ROWS = 512
VMEM_LIMIT = 56 * 1024 * 1024

F32 = jnp.float32
BF16 = jnp.bfloat16


def _dot(a, b):
    return jnp.dot(a, b, preferred_element_type=F32)


def _dot_nt(a, b):
    return lax.dot_general(a, b, (((1,), (1,)), ((), ())), preferred_element_type=F32)


def _silu(x):
    return x * jax.nn.sigmoid(x)


def _mod_kernel(condt_ref, w_ref, b_ref, out_ref, *, n_cond):
    s = _silu(condt_ref[...])
    w = w_ref[...]
    out_ref[...] = jnp.zeros(out_ref.shape, F32)
    for r in range(n_cond):
        out_ref[r] = jnp.sum(w * s[:, r:r + 1], axis=0, keepdims=True) + b_ref[...]


def _modulation(cond_rows, w_mod, b_mod):
    n_cond = cond_rows.shape[0]
    condt = jnp.zeros((D_MODEL, 8), F32).at[:, :n_cond].set(cond_rows.T)
    n_out = w_mod.shape[1]
    return pl.pallas_call(
        functools.partial(_mod_kernel, n_cond=n_cond),
        out_shape=jax.ShapeDtypeStruct((8, 1, n_out), F32),
        grid=(n_out // MOD_TILE,),
        in_specs=[
            pl.BlockSpec((D_MODEL, 8), lambda j: (0, 0)),
            pl.BlockSpec((D_MODEL, MOD_TILE), lambda j: (0, j)),
            pl.BlockSpec((1, MOD_TILE), lambda j: (0, j)),
        ],
        out_specs=pl.BlockSpec((8, 1, MOD_TILE), lambda j: (0, 0, j)),
        compiler_params=pltpu.CompilerParams(dimension_semantics=("arbitrary",)),
        name="mod",
    )(condt, w_mod, b_mod.reshape(1, n_out))


def _rms(x):
    return x * lax.rsqrt(jnp.mean(x * x, axis=-1, keepdims=True) + EPS)


def _groupnorm(o):
    mu = jnp.mean(o, axis=-1, keepdims=True)
    c = o - mu
    return c * lax.rsqrt(jnp.mean(c * c, axis=-1, keepdims=True) + EPS)


def _split_hi_lo(x):
    hi = x.astype(BF16)
    lo = (x - hi.astype(F32)).astype(BF16)
    return hi, lo


def _mixer_kernel(*refs, n, use_rope, has_state_in, emit_state):
    it = iter(refs)
    x_ref, mod_ref, g1_ref, win_ref, wfmix_ref = (next(it) for _ in range(5))
    cw_ref, cn_ref, sn_ref, dmat_ref, qdec_ref, kdec_ref, sdec_ref = (next(it) for _ in range(7))
    cos_ref = sin_ref = s0_ref = st_ref = None
    if use_rope:
        cos_ref, sin_ref = next(it), next(it)
    if has_state_in:
        s0_ref = next(it)
    mix_ref = next(it)
    if emit_state:
        st_ref = next(it)
    p_ref, of_ref, ob_ref = next(it), next(it), next(it)

    n_seq = MIXER_ROWS // n
    chunks_per_seq = n // CHUNK
    mod = mod_ref[0]
    shift1 = mod[:, 0:D_MODEL]
    scale1 = mod[:, D_MODEL:2 * D_MODEL]

    h = (_rms(x_ref[...]) * g1_ref[...] * (1.0 + scale1) + shift1).astype(BF16)
    for j in range(D_IN_PROJ // 512):
        p_ref[:, j * 512:(j + 1) * 512] = _dot(h, win_ref[:, j * 512:(j + 1) * 512].astype(BF16))

    xf = p_ref[:, 0:D_FOURIER].astype(BF16)
    xc, xs = [], []
    cw = cw_ref[...].astype(BF16)
    for g in range(N_FOURIER_GROUPS):
        t = _dot(xf[:, g * FOURIER_GROUP_W:(g + 1) * FOURIER_GROUP_W], cw)
        xc.append(t[:, :FOURIER_GROUP_W].astype(BF16))
        xs.append(t[:, FOURIER_GROUP_W:].astype(BF16))
    xc = jnp.concatenate(xc, axis=1)
    xs = jnp.concatenate(xs, axis=1)
    cn = cn_ref[...].astype(BF16)
    sn = sn_ref[...].astype(BF16)
    for s in range(n_seq):
        rs = slice(s * n, (s + 1) * n)
        fre = (_dot(cn, xc[rs]) - _dot(sn, xs[rs])) * (1.0 / math.sqrt(n * FOURIER_GROUP_W))
        fre = fre.astype(BF16)
        for g in range(N_FOURIER_GROUPS):
            sl = slice(g * FOURIER_GROUP_W, (g + 1) * FOURIER_GROUP_W)
            mix_ref[rs, sl] = _dot(fre[:, sl], wfmix_ref[g].astype(BF16)).astype(BF16)

    for hh in range(N_RET_HEADS):
        base = D_FOURIER + hh * HEAD_DIM
        q = p_ref[:, base:base + HEAD_DIM]
        k = p_ref[:, base + D_RET:base + D_RET + HEAD_DIM]
        v = p_ref[:, base + 2 * D_RET:base + 2 * D_RET + HEAD_DIM]
        if use_rope:
            lane = lax.broadcasted_iota(jnp.int32, (MIXER_ROWS, HEAD_DIM), 1)
            first = (lane % 64) < 32

            def rope(t):
                swapped = jnp.where(first, pltpu.roll(t, HEAD_DIM - 32, 1), pltpu.roll(t, 32, 1))
                return t * cos_ref[...] + swapped * sin_ref[...]

            q, k = rope(q), rope(k)
        k = k * (HEAD_DIM ** -0.5)
        qb, vb = q.astype(BF16), v.astype(BF16)
        kb = k.astype(BF16)

        def initial(s, direction):
            if has_state_in:
                return s0_ref[s, direction, hh]
            return jnp.zeros((HEAD_DIM, HEAD_DIM), F32)

        for s in range(n_seq):
            parts = []
            for ci in range(chunks_per_seq):
                c = s * chunks_per_seq + ci
                rs = slice(c * CHUNK, (c + 1) * CHUNK)
                qc, kc, vc = qb[rs], kb[rs], vb[rs]
                qk = _dot_nt(qc, kc)
                lhs = jnp.concatenate([(qk * dmat_ref[0, hh]).astype(BF16),
                                       (qk * dmat_ref[1, hh]).astype(BF16),
                                       (k[rs] * kdec_ref[0, hh]).T.astype(BF16),
                                       (k[rs] * kdec_ref[1, hh]).T.astype(BF16)], axis=0)
                parts.append((rs, qc, _dot(lhs, vc)))
            sf = initial(s, 0)
            for ci in range(chunks_per_seq):
                rs, qc, r = parts[ci]
                o = r[0:CHUNK]
                if has_state_in or ci > 0:
                    o = o + qdec_ref[0, hh] * _dot(qc, sf.astype(BF16))
                of_ref[rs, :] = o
                sf = sf * sdec_ref[0, hh] + r[2 * CHUNK:2 * CHUNK + HEAD_DIM]
            sb = initial(s, 1)
            for ci in reversed(range(chunks_per_seq)):
                rs, qc, r = parts[ci]
                o = r[CHUNK:2 * CHUNK]
                if has_state_in or ci < chunks_per_seq - 1:
                    o = o + qdec_ref[1, hh] * _dot(qc, sb.astype(BF16))
                ob_ref[rs, :] = o
                sb = sb * sdec_ref[1, hh] + r[2 * CHUNK + HEAD_DIM:]
            if emit_state:
                st_ref[s, 0, hh] = sf
                st_ref[s, 1, hh] = sb

        gf = p_ref[:, base + 3 * D_RET:base + 3 * D_RET + HEAD_DIM]
        gb = p_ref[:, base + 4 * D_RET:base + 4 * D_RET + HEAD_DIM]
        y = _silu(gf) * _groupnorm(of_ref[...]) + _silu(gb) * _groupnorm(ob_ref[...])
        mix_ref[:, base:base + HEAD_DIM] = y.astype(BF16)


def _post_kernel(x_ref, mix_ref, mod_ref, g2_ref, wout_ref, wr_ref, x1_ref, h2_ref, aff_ref):
    mod = mod_ref[0]
    gate1 = mod[:, 2 * D_MODEL:3 * D_MODEL]
    shift2 = mod[:, 3 * D_MODEL:4 * D_MODEL]
    scale2 = mod[:, 4 * D_MODEL:5 * D_MODEL]
    wr_hi, wr_lo = _split_hi_lo(wr_ref[...])
    wr_both = jnp.concatenate([wr_hi, wr_lo], axis=1)
    parts = [slice(p * POST_PART, (p + 1) * POST_PART) for p in range(POST_ROWS // POST_PART)]
    x1 = []
    for rows in parts:
        x1.append(x_ref[rows, :] + gate1 * _dot(mix_ref[rows, :], wout_ref[...]))
        x1_ref[rows, :] = x1[-1]
    for rows, x1_part in zip(parts, x1):
        h2 = _rms(x1_part) * g2_ref[...] * (1.0 + scale2) + shift2
        h2_hi, h2_lo = _split_hi_lo(h2)
        h2_ref[rows, :] = h2_hi
        by_hi = _dot(h2_hi, wr_both)
        logits = by_hi[:, :N_EXPERTS] + (_dot(h2_lo, wr_hi) + by_hi[:, N_EXPERTS:])
        z = jnp.exp(logits - jnp.max(logits, axis=-1, keepdims=True))
        aff = z / jnp.sum(z, axis=-1, keepdims=True)
        lanes = jnp.concatenate([aff, jnp.zeros((POST_PART, LANES - N_EXPERTS), F32)], axis=1)
        aff_ref[:, rows] = lanes.T[0:N_EXPERTS, :]


def _dft_consts(n):
    w = FOURIER_GROUP_W
    jw = np.arange(w)
    angw = 2.0 * np.pi * np.outer(jw, jw) / w
    cw = np.concatenate([np.cos(angw), np.sin(angw)], axis=1)
    jn = np.arange(n)
    angn = 2.0 * np.pi * (np.outer(jn, jn) % n) / n
    return (jnp.asarray(cw, F32), jnp.asarray(np.cos(angn), F32), jnp.asarray(np.sin(angn), F32))


def _retention_consts():
    i = np.arange(CHUNK, dtype=np.float64)
    diff = i[:, None] - i[None, :]
    dmat = np.zeros((2, N_RET_HEADS, CHUNK, CHUNK))
    qdec = np.zeros((2, N_RET_HEADS, CHUNK, HEAD_DIM))
    kdec = np.zeros((2, N_RET_HEADS, CHUNK, HEAD_DIM))
    sdec = np.zeros((2, N_RET_HEADS, HEAD_DIM, HEAD_DIM))
    for hh in range(N_RET_HEADS):
        lf = float(LOG_GAMMA_FWD[hh])
        lb = float(LOG_GAMMA_BWD[hh])
        dmat[0, hh] = np.where(diff >= 0, np.exp(lf * np.maximum(diff, 0.0)), 0.0)
        dmat[1, hh] = np.where(diff <= 0, np.exp(lb * np.maximum(-diff, 0.0)), 0.0)
        qdec[0, hh] = np.exp(lf * (i + 1.0))[:, None]
        qdec[1, hh] = np.exp(lb * (CHUNK - i))[:, None]
        kdec[0, hh] = np.exp(lf * (CHUNK - 1.0 - i))[:, None]
        kdec[1, hh] = np.exp(lb * i)[:, None]
        sdec[0, hh] = math.exp(lf * CHUNK)
        sdec[1, hh] = math.exp(lb * CHUNK)
    return tuple(jnp.asarray(a, F32) for a in (dmat, qdec, kdec, sdec))


def _rope_consts(n):
    rows_n = n // GRID_W
    row = np.repeat(np.arange(rows_n, dtype=np.float64), GRID_W)
    col = np.tile(np.arange(GRID_W, dtype=np.float64), rows_n)
    n_pairs = HEAD_DIM // 4
    freqs = (np.float32(ROPE_BASE) ** (-np.arange(n_pairs, dtype=np.float32) / n_pairs)).astype(np.float64)
    ar = row[:, None] * freqs[None, :]
    ac = col[:, None] * freqs[None, :]
    cos = np.concatenate([np.cos(ar), np.cos(ar), np.cos(ac), np.cos(ac)], axis=1)
    sin = np.concatenate([-np.sin(ar), np.sin(ar), -np.sin(ac), np.sin(ac)], axis=1)
    return jnp.asarray(cos, F32), jnp.asarray(sin, F32)


def _const_spec(shape):
    nd = len(shape)
    return pl.BlockSpec(shape, lambda b, _nd=nd: (0,) * _nd, pipeline_mode=pl.Buffered(1))


def _mod_row_spec(first_row, blocks_per_row):
    if blocks_per_row is None:
        return pl.BlockSpec((1, 1, 6 * D_MODEL), lambda b, *_: (first_row, 0, 0))
    return pl.BlockSpec((1, 1, 6 * D_MODEL), lambda b, *_: (first_row + b // blocks_per_row, 0, 0))


def _mixer(x, mod_rows, mod_first_row, mod_per_batch, state_in, emit_state, use_rope, g1, w_in,
           w_fmix):
    nb, n, _ = x.shape
    assert MIXER_ROWS % n == 0 and (nb * n) % MIXER_ROWS == 0
    n_seq = MIXER_ROWS // n
    has_state_in = state_in is not None
    cw, cn, sn = _dft_consts(n)
    dmat, qdec, kdec, sdec = _retention_consts()
    consts = [cw, cn, sn, dmat, qdec, kdec, sdec]
    if use_rope:
        assert n_seq == 1
        consts += list(_rope_consts(n))
    weights = [g1.reshape(1, D_MODEL), w_in, w_fmix]

    if mod_per_batch:
        assert n % MIXER_ROWS == 0
    state_spec = pl.BlockSpec((n_seq, 2, N_RET_HEADS, HEAD_DIM, HEAD_DIM), lambda b: (b, 0, 0, 0, 0))
    row_spec = pl.BlockSpec((MIXER_ROWS, D_MODEL), lambda b: (b, 0))
    in_specs = [row_spec, _mod_row_spec(mod_first_row, n // MIXER_ROWS if mod_per_batch else None)]
    in_specs += [_const_spec(a.shape) for a in weights + consts]
    args = [x.reshape(nb * n, D_MODEL), mod_rows] + weights + consts
    if has_state_in:
        in_specs.append(state_spec)
        args.append(state_in)

    out_shape = [jax.ShapeDtypeStruct((nb * n, D_MODEL), BF16)]
    out_specs = [row_spec]
    if emit_state:
        out_shape.append(jax.ShapeDtypeStruct((nb, 2, N_RET_HEADS, HEAD_DIM, HEAD_DIM), F32))
        out_specs.append(state_spec)

    return pl.pallas_call(
        functools.partial(_mixer_kernel, n=n, use_rope=use_rope, has_state_in=has_state_in,
                          emit_state=emit_state),
        out_shape=out_shape,
        grid=(nb * n // MIXER_ROWS,),
        in_specs=in_specs,
        out_specs=out_specs,
        scratch_shapes=[pltpu.VMEM((MIXER_ROWS, D_IN_PROJ), F32),
                        pltpu.VMEM((MIXER_ROWS, HEAD_DIM), F32), pltpu.VMEM((MIXER_ROWS, HEAD_DIM), F32)],
        compiler_params=pltpu.CompilerParams(dimension_semantics=("arbitrary",),
                                             vmem_limit_bytes=VMEM_LIMIT),
        name="mixer_rope" if use_rope else "mixer",
    )(*args)


def _post(x, mix, mod_rows, mod_first_row, tokens_per_mod_row, g2, w_out_bf, w_router):
    t = x.shape[0]
    assert tokens_per_mod_row is None or tokens_per_mod_row % POST_ROWS == 0
    blocks_per_row = None if tokens_per_mod_row is None else tokens_per_mod_row // POST_ROWS
    row_spec = pl.BlockSpec((POST_ROWS, D_MODEL), lambda b: (b, 0))
    return pl.pallas_call(
        _post_kernel,
        out_shape=[jax.ShapeDtypeStruct((t, D_MODEL), F32),
                   jax.ShapeDtypeStruct((t, D_MODEL), BF16),
                   jax.ShapeDtypeStruct((N_EXPERTS, t), F32)],
        grid=(t // POST_ROWS,),
        in_specs=[row_spec, row_spec,
                  _mod_row_spec(mod_first_row, blocks_per_row),
                  _const_spec((1, D_MODEL)), _const_spec((D_MODEL, D_MODEL)),
                  _const_spec((D_MODEL, N_EXPERTS))],
        out_specs=[row_spec, row_spec, pl.BlockSpec((N_EXPERTS, POST_ROWS), lambda b: (0, b))],
        compiler_params=pltpu.CompilerParams(dimension_semantics=("arbitrary",),
                                             vmem_limit_bytes=VMEM_LIMIT),
        name="post",
    )(x, mix, mod_rows, g2.reshape(1, D_MODEL), w_out_bf, w_router)


def _route_kernel(*refs, sizes):
    n = len(sizes)
    affs = [ref[...] for ref in refs[:n]]
    u_ref = refs[n]
    outs = [refs[n + 1 + 4 * g:n + 5 + 4 * g] for g in range(n)]

    def count(mask):
        return jnp.sum(mask.astype(jnp.int32), axis=1, keepdims=True)

    def as_float(word):
        return lax.bitcast_convert_type(word, F32)

    def zeros():
        return tuple(jnp.zeros((N_EXPERTS, 1), jnp.int32) for _ in sizes)

    def value_step(i, curs):
        bit = jnp.left_shift(jnp.int32(1), 30 - i)
        return tuple(jnp.where(count(aff >= as_float(cur | bit)) >= cap, cur | bit, cur)
                     for aff, (_, cap), cur in zip(affs, sizes, curs))

    thrs = lax.fori_loop(0, 31, value_step, zeros())
    gts = [aff >= as_float(thr + 1) for aff, thr in zip(affs, thrs)]
    eqs = [(aff >= as_float(thr)) & jnp.logical_not(gt) for aff, thr, gt in zip(affs, thrs, gts)]
    needs = [cap - count(gt) for (_, cap), gt in zip(sizes, gts)]
    toks = [lax.broadcasted_iota(jnp.int32, (N_EXPERTS, t), 1) for t, _ in sizes]
    nbits = [t.bit_length() - 1 for t, _ in sizes]

    def index_step(i, curs):
        new = []
        for eq, need, tok, bits, cur in zip(eqs, needs, toks, nbits, curs):
            shift = bits - 1 - i
            cand = cur | jnp.where(shift >= 0, jnp.left_shift(jnp.int32(1), jnp.maximum(shift, 0)), 0)
            new.append(jnp.where(count(eq & (tok < cand)) < need, cand, cur))
        return tuple(new)

    lasts = lax.fori_loop(0, max(nbits), index_step, zeros())

    for aff, (t, _), gt, eq, tok, last, (slot_ref, slot_te_ref, gate_ref, starts_ref) in zip(
            affs, sizes, gts, eqs, toks, lasts, outs):
        self = jnp.where(gt | (eq & (tok <= last)), 1.0, 0.0).astype(F32)
        carry = jnp.zeros((N_EXPERTS, 1), F32)
        starts_ref[...] = jnp.zeros(starts_ref.shape, jnp.int32)
        for b in range(t // TOKEN_BLOCK):
            sl = slice(b * TOKEN_BLOCK, (b + 1) * TOKEN_BLOCK)
            sbf = self[:, sl]
            pre = _dot(sbf.astype(BF16), u_ref[...]) + carry
            slots = jnp.where(sbf > 0.5, pre.astype(jnp.int32), -1)
            slot_ref[:, b, :] = slots
            rows = jnp.concatenate([slots, jnp.zeros((LANES - N_EXPERTS, TOKEN_BLOCK), jnp.int32)],
                                   axis=0)
            slot_te_ref[sl, :] = rows.T[:, 0:N_EXPERTS]
            gate_ref[:, b, :] = aff[:, sl]
            starts_ref[:, b:b + 1] = carry.astype(jnp.int32)
            carry = carry + jnp.sum(sbf, axis=1, keepdims=True)
        nblk = t // TOKEN_BLOCK
        starts_ref[:, nblk:nblk + 1] = carry.astype(jnp.int32)


def _route(*affs_et):
    sizes = tuple((a.shape[1], EC_CAPACITY_FACTOR * a.shape[1] // N_EXPERTS) for a in affs_et)
    upper = jnp.asarray(np.triu(np.ones((TOKEN_BLOCK, TOKEN_BLOCK)), 1), BF16)
    whole = lambda shape: pl.BlockSpec(shape, lambda i, _n=len(shape): (0,) * _n)
    out_shape, out_specs = [], []
    for t, _ in sizes:
        nblk = t // TOKEN_BLOCK
        assert nblk + 1 <= LANES
        for shape, dtype in (((N_EXPERTS, nblk, TOKEN_BLOCK), jnp.int32), ((t, N_EXPERTS), jnp.int32),
                             ((N_EXPERTS, nblk, TOKEN_BLOCK), F32), ((N_EXPERTS, LANES), jnp.int32)):
            out_shape.append(jax.ShapeDtypeStruct(shape, dtype))
            out_specs.append(whole(shape))
    outs = pl.pallas_call(
        functools.partial(_route_kernel, sizes=sizes),
        out_shape=out_shape,
        grid=(1,),
        in_specs=[whole(a.shape) for a in affs_et] + [whole((TOKEN_BLOCK, TOKEN_BLOCK))],
        out_specs=out_specs,
        compiler_params=pltpu.CompilerParams(dimension_semantics=("arbitrary",)),
        name="route",
    )(*affs_et, upper)
    return [outs[4 * g:4 * g + 4] for g in range(len(sizes))]


def _pack_windows(starts_ref, b, experts, cap):
    first = [jnp.minimum((starts_ref[e, b] // BF16_ROWS) * BF16_ROWS, cap - SLOT_WINDOW)
             for e in experts]
    rows = [jnp.where(starts_ref[e, b + 1] > starts_ref[e, b], starts_ref[e, b + 1] - w, 0)
            for e, w in zip(experts, first)]
    return first, pl.cdiv(functools.reduce(jnp.maximum, rows), SLOT_WINDOW)


def _block_copy(hbm_ref, vmem_ref, sem_ref, b):
    rows = pl.ds(pl.multiple_of(b * TOKEN_BLOCK, TOKEN_BLOCK), TOKEN_BLOCK)
    return pltpu.make_async_copy(hbm_ref.at[rows], vmem_ref.at[rows], sem_ref.at[b])


def _gather_group(g, starts_ref, slot_ref, gate_ref, h2_ref, xs_ref, gs_ref, row0, t, cap, arrive):
    sub = lax.broadcasted_iota(jnp.int32, (SLOT_WINDOW, TOKEN_BLOCK), 0)
    experts = [g * PACK + j for j in range(PACK)]
    assert (t // TOKEN_BLOCK) % GATHER_UNROLL == 0

    def window(b, first, i):
        hb = pl.ds(pl.multiple_of(b * TOKEN_BLOCK, TOKEN_BLOCK), TOKEN_BLOCK)
        hits, dst = [], []
        for j in range(PACK):
            lo = first[j] + i * SLOT_WINDOW
            w = jnp.minimum(lo, cap - SLOT_WINDOW)
            srow = slot_ref[j, pl.ds(b, 1), :]
            hits.append((srow == w + sub) & (srow >= lo))
            dst.append(pl.ds(pl.multiple_of(row0 + w, BF16_ROWS), SLOT_WINDOW))
        onehot = jnp.concatenate([jnp.where(h, 1.0, 0.0) for h in hits], axis=0).astype(BF16)
        got = _dot(onehot, h2_ref[hb, :])
        for j in range(PACK):
            piece = got[j * SLOT_WINDOW:(j + 1) * SLOT_WINDOW].astype(BF16)
            xs_ref[j, dst[j], :] = xs_ref[j, dst[j], :] + piece
            grow = gate_ref[j, pl.ds(b, 1), :]
            gs_ref[j, dst[j], :] += jnp.sum(jnp.where(hits[j], grow, 0.0), axis=1, keepdims=True)

    def blocks(q, carry):
        pending = []
        for u in range(GATHER_UNROLL):
            arrive(q * GATHER_UNROLL + u)
        for u in range(GATHER_UNROLL):
            b = q * GATHER_UNROLL + u
            first, n_windows = _pack_windows(starts_ref, b, experts, cap)
            window(b, first, 0)
            pending.append((b, first, n_windows))
        for b, first, n_windows in pending:
            def more(i, carry, b=b, first=first):
                window(b, first, i)
                return carry

            lax.fori_loop(1, n_windows, more, 0)
        return carry

    lax.fori_loop(0, t // TOKEN_BLOCK // GATHER_UNROLL, blocks, 0)


def _experts_kernel(sp_ref, ss_ref, h2p_hbm, h2s_hbm, slotp_ref, slots_ref, gatep_ref, gates_ref,
                    wg_ref, wu_ref, wd_ref, yp_ref, ys_ref, xs_ref, gs_ref, acc_ref,
                    h2p_ref, h2s_ref, semp_ref, sems_ref, *, tp, ts, capp, caps):
    g = pl.program_id(0)
    step = pl.program_id(1)
    j = step // N_FF_TILES
    f = step % N_FF_TILES
    loading = [(h2p_hbm, h2p_ref, semp_ref, tp // TOKEN_BLOCK),
               (h2s_hbm, h2s_ref, sems_ref, ts // TOKEN_BLOCK)]

    @pl.when((step == 0) & (g == 0))
    def _():
        for hbm_ref, vmem_ref, sem_ref, n_blocks in loading:
            for b in range(n_blocks):
                _block_copy(hbm_ref, vmem_ref, sem_ref, b).start()

    def arrive(hbm_ref, vmem_ref, sem_ref, _):
        def wait(b):
            @pl.when(g == 0)
            def _():
                _block_copy(hbm_ref, vmem_ref, sem_ref, b).wait()

        return wait

    @pl.when(step == 0)
    def _():
        xs_ref[...] = jnp.zeros(xs_ref.shape, BF16)
        gs_ref[...] = jnp.zeros(gs_ref.shape, F32)
        _gather_group(g, sp_ref, slotp_ref, gatep_ref, h2p_ref, xs_ref, gs_ref, 0, tp, capp,
                      arrive(*loading[0]))
        _gather_group(g, ss_ref, slots_ref, gates_ref, h2s_ref, xs_ref, gs_ref, capp, ts, caps,
                      arrive(*loading[1]))

    @pl.when(f == 0)
    def _():
        acc_ref[...] = jnp.zeros(acc_ref.shape, F32)

    x = xs_ref[j]
    a = _dot(x, wg_ref[0].astype(BF16))
    u = _dot(x, wu_ref[0].astype(BF16))
    acc_ref[...] += _dot((_silu(a) * u).astype(BF16), wd_ref[0].astype(BF16))

    @pl.when(f == N_FF_TILES - 1)
    def _():
        yp_ref[0] = (acc_ref[0:capp, :] * gs_ref[j, 0:capp, :]).astype(BF16)
        ys_ref[0] = (acc_ref[capp:capp + caps, :] * gs_ref[j, capp:capp + caps, :]).astype(BF16)


def _experts(starts_p, starts_s, h2p, h2s, slot_p, slot_s, gate_p, gate_s, w_gate, w_up, w_down):
    tp, ts = h2p.shape[0], h2s.shape[0]
    capp = EC_CAPACITY_FACTOR * tp // N_EXPERTS
    caps = EC_CAPACITY_FACTOR * ts // N_EXPERTS
    rows = capp + caps
    nbp, nbs = tp // TOKEN_BLOCK, ts // TOKEN_BLOCK
    expert = lambda g, s: g * PACK + s // N_FF_TILES
    grid_spec = pltpu.PrefetchScalarGridSpec(
        num_scalar_prefetch=2,
        grid=(N_EXPERTS // PACK, PACK * N_FF_TILES),
        in_specs=[
            pl.BlockSpec(memory_space=pl.ANY),
            pl.BlockSpec(memory_space=pl.ANY),
            pl.BlockSpec((PACK, nbp, TOKEN_BLOCK), lambda g, s, *_: (g, 0, 0)),
            pl.BlockSpec((PACK, nbs, TOKEN_BLOCK), lambda g, s, *_: (g, 0, 0)),
            pl.BlockSpec((PACK, nbp, TOKEN_BLOCK), lambda g, s, *_: (g, 0, 0)),
            pl.BlockSpec((PACK, nbs, TOKEN_BLOCK), lambda g, s, *_: (g, 0, 0)),
            pl.BlockSpec((1, D_MODEL, FF_TILE), lambda g, s, *_: (expert(g, s), 0, s % N_FF_TILES)),
            pl.BlockSpec((1, D_MODEL, FF_TILE), lambda g, s, *_: (expert(g, s), 0, s % N_FF_TILES)),
            pl.BlockSpec((1, FF_TILE, D_MODEL), lambda g, s, *_: (expert(g, s), s % N_FF_TILES, 0)),
        ],
        out_specs=[
            pl.BlockSpec((1, capp, D_MODEL), lambda g, s, *_: (expert(g, s), 0, 0)),
            pl.BlockSpec((1, caps, D_MODEL), lambda g, s, *_: (expert(g, s), 0, 0)),
        ],
        scratch_shapes=[pltpu.VMEM((PACK, rows, D_MODEL), BF16), pltpu.VMEM((PACK, rows, 1), F32),
                        pltpu.VMEM((rows, D_MODEL), F32),
                        pltpu.VMEM((tp, D_MODEL), BF16), pltpu.VMEM((ts, D_MODEL), BF16),
                        pltpu.SemaphoreType.DMA((nbp,)), pltpu.SemaphoreType.DMA((nbs,))],
    )
    return pl.pallas_call(
        functools.partial(_experts_kernel, tp=tp, ts=ts, capp=capp, caps=caps),
        out_shape=[jax.ShapeDtypeStruct((N_EXPERTS, capp, D_MODEL), BF16),
                   jax.ShapeDtypeStruct((N_EXPERTS, caps, D_MODEL), BF16)],
        grid_spec=grid_spec,
        compiler_params=pltpu.CompilerParams(dimension_semantics=("arbitrary", "arbitrary"),
                                             vmem_limit_bytes=VMEM_LIMIT),
        name="experts",
    )(starts_p, starts_s, h2p, h2s, slot_p, slot_s, gate_p, gate_s, w_gate, w_up, w_down)


def _combine_kernel(st_ref, x1_ref, slot_ref, mod_ref, fg_ref, y_hbm, out_ref, acc_ref, y_ref,
                    sem_ref, arrived_ref, *, cap):
    step = pl.program_id(0)
    n_chunks = Y_CHUNKS
    chunk_rows = cap // Y_CHUNKS
    n_parts = COMBINE_ROWS // TOKEN_BLOCK
    last_block = step * n_parts + n_parts - 1

    def chunk_copy(c):
        rows = pl.ds(c * chunk_rows, chunk_rows)
        return pltpu.make_async_copy(y_hbm.at[:, rows, :], y_ref.at[:, rows, :], sem_ref.at[c])

    @pl.when(step == 0)
    def _():
        arrived_ref[0] = 0
        for c in range(n_chunks):
            chunk_copy(c).start()

    top = functools.reduce(jnp.maximum, [st_ref[e, last_block + 1] for e in range(N_EXPERTS)])
    want = pl.cdiv(jnp.minimum(top + SLOT_WINDOW, cap), chunk_rows)
    have = arrived_ref[0]
    for c in range(n_chunks):
        @pl.when((c >= have) & (c < want))
        def _(c=c):
            chunk_copy(c).wait()
    arrived_ref[0] = jnp.maximum(have, want)

    lane = lax.broadcasted_iota(jnp.int32, (TOKEN_BLOCK, PACK * SLOT_WINDOW), 1)
    gate2 = mod_ref[0][:, 5 * D_MODEL:6 * D_MODEL]

    def scatter(slot, experts, first, i):
        target = None
        windows = []
        for j, e in enumerate(experts):
            lo = first[j] + i * SLOT_WINDOW
            w = jnp.minimum(lo, cap - SLOT_WINDOW)
            sc = slot[:, e:e + 1]
            col = jnp.where(sc >= lo, sc - w + j * SLOT_WINDOW, -1)
            target = col if target is None else jnp.where(lane < j * SLOT_WINDOW, target, col)
            windows.append(y_ref[e, pl.ds(pl.multiple_of(w, BF16_ROWS), SLOT_WINDOW), :])
        onehot = jnp.where(target == lane, 1.0, 0.0).astype(BF16)
        return _dot(onehot, jnp.concatenate(windows, axis=0))

    def finish(rows):
        x2 = x1_ref[rows, :] + gate2 * acc_ref[rows, :]
        out_ref[rows, :] = _rms(x2) * fg_ref[...]

    parts = []
    for p in range(n_parts):
        rows = slice(p * TOKEN_BLOCK, (p + 1) * TOKEN_BLOCK)
        slot = slot_ref[rows, :]
        groups = []
        total = None
        for g in range(N_EXPERTS // PACK):
            experts = list(range(g * PACK, (g + 1) * PACK))
            first, n_windows = _pack_windows(st_ref, step * n_parts + p, experts, cap)
            groups.append((experts, first, n_windows))
            part = scatter(slot, experts, first, 0)
            total = part if total is None else total + part
        acc_ref[rows, :] = total
        parts.append((rows, slot, groups))
    for rows, _, _ in parts:
        finish(rows)

    for rows, slot, groups in parts:
        most = functools.reduce(jnp.maximum, [n_windows for _, _, n_windows in groups])

        @pl.when(most > 1)
        def _(rows=rows, slot=slot, groups=groups):
            for experts, first, n_windows in groups:
                def more(i, carry, experts=experts, first=first):
                    acc_ref[rows, :] += scatter(slot, experts, first, i)
                    return carry

                lax.fori_loop(1, n_windows, more, 0)
            finish(rows)


def _combine(starts, x1, slot_te, mod_rows, mod_first_row, tokens_per_mod_row, final_g, y):
    t = x1.shape[0]
    cap = y.shape[1]
    assert tokens_per_mod_row is None or tokens_per_mod_row % COMBINE_ROWS == 0
    steps_per_row = None if tokens_per_mod_row is None else tokens_per_mod_row // COMBINE_ROWS
    grid_spec = pltpu.PrefetchScalarGridSpec(
        num_scalar_prefetch=1,
        grid=(t // COMBINE_ROWS,),
        in_specs=[
            pl.BlockSpec((COMBINE_ROWS, D_MODEL), lambda b, *_: (b, 0)),
            pl.BlockSpec((COMBINE_ROWS, N_EXPERTS), lambda b, *_: (b, 0)),
            _mod_row_spec(mod_first_row, steps_per_row),
            pl.BlockSpec((1, D_MODEL), lambda b, *_: (0, 0)),
            pl.BlockSpec(memory_space=pl.ANY),
        ],
        out_specs=pl.BlockSpec((COMBINE_ROWS, D_MODEL), lambda b, *_: (b, 0)),
        scratch_shapes=[pltpu.VMEM((COMBINE_ROWS, D_MODEL), F32), pltpu.VMEM(y.shape, BF16),
                        pltpu.SemaphoreType.DMA((Y_CHUNKS,)), pltpu.SMEM((1,), jnp.int32)],
    )
    return pl.pallas_call(
        functools.partial(_combine_kernel, cap=cap),
        out_shape=jax.ShapeDtypeStruct((t, D_MODEL), F32),
        grid_spec=grid_spec,
        compiler_params=pltpu.CompilerParams(dimension_semantics=("arbitrary",),
                                             vmem_limit_bytes=VMEM_LIMIT),
        name="combine",
    )(starts, x1, slot_te, mod_rows, final_g.reshape(1, D_MODEL), y)


def kernel(x_prompt, x_sample, state_ret, c, c_ctx, norm1_g, norm2_g, final_g, w_mod, b_mod, w_in,
           w_fmix, w_out, w_router, w_gate, w_up, w_down):
    bp, seq, _ = x_prompt.shape
    bs, dec_seq, _ = x_sample.shape
    assert w_mod.shape[0] == 1, "single-layer trunk"
    tp, ts = bp * seq, bs * dec_seq

    cond = jnp.concatenate([c_ctx[None, :], c], axis=0)
    mod = _modulation(cond, w_mod[0], b_mod[0])
    ctx_row, lat_row = 0, 1

    w_out_bf = w_out[0].astype(BF16)
    mix_p, states = _mixer(x_prompt, mod, ctx_row, False, None, True, False, norm1_g[0], w_in[0],
                           w_fmix[0])
    (mix_s,) = _mixer(x_sample, mod, lat_row, True, state_ret[:, 0], False, True, norm1_g[0],
                      w_in[0], w_fmix[0])
    x1p, h2p, affp = _post(x_prompt.reshape(tp, D_MODEL), mix_p, mod, ctx_row, None,
                           norm2_g[0], w_out_bf, w_router[0])
    x1s, h2s, affs = _post(x_sample.reshape(ts, D_MODEL), mix_s, mod, lat_row, dec_seq,
                           norm2_g[0], w_out_bf, w_router[0])

    (slot_p, slot_te_p, gate_p, starts_p), (slot_s, slot_te_s, gate_s, starts_s) = _route(affp, affs)

    yp, ys = _experts(starts_p, starts_s, h2p, h2s, slot_p, slot_s, gate_p, gate_s,
                      w_gate[0], w_up[0], w_down[0])

    out_p = _combine(starts_p, x1p, slot_te_p, mod, ctx_row, None, final_g, yp)
    out_s = _combine(starts_s, x1s, slot_te_s, mod, lat_row, dec_seq, final_g, ys)

    y_prompt = out_p.reshape(bp, seq, D_MODEL)
    y_sample = out_s.reshape(bs, dec_seq, D_MODEL)
    state_new = states.reshape(bp, 1, 2, N_RET_HEADS, HEAD_DIM, HEAD_DIM).astype(x_prompt.dtype)
    return (y_prompt, y_sample, state_new)
```

```python
import functools
import math

import jax
import jax.numpy as jnp
import numpy as np
from jax import lax
from jax.experimental import pallas as pl
from jax.experimental.pallas import tpu as pltpu

D_MODEL = 1024
D_FOURIER = 512
N_FOURIER_GROUPS = 4
FOURIER_GROUP_W = 128
D_RET = 512
N_RET_HEADS = 4
HEAD_DIM = 128
CHUNK = 256
GRID_W = 64
N_EXPERTS = 16
EC_CAPACITY_FACTOR = 2
D_EXPERT_FF = 2816
ROPE_BASE = 10000.0
EPS = 1e-6
D_IN_PROJ = D_FOURIER + 5 * D_RET
LOG_GAMMA_FWD = np.log(1.0 - 2.0 ** (-5.0 - np.arange(N_RET_HEADS))).astype(np.float32)
LOG_GAMMA_BWD = np.log(1.0 - 2.0 ** (-5.5 - np.arange(N_RET_HEADS))).astype(np.float32)

LANES = 128
BF16_ROWS = 16
TOKEN_BLOCK = 256
SLOT_WINDOW = 64
PACK = TOKEN_BLOCK // SLOT_WINDOW
GATHER_UNROLL = 4
Y_CHUNKS = 4
FF_TILE = 256
N_FF_TILES = D_EXPERT_FF // FF_TILE
MOD_TILE = 1024
MIXER_ROWS = 1024
POST_ROWS = 1024
POST_PART = 256
COMBINE_ROWS = 512
VMEM_LIMIT = 56 * 1024 * 1024

F32 = jnp.float32
BF16 = jnp.bfloat16


def _dot(a, b):
    return jnp.dot(a, b, preferred_element_type=F32)


def _dot_nt(a, b):
    return lax.dot_general(a, b, (((1,), (1,)), ((), ())), preferred_element_type=F32)


def _silu(x):
    return x * jax.nn.sigmoid(x)


def _mod_kernel(condt_ref, w_ref, b_ref, out_ref, *, n_cond):
    s = _silu(condt_ref[...])
    w = w_ref[...]
    out_ref[...] = jnp.zeros(out_ref.shape, F32)
    for r in range(n_cond):
        out_ref[r] = jnp.sum(w * s[:, r:r + 1], axis=0, keepdims=True) + b_ref[...]


def _modulation(cond_rows, w_mod, b_mod):
    n_cond = cond_rows.shape[0]
    condt = jnp.zeros((D_MODEL, 8), F32).at[:, :n_cond].set(cond_rows.T)
    n_out = w_mod.shape[1]
    return pl.pallas_call(
        functools.partial(_mod_kernel, n_cond=n_cond),
        out_shape=jax.ShapeDtypeStruct((8, 1, n_out), F32),
        grid=(n_out // MOD_TILE,),
        in_specs=[
            pl.BlockSpec((D_MODEL, 8), lambda j: (0, 0)),
            pl.BlockSpec((D_MODEL, MOD_TILE), lambda j: (0, j)),
            pl.BlockSpec((1, MOD_TILE), lambda j: (0, j)),
        ],
        out_specs=pl.BlockSpec((8, 1, MOD_TILE), lambda j: (0, 0, j)),
        compiler_params=pltpu.CompilerParams(dimension_semantics=("arbitrary",)),
        name="mod",
    )(condt, w_mod, b_mod.reshape(1, n_out))


def _rms(x):
    return x * lax.rsqrt(jnp.mean(x * x, axis=-1, keepdims=True) + EPS)


def _groupnorm(o):
    mu = jnp.mean(o, axis=-1, keepdims=True)
    c = o - mu
    return c * lax.rsqrt(jnp.mean(c * c, axis=-1, keepdims=True) + EPS)


def _split_hi_lo(x):
    hi = x.astype(BF16)
    lo = (x - hi.astype(F32)).astype(BF16)
    return hi, lo


def _mixer_kernel(*refs, n, use_rope, has_state_in, emit_state):
    it = iter(refs)
    x_ref, mod_ref, g1_ref, win_ref, wfmix_ref = (next(it) for _ in range(5))
    cw_ref, cn_ref, sn_ref, dmat_ref, qdec_ref, kdec_ref, sdec_ref = (next(it) for _ in range(7))
    cos_ref = sin_ref = s0_ref = st_ref = None
    if use_rope:
        cos_ref, sin_ref = next(it), next(it)
    if has_state_in:
        s0_ref = next(it)
    mix_ref = next(it)
    if emit_state:
        st_ref = next(it)
    p_ref, of_ref, ob_ref = next(it), next(it), next(it)

    n_seq = MIXER_ROWS // n
    chunks_per_seq = n // CHUNK
    mod = mod_ref[0]
    shift1 = mod[:, 0:D_MODEL]
    scale1 = mod[:, D_MODEL:2 * D_MODEL]

    h = (_rms(x_ref[...]) * g1_ref[...] * (1.0 + scale1) + shift1).astype(BF16)
    for j in range(D_IN_PROJ // 512):
        p_ref[:, j * 512:(j + 1) * 512] = _dot(h, win_ref[:, j * 512:(j + 1) * 512].astype(BF16))

    xf = p_ref[:, 0:D_FOURIER].astype(BF16)
    xc, xs = [], []
    cw = cw_ref[...].astype(BF16)
    for g in range(N_FOURIER_GROUPS):
        t = _dot(xf[:, g * FOURIER_GROUP_W:(g + 1) * FOURIER_GROUP_W], cw)
        xc.append(t[:, :FOURIER_GROUP_W].astype(BF16))
        xs.append(t[:, FOURIER_GROUP_W:].astype(BF16))
    xc = jnp.concatenate(xc, axis=1)
    xs = jnp.concatenate(xs, axis=1)
    cn = cn_ref[...].astype(BF16)
    sn = sn_ref[...].astype(BF16)
    for s in range(n_seq):
        rs = slice(s * n, (s + 1) * n)
        fre = (_dot(cn, xc[rs]) - _dot(sn, xs[rs])) * (1.0 / math.sqrt(n * FOURIER_GROUP_W))
        fre = fre.astype(BF16)
        for g in range(N_FOURIER_GROUPS):
            sl = slice(g * FOURIER_GROUP_W, (g + 1) * FOURIER_GROUP_W)
            mix_ref[rs, sl] = _dot(fre[:, sl], wfmix_ref[g].astype(BF16)).astype(BF16)

    for hh in range(N_RET_HEADS):
        base = D_FOURIER + hh * HEAD_DIM
        q = p_ref[:, base:base + HEAD_DIM]
        k = p_ref[:, base + D_RET:base + D_RET + HEAD_DIM]
        v = p_ref[:, base + 2 * D_RET:base + 2 * D_RET + HEAD_DIM]
        if use_rope:
            lane = lax.broadcasted_iota(jnp.int32, (MIXER_ROWS, HEAD_DIM), 1)
            first = (lane % 64) < 32

            def rope(t):
                swapped = jnp.where(first, pltpu.roll(t, HEAD_DIM - 32, 1), pltpu.roll(t, 32, 1))
                return t * cos_ref[...] + swapped * sin_ref[...]

            q, k = rope(q), rope(k)
        k = k * (HEAD_DIM ** -0.5)
        qb, vb = q.astype(BF16), v.astype(BF16)
        kb = k.astype(BF16)

        def initial(s, direction):
            if has_state_in:
                return s0_ref[s, direction, hh]
            return jnp.zeros((HEAD_DIM, HEAD_DIM), F32)

        for s in range(n_seq):
            parts = []
            for ci in range(chunks_per_seq):
                c = s * chunks_per_seq + ci
                rs = slice(c * CHUNK, (c + 1) * CHUNK)
                qc, kc, vc = qb[rs], kb[rs], vb[rs]
                qk = _dot_nt(qc, kc)
                lhs = jnp.concatenate([(qk * dmat_ref[0, hh]).astype(BF16),
                                       (qk * dmat_ref[1, hh]).astype(BF16),
                                       (k[rs] * kdec_ref[0, hh]).T.astype(BF16),
                                       (k[rs] * kdec_ref[1, hh]).T.astype(BF16)], axis=0)
                parts.append((rs, qc, _dot(lhs, vc)))
            sf = initial(s, 0)
            for ci in range(chunks_per_seq):
                rs, qc, r = parts[ci]
                o = r[0:CHUNK]
                if has_state_in or ci > 0:
                    o = o + qdec_ref[0, hh] * _dot(qc, sf.astype(BF16))
                of_ref[rs, :] = o
                sf = sf * sdec_ref[0, hh] + r[2 * CHUNK:2 * CHUNK + HEAD_DIM]
            sb = initial(s, 1)
            for ci in reversed(range(chunks_per_seq)):
                rs, qc, r = parts[ci]
                o = r[CHUNK:2 * CHUNK]
                if has_state_in or ci < chunks_per_seq - 1:
                    o = o + qdec_ref[1, hh] * _dot(qc, sb.astype(BF16))
                ob_ref[rs, :] = o
                sb = sb * sdec_ref[1, hh] + r[2 * CHUNK + HEAD_DIM:]
            if emit_state:
                st_ref[s, 0, hh] = sf
                st_ref[s, 1, hh] = sb

        gf = p_ref[:, base + 3 * D_RET:base + 3 * D_RET + HEAD_DIM]
        gb = p_ref[:, base + 4 * D_RET:base + 4 * D_RET + HEAD_DIM]
        y = _silu(gf) * _groupnorm(of_ref[...]) + _silu(gb) * _groupnorm(ob_ref[...])
        mix_ref[:, base:base + HEAD_DIM] = y.astype(BF16)


def _post_kernel(x_ref, mix_ref, mod_ref, g2_ref, wout_ref, wr_ref, x1_ref, h2_ref, aff_ref):
    mod = mod_ref[0]
    gate1 = mod[:, 2 * D_MODEL:3 * D_MODEL]
    shift2 = mod[:, 3 * D_MODEL:4 * D_MODEL]
    scale2 = mod[:, 4 * D_MODEL:5 * D_MODEL]
    wr_hi, wr_lo = _split_hi_lo(wr_ref[...])
    wr_both = jnp.concatenate([wr_hi, wr_lo], axis=1)
    parts = [slice(p * POST_PART, (p + 1) * POST_PART) for p in range(POST_ROWS // POST_PART)]
    x1 = []
    for rows in parts:
        x1.append(x_ref[rows, :] + gate1 * _dot(mix_ref[rows, :], wout_ref[...]))
        x1_ref[rows, :] = x1[-1]
    for rows, x1_part in zip(parts, x1):
        h2 = _rms(x1_part) * g2_ref[...] * (1.0 + scale2) + shift2
        h2_hi, h2_lo = _split_hi_lo(h2)
        h2_ref[rows, :] = h2_hi
        by_hi = _dot(h2_hi, wr_both)
        logits = by_hi[:, :N_EXPERTS] + (_dot(h2_lo, wr_hi) + by_hi[:, N_EXPERTS:])
        z = jnp.exp(logits - jnp.max(logits, axis=-1, keepdims=True))
        aff = z / jnp.sum(z, axis=-1, keepdims=True)
        lanes = jnp.concatenate([aff, jnp.zeros((POST_PART, LANES - N_EXPERTS), F32)], axis=1)
        aff_ref[:, rows] = lanes.T[0:N_EXPERTS, :]


def _dft_consts(n):
    w = FOURIER_GROUP_W
    jw = np.arange(w)
    angw = 2.0 * np.pi * np.outer(jw, jw) / w
    cw = np.concatenate([np.cos(angw), np.sin(angw)], axis=1)
    jn = np.arange(n)
    angn = 2.0 * np.pi * (np.outer(jn, jn) % n) / n
    return (jnp.asarray(cw, F32), jnp.asarray(np.cos(angn), F32), jnp.asarray(np.sin(angn), F32))


def _retention_consts():
    i = np.arange(CHUNK, dtype=np.float64)
    diff = i[:, None] - i[None, :]
    dmat = np.zeros((2, N_RET_HEADS, CHUNK, CHUNK))
    qdec = np.zeros((2, N_RET_HEADS, CHUNK, HEAD_DIM))
    kdec = np.zeros((2, N_RET_HEADS, CHUNK, HEAD_DIM))
    sdec = np.zeros((2, N_RET_HEADS, HEAD_DIM, HEAD_DIM))
    for hh in range(N_RET_HEADS):
        lf = float(LOG_GAMMA_FWD[hh])
        lb = float(LOG_GAMMA_BWD[hh])
        dmat[0, hh] = np.where(diff >= 0, np.exp(lf * np.maximum(diff, 0.0)), 0.0)
        dmat[1, hh] = np.where(diff <= 0, np.exp(lb * np.maximum(-diff, 0.0)), 0.0)
        qdec[0, hh] = np.exp(lf * (i + 1.0))[:, None]
        qdec[1, hh] = np.exp(lb * (CHUNK - i))[:, None]
        kdec[0, hh] = np.exp(lf * (CHUNK - 1.0 - i))[:, None]
        kdec[1, hh] = np.exp(lb * i)[:, None]
        sdec[0, hh] = math.exp(lf * CHUNK)
        sdec[1, hh] = math.exp(lb * CHUNK)
    return tuple(jnp.asarray(a, F32) for a in (dmat, qdec, kdec, sdec))


def _rope_consts(n):
    rows_n = n // GRID_W
    row = np.repeat(np.arange(rows_n, dtype=np.float64), GRID_W)
    col = np.tile(np.arange(GRID_W, dtype=np.float64), rows_n)
    n_pairs = HEAD_DIM // 4
    freqs = (np.float32(ROPE_BASE) ** (-np.arange(n_pairs, dtype=np.float32) / n_pairs)).astype(np.float64)
    ar = row[:, None] * freqs[None, :]
    ac = col[:, None] * freqs[None, :]
    cos = np.concatenate([np.cos(ar), np.cos(ar), np.cos(ac), np.cos(ac)], axis=1)
    sin = np.concatenate([-np.sin(ar), np.sin(ar), -np.sin(ac), np.sin(ac)], axis=1)
    return jnp.asarray(cos, F32), jnp.asarray(sin, F32)


def _const_spec(shape):
    nd = len(shape)
    return pl.BlockSpec(shape, lambda b, _nd=nd: (0,) * _nd, pipeline_mode=pl.Buffered(1))


def _mod_row_spec(first_row, blocks_per_row):
    if blocks_per_row is None:
        return pl.BlockSpec((1, 1, 6 * D_MODEL), lambda b, *_: (first_row, 0, 0))
    return pl.BlockSpec((1, 1, 6 * D_MODEL), lambda b, *_: (first_row + b // blocks_per_row, 0, 0))


def _mixer(x, mod_rows, mod_first_row, mod_per_batch, state_in, emit_state, use_rope, g1, w_in,
           w_fmix):
    nb, n, _ = x.shape
    assert MIXER_ROWS % n == 0 and (nb * n) % MIXER_ROWS == 0
    n_seq = MIXER_ROWS // n
    has_state_in = state_in is not None
    cw, cn, sn = _dft_consts(n)
    dmat, qdec, kdec, sdec = _retention_consts()
    consts = [cw, cn, sn, dmat, qdec, kdec, sdec]
    if use_rope:
        assert n_seq == 1
        consts += list(_rope_consts(n))
    weights = [g1.reshape(1, D_MODEL), w_in, w_fmix]

    if mod_per_batch:
        assert n % MIXER_ROWS == 0
    state_spec = pl.BlockSpec((n_seq, 2, N_RET_HEADS, HEAD_DIM, HEAD_DIM), lambda b: (b, 0, 0, 0, 0))
    row_spec = pl.BlockSpec((MIXER_ROWS, D_MODEL), lambda b: (b, 0))
    in_specs = [row_spec, _mod_row_spec(mod_first_row, n // MIXER_ROWS if mod_per_batch else None)]
    in_specs += [_const_spec(a.shape) for a in weights + consts]
    args = [x.reshape(nb * n, D_MODEL), mod_rows] + weights + consts
    if has_state_in:
        in_specs.append(state_spec)
        args.append(state_in)

    out_shape = [jax.ShapeDtypeStruct((nb * n, D_MODEL), BF16)]
    out_specs = [row_spec]
    if emit_state:
        out_shape.append(jax.ShapeDtypeStruct((nb, 2, N_RET_HEADS, HEAD_DIM, HEAD_DIM), F32))
        out_specs.append(state_spec)

    return pl.pallas_call(
        functools.partial(_mixer_kernel, n=n, use_rope=use_rope, has_state_in=has_state_in,
                          emit_state=emit_state),
        out_shape=out_shape,
        grid=(nb * n // MIXER_ROWS,),
        in_specs=in_specs,
        out_specs=out_specs,
        scratch_shapes=[pltpu.VMEM((MIXER_ROWS, D_IN_PROJ), F32),
                        pltpu.VMEM((MIXER_ROWS, HEAD_DIM), F32), pltpu.VMEM((MIXER_ROWS, HEAD_DIM), F32)],
        compiler_params=pltpu.CompilerParams(dimension_semantics=("arbitrary",),
                                             vmem_limit_bytes=VMEM_LIMIT),
        name="mixer_rope" if use_rope else "mixer",
    )(*args)


def _post(x, mix, mod_rows, mod_first_row, tokens_per_mod_row, g2, w_out_bf, w_router):
    t = x.shape[0]
    assert tokens_per_mod_row is None or tokens_per_mod_row % POST_ROWS == 0
    blocks_per_row = None if tokens_per_mod_row is None else tokens_per_mod_row // POST_ROWS
    row_spec = pl.BlockSpec((POST_ROWS, D_MODEL), lambda b: (b, 0))
    return pl.pallas_call(
        _post_kernel,
        out_shape=[jax.ShapeDtypeStruct((t, D_MODEL), F32),
                   jax.ShapeDtypeStruct((t, D_MODEL), BF16),
                   jax.ShapeDtypeStruct((N_EXPERTS, t), F32)],
        grid=(t // POST_ROWS,),
        in_specs=[row_spec, row_spec,
                  _mod_row_spec(mod_first_row, blocks_per_row),
                  _const_spec((1, D_MODEL)), _const_spec((D_MODEL, D_MODEL)),
                  _const_spec((D_MODEL, N_EXPERTS))],
        out_specs=[row_spec, row_spec, pl.BlockSpec((N_EXPERTS, POST_ROWS), lambda b: (0, b))],
        compiler_params=pltpu.CompilerParams(dimension_semantics=("arbitrary",),
                                             vmem_limit_bytes=VMEM_LIMIT),
        name="post",
    )(x, mix, mod_rows, g2.reshape(1, D_MODEL), w_out_bf, w_router)


def _route_kernel(*refs, sizes):
    n = len(sizes)
    affs = [ref[...] for ref in refs[:n]]
    u_ref = refs[n]
    outs = [refs[n + 1 + 4 * g:n + 5 + 4 * g] for g in range(n)]

    def count(mask):
        return jnp.sum(mask.astype(jnp.int32), axis=1, keepdims=True)

    def as_float(word):
        return lax.bitcast_convert_type(word, F32)

    def zeros():
        return tuple(jnp.zeros((N_EXPERTS, 1), jnp.int32) for _ in sizes)

    def value_step(i, curs):
        bit = jnp.left_shift(jnp.int32(1), 30 - i)
        return tuple(jnp.where(count(aff >= as_float(cur | bit)) >= cap, cur | bit, cur)
                     for aff, (_, cap), cur in zip(affs, sizes, curs))

    thrs = lax.fori_loop(0, 31, value_step, zeros())
    gts = [aff >= as_float(thr + 1) for aff, thr in zip(affs, thrs)]
    eqs = [(aff >= as_float(thr)) & jnp.logical_not(gt) for aff, thr, gt in zip(affs, thrs, gts)]
    needs = [cap - count(gt) for (_, cap), gt in zip(sizes, gts)]
    toks = [lax.broadcasted_iota(jnp.int32, (N_EXPERTS, t), 1) for t, _ in sizes]
    nbits = [t.bit_length() - 1 for t, _ in sizes]

    def index_step(i, curs):
        new = []
        for eq, need, tok, bits, cur in zip(eqs, needs, toks, nbits, curs):
            shift = bits - 1 - i
            cand = cur | jnp.where(shift >= 0, jnp.left_shift(jnp.int32(1), jnp.maximum(shift, 0)), 0)
            new.append(jnp.where(count(eq & (tok < cand)) < need, cand, cur))
        return tuple(new)

    lasts = lax.fori_loop(0, max(nbits), index_step, zeros())

    for aff, (t, _), gt, eq, tok, last, (slot_ref, slot_te_ref, gate_ref, starts_ref) in zip(
            affs, sizes, gts, eqs, toks, lasts, outs):
        self = jnp.where(gt | (eq & (tok <= last)), 1.0, 0.0).astype(F32)
        carry = jnp.zeros((N_EXPERTS, 1), F32)
        starts_ref[...] = jnp.zeros(starts_ref.shape, jnp.int32)
        for b in range(t // TOKEN_BLOCK):
            sl = slice(b * TOKEN_BLOCK, (b + 1) * TOKEN_BLOCK)
            sbf = self[:, sl]
            pre = _dot(sbf.astype(BF16), u_ref[...]) + carry
            slots = jnp.where(sbf > 0.5, pre.astype(jnp.int32), -1)
            slot_ref[:, b, :] = slots
            rows = jnp.concatenate([slots, jnp.zeros((LANES - N_EXPERTS, TOKEN_BLOCK), jnp.int32)],
                                   axis=0)
            slot_te_ref[sl, :] = rows.T[:, 0:N_EXPERTS]
            gate_ref[:, b, :] = aff[:, sl]
            starts_ref[:, b:b + 1] = carry.astype(jnp.int32)
            carry = carry + jnp.sum(sbf, axis=1, keepdims=True)
        nblk = t // TOKEN_BLOCK
        starts_ref[:, nblk:nblk + 1] = carry.astype(jnp.int32)


def _route(*affs_et):
    sizes = tuple((a.shape[1], EC_CAPACITY_FACTOR * a.shape[1] // N_EXPERTS) for a in affs_et)
    upper = jnp.asarray(np.triu(np.ones((TOKEN_BLOCK, TOKEN_BLOCK)), 1), BF16)
    whole = lambda shape: pl.BlockSpec(shape, lambda i, _n=len(shape): (0,) * _n)
    out_shape, out_specs = [], []
    for t, _ in sizes:
        nblk = t // TOKEN_BLOCK
        assert nblk + 1 <= LANES
        for shape, dtype in (((N_EXPERTS, nblk, TOKEN_BLOCK), jnp.int32), ((t, N_EXPERTS), jnp.int32),
                             ((N_EXPERTS, nblk, TOKEN_BLOCK), F32), ((N_EXPERTS, LANES), jnp.int32)):
            out_shape.append(jax.ShapeDtypeStruct(shape, dtype))
            out_specs.append(whole(shape))
    outs = pl.pallas_call(
        functools.partial(_route_kernel, sizes=sizes),
        out_shape=out_shape,
        grid=(1,),
        in_specs=[whole(a.shape) for a in affs_et] + [whole((TOKEN_BLOCK, TOKEN_BLOCK))],
        out_specs=out_specs,
        compiler_params=pltpu.CompilerParams(dimension_semantics=("arbitrary",)),
        name="route",
    )(*affs_et, upper)
    return [outs[4 * g:4 * g + 4] for g in range(len(sizes))]


def _pack_windows(starts_ref, b, experts, cap):
    first = [jnp.minimum((starts_ref[e, b] // BF16_ROWS) * BF16_ROWS, cap - SLOT_WINDOW)
             for e in experts]
    rows = [jnp.where(starts_ref[e, b + 1] > starts_ref[e, b], starts_ref[e, b + 1] - w, 0)
            for e, w in zip(experts, first)]
    return first, pl.cdiv(functools.reduce(jnp.maximum, rows), SLOT_WINDOW)


def _block_copy(hbm_ref, vmem_ref, sem_ref, b):
    rows = pl.ds(pl.multiple_of(b * TOKEN_BLOCK, TOKEN_BLOCK), TOKEN_BLOCK)
    return pltpu.make_async_copy(hbm_ref.at[rows], vmem_ref.at[rows], sem_ref.at[b])


def _gather_group(g, starts_ref, slot_ref, gate_ref, h2_ref, xs_ref, gs_ref, row0, t, cap, arrive):
    sub = lax.broadcasted_iota(jnp.int32, (SLOT_WINDOW, TOKEN_BLOCK), 0)
    experts = [g * PACK + j for j in range(PACK)]
    assert (t // TOKEN_BLOCK) % GATHER_UNROLL == 0

    def window(b, first, i):
        hb = pl.ds(pl.multiple_of(b * TOKEN_BLOCK, TOKEN_BLOCK), TOKEN_BLOCK)
        hits, dst = [], []
        for j in range(PACK):
            lo = first[j] + i * SLOT_WINDOW
            w = jnp.minimum(lo, cap - SLOT_WINDOW)
            srow = slot_ref[j, pl.ds(b, 1), :]
            hits.append((srow == w + sub) & (srow >= lo))
            dst.append(pl.ds(pl.multiple_of(row0 + w, BF16_ROWS), SLOT_WINDOW))
        onehot = jnp.concatenate([jnp.where(h, 1.0, 0.0) for h in hits], axis=0).astype(BF16)
        got = _dot(onehot, h2_ref[hb, :])
        for j in range(PACK):
            piece = got[j * SLOT_WINDOW:(j + 1) * SLOT_WINDOW].astype(BF16)
            xs_ref[j, dst[j], :] = xs_ref[j, dst[j], :] + piece
            grow = gate_ref[j, pl.ds(b, 1), :]
            gs_ref[j, dst[j], :] += jnp.sum(jnp.where(hits[j], grow, 0.0), axis=1, keepdims=True)

    def blocks(q, carry):
        pending = []
        for u in range(GATHER_UNROLL):
            arrive(q * GATHER_UNROLL + u)
        for u in range(GATHER_UNROLL):
            b = q * GATHER_UNROLL + u
            first, n_windows = _pack_windows(starts_ref, b, experts, cap)
            window(b, first, 0)
            pending.append((b, first, n_windows))
        for b, first, n_windows in pending:
            def more(i, carry, b=b, first=first):
                window(b, first, i)
                return carry

            lax.fori_loop(1, n_windows, more, 0)
        return carry

    lax.fori_loop(0, t // TOKEN_BLOCK // GATHER_UNROLL, blocks, 0)


def _experts_kernel(sp_ref, ss_ref, h2p_hbm, h2s_hbm, slotp_ref, slots_ref, gatep_ref, gates_ref,
                    wg_ref, wu_ref, wd_ref, yp_ref, ys_ref, xs_ref, gs_ref, acc_ref,
                    h2p_ref, h2s_ref, semp_ref, sems_ref, *, tp, ts, capp, caps):
    g = pl.program_id(0)
    step = pl.program_id(1)
    j = step // N_FF_TILES
    f = step % N_FF_TILES
    loading = [(h2p_hbm, h2p_ref, semp_ref, tp // TOKEN_BLOCK),
               (h2s_hbm, h2s_ref, sems_ref, ts // TOKEN_BLOCK)]

    @pl.when((step == 0) & (g == 0))
    def _():
        for hbm_ref, vmem_ref, sem_ref, n_blocks in loading:
            for b in range(n_blocks):
                _block_copy(hbm_ref, vmem_ref, sem_ref, b).start()

    def arrive(hbm_ref, vmem_ref, sem_ref, _):
        def wait(b):
            @pl.when(g == 0)
            def _():
                _block_copy(hbm_ref, vmem_ref, sem_ref, b).wait()

        return wait

    @pl.when(step == 0)
    def _():
        xs_ref[...] = jnp.zeros(xs_ref.shape, BF16)
        gs_ref[...] = jnp.zeros(gs_ref.shape, F32)
        _gather_group(g, sp_ref, slotp_ref, gatep_ref, h2p_ref, xs_ref, gs_ref, 0, tp, capp,
                      arrive(*loading[0]))
        _gather_group(g, ss_ref, slots_ref, gates_ref, h2s_ref, xs_ref, gs_ref, capp, ts, caps,
                      arrive(*loading[1]))

    @pl.when(f == 0)
    def _():
        acc_ref[...] = jnp.zeros(acc_ref.shape, F32)

    x = xs_ref[j]
    a = _dot(x, wg_ref[0].astype(BF16))
    u = _dot(x, wu_ref[0].astype(BF16))
    acc_ref[...] += _dot((_silu(a) * u).astype(BF16), wd_ref[0].astype(BF16))

    @pl.when(f == N_FF_TILES - 1)
    def _():
        yp_ref[0] = (acc_ref[0:capp, :] * gs_ref[j, 0:capp, :]).astype(BF16)
        ys_ref[0] = (acc_ref[capp:capp + caps, :] * gs_ref[j, capp:capp + caps, :]).astype(BF16)


def _experts(starts_p, starts_s, h2p, h2s, slot_p, slot_s, gate_p, gate_s, w_gate, w_up, w_down):
    tp, ts = h2p.shape[0], h2s.shape[0]
    capp = EC_CAPACITY_FACTOR * tp // N_EXPERTS
    caps = EC_CAPACITY_FACTOR * ts // N_EXPERTS
    rows = capp + caps
    nbp, nbs = tp // TOKEN_BLOCK, ts // TOKEN_BLOCK
    expert = lambda g, s: g * PACK + s // N_FF_TILES
    grid_spec = pltpu.PrefetchScalarGridSpec(
        num_scalar_prefetch=2,
        grid=(N_EXPERTS // PACK, PACK * N_FF_TILES),
        in_specs=[
            pl.BlockSpec(memory_space=pl.ANY),
            pl.BlockSpec(memory_space=pl.ANY),
            pl.BlockSpec((PACK, nbp, TOKEN_BLOCK), lambda g, s, *_: (g, 0, 0)),
            pl.BlockSpec((PACK, nbs, TOKEN_BLOCK), lambda g, s, *_: (g, 0, 0)),
            pl.BlockSpec((PACK, nbp, TOKEN_BLOCK), lambda g, s, *_: (g, 0, 0)),
            pl.BlockSpec((PACK, nbs, TOKEN_BLOCK), lambda g, s, *_: (g, 0, 0)),
            pl.BlockSpec((1, D_MODEL, FF_TILE), lambda g, s, *_: (expert(g, s), 0, s % N_FF_TILES)),
            pl.BlockSpec((1, D_MODEL, FF_TILE), lambda g, s, *_: (expert(g, s), 0, s % N_FF_TILES)),
            pl.BlockSpec((1, FF_TILE, D_MODEL), lambda g, s, *_: (expert(g, s), s % N_FF_TILES, 0)),
        ],
        out_specs=[
            pl.BlockSpec((1, capp, D_MODEL), lambda g, s, *_: (expert(g, s), 0, 0)),
            pl.BlockSpec((1, caps, D_MODEL), lambda g, s, *_: (expert(g, s), 0, 0)),
        ],
        scratch_shapes=[pltpu.VMEM((PACK, rows, D_MODEL), BF16), pltpu.VMEM((PACK, rows, 1), F32),
                        pltpu.VMEM((rows, D_MODEL), F32),
                        pltpu.VMEM((tp, D_MODEL), BF16), pltpu.VMEM((ts, D_MODEL), BF16),
                        pltpu.SemaphoreType.DMA((nbp,)), pltpu.SemaphoreType.DMA((nbs,))],
    )
    return pl.pallas_call(
        functools.partial(_experts_kernel, tp=tp, ts=ts, capp=capp, caps=caps),
        out_shape=[jax.ShapeDtypeStruct((N_EXPERTS, capp, D_MODEL), BF16),
                   jax.ShapeDtypeStruct((N_EXPERTS, caps, D_MODEL), BF16)],
        grid_spec=grid_spec,
        compiler_params=pltpu.CompilerParams(dimension_semantics=("arbitrary", "arbitrary"),
                                             vmem_limit_bytes=VMEM_LIMIT),
        name="experts",
    )(starts_p, starts_s, h2p, h2s, slot_p, slot_s, gate_p, gate_s, w_gate, w_up, w_down)


def _combine_kernel(st_ref, x1_ref, slot_ref, mod_ref, fg_ref, y_hbm, out_ref, acc_ref, y_ref,
                    sem_ref, arrived_ref, *, cap):
    step = pl.program_id(0)
    n_chunks = Y_CHUNKS
    chunk_rows = cap // Y_CHUNKS
    n_parts = COMBINE_ROWS // TOKEN_BLOCK
    last_block = step * n_parts + n_parts - 1

    def chunk_copy(c):
        rows = pl.ds(c * chunk_rows, chunk_rows)
        return pltpu.make_async_copy(y_hbm.at[:, rows, :], y_ref.at[:, rows, :], sem_ref.at[c])

    @pl.when(step == 0)
    def _():
        arrived_ref[0] = 0
        for c in range(n_chunks):
            chunk_copy(c).start()

    top = functools.reduce(jnp.maximum, [st_ref[e, last_block + 1] for e in range(N_EXPERTS)])
    want = pl.cdiv(jnp.minimum(top + SLOT_WINDOW, cap), chunk_rows)
    have = arrived_ref[0]
    for c in range(n_chunks):
        @pl.when((c >= have) & (c < want))
        def _(c=c):
            chunk_copy(c).wait()
    arrived_ref[0] = jnp.maximum(have, want)

    lane = lax.broadcasted_iota(jnp.int32, (TOKEN_BLOCK, PACK * SLOT_WINDOW), 1)
    gate2 = mod_ref[0][:, 5 * D_MODEL:6 * D_MODEL]

    def scatter(slot, experts, first, i):
        target = None
        windows = []
        for j, e in enumerate(experts):
            lo = first[j] + i * SLOT_WINDOW
            w = jnp.minimum(lo, cap - SLOT_WINDOW)
            sc = slot[:, e:e + 1]
            col = jnp.where(sc >= lo, sc - w + j * SLOT_WINDOW, -1)
            target = col if target is None else jnp.where(lane < j * SLOT_WINDOW, target, col)
            windows.append(y_ref[e, pl.ds(pl.multiple_of(w, BF16_ROWS), SLOT_WINDOW), :])
        onehot = jnp.where(target == lane, 1.0, 0.0).astype(BF16)
        return _dot(onehot, jnp.concatenate(windows, axis=0))

    def finish(rows):
        x2 = x1_ref[rows, :] + gate2 * acc_ref[rows, :]
        out_ref[rows, :] = _rms(x2) * fg_ref[...]

    parts = []
    for p in range(n_parts):
        rows = slice(p * TOKEN_BLOCK, (p + 1) * TOKEN_BLOCK)
        slot = slot_ref[rows, :]
        groups = []
        total = None
        for g in range(N_EXPERTS // PACK):
            experts = list(range(g * PACK, (g + 1) * PACK))
            first, n_windows = _pack_windows(st_ref, step * n_parts + p, experts, cap)
            groups.append((experts, first, n_windows))
            part = scatter(slot, experts, first, 0)
            total = part if total is None else total + part
        acc_ref[rows, :] = total
        parts.append((rows, slot, groups))
    for rows, _, _ in parts:
        finish(rows)

    for rows, slot, groups in parts:
        most = functools.reduce(jnp.maximum, [n_windows for _, _, n_windows in groups])

        @pl.when(most > 1)
        def _(rows=rows, slot=slot, groups=groups):
            for experts, first, n_windows in groups:
                def more(i, carry, experts=experts, first=first):
                    acc_ref[rows, :] += scatter(slot, experts, first, i)
                    return carry

                lax.fori_loop(1, n_windows, more, 0)
            finish(rows)


def _combine(starts, x1, slot_te, mod_rows, mod_first_row, tokens_per_mod_row, final_g, y):
    t = x1.shape[0]
    cap = y.shape[1]
    assert tokens_per_mod_row is None or tokens_per_mod_row % COMBINE_ROWS == 0
    steps_per_row = None if tokens_per_mod_row is None else tokens_per_mod_row // COMBINE_ROWS
    grid_spec = pltpu.PrefetchScalarGridSpec(
        num_scalar_prefetch=1,
        grid=(t // COMBINE_ROWS,),
        in_specs=[
            pl.BlockSpec((COMBINE_ROWS, D_MODEL), lambda b, *_: (b, 0)),
            pl.BlockSpec((COMBINE_ROWS, N_EXPERTS), lambda b, *_: (b, 0)),
            _mod_row_spec(mod_first_row, steps_per_row),
            pl.BlockSpec((1, D_MODEL), lambda b, *_: (0, 0)),
            pl.BlockSpec(memory_space=pl.ANY),
        ],
        out_specs=pl.BlockSpec((COMBINE_ROWS, D_MODEL), lambda b, *_: (b, 0)),
        scratch_shapes=[pltpu.VMEM((COMBINE_ROWS, D_MODEL), F32), pltpu.VMEM(y.shape, BF16),
                        pltpu.SemaphoreType.DMA((Y_CHUNKS,)), pltpu.SMEM((1,), jnp.int32)],
    )
    return pl.pallas_call(
        functools.partial(_combine_kernel, cap=cap),
        out_shape=jax.ShapeDtypeStruct((t, D_MODEL), F32),
        grid_spec=grid_spec,
        compiler_params=pltpu.CompilerParams(dimension_semantics=("arbitrary",),
                                             vmem_limit_bytes=VMEM_LIMIT),
        name="combine",
    )(starts, x1, slot_te, mod_rows, final_g.reshape(1, D_MODEL), y)


def kernel(x_prompt, x_sample, state_ret, c, c_ctx, norm1_g, norm2_g, final_g, w_mod, b_mod, w_in,
           w_fmix, w_out, w_router, w_gate, w_up, w_down):
    bp, seq, _ = x_prompt.shape
    bs, dec_seq, _ = x_sample.shape
    assert w_mod.shape[0] == 1, "single-layer trunk"
    tp, ts = bp * seq, bs * dec_seq

    cond = jnp.concatenate([c_ctx[None, :], c], axis=0)
    mod = _modulation(cond, w_mod[0], b_mod[0])
    ctx_row, lat_row = 0, 1

    w_out_bf = w_out[0].astype(BF16)
    mix_p, states = _mixer(x_prompt, mod, ctx_row, False, None, True, False, norm1_g[0], w_in[0],
                           w_fmix[0])
    (mix_s,) = _mixer(x_sample, mod, lat_row, True, state_ret[:, 0], False, True, norm1_g[0],
                      w_in[0], w_fmix[0])
    x1p, h2p, affp = _post(x_prompt.reshape(tp, D_MODEL), mix_p, mod, ctx_row, None,
                           norm2_g[0], w_out_bf, w_router[0])
    x1s, h2s, affs = _post(x_sample.reshape(ts, D_MODEL), mix_s, mod, lat_row, dec_seq,
                           norm2_g[0], w_out_bf, w_router[0])

    (slot_p, slot_te_p, gate_p, starts_p), (slot_s, slot_te_s, gate_s, starts_s) = _route(affp, affs)

    yp, ys = _experts(starts_p, starts_s, h2p, h2s, slot_p, slot_s, gate_p, gate_s,
                      w_gate[0], w_up[0], w_down[0])

    out_p = _combine(starts_p, x1p, slot_te_p, mod, ctx_row, None, final_g, yp)
    out_s = _combine(starts_s, x1s, slot_te_s, mod, lat_row, dec_seq, final_g, ys)

    y_prompt = out_p.reshape(bp, seq, D_MODEL)
    y_sample = out_s.reshape(bs, dec_seq, D_MODEL)
    state_new = states.reshape(bp, 1, 2, N_RET_HEADS, HEAD_DIM, HEAD_DIM).astype(x_prompt.dtype)
    return (y_prompt, y_sample, state_new)
```

```python
import functools
import math

import jax
import jax.numpy as jnp
import numpy as np
from jax import lax
from jax.experimental import pallas as pl
from jax.experimental.pallas import tpu as pltpu

D_MODEL = 1024
D_FOURIER = 512
N_FOURIER_GROUPS = 4
FOURIER_GROUP_W = 128
D_RET = 512
N_RET_HEADS = 4
HEAD_DIM = 128
CHUNK = 256
GRID_W = 64
N_EXPERTS = 16
EC_CAPACITY_FACTOR = 2
D_EXPERT_FF = 2816
ROPE_BASE = 10000.0
EPS = 1e-6
D_IN_PROJ = D_FOURIER + 5 * D_RET
LOG_GAMMA_FWD = np.log(1.0 - 2.0 ** (-5.0 - np.arange(N_RET_HEADS))).astype(np.float32)
LOG_GAMMA_BWD = np.log(1.0 - 2.0 ** (-5.5 - np.arange(N_RET_HEADS))).astype(np.float32)

LANES = 128
BF16_ROWS = 16
TOKEN_BLOCK = 256
SLOT_WINDOW = 64
PACK = TOKEN_BLOCK // SLOT_WINDOW
GATHER_UNROLL = 4
Y_CHUNKS = 4
FF_TILE = 256
N_FF_TILES = D_EXPERT_FF // FF_TILE
MOD_TILE = 1024
MIXER_ROWS = 1024
POST_ROWS = 512
POST_PART = 256
COMBINE_ROWS = 512
VMEM_LIMIT = 56 * 1024 * 1024
_ROW_STREAM_VMEM = 10 * COMBINE_ROWS * D_MODEL * 4

F32 = jnp.float32
BF16 = jnp.bfloat16


def _dot(a, b):
    return jnp.dot(a, b, preferred_element_type=F32)


def _dot_nt(a, b):
    return lax.dot_general(a, b, (((1,), (1,)), ((), ())), preferred_element_type=F32)


def _silu(x):
    return x * jax.nn.sigmoid(x)


def _mod_kernel(condt_ref, w_ref, b_ref, out_ref, *, n_cond):
    s = _silu(condt_ref[...])
    w = w_ref[...]
    out_ref[...] = jnp.zeros(out_ref.shape, F32)
    for r in range(n_cond):
        out_ref[r] = jnp.sum(w * s[:, r:r + 1], axis=0, keepdims=True) + b_ref[...]


def _modulation(cond_rows, w_mod, b_mod):
    n_cond = cond_rows.shape[0]
    condt = jnp.zeros((D_MODEL, 8), F32).at[:, :n_cond].set(cond_rows.T)
    n_out = w_mod.shape[1]
    return pl.pallas_call(
        functools.partial(_mod_kernel, n_cond=n_cond),
        out_shape=jax.ShapeDtypeStruct((8, 1, n_out), F32),
        grid=(n_out // MOD_TILE,),
        in_specs=[
            pl.BlockSpec((D_MODEL, 8), lambda j: (0, 0)),
            pl.BlockSpec((D_MODEL, MOD_TILE), lambda j: (0, j)),
            pl.BlockSpec((1, MOD_TILE), lambda j: (0, j)),
        ],
        out_specs=pl.BlockSpec((8, 1, MOD_TILE), lambda j: (0, 0, j)),
        compiler_params=pltpu.CompilerParams(dimension_semantics=("arbitrary",)),
        name="mod",
    )(condt, w_mod, b_mod.reshape(1, n_out))


def _rms(x):
    return x * lax.rsqrt(jnp.mean(x * x, axis=-1, keepdims=True) + EPS)


def _groupnorm(o):
    mu = jnp.mean(o, axis=-1, keepdims=True)
    c = o - mu
    return c * lax.rsqrt(jnp.mean(c * c, axis=-1, keepdims=True) + EPS)


def _split_hi_lo(x):
    hi = x.astype(BF16)
    lo = (x - hi.astype(F32)).astype(BF16)
    return hi, lo


def _mixer_kernel(*refs, n, use_rope, has_state_in, emit_state):
    it = iter(refs)
    x_ref, mod_ref, g1_ref, win_ref, wfmix_ref = (next(it) for _ in range(5))
    cw_ref, cn_ref, sn_ref, dmat_ref, qdec_ref, kdec_ref, sdec_ref = (next(it) for _ in range(7))
    cos_ref = sin_ref = s0_ref = st_ref = None
    if use_rope:
        cos_ref, sin_ref = next(it), next(it)
    if has_state_in:
        s0_ref = next(it)
    mix_ref = next(it)
    if emit_state:
        st_ref = next(it)
    p_ref, of_ref, ob_ref = next(it), next(it), next(it)

    n_seq = MIXER_ROWS // n
    chunks_per_seq = n // CHUNK
    mod = mod_ref[0]
    shift1 = mod[:, 0:D_MODEL]
    scale1 = mod[:, D_MODEL:2 * D_MODEL]

    h = (_rms(x_ref[...]) * g1_ref[...] * (1.0 + scale1) + shift1).astype(BF16)
    for j in range(D_IN_PROJ // 512):
        p_ref[:, j * 512:(j + 1) * 512] = _dot(h, win_ref[:, j * 512:(j + 1) * 512].astype(BF16))

    xf = p_ref[:, 0:D_FOURIER].astype(BF16)
    xc, xs = [], []
    cw = cw_ref[...].astype(BF16)
    for g in range(N_FOURIER_GROUPS):
        t = _dot(xf[:, g * FOURIER_GROUP_W:(g + 1) * FOURIER_GROUP_W], cw)
        xc.append(t[:, :FOURIER_GROUP_W].astype(BF16))
        xs.append(t[:, FOURIER_GROUP_W:].astype(BF16))
    xc = jnp.concatenate(xc, axis=1)
    xs = jnp.concatenate(xs, axis=1)
    cn = cn_ref[...].astype(BF16)
    sn = sn_ref[...].astype(BF16)
    for s in range(n_seq):
        rs = slice(s * n, (s + 1) * n)
        fre = (_dot(cn, xc[rs]) - _dot(sn, xs[rs])) * (1.0 / math.sqrt(n * FOURIER_GROUP_W))
        fre = fre.astype(BF16)
        for g in range(N_FOURIER_GROUPS):
            sl = slice(g * FOURIER_GROUP_W, (g + 1) * FOURIER_GROUP_W)
            mix_ref[rs, sl] = _dot(fre[:, sl], wfmix_ref[g].astype(BF16)).astype(BF16)

    for hh in range(N_RET_HEADS):
        base = D_FOURIER + hh * HEAD_DIM
        q = p_ref[:, base:base + HEAD_DIM]
        k = p_ref[:, base + D_RET:base + D_RET + HEAD_DIM]
        v = p_ref[:, base + 2 * D_RET:base + 2 * D_RET + HEAD_DIM]
        if use_rope:
            lane = lax.broadcasted_iota(jnp.int32, (MIXER_ROWS, HEAD_DIM), 1)
            first = (lane % 64) < 32

            def rope(t):
                swapped = jnp.where(first, pltpu.roll(t, HEAD_DIM - 32, 1), pltpu.roll(t, 32, 1))
                return t * cos_ref[...] + swapped * sin_ref[...]

            q, k = rope(q), rope(k)
        k = k * (HEAD_DIM ** -0.5)
        qb, vb = q.astype(BF16), v.astype(BF16)
        kb = k.astype(BF16)

        def initial(s, direction):
            if has_state_in:
                return s0_ref[s, direction, hh]
            return jnp.zeros((HEAD_DIM, HEAD_DIM), F32)

        for s in range(n_seq):
            parts = []
            for ci in range(chunks_per_seq):
                c = s * chunks_per_seq + ci
                rs = slice(c * CHUNK, (c + 1) * CHUNK)
                qc, kc, vc = qb[rs], kb[rs], vb[rs]
                qk = _dot_nt(qc, kc)
                lhs = jnp.concatenate([(qk * dmat_ref[0, hh]).astype(BF16),
                                       (qk * dmat_ref[1, hh]).astype(BF16),
                                       (k[rs] * kdec_ref[0, hh]).T.astype(BF16),
                                       (k[rs] * kdec_ref[1, hh]).T.astype(BF16)], axis=0)
                parts.append((rs, qc, _dot(lhs, vc)))
            sf = initial(s, 0)
            for ci in range(chunks_per_seq):
                rs, qc, r = parts[ci]
                o = r[0:CHUNK]
                if has_state_in or ci > 0:
                    o = o + qdec_ref[0, hh] * _dot(qc, sf.astype(BF16))
                of_ref[rs, :] = o
                sf = sf * sdec_ref[0, hh] + r[2 * CHUNK:2 * CHUNK + HEAD_DIM]
            sb = initial(s, 1)
            for ci in reversed(range(chunks_per_seq)):
                rs, qc, r = parts[ci]
                o = r[CHUNK:2 * CHUNK]
                if has_state_in or ci < chunks_per_seq - 1:
                    o = o + qdec_ref[1, hh] * _dot(qc, sb.astype(BF16))
                ob_ref[rs, :] = o
                sb = sb * sdec_ref[1, hh] + r[2 * CHUNK + HEAD_DIM:]
            if emit_state:
                st_ref[s, 0, hh] = sf
                st_ref[s, 1, hh] = sb

        gf = p_ref[:, base + 3 * D_RET:base + 3 * D_RET + HEAD_DIM]
        gb = p_ref[:, base + 4 * D_RET:base + 4 * D_RET + HEAD_DIM]
        y = _silu(gf) * _groupnorm(of_ref[...]) + _silu(gb) * _groupnorm(ob_ref[...])
        mix_ref[:, base:base + HEAD_DIM] = y.astype(BF16)


def _post_kernel(x_ref, mix_ref, mod_ref, g2_ref, wout_ref, wr_ref, x1_ref, h2_ref, aff_ref):
    mod = mod_ref[0]
    gate1 = mod[:, 2 * D_MODEL:3 * D_MODEL]
    shift2 = mod[:, 3 * D_MODEL:4 * D_MODEL]
    scale2 = mod[:, 4 * D_MODEL:5 * D_MODEL]
    wr_hi, wr_lo = _split_hi_lo(wr_ref[...])
    wr_both = jnp.concatenate([wr_hi, wr_lo], axis=1)
    parts = [slice(p * POST_PART, (p + 1) * POST_PART) for p in range(POST_ROWS // POST_PART)]
    x1 = []
    for rows in parts:
        x1.append(x_ref[rows, :] + gate1 * _dot(mix_ref[rows, :], wout_ref[...]))
        x1_ref[rows, :] = x1[-1]
    for rows, x1_part in zip(parts, x1):
        h2 = _rms(x1_part) * g2_ref[...] * (1.0 + scale2) + shift2
        h2_hi, h2_lo = _split_hi_lo(h2)
        h2_ref[rows, :] = h2_hi
        by_hi = _dot(h2_hi, wr_both)
        logits = by_hi[:, :N_EXPERTS] + (_dot(h2_lo, wr_hi) + by_hi[:, N_EXPERTS:])
        z = jnp.exp(logits - jnp.max(logits, axis=-1, keepdims=True))
        aff = z / jnp.sum(z, axis=-1, keepdims=True)
        lanes = jnp.concatenate([aff, jnp.zeros((POST_PART, LANES - N_EXPERTS), F32)], axis=1)
        aff_ref[:, rows] = lanes.T[0:N_EXPERTS, :]


def _dft_consts(n):
    w = FOURIER_GROUP_W
    jw = np.arange(w)
    angw = 2.0 * np.pi * np.outer(jw, jw) / w
    cw = np.concatenate([np.cos(angw), np.sin(angw)], axis=1)
    jn = np.arange(n)
    angn = 2.0 * np.pi * (np.outer(jn, jn) % n) / n
    return (jnp.asarray(cw, F32), jnp.asarray(np.cos(angn), F32), jnp.asarray(np.sin(angn), F32))


def _retention_consts():
    i = np.arange(CHUNK, dtype=np.float64)
    diff = i[:, None] - i[None, :]
    dmat = np.zeros((2, N_RET_HEADS, CHUNK, CHUNK))
    qdec = np.zeros((2, N_RET_HEADS, CHUNK, HEAD_DIM))
    kdec = np.zeros((2, N_RET_HEADS, CHUNK, HEAD_DIM))
    sdec = np.zeros((2, N_RET_HEADS, HEAD_DIM, HEAD_DIM))
    for hh in range(N_RET_HEADS):
        lf = float(LOG_GAMMA_FWD[hh])
        lb = float(LOG_GAMMA_BWD[hh])
        dmat[0, hh] = np.where(diff >= 0, np.exp(lf * np.maximum(diff, 0.0)), 0.0)
        dmat[1, hh] = np.where(diff <= 0, np.exp(lb * np.maximum(-diff, 0.0)), 0.0)
        qdec[0, hh] = np.exp(lf * (i + 1.0))[:, None]
        qdec[1, hh] = np.exp(lb * (CHUNK - i))[:, None]
        kdec[0, hh] = np.exp(lf * (CHUNK - 1.0 - i))[:, None]
        kdec[1, hh] = np.exp(lb * i)[:, None]
        sdec[0, hh] = math.exp(lf * CHUNK)
        sdec[1, hh] = math.exp(lb * CHUNK)
    return tuple(jnp.asarray(a, F32) for a in (dmat, qdec, kdec, sdec))


def _rope_consts(n):
    rows_n = n // GRID_W
    row = np.repeat(np.arange(rows_n, dtype=np.float64), GRID_W)
    col = np.tile(np.arange(GRID_W, dtype=np.float64), rows_n)
    n_pairs = HEAD_DIM // 4
    freqs = (np.float32(ROPE_BASE) ** (-np.arange(n_pairs, dtype=np.float32) / n_pairs)).astype(np.float64)
    ar = row[:, None] * freqs[None, :]
    ac = col[:, None] * freqs[None, :]
    cos = np.concatenate([np.cos(ar), np.cos(ar), np.cos(ac), np.cos(ac)], axis=1)
    sin = np.concatenate([-np.sin(ar), np.sin(ar), -np.sin(ac), np.sin(ac)], axis=1)
    return jnp.asarray(cos, F32), jnp.asarray(sin, F32)


def _const_spec(shape):
    nd = len(shape)
    return pl.BlockSpec(shape, lambda b, _nd=nd: (0,) * _nd, pipeline_mode=pl.Buffered(1))


def _mod_row_spec(first_row, blocks_per_row):
    if blocks_per_row is None:
        return pl.BlockSpec((1, 1, 6 * D_MODEL), lambda b, *_: (first_row, 0, 0))
    return pl.BlockSpec((1, 1, 6 * D_MODEL), lambda b, *_: (first_row + b // blocks_per_row, 0, 0))


def _mixer(x, mod_rows, mod_first_row, mod_per_batch, state_in, emit_state, use_rope, g1, w_in,
           w_fmix):
    nb, n, _ = x.shape
    assert MIXER_ROWS % n == 0 and (nb * n) % MIXER_ROWS == 0
    n_seq = MIXER_ROWS // n
    has_state_in = state_in is not None
    cw, cn, sn = _dft_consts(n)
    dmat, qdec, kdec, sdec = _retention_consts()
    consts = [cw, cn, sn, dmat, qdec, kdec, sdec]
    if use_rope:
        assert n_seq == 1
        consts += list(_rope_consts(n))
    weights = [g1.reshape(1, D_MODEL), w_in, w_fmix]

    if mod_per_batch:
        assert n % MIXER_ROWS == 0
    state_spec = pl.BlockSpec((n_seq, 2, N_RET_HEADS, HEAD_DIM, HEAD_DIM), lambda b: (b, 0, 0, 0, 0))
    row_spec = pl.BlockSpec((MIXER_ROWS, D_MODEL), lambda b: (b, 0))
    in_specs = [row_spec, _mod_row_spec(mod_first_row, n // MIXER_ROWS if mod_per_batch else None)]
    in_specs += [_const_spec(a.shape) for a in weights + consts]
    args = [x.reshape(nb * n, D_MODEL), mod_rows] + weights + consts
    if has_state_in:
        in_specs.append(state_spec)
        args.append(state_in)

    out_shape = [jax.ShapeDtypeStruct((nb * n, D_MODEL), BF16)]
    out_specs = [row_spec]
    if emit_state:
        out_shape.append(jax.ShapeDtypeStruct((nb, 2, N_RET_HEADS, HEAD_DIM, HEAD_DIM), F32))
        out_specs.append(state_spec)

    return pl.pallas_call(
        functools.partial(_mixer_kernel, n=n, use_rope=use_rope, has_state_in=has_state_in,
                          emit_state=emit_state),
        out_shape=out_shape,
        grid=(nb * n // MIXER_ROWS,),
        in_specs=in_specs,
        out_specs=out_specs,
        scratch_shapes=[pltpu.VMEM((MIXER_ROWS, D_IN_PROJ), F32),
                        pltpu.VMEM((MIXER_ROWS, HEAD_DIM), F32), pltpu.VMEM((MIXER_ROWS, HEAD_DIM), F32)],
        compiler_params=pltpu.CompilerParams(dimension_semantics=("arbitrary",),
                                             vmem_limit_bytes=VMEM_LIMIT),
        name="mixer_rope" if use_rope else "mixer",
    )(*args)


def _post(x, mix, mod_rows, mod_first_row, tokens_per_mod_row, g2, w_out_bf, w_router):
    t = x.shape[0]
    assert tokens_per_mod_row is None or tokens_per_mod_row % POST_ROWS == 0
    blocks_per_row = None if tokens_per_mod_row is None else tokens_per_mod_row // POST_ROWS
    row_spec = pl.BlockSpec((POST_ROWS, D_MODEL), lambda b: (b, 0))
    return pl.pallas_call(
        _post_kernel,
        out_shape=[jax.ShapeDtypeStruct((t, D_MODEL), F32),
                   jax.ShapeDtypeStruct((t, D_MODEL), BF16),
                   jax.ShapeDtypeStruct((N_EXPERTS, t), F32)],
        grid=(t // POST_ROWS,),
        in_specs=[row_spec, row_spec,
                  _mod_row_spec(mod_first_row, blocks_per_row),
                  _const_spec((1, D_MODEL)), _const_spec((D_MODEL, D_MODEL)),
                  _const_spec((D_MODEL, N_EXPERTS))],
        out_specs=[row_spec, row_spec, pl.BlockSpec((N_EXPERTS, POST_ROWS), lambda b: (0, b))],
        compiler_params=pltpu.CompilerParams(dimension_semantics=("arbitrary",),
                                             vmem_limit_bytes=_ROW_STREAM_VMEM),
        name="post",
    )(x, mix, mod_rows, g2.reshape(1, D_MODEL), w_out_bf, w_router)


def _route_kernel(*refs, sizes):
    n = len(sizes)
    affs = [ref[...] for ref in refs[:n]]
    u_ref = refs[n]
    outs = [refs[n + 1 + 4 * g:n + 5 + 4 * g] for g in range(n)]

    def count(mask):
        return jnp.sum(mask.astype(jnp.int32), axis=1, keepdims=True)

    def as_float(word):
        return lax.bitcast_convert_type(word, F32)

    def zeros():
        return tuple(jnp.zeros((N_EXPERTS, 1), jnp.int32) for _ in sizes)

    def value_step(i, curs):
        bit = jnp.left_shift(jnp.int32(1), 30 - i)
        return tuple(jnp.where(count(aff >= as_float(cur | bit)) >= cap, cur | bit, cur)
                     for aff, (_, cap), cur in zip(affs, sizes, curs))

    thrs = lax.fori_loop(0, 31, value_step, zeros())
    gts = [aff >= as_float(thr + 1) for aff, thr in zip(affs, thrs)]
    eqs = [(aff >= as_float(thr)) & jnp.logical_not(gt) for aff, thr, gt in zip(affs, thrs, gts)]
    needs = [cap - count(gt) for (_, cap), gt in zip(sizes, gts)]
    toks = [lax.broadcasted_iota(jnp.int32, (N_EXPERTS, t), 1) for t, _ in sizes]
    nbits = [t.bit_length() - 1 for t, _ in sizes]

    def index_step(i, curs):
        new = []
        for eq, need, tok, bits, cur in zip(eqs, needs, toks, nbits, curs):
            shift = bits - 1 - i
            cand = cur | jnp.where(shift >= 0, jnp.left_shift(jnp.int32(1), jnp.maximum(shift, 0)), 0)
            new.append(jnp.where(count(eq & (tok < cand)) < need, cand, cur))
        return tuple(new)

    lasts = lax.fori_loop(0, max(nbits), index_step, zeros())

    for aff, (t, _), gt, eq, tok, last, (slot_ref, slot_te_ref, gate_ref, starts_ref) in zip(
            affs, sizes, gts, eqs, toks, lasts, outs):
        self = jnp.where(gt | (eq & (tok <= last)), 1.0, 0.0).astype(F32)
        carry = jnp.zeros((N_EXPERTS, 1), F32)
        starts_ref[...] = jnp.zeros(starts_ref.shape, jnp.int32)
        for b in range(t // TOKEN_BLOCK):
            sl = slice(b * TOKEN_BLOCK, (b + 1) * TOKEN_BLOCK)
            sbf = self[:, sl]
            pre = _dot(sbf.astype(BF16), u_ref[...]) + carry
            slots = jnp.where(sbf > 0.5, pre.astype(jnp.int32), -1)
            slot_ref[:, b, :] = slots
            rows = jnp.concatenate([slots, jnp.zeros((LANES - N_EXPERTS, TOKEN_BLOCK), jnp.int32)],
                                   axis=0)
            slot_te_ref[sl, :] = rows.T[:, 0:N_EXPERTS]
            gate_ref[:, b, :] = aff[:, sl]
            starts_ref[:, b:b + 1] = carry.astype(jnp.int32)
            carry = carry + jnp.sum(sbf, axis=1, keepdims=True)
        nblk = t // TOKEN_BLOCK
        starts_ref[:, nblk:nblk + 1] = carry.astype(jnp.int32)


def _route(*affs_et):
    sizes = tuple((a.shape[1], EC_CAPACITY_FACTOR * a.shape[1] // N_EXPERTS) for a in affs_et)
    upper = jnp.asarray(np.triu(np.ones((TOKEN_BLOCK, TOKEN_BLOCK)), 1), BF16)
    whole = lambda shape: pl.BlockSpec(shape, lambda i, _n=len(shape): (0,) * _n)
    out_shape, out_specs = [], []
    for t, _ in sizes:
        nblk = t // TOKEN_BLOCK
        assert nblk + 1 <= LANES
        for shape, dtype in (((N_EXPERTS, nblk, TOKEN_BLOCK), jnp.int32), ((t, N_EXPERTS), jnp.int32),
                             ((N_EXPERTS, nblk, TOKEN_BLOCK), F32), ((N_EXPERTS, LANES), jnp.int32)):
            out_shape.append(jax.ShapeDtypeStruct(shape, dtype))
            out_specs.append(whole(shape))
    outs = pl.pallas_call(
        functools.partial(_route_kernel, sizes=sizes),
        out_shape=out_shape,
        grid=(1,),
        in_specs=[whole(a.shape) for a in affs_et] + [whole((TOKEN_BLOCK, TOKEN_BLOCK))],
        out_specs=out_specs,
        compiler_params=pltpu.CompilerParams(dimension_semantics=("arbitrary",)),
        name="route",
    )(*affs_et, upper)
    return [outs[4 * g:4 * g + 4] for g in range(len(sizes))]


def _pack_windows(starts_ref, b, experts, cap):
    first = [jnp.minimum((starts_ref[e, b] // BF16_ROWS) * BF16_ROWS, cap - SLOT_WINDOW)
             for e in experts]
    rows = [jnp.where(starts_ref[e, b + 1] > starts_ref[e, b], starts_ref[e, b + 1] - w, 0)
            for e, w in zip(experts, first)]
    return first, pl.cdiv(functools.reduce(jnp.maximum, rows), SLOT_WINDOW)


def _block_copy(hbm_ref, vmem_ref, sem_ref, b):
    rows = pl.ds(pl.multiple_of(b * TOKEN_BLOCK, TOKEN_BLOCK), TOKEN_BLOCK)
    return pltpu.make_async_copy(hbm_ref.at[rows], vmem_ref.at[rows], sem_ref.at[b])


def _gather_group(g, starts_ref, slot_ref, gate_ref, h2_ref, xs_ref, gs_ref, row0, t, cap, arrive):
    sub = lax.broadcasted_iota(jnp.int32, (SLOT_WINDOW, TOKEN_BLOCK), 0)
    experts = [g * PACK + j for j in range(PACK)]
    assert (t // TOKEN_BLOCK) % GATHER_UNROLL == 0

    def window(b, first, i):
        hb = pl.ds(pl.multiple_of(b * TOKEN_BLOCK, TOKEN_BLOCK), TOKEN_BLOCK)
        hits, dst = [], []
        for j in range(PACK):
            lo = first[j] + i * SLOT_WINDOW
            w = jnp.minimum(lo, cap - SLOT_WINDOW)
            srow = slot_ref[j, pl.ds(b, 1), :]
            hits.append((srow == w + sub) & (srow >= lo))
            dst.append(pl.ds(pl.multiple_of(row0 + w, BF16_ROWS), SLOT_WINDOW))
        onehot = jnp.concatenate([jnp.where(h, 1.0, 0.0) for h in hits], axis=0).astype(BF16)
        got = _dot(onehot, h2_ref[hb, :])
        for j in range(PACK):
            piece = got[j * SLOT_WINDOW:(j + 1) * SLOT_WINDOW].astype(BF16)
            xs_ref[j, dst[j], :] = xs_ref[j, dst[j], :] + piece
            grow = gate_ref[j, pl.ds(b, 1), :]
            gs_ref[j, dst[j], :] += jnp.sum(jnp.where(hits[j], grow, 0.0), axis=1, keepdims=True)

    def blocks(q, carry):
        pending = []
        for u in range(GATHER_UNROLL):
            arrive(q * GATHER_UNROLL + u)
        for u in range(GATHER_UNROLL):
            b = q * GATHER_UNROLL + u
            first, n_windows = _pack_windows(starts_ref, b, experts, cap)
            window(b, first, 0)
            pending.append((b, first, n_windows))
        for b, first, n_windows in pending:
            def more(i, carry, b=b, first=first):
                window(b, first, i)
                return carry

            lax.fori_loop(1, n_windows, more, 0)
        return carry

    lax.fori_loop(0, t // TOKEN_BLOCK // GATHER_UNROLL, blocks, 0)


def _experts_kernel(sp_ref, ss_ref, h2p_hbm, h2s_hbm, slotp_ref, slots_ref, gatep_ref, gates_ref,
                    wg_ref, wu_ref, wd_ref, yp_ref, ys_ref, xs_ref, gs_ref, acc_ref,
                    h2p_ref, h2s_ref, semp_ref, sems_ref, *, tp, ts, capp, caps):
    g = pl.program_id(0)
    step = pl.program_id(1)
    j = step // N_FF_TILES
    f = step % N_FF_TILES
    loading = [(h2p_hbm, h2p_ref, semp_ref, tp // TOKEN_BLOCK),
               (h2s_hbm, h2s_ref, sems_ref, ts // TOKEN_BLOCK)]

    @pl.when((step == 0) & (g == 0))
    def _():
        for hbm_ref, vmem_ref, sem_ref, n_blocks in loading:
            for b in range(n_blocks):
                _block_copy(hbm_ref, vmem_ref, sem_ref, b).start()

    def arrive(hbm_ref, vmem_ref, sem_ref, _):
        def wait(b):
            @pl.when(g == 0)
            def _():
                _block_copy(hbm_ref, vmem_ref, sem_ref, b).wait()

        return wait

    @pl.when(step == 0)
    def _():
        xs_ref[...] = jnp.zeros(xs_ref.shape, BF16)
        gs_ref[...] = jnp.zeros(gs_ref.shape, F32)
        _gather_group(g, sp_ref, slotp_ref, gatep_ref, h2p_ref, xs_ref, gs_ref, 0, tp, capp,
                      arrive(*loading[0]))
        _gather_group(g, ss_ref, slots_ref, gates_ref, h2s_ref, xs_ref, gs_ref, capp, ts, caps,
                      arrive(*loading[1]))

    @pl.when(f == 0)
    def _():
        acc_ref[...] = jnp.zeros(acc_ref.shape, F32)

    x = xs_ref[j]
    a = _dot(x, wg_ref[0].astype(BF16))
    u = _dot(x, wu_ref[0].astype(BF16))
    acc_ref[...] += _dot((_silu(a) * u).astype(BF16), wd_ref[0].astype(BF16))

    @pl.when(f == N_FF_TILES - 1)
    def _():
        yp_ref[0] = (acc_ref[0:capp, :] * gs_ref[j, 0:capp, :]).astype(BF16)
        ys_ref[0] = (acc_ref[capp:capp + caps, :] * gs_ref[j, capp:capp + caps, :]).astype(BF16)


def _experts(starts_p, starts_s, h2p, h2s, slot_p, slot_s, gate_p, gate_s, w_gate, w_up, w_down):
    tp, ts = h2p.shape[0], h2s.shape[0]
    capp = EC_CAPACITY_FACTOR * tp // N_EXPERTS
    caps = EC_CAPACITY_FACTOR * ts // N_EXPERTS
    rows = capp + caps
    nbp, nbs = tp // TOKEN_BLOCK, ts // TOKEN_BLOCK
    expert = lambda g, s: g * PACK + s // N_FF_TILES
    grid_spec = pltpu.PrefetchScalarGridSpec(
        num_scalar_prefetch=2,
        grid=(N_EXPERTS // PACK, PACK * N_FF_TILES),
        in_specs=[
            pl.BlockSpec(memory_space=pl.ANY),
            pl.BlockSpec(memory_space=pl.ANY),
            pl.BlockSpec((PACK, nbp, TOKEN_BLOCK), lambda g, s, *_: (g, 0, 0)),
            pl.BlockSpec((PACK, nbs, TOKEN_BLOCK), lambda g, s, *_: (g, 0, 0)),
            pl.BlockSpec((PACK, nbp, TOKEN_BLOCK), lambda g, s, *_: (g, 0, 0)),
            pl.BlockSpec((PACK, nbs, TOKEN_BLOCK), lambda g, s, *_: (g, 0, 0)),
            pl.BlockSpec((1, D_MODEL, FF_TILE), lambda g, s, *_: (expert(g, s), 0, s % N_FF_TILES)),
            pl.BlockSpec((1, D_MODEL, FF_TILE), lambda g, s, *_: (expert(g, s), 0, s % N_FF_TILES)),
            pl.BlockSpec((1, FF_TILE, D_MODEL), lambda g, s, *_: (expert(g, s), s % N_FF_TILES, 0)),
        ],
        out_specs=[
            pl.BlockSpec((1, capp, D_MODEL), lambda g, s, *_: (expert(g, s), 0, 0)),
            pl.BlockSpec((1, caps, D_MODEL), lambda g, s, *_: (expert(g, s), 0, 0)),
        ],
        scratch_shapes=[pltpu.VMEM((PACK, rows, D_MODEL), BF16), pltpu.VMEM((PACK, rows, 1), F32),
                        pltpu.VMEM((rows, D_MODEL), F32),
                        pltpu.VMEM((tp, D_MODEL), BF16), pltpu.VMEM((ts, D_MODEL), BF16),
                        pltpu.SemaphoreType.DMA((nbp,)), pltpu.SemaphoreType.DMA((nbs,))],
    )
    return pl.pallas_call(
        functools.partial(_experts_kernel, tp=tp, ts=ts, capp=capp, caps=caps),
        out_shape=[jax.ShapeDtypeStruct((N_EXPERTS, capp, D_MODEL), BF16),
                   jax.ShapeDtypeStruct((N_EXPERTS, caps, D_MODEL), BF16)],
        grid_spec=grid_spec,
        compiler_params=pltpu.CompilerParams(dimension_semantics=("arbitrary", "arbitrary"),
                                             vmem_limit_bytes=VMEM_LIMIT),
        name="experts",
    )(starts_p, starts_s, h2p, h2s, slot_p, slot_s, gate_p, gate_s, w_gate, w_up, w_down)


def _combine_kernel(st_ref, x1_ref, slot_ref, mod_ref, fg_ref, y_hbm, out_ref, acc_ref, y_ref,
                    sem_ref, arrived_ref, *, cap):
    step = pl.program_id(0)
    n_chunks = Y_CHUNKS
    chunk_rows = cap // Y_CHUNKS
    n_parts = COMBINE_ROWS // TOKEN_BLOCK
    last_block = step * n_parts + n_parts - 1

    def chunk_copy(c):
        rows = pl.ds(c * chunk_rows, chunk_rows)
        return pltpu.make_async_copy(y_hbm.at[:, rows, :], y_ref.at[:, rows, :], sem_ref.at[c])

    @pl.when(step == 0)
    def _():
        arrived_ref[0] = 0
        for c in range(n_chunks):
            chunk_copy(c).start()

    top = functools.reduce(jnp.maximum, [st_ref[e, last_block + 1] for e in range(N_EXPERTS)])
    want = pl.cdiv(jnp.minimum(top + SLOT_WINDOW, cap), chunk_rows)
    have = arrived_ref[0]
    for c in range(n_chunks):
        @pl.when((c >= have) & (c < want))
        def _(c=c):
            chunk_copy(c).wait()
    arrived_ref[0] = jnp.maximum(have, want)

    lane = lax.broadcasted_iota(jnp.int32, (TOKEN_BLOCK, PACK * SLOT_WINDOW), 1)
    gate2 = mod_ref[0][:, 5 * D_MODEL:6 * D_MODEL]

    def scatter(slot, experts, first, i):
        target = None
        windows = []
        for j, e in enumerate(experts):
            lo = first[j] + i * SLOT_WINDOW
            w = jnp.minimum(lo, cap - SLOT_WINDOW)
            sc = slot[:, e:e + 1]
            col = jnp.where(sc >= lo, sc - w + j * SLOT_WINDOW, -1)
            target = col if target is None else jnp.where(lane < j * SLOT_WINDOW, target, col)
            windows.append(y_ref[e, pl.ds(pl.multiple_of(w, BF16_ROWS), SLOT_WINDOW), :])
        onehot = jnp.where(target == lane, 1.0, 0.0).astype(BF16)
        return _dot(onehot, jnp.concatenate(windows, axis=0))

    def finish(rows):
        x2 = x1_ref[rows, :] + gate2 * acc_ref[rows, :]
        out_ref[rows, :] = _rms(x2) * fg_ref[...]

    parts = []
    for p in range(n_parts):
        rows = slice(p * TOKEN_BLOCK, (p + 1) * TOKEN_BLOCK)
        slot = slot_ref[rows, :]
        groups = []
        total = None
        for g in range(N_EXPERTS // PACK):
            experts = list(range(g * PACK, (g + 1) * PACK))
            first, n_windows = _pack_windows(st_ref, step * n_parts + p, experts, cap)
            groups.append((experts, first, n_windows))
            part = scatter(slot, experts, first, 0)
            total = part if total is None else total + part
        acc_ref[rows, :] = total
        parts.append((rows, slot, groups))
    for rows, _, _ in parts:
        finish(rows)

    for rows, slot, groups in parts:
        most = functools.reduce(jnp.maximum, [n_windows for _, _, n_windows in groups])

        @pl.when(most > 1)
        def _(rows=rows, slot=slot, groups=groups):
            for experts, first, n_windows in groups:
                def more(i, carry, experts=experts, first=first):
                    acc_ref[rows, :] += scatter(slot, experts, first, i)
                    return carry

                lax.fori_loop(1, n_windows, more, 0)
            finish(rows)


def _combine(starts, x1, slot_te, mod_rows, mod_first_row, tokens_per_mod_row, final_g, y):
    t = x1.shape[0]
    cap = y.shape[1]
    assert tokens_per_mod_row is None or tokens_per_mod_row % COMBINE_ROWS == 0
    steps_per_row = None if tokens_per_mod_row is None else tokens_per_mod_row // COMBINE_ROWS
    grid_spec = pltpu.PrefetchScalarGridSpec(
        num_scalar_prefetch=1,
        grid=(t // COMBINE_ROWS,),
        in_specs=[
            pl.BlockSpec((COMBINE_ROWS, D_MODEL), lambda b, *_: (b, 0)),
            pl.BlockSpec((COMBINE_ROWS, N_EXPERTS), lambda b, *_: (b, 0)),
            _mod_row_spec(mod_first_row, steps_per_row),
            pl.BlockSpec((1, D_MODEL), lambda b, *_: (0, 0)),
            pl.BlockSpec(memory_space=pl.ANY),
        ],
        out_specs=pl.BlockSpec((COMBINE_ROWS, D_MODEL), lambda b, *_: (b, 0)),
        scratch_shapes=[pltpu.VMEM((COMBINE_ROWS, D_MODEL), F32), pltpu.VMEM(y.shape, BF16),
                        pltpu.SemaphoreType.DMA((Y_CHUNKS,)), pltpu.SMEM((1,), jnp.int32)],
    )
    return pl.pallas_call(
        functools.partial(_combine_kernel, cap=cap),
        out_shape=jax.ShapeDtypeStruct((t, D_MODEL), F32),
        grid_spec=grid_spec,
        compiler_params=pltpu.CompilerParams(
            dimension_semantics=("arbitrary",),
            vmem_limit_bytes=_ROW_STREAM_VMEM + y.size * y.dtype.itemsize),
        name="combine",
    )(starts, x1, slot_te, mod_rows, final_g.reshape(1, D_MODEL), y)


def kernel(x_prompt, x_sample, state_ret, c, c_ctx, norm1_g, norm2_g, final_g, w_mod, b_mod, w_in,
           w_fmix, w_out, w_router, w_gate, w_up, w_down):
    bp, seq, _ = x_prompt.shape
    bs, dec_seq, _ = x_sample.shape
    assert w_mod.shape[0] == 1, "single-layer trunk"
    tp, ts = bp * seq, bs * dec_seq

    cond = jnp.concatenate([c_ctx[None, :], c], axis=0)
    mod = _modulation(cond, w_mod[0], b_mod[0])
    ctx_row, lat_row = 0, 1

    w_out_bf = w_out[0].astype(BF16)
    mix_p, states = _mixer(x_prompt, mod, ctx_row, False, None, True, False, norm1_g[0], w_in[0],
                           w_fmix[0])
    (mix_s,) = _mixer(x_sample, mod, lat_row, True, state_ret[:, 0], False, True, norm1_g[0],
                      w_in[0], w_fmix[0])
    x1p, h2p, affp = _post(x_prompt.reshape(tp, D_MODEL), mix_p, mod, ctx_row, None,
                           norm2_g[0], w_out_bf, w_router[0])
    x1s, h2s, affs = _post(x_sample.reshape(ts, D_MODEL), mix_s, mod, lat_row, dec_seq,
                           norm2_g[0], w_out_bf, w_router[0])

    (slot_p, slot_te_p, gate_p, starts_p), (slot_s, slot_te_s, gate_s, starts_s) = _route(affp, affs)

    yp, ys = _experts(starts_p, starts_s, h2p, h2s, slot_p, slot_s, gate_p, gate_s,
                      w_gate[0], w_up[0], w_down[0])

    out_p = _combine(starts_p, x1p, slot_te_p, mod, ctx_row, None, final_g, yp)
    out_s = _combine(starts_s, x1s, slot_te_s, mod, lat_row, dec_seq, final_g, ys)

    y_prompt = out_p.reshape(bp, seq, D_MODEL)
    y_sample = out_s.reshape(bs, dec_seq, D_MODEL)
    state_new = states.reshape(bp, 1, 2, N_RET_HEADS, HEAD_DIM, HEAD_DIM).astype(x_prompt.dtype)
    return (y_prompt, y_sample, state_new)
```

```python
import functools
import math

import jax
import jax.numpy as jnp
import numpy as np
from jax import lax
from jax.experimental import pallas as pl
from jax.experimental.pallas import tpu as pltpu

D_MODEL = 1024
D_FOURIER = 512
N_FOURIER_GROUPS = 4
FOURIER_GROUP_W = 128
D_RET = 512
N_RET_HEADS = 4
HEAD_DIM = 128
CHUNK = 256
GRID_W = 64
N_EXPERTS = 16
EC_CAPACITY_FACTOR = 2
D_EXPERT_FF = 2816
ROPE_BASE = 10000.0
EPS = 1e-6
D_IN_PROJ = D_FOURIER + 5 * D_RET
LOG_GAMMA_FWD = np.log(1.0 - 2.0 ** (-5.0 - np.arange(N_RET_HEADS))).astype(np.float32)
LOG_GAMMA_BWD = np.log(1.0 - 2.0 ** (-5.5 - np.arange(N_RET_HEADS))).astype(np.float32)

LANES = 128
BF16_ROWS = 16
TOKEN_BLOCK = 256
SLOT_WINDOW = 64
PACK = TOKEN_BLOCK // SLOT_WINDOW
GATHER_UNROLL = 4
Y_CHUNKS = 4
FF_TILE = 256
N_FF_TILES = D_EXPERT_FF // FF_TILE
MOD_TILE = 1024
MIXER_ROWS = 1024
POST_ROWS = 512
POST_PART = 256
COMBINE_ROWS = 512
VMEM_LIMIT = 56 * 1024 * 1024
_ROW_STREAM_VMEM = 10 * POST_ROWS * D_MODEL * 4

F32 = jnp.float32
BF16 = jnp.bfloat16


def _dot(a, b):
    return jnp.dot(a, b, preferred_element_type=F32)


def _dot_nt(a, b):
    return lax.dot_general(a, b, (((1,), (1,)), ((), ())), preferred_element_type=F32)


def _silu(x):
    return x * jax.nn.sigmoid(x)


def _mod_kernel(condt_ref, w_ref, b_ref, out_ref, *, n_cond):
    s = _silu(condt_ref[...])
    w = w_ref[...]
    out_ref[...] = jnp.zeros(out_ref.shape, F32)
    for r in range(n_cond):
        out_ref[r] = jnp.sum(w * s[:, r:r + 1], axis=0, keepdims=True) + b_ref[...]


def _modulation(cond_rows, w_mod, b_mod):
    n_cond = cond_rows.shape[0]
    condt = jnp.zeros((D_MODEL, 8), F32).at[:, :n_cond].set(cond_rows.T)
    n_out = w_mod.shape[1]
    return pl.pallas_call(
        functools.partial(_mod_kernel, n_cond=n_cond),
        out_shape=jax.ShapeDtypeStruct((8, 1, n_out), F32),
        grid=(n_out // MOD_TILE,),
        in_specs=[
            pl.BlockSpec((D_MODEL, 8), lambda j: (0, 0)),
            pl.BlockSpec((D_MODEL, MOD_TILE), lambda j: (0, j)),
            pl.BlockSpec((1, MOD_TILE), lambda j: (0, j)),
        ],
        out_specs=pl.BlockSpec((8, 1, MOD_TILE), lambda j: (0, 0, j)),
        compiler_params=pltpu.CompilerParams(dimension_semantics=("arbitrary",)),
        name="mod",
    )(condt, w_mod, b_mod.reshape(1, n_out))


def _rms(x):
    return x * lax.rsqrt(jnp.mean(x * x, axis=-1, keepdims=True) + EPS)


def _groupnorm(o):
    mu = jnp.mean(o, axis=-1, keepdims=True)
    c = o - mu
    return c * lax.rsqrt(jnp.mean(c * c, axis=-1, keepdims=True) + EPS)


def _split_hi_lo(x):
    hi = x.astype(BF16)
    lo = (x - hi.astype(F32)).astype(BF16)
    return hi, lo


def _mixer_kernel(*refs, n, use_rope, has_state_in, emit_state):
    it = iter(refs)
    x_ref, mod_ref, g1_ref, win_ref, wfmix_ref = (next(it) for _ in range(5))
    cw_ref, cn_ref, sn_ref, dmat_ref, qdec_ref, kdec_ref, sdec_ref = (next(it) for _ in range(7))
    cos_ref = sin_ref = s0_ref = st_ref = None
    if use_rope:
        cos_ref, sin_ref = next(it), next(it)
    if has_state_in:
        s0_ref = next(it)
    mix_ref = next(it)
    if emit_state:
        st_ref = next(it)
    p_ref, of_ref, ob_ref = next(it), next(it), next(it)

    n_seq = MIXER_ROWS // n
    chunks_per_seq = n // CHUNK
    mod = mod_ref[0]
    shift1 = mod[:, 0:D_MODEL]
    scale1 = mod[:, D_MODEL:2 * D_MODEL]

    h = (_rms(x_ref[...]) * g1_ref[...] * (1.0 + scale1) + shift1).astype(BF16)
    for j in range(D_IN_PROJ // 512):
        p_ref[:, j * 512:(j + 1) * 512] = _dot(h, win_ref[:, j * 512:(j + 1) * 512].astype(BF16))

    xf = p_ref[:, 0:D_FOURIER].astype(BF16)
    xc, xs = [], []
    cw = cw_ref[...].astype(BF16)
    for g in range(N_FOURIER_GROUPS):
        t = _dot(xf[:, g * FOURIER_GROUP_W:(g + 1) * FOURIER_GROUP_W], cw)
        xc.append(t[:, :FOURIER_GROUP_W].astype(BF16))
        xs.append(t[:, FOURIER_GROUP_W:].astype(BF16))
    xc = jnp.concatenate(xc, axis=1)
    xs = jnp.concatenate(xs, axis=1)
    cn = cn_ref[...].astype(BF16)
    sn = sn_ref[...].astype(BF16)
    for s in range(n_seq):
        rs = slice(s * n, (s + 1) * n)
        fre = (_dot(cn, xc[rs]) - _dot(sn, xs[rs])) * (1.0 / math.sqrt(n * FOURIER_GROUP_W))
        fre = fre.astype(BF16)
        for g in range(N_FOURIER_GROUPS):
            sl = slice(g * FOURIER_GROUP_W, (g + 1) * FOURIER_GROUP_W)
            mix_ref[rs, sl] = _dot(fre[:, sl], wfmix_ref[g].astype(BF16)).astype(BF16)

    for hh in range(N_RET_HEADS):
        base = D_FOURIER + hh * HEAD_DIM
        q = p_ref[:, base:base + HEAD_DIM]
        k = p_ref[:, base + D_RET:base + D_RET + HEAD_DIM]
        v = p_ref[:, base + 2 * D_RET:base + 2 * D_RET + HEAD_DIM]
        if use_rope:
            lane = lax.broadcasted_iota(jnp.int32, (MIXER_ROWS, HEAD_DIM), 1)
            first = (lane % 64) < 32

            def rope(t):
                swapped = jnp.where(first, pltpu.roll(t, HEAD_DIM - 32, 1), pltpu.roll(t, 32, 1))
                return t * cos_ref[...] + swapped * sin_ref[...]

            q, k = rope(q), rope(k)
        k = k * (HEAD_DIM ** -0.5)
        qb, vb = q.astype(BF16), v.astype(BF16)
        kb = k.astype(BF16)

        def initial(s, direction):
            if has_state_in:
                return s0_ref[s, direction, hh]
            return jnp.zeros((HEAD_DIM, HEAD_DIM), F32)

        for s in range(n_seq):
            parts = []
            for ci in range(chunks_per_seq):
                c = s * chunks_per_seq + ci
                rs = slice(c * CHUNK, (c + 1) * CHUNK)
                qc, kc, vc = qb[rs], kb[rs], vb[rs]
                qk = _dot_nt(qc, kc)
                lhs = jnp.concatenate([(qk * dmat_ref[0, hh]).astype(BF16),
                                       (qk * dmat_ref[1, hh]).astype(BF16),
                                       (k[rs] * kdec_ref[0, hh]).T.astype(BF16),
                                       (k[rs] * kdec_ref[1, hh]).T.astype(BF16)], axis=0)
                parts.append((rs, qc, _dot(lhs, vc)))
            sf = initial(s, 0)
            for ci in range(chunks_per_seq):
                rs, qc, r = parts[ci]
                o = r[0:CHUNK]
                if has_state_in or ci > 0:
                    o = o + qdec_ref[0, hh] * _dot(qc, sf.astype(BF16))
                of_ref[rs, :] = o
                sf = sf * sdec_ref[0, hh] + r[2 * CHUNK:2 * CHUNK + HEAD_DIM]
            sb = initial(s, 1)
            for ci in reversed(range(chunks_per_seq)):
                rs, qc, r = parts[ci]
                o = r[CHUNK:2 * CHUNK]
                if has_state_in or ci < chunks_per_seq - 1:
                    o = o + qdec_ref[1, hh] * _dot(qc, sb.astype(BF16))
                ob_ref[rs, :] = o
                sb = sb * sdec_ref[1, hh] + r[2 * CHUNK + HEAD_DIM:]
            if emit_state:
                st_ref[s, 0, hh] = sf
                st_ref[s, 1, hh] = sb

        gf = p_ref[:, base + 3 * D_RET:base + 3 * D_RET + HEAD_DIM]
        gb = p_ref[:, base + 4 * D_RET:base + 4 * D_RET + HEAD_DIM]
        y = _silu(gf) * _groupnorm(of_ref[...]) + _silu(gb) * _groupnorm(ob_ref[...])
        mix_ref[:, base:base + HEAD_DIM] = y.astype(BF16)


def _post_kernel(x_ref, mix_ref, mod_ref, g2_ref, wout_ref, wr_ref, x1_ref, h2_ref, aff_ref):
    mod = mod_ref[0]
    gate1 = mod[:, 2 * D_MODEL:3 * D_MODEL]
    shift2 = mod[:, 3 * D_MODEL:4 * D_MODEL]
    scale2 = mod[:, 4 * D_MODEL:5 * D_MODEL]
    wr_hi, wr_lo = _split_hi_lo(wr_ref[...])
    wr_both = jnp.concatenate([wr_hi, wr_lo], axis=1)
    parts = [slice(p * POST_PART, (p + 1) * POST_PART) for p in range(POST_ROWS // POST_PART)]
    x1 = []
    for rows in parts:
        x1.append(x_ref[rows, :] + gate1 * _dot(mix_ref[rows, :], wout_ref[...]))
        x1_ref[rows, :] = x1[-1]
    for rows, x1_part in zip(parts, x1):
        h2 = _rms(x1_part) * g2_ref[...] * (1.0 + scale2) + shift2
        h2_hi, h2_lo = _split_hi_lo(h2)
        h2_ref[rows, :] = h2_hi
        by_hi = _dot(h2_hi, wr_both)
        logits = by_hi[:, :N_EXPERTS] + (_dot(h2_lo, wr_hi) + by_hi[:, N_EXPERTS:])
        z = jnp.exp(logits - jnp.max(logits, axis=-1, keepdims=True))
        aff = z / jnp.sum(z, axis=-1, keepdims=True)
        lanes = jnp.concatenate([aff, jnp.zeros((POST_PART, LANES - N_EXPERTS), F32)], axis=1)
        aff_ref[:, rows] = lanes.T[0:N_EXPERTS, :]


def _dft_consts(n):
    w = FOURIER_GROUP_W
    jw = np.arange(w)
    angw = 2.0 * np.pi * np.outer(jw, jw) / w
    cw = np.concatenate([np.cos(angw), np.sin(angw)], axis=1)
    jn = np.arange(n)
    angn = 2.0 * np.pi * (np.outer(jn, jn) % n) / n
    return (jnp.asarray(cw, F32), jnp.asarray(np.cos(angn), F32), jnp.asarray(np.sin(angn), F32))


def _retention_consts():
    i = np.arange(CHUNK, dtype=np.float64)
    diff = i[:, None] - i[None, :]
    dmat = np.zeros((2, N_RET_HEADS, CHUNK, CHUNK))
    qdec = np.zeros((2, N_RET_HEADS, CHUNK, HEAD_DIM))
    kdec = np.zeros((2, N_RET_HEADS, CHUNK, HEAD_DIM))
    sdec = np.zeros((2, N_RET_HEADS, HEAD_DIM, HEAD_DIM))
    for hh in range(N_RET_HEADS):
        lf = float(LOG_GAMMA_FWD[hh])
        lb = float(LOG_GAMMA_BWD[hh])
        dmat[0, hh] = np.where(diff >= 0, np.exp(lf * np.maximum(diff, 0.0)), 0.0)
        dmat[1, hh] = np.where(diff <= 0, np.exp(lb * np.maximum(-diff, 0.0)), 0.0)
        qdec[0, hh] = np.exp(lf * (i + 1.0))[:, None]
        qdec[1, hh] = np.exp(lb * (CHUNK - i))[:, None]
        kdec[0, hh] = np.exp(lf * (CHUNK - 1.0 - i))[:, None]
        kdec[1, hh] = np.exp(lb * i)[:, None]
        sdec[0, hh] = math.exp(lf * CHUNK)
        sdec[1, hh] = math.exp(lb * CHUNK)
    return tuple(jnp.asarray(a, F32) for a in (dmat, qdec, kdec, sdec))


def _rope_consts(n):
    rows_n = n // GRID_W
    row = np.repeat(np.arange(rows_n, dtype=np.float64), GRID_W)
    col = np.tile(np.arange(GRID_W, dtype=np.float64), rows_n)
    n_pairs = HEAD_DIM // 4
    freqs = (np.float32(ROPE_BASE) ** (-np.arange(n_pairs, dtype=np.float32) / n_pairs)).astype(np.float64)
    ar = row[:, None] * freqs[None, :]
    ac = col[:, None] * freqs[None, :]
    cos = np.concatenate([np.cos(ar), np.cos(ar), np.cos(ac), np.cos(ac)], axis=1)
    sin = np.concatenate([-np.sin(ar), np.sin(ar), -np.sin(ac), np.sin(ac)], axis=1)
    return jnp.asarray(cos, F32), jnp.asarray(sin, F32)


def _const_spec(shape):
    nd = len(shape)
    return pl.BlockSpec(shape, lambda b, _nd=nd: (0,) * _nd, pipeline_mode=pl.Buffered(1))


def _mod_row_spec(first_row, blocks_per_row):
    if blocks_per_row is None:
        return pl.BlockSpec((1, 1, 6 * D_MODEL), lambda b, *_: (first_row, 0, 0))
    return pl.BlockSpec((1, 1, 6 * D_MODEL), lambda b, *_: (first_row + b // blocks_per_row, 0, 0))


def _mixer(x, mod_rows, mod_first_row, mod_per_batch, state_in, emit_state, use_rope, g1, w_in,
           w_fmix):
    nb, n, _ = x.shape
    assert MIXER_ROWS % n == 0 and (nb * n) % MIXER_ROWS == 0
    n_seq = MIXER_ROWS // n
    has_state_in = state_in is not None
    cw, cn, sn = _dft_consts(n)
    dmat, qdec, kdec, sdec = _retention_consts()
    consts = [cw, cn, sn, dmat, qdec, kdec, sdec]
    if use_rope:
        assert n_seq == 1
        consts += list(_rope_consts(n))
    weights = [g1.reshape(1, D_MODEL), w_in, w_fmix]

    if mod_per_batch:
        assert n % MIXER_ROWS == 0
    state_spec = pl.BlockSpec((n_seq, 2, N_RET_HEADS, HEAD_DIM, HEAD_DIM), lambda b: (b, 0, 0, 0, 0))
    row_spec = pl.BlockSpec((MIXER_ROWS, D_MODEL), lambda b: (b, 0))
    in_specs = [row_spec, _mod_row_spec(mod_first_row, n // MIXER_ROWS if mod_per_batch else None)]
    in_specs += [_const_spec(a.shape) for a in weights + consts]
    args = [x.reshape(nb * n, D_MODEL), mod_rows] + weights + consts
    if has_state_in:
        in_specs.append(state_spec)
        args.append(state_in)

    out_shape = [jax.ShapeDtypeStruct((nb * n, D_MODEL), BF16)]
    out_specs = [row_spec]
    if emit_state:
        out_shape.append(jax.ShapeDtypeStruct((nb, 2, N_RET_HEADS, HEAD_DIM, HEAD_DIM), F32))
        out_specs.append(state_spec)

    return pl.pallas_call(
        functools.partial(_mixer_kernel, n=n, use_rope=use_rope, has_state_in=has_state_in,
                          emit_state=emit_state),
        out_shape=out_shape,
        grid=(nb * n // MIXER_ROWS,),
        in_specs=in_specs,
        out_specs=out_specs,
        scratch_shapes=[pltpu.VMEM((MIXER_ROWS, D_IN_PROJ), F32),
                        pltpu.VMEM((MIXER_ROWS, HEAD_DIM), F32), pltpu.VMEM((MIXER_ROWS, HEAD_DIM), F32)],
        compiler_params=pltpu.CompilerParams(dimension_semantics=("arbitrary",),
                                             vmem_limit_bytes=VMEM_LIMIT),
        name="mixer_rope" if use_rope else "mixer",
    )(*args)


def _post(x, mix, mod_rows, mod_first_row, tokens_per_mod_row, g2, w_out_bf, w_router):
    t = x.shape[0]
    assert tokens_per_mod_row is None or tokens_per_mod_row % POST_ROWS == 0
    blocks_per_row = None if tokens_per_mod_row is None else tokens_per_mod_row // POST_ROWS
    row_spec = pl.BlockSpec((POST_ROWS, D_MODEL), lambda b: (b, 0))
    return pl.pallas_call(
        _post_kernel,
        out_shape=[jax.ShapeDtypeStruct((t, D_MODEL), F32),
                   jax.ShapeDtypeStruct((t, D_MODEL), BF16),
                   jax.ShapeDtypeStruct((N_EXPERTS, t), F32)],
        grid=(t // POST_ROWS,),
        in_specs=[row_spec, row_spec,
                  _mod_row_spec(mod_first_row, blocks_per_row),
                  _const_spec((1, D_MODEL)), _const_spec((D_MODEL, D_MODEL)),
                  _const_spec((D_MODEL, N_EXPERTS))],
        out_specs=[row_spec, row_spec, pl.BlockSpec((N_EXPERTS, POST_ROWS), lambda b: (0, b))],
        compiler_params=pltpu.CompilerParams(dimension_semantics=("arbitrary",),
                                             vmem_limit_bytes=_ROW_STREAM_VMEM),
        name="post",
    )(x, mix, mod_rows, g2.reshape(1, D_MODEL), w_out_bf, w_router)


def _route_kernel(*refs, sizes):
    n = len(sizes)
    affs = [ref[...] for ref in refs[:n]]
    u_ref = refs[n]
    outs = [refs[n + 1 + 4 * g:n + 5 + 4 * g] for g in range(n)]

    def count(mask):
        return jnp.sum(mask.astype(jnp.int32), axis=1, keepdims=True)

    def as_float(word):
        return lax.bitcast_convert_type(word, F32)

    def zeros():
        return tuple(jnp.zeros((N_EXPERTS, 1), jnp.int32) for _ in sizes)

    def value_step(i, curs):
        bit = jnp.left_shift(jnp.int32(1), 30 - i)
        return tuple(jnp.where(count(aff >= as_float(cur | bit)) >= cap, cur | bit, cur)
                     for aff, (_, cap), cur in zip(affs, sizes, curs))

    thrs = lax.fori_loop(0, 31, value_step, zeros())
    gts = [aff >= as_float(thr + 1) for aff, thr in zip(affs, thrs)]
    eqs = [(aff >= as_float(thr)) & jnp.logical_not(gt) for aff, thr, gt in zip(affs, thrs, gts)]
    needs = [cap - count(gt) for (_, cap), gt in zip(sizes, gts)]
    toks = [lax.broadcasted_iota(jnp.int32, (N_EXPERTS, t), 1) for t, _ in sizes]
    nbits = [t.bit_length() - 1 for t, _ in sizes]

    def index_step(i, curs):
        new = []
        for eq, need, tok, bits, cur in zip(eqs, needs, toks, nbits, curs):
            shift = bits - 1 - i
            cand = cur | jnp.where(shift >= 0, jnp.left_shift(jnp.int32(1), jnp.maximum(shift, 0)), 0)
            new.append(jnp.where(count(eq & (tok < cand)) < need, cand, cur))
        return tuple(new)

    lasts = lax.fori_loop(0, max(nbits), index_step, zeros())

    for aff, (t, _), gt, eq, tok, last, (slot_ref, slot_te_ref, gate_ref, starts_ref) in zip(
            affs, sizes, gts, eqs, toks, lasts, outs):
        self = jnp.where(gt | (eq & (tok <= last)), 1.0, 0.0).astype(F32)
        carry = jnp.zeros((N_EXPERTS, 1), F32)
        starts_ref[...] = jnp.zeros(starts_ref.shape, jnp.int32)
        for b in range(t // TOKEN_BLOCK):
            sl = slice(b * TOKEN_BLOCK, (b + 1) * TOKEN_BLOCK)
            sbf = self[:, sl]
            pre = _dot(sbf.astype(BF16), u_ref[...]) + carry
            slots = jnp.where(sbf > 0.5, pre.astype(jnp.int32), -1)
            slot_ref[:, b, :] = slots
            rows = jnp.concatenate([slots, jnp.zeros((LANES - N_EXPERTS, TOKEN_BLOCK), jnp.int32)],
                                   axis=0)
            slot_te_ref[sl, :] = rows.T[:, 0:N_EXPERTS]
            gate_ref[:, b, :] = aff[:, sl]
            starts_ref[:, b:b + 1] = carry.astype(jnp.int32)
            carry = carry + jnp.sum(sbf, axis=1, keepdims=True)
        nblk = t // TOKEN_BLOCK
        starts_ref[:, nblk:nblk + 1] = carry.astype(jnp.int32)


def _route(*affs_et):
    sizes = tuple((a.shape[1], EC_CAPACITY_FACTOR * a.shape[1] // N_EXPERTS) for a in affs_et)
    upper = jnp.asarray(np.triu(np.ones((TOKEN_BLOCK, TOKEN_BLOCK)), 1), BF16)
    whole = lambda shape: pl.BlockSpec(shape, lambda i, _n=len(shape): (0,) * _n)
    out_shape, out_specs = [], []
    for t, _ in sizes:
        nblk = t // TOKEN_BLOCK
        assert nblk + 1 <= LANES
        for shape, dtype in (((N_EXPERTS, nblk, TOKEN_BLOCK), jnp.int32), ((t, N_EXPERTS), jnp.int32),
                             ((N_EXPERTS, nblk, TOKEN_BLOCK), F32), ((N_EXPERTS, LANES), jnp.int32)):
            out_shape.append(jax.ShapeDtypeStruct(shape, dtype))
            out_specs.append(whole(shape))
    outs = pl.pallas_call(
        functools.partial(_route_kernel, sizes=sizes),
        out_shape=out_shape,
        grid=(1,),
        in_specs=[whole(a.shape) for a in affs_et] + [whole((TOKEN_BLOCK, TOKEN_BLOCK))],
        out_specs=out_specs,
        compiler_params=pltpu.CompilerParams(dimension_semantics=("arbitrary",)),
        name="route",
    )(*affs_et, upper)
    return [outs[4 * g:4 * g + 4] for g in range(len(sizes))]


def _pack_windows(starts_ref, b, experts, cap):
    first = [jnp.minimum((starts_ref[e, b] // BF16_ROWS) * BF16_ROWS, cap - SLOT_WINDOW)
             for e in experts]
    rows = [jnp.where(starts_ref[e, b + 1] > starts_ref[e, b], starts_ref[e, b + 1] - w, 0)
            for e, w in zip(experts, first)]
    return first, pl.cdiv(functools.reduce(jnp.maximum, rows), SLOT_WINDOW)


def _block_copy(hbm_ref, vmem_ref, sem_ref, b):
    rows = pl.ds(pl.multiple_of(b * TOKEN_BLOCK, TOKEN_BLOCK), TOKEN_BLOCK)
    return pltpu.make_async_copy(hbm_ref.at[rows], vmem_ref.at[rows], sem_ref.at[b])


def _gather_group(g, starts_ref, slot_ref, gate_ref, h2_ref, xs_ref, gs_ref, row0, t, cap, arrive):
    sub = lax.broadcasted_iota(jnp.int32, (SLOT_WINDOW, TOKEN_BLOCK), 0)
    experts = [g * PACK + j for j in range(PACK)]
    assert (t // TOKEN_BLOCK) % GATHER_UNROLL == 0

    def window(b, first, i):
        hb = pl.ds(pl.multiple_of(b * TOKEN_BLOCK, TOKEN_BLOCK), TOKEN_BLOCK)
        hits, dst = [], []
        for j in range(PACK):
            lo = first[j] + i * SLOT_WINDOW
            w = jnp.minimum(lo, cap - SLOT_WINDOW)
            srow = slot_ref[j, pl.ds(b, 1), :]
            hits.append((srow == w + sub) & (srow >= lo))
            dst.append(pl.ds(pl.multiple_of(row0 + w, BF16_ROWS), SLOT_WINDOW))
        onehot = jnp.concatenate([jnp.where(h, 1.0, 0.0) for h in hits], axis=0).astype(BF16)
        got = _dot(onehot, h2_ref[hb, :])
        for j in range(PACK):
            piece = got[j * SLOT_WINDOW:(j + 1) * SLOT_WINDOW].astype(BF16)
            xs_ref[j, dst[j], :] = xs_ref[j, dst[j], :] + piece
            grow = gate_ref[j, pl.ds(b, 1), :]
            gs_ref[j, dst[j], :] += jnp.sum(jnp.where(hits[j], grow, 0.0), axis=1, keepdims=True)

    def blocks(q, carry):
        pending = []
        for u in range(GATHER_UNROLL):
            arrive(q * GATHER_UNROLL + u)
        for u in range(GATHER_UNROLL):
            b = q * GATHER_UNROLL + u
            first, n_windows = _pack_windows(starts_ref, b, experts, cap)
            window(b, first, 0)
            pending.append((b, first, n_windows))
        for b, first, n_windows in pending:
            def more(i, carry, b=b, first=first):
                window(b, first, i)
                return carry

            lax.fori_loop(1, n_windows, more, 0)
        return carry

    lax.fori_loop(0, t // TOKEN_BLOCK // GATHER_UNROLL, blocks, 0)


def _experts_kernel(sp_ref, ss_ref, h2p_hbm, h2s_hbm, slotp_ref, slots_ref, gatep_ref, gates_ref,
                    wg_ref, wu_ref, wd_ref, yp_ref, ys_ref, xs_ref, gs_ref, acc_ref,
                    h2p_ref, h2s_ref, semp_ref, sems_ref, *, tp, ts, capp, caps):
    g = pl.program_id(0)
    step = pl.program_id(1)
    j = step // N_FF_TILES
    f = step % N_FF_TILES
    loading = [(h2p_hbm, h2p_ref, semp_ref, tp // TOKEN_BLOCK),
               (h2s_hbm, h2s_ref, sems_ref, ts // TOKEN_BLOCK)]

    @pl.when((step == 0) & (g == 0))
    def _():
        for hbm_ref, vmem_ref, sem_ref, n_blocks in loading:
            for b in range(n_blocks):
                _block_copy(hbm_ref, vmem_ref, sem_ref, b).start()

    def arrive(hbm_ref, vmem_ref, sem_ref, _):
        def wait(b):
            @pl.when(g == 0)
            def _():
                _block_copy(hbm_ref, vmem_ref, sem_ref, b).wait()

        return wait

    @pl.when(step == 0)
    def _():
        xs_ref[...] = jnp.zeros(xs_ref.shape, BF16)
        gs_ref[...] = jnp.zeros(gs_ref.shape, F32)
        _gather_group(g, sp_ref, slotp_ref, gatep_ref, h2p_ref, xs_ref, gs_ref, 0, tp, capp,
                      arrive(*loading[0]))
        _gather_group(g, ss_ref, slots_ref, gates_ref, h2s_ref, xs_ref, gs_ref, capp, ts, caps,
                      arrive(*loading[1]))

    @pl.when(f == 0)
    def _():
        acc_ref[...] = jnp.zeros(acc_ref.shape, F32)

    x = xs_ref[j]
    a = _dot(x, wg_ref[0].astype(BF16))
    u = _dot(x, wu_ref[0].astype(BF16))
    acc_ref[...] += _dot((_silu(a) * u).astype(BF16), wd_ref[0].astype(BF16))

    @pl.when(f == N_FF_TILES - 1)
    def _():
        yp_ref[0] = (acc_ref[0:capp, :] * gs_ref[j, 0:capp, :]).astype(BF16)
        ys_ref[0] = (acc_ref[capp:capp + caps, :] * gs_ref[j, capp:capp + caps, :]).astype(BF16)


def _experts(starts_p, starts_s, h2p, h2s, slot_p, slot_s, gate_p, gate_s, w_gate, w_up, w_down):
    tp, ts = h2p.shape[0], h2s.shape[0]
    capp = EC_CAPACITY_FACTOR * tp // N_EXPERTS
    caps = EC_CAPACITY_FACTOR * ts // N_EXPERTS
    rows = capp + caps
    nbp, nbs = tp // TOKEN_BLOCK, ts // TOKEN_BLOCK
    expert = lambda g, s: g * PACK + s // N_FF_TILES
    grid_spec = pltpu.PrefetchScalarGridSpec(
        num_scalar_prefetch=2,
        grid=(N_EXPERTS // PACK, PACK * N_FF_TILES),
        in_specs=[
            pl.BlockSpec(memory_space=pl.ANY),
            pl.BlockSpec(memory_space=pl.ANY),
            pl.BlockSpec((PACK, nbp, TOKEN_BLOCK), lambda g, s, *_: (g, 0, 0)),
            pl.BlockSpec((PACK, nbs, TOKEN_BLOCK), lambda g, s, *_: (g, 0, 0)),
            pl.BlockSpec((PACK, nbp, TOKEN_BLOCK), lambda g, s, *_: (g, 0, 0)),
            pl.BlockSpec((PACK, nbs, TOKEN_BLOCK), lambda g, s, *_: (g, 0, 0)),
            pl.BlockSpec((1, D_MODEL, FF_TILE), lambda g, s, *_: (expert(g, s), 0, s % N_FF_TILES)),
            pl.BlockSpec((1, D_MODEL, FF_TILE), lambda g, s, *_: (expert(g, s), 0, s % N_FF_TILES)),
            pl.BlockSpec((1, FF_TILE, D_MODEL), lambda g, s, *_: (expert(g, s), s % N_FF_TILES, 0)),
        ],
        out_specs=[
            pl.BlockSpec((1, capp, D_MODEL), lambda g, s, *_: (expert(g, s), 0, 0)),
            pl.BlockSpec((1, caps, D_MODEL), lambda g, s, *_: (expert(g, s), 0, 0)),
        ],
        scratch_shapes=[pltpu.VMEM((PACK, rows, D_MODEL), BF16), pltpu.VMEM((PACK, rows, 1), F32),
                        pltpu.VMEM((rows, D_MODEL), F32),
                        pltpu.VMEM((tp, D_MODEL), BF16), pltpu.VMEM((ts, D_MODEL), BF16),
                        pltpu.SemaphoreType.DMA((nbp,)), pltpu.SemaphoreType.DMA((nbs,))],
    )
    return pl.pallas_call(
        functools.partial(_experts_kernel, tp=tp, ts=ts, capp=capp, caps=caps),
        out_shape=[jax.ShapeDtypeStruct((N_EXPERTS, capp, D_MODEL), BF16),
                   jax.ShapeDtypeStruct((N_EXPERTS, caps, D_MODEL), BF16)],
        grid_spec=grid_spec,
        compiler_params=pltpu.CompilerParams(dimension_semantics=("arbitrary", "arbitrary"),
                                             vmem_limit_bytes=VMEM_LIMIT),
        name="experts",
    )(starts_p, starts_s, h2p, h2s, slot_p, slot_s, gate_p, gate_s, w_gate, w_up, w_down)


def _combine_kernel(st_ref, x1_ref, slot_ref, mod_ref, fg_ref, y_hbm, out_ref, acc_ref, y_ref,
                    sem_ref, arrived_ref, *, cap):
    step = pl.program_id(0)
    n_chunks = Y_CHUNKS
    chunk_rows = cap // Y_CHUNKS
    n_parts = COMBINE_ROWS // TOKEN_BLOCK
    last_block = step * n_parts + n_parts - 1

    def chunk_copy(c):
        rows = pl.ds(c * chunk_rows, chunk_rows)
        return pltpu.make_async_copy(y_hbm.at[:, rows, :], y_ref.at[:, rows, :], sem_ref.at[c])

    @pl.when(step == 0)
    def _():
        arrived_ref[0] = 0
        for c in range(n_chunks):
            chunk_copy(c).start()

    top = functools.reduce(jnp.maximum, [st_ref[e, last_block + 1] for e in range(N_EXPERTS)])
    want = pl.cdiv(jnp.minimum(top + SLOT_WINDOW, cap), chunk_rows)
    have = arrived_ref[0]
    for c in range(n_chunks):
        @pl.when((c >= have) & (c < want))
        def _(c=c):
            chunk_copy(c).wait()
    arrived_ref[0] = jnp.maximum(have, want)

    lane = lax.broadcasted_iota(jnp.int32, (TOKEN_BLOCK, PACK * SLOT_WINDOW), 1)
    gate2 = mod_ref[0][:, 5 * D_MODEL:6 * D_MODEL]

    def scatter(slot, experts, first, i):
        target = None
        windows = []
        for j, e in enumerate(experts):
            lo = first[j] + i * SLOT_WINDOW
            w = jnp.minimum(lo, cap - SLOT_WINDOW)
            sc = slot[:, e:e + 1]
            col = jnp.where(sc >= lo, sc - w + j * SLOT_WINDOW, -1)
            target = col if target is None else jnp.where(lane < j * SLOT_WINDOW, target, col)
            windows.append(y_ref[e, pl.ds(pl.multiple_of(w, BF16_ROWS), SLOT_WINDOW), :])
        onehot = jnp.where(target == lane, 1.0, 0.0).astype(BF16)
        return _dot(onehot, jnp.concatenate(windows, axis=0))

    def finish(rows):
        x2 = x1_ref[rows, :] + gate2 * acc_ref[rows, :]
        out_ref[rows, :] = _rms(x2) * fg_ref[...]

    parts = []
    for p in range(n_parts):
        rows = slice(p * TOKEN_BLOCK, (p + 1) * TOKEN_BLOCK)
        slot = slot_ref[rows, :]
        groups = []
        total = None
        for g in range(N_EXPERTS // PACK):
            experts = list(range(g * PACK, (g + 1) * PACK))
            first, n_windows = _pack_windows(st_ref, step * n_parts + p, experts, cap)
            groups.append((experts, first, n_windows))
            part = scatter(slot, experts, first, 0)
            total = part if total is None else total + part
        acc_ref[rows, :] = total
        parts.append((rows, slot, groups))
    for rows, _, _ in parts:
        finish(rows)

    for rows, slot, groups in parts:
        most = functools.reduce(jnp.maximum, [n_windows for _, _, n_windows in groups])

        @pl.when(most > 1)
        def _(rows=rows, slot=slot, groups=groups):
            for experts, first, n_windows in groups:
                def more(i, carry, experts=experts, first=first):
                    acc_ref[rows, :] += scatter(slot, experts, first, i)
                    return carry

                lax.fori_loop(1, n_windows, more, 0)
            finish(rows)


def _combine(starts, x1, slot_te, mod_rows, mod_first_row, tokens_per_mod_row, final_g, y):
    t = x1.shape[0]
    cap = y.shape[1]
    assert tokens_per_mod_row is None or tokens_per_mod_row % COMBINE_ROWS == 0
    steps_per_row = None if tokens_per_mod_row is None else tokens_per_mod_row // COMBINE_ROWS
    grid_spec = pltpu.PrefetchScalarGridSpec(
        num_scalar_prefetch=1,
        grid=(t // COMBINE_ROWS,),
        in_specs=[
            pl.BlockSpec((COMBINE_ROWS, D_MODEL), lambda b, *_: (b, 0)),
            pl.BlockSpec((COMBINE_ROWS, N_EXPERTS), lambda b, *_: (b, 0)),
            _mod_row_spec(mod_first_row, steps_per_row),
            pl.BlockSpec((1, D_MODEL), lambda b, *_: (0, 0)),
            pl.BlockSpec(memory_space=pl.ANY),
        ],
        out_specs=pl.BlockSpec((COMBINE_ROWS, D_MODEL), lambda b, *_: (b, 0)),
        scratch_shapes=[pltpu.VMEM((COMBINE_ROWS, D_MODEL), F32), pltpu.VMEM(y.shape, BF16),
                        pltpu.SemaphoreType.DMA((Y_CHUNKS,)), pltpu.SMEM((1,), jnp.int32)],
    )
    return pl.pallas_call(
        functools.partial(_combine_kernel, cap=cap),
        out_shape=jax.ShapeDtypeStruct((t, D_MODEL), F32),
        grid_spec=grid_spec,
        compiler_params=pltpu.CompilerParams(dimension_semantics=("arbitrary",),
                                             vmem_limit_bytes=VMEM_LIMIT),
        name="combine",
    )(starts, x1, slot_te, mod_rows, final_g.reshape(1, D_MODEL), y)


def kernel(x_prompt, x_sample, state_ret, c, c_ctx, norm1_g, norm2_g, final_g, w_mod, b_mod, w_in,
           w_fmix, w_out, w_router, w_gate, w_up, w_down):
    bp, seq, _ = x_prompt.shape
    bs, dec_seq, _ = x_sample.shape
    assert w_mod.shape[0] == 1, "single-layer trunk"
    tp, ts = bp * seq, bs * dec_seq

    cond = jnp.concatenate([c_ctx[None, :], c], axis=0)
    mod = _modulation(cond, w_mod[0], b_mod[0])
    ctx_row, lat_row = 0, 1

    w_out_bf = w_out[0].astype(BF16)
    mix_p, states = _mixer(x_prompt, mod, ctx_row, False, None, True, False, norm1_g[0], w_in[0],
                           w_fmix[0])
    (mix_s,) = _mixer(x_sample, mod, lat_row, True, state_ret[:, 0], False, True, norm1_g[0],
                      w_in[0], w_fmix[0])
    x1p, h2p, affp = _post(x_prompt.reshape(tp, D_MODEL), mix_p, mod, ctx_row, None,
                           norm2_g[0], w_out_bf, w_router[0])
    x1s, h2s, affs = _post(x_sample.reshape(ts, D_MODEL), mix_s, mod, lat_row, dec_seq,
                           norm2_g[0], w_out_bf, w_router[0])

    (slot_p, slot_te_p, gate_p, starts_p), (slot_s, slot_te_s, gate_s, starts_s) = _route(affp, affs)

    yp, ys = _experts(starts_p, starts_s, h2p, h2s, slot_p, slot_s, gate_p, gate_s,
                      w_gate[0], w_up[0], w_down[0])

    out_p = _combine(starts_p, x1p, slot_te_p, mod, ctx_row, None, final_g, yp)
    out_s = _combine(starts_s, x1s, slot_te_s, mod, lat_row, dec_seq, final_g, ys)

    y_prompt = out_p.reshape(bp, seq, D_MODEL)
    y_sample = out_s.reshape(bs, dec_seq, D_MODEL)
    state_new = states.reshape(bp, 1, 2, N_RET_HEADS, HEAD_DIM, HEAD_DIM).astype(x_prompt.dtype)
    return (y_prompt, y_sample, state_new)
```

```python
import functools
import math

import jax
import jax.numpy as jnp
import numpy as np
from jax import lax
from jax.experimental import pallas as pl
from jax.experimental.pallas import tpu as pltpu

D_MODEL = 1024
D_FOURIER = 512
N_FOURIER_GROUPS = 4
FOURIER_GROUP_W = 128
D_RET = 512
N_RET_HEADS = 4
HEAD_DIM = 128
CHUNK = 256
GRID_W = 64
N_EXPERTS = 16
EC_CAPACITY_FACTOR = 2
D_EXPERT_FF = 2816
ROPE_BASE = 10000.0
EPS = 1e-6
D_IN_PROJ = D_FOURIER + 5 * D_RET
LOG_GAMMA_FWD = np.log(1.0 - 2.0 ** (-5.0 - np.arange(N_RET_HEADS))).astype(np.float32)
LOG_GAMMA_BWD = np.log(1.0 - 2.0 ** (-5.5 - np.arange(N_RET_HEADS))).astype(np.float32)

LANES = 128
BF16_ROWS = 16
TOKEN_BLOCK = 256
SLOT_WINDOW = 64
PACK = TOKEN_BLOCK // SLOT_WINDOW
GATHER_UNROLL = 4
Y_CHUNKS = 4
FF_TILE = 256
N_FF_TILES = D_EXPERT_FF // FF_TILE
MOD_TILE = 1024
MIXER_ROWS = 1024
POST_ROWS = 512
POST_PART = 256
POST_BUFFERS = 3
COMBINE_ROWS = 512
VMEM_LIMIT = 56 * 1024 * 1024

F32 = jnp.float32
BF16 = jnp.bfloat16


def _dot(a, b):
    return jnp.dot(a, b, preferred_element_type=F32)


def _dot_nt(a, b):
    return lax.dot_general(a, b, (((1,), (1,)), ((), ())), preferred_element_type=F32)


def _silu(x):
    return x * jax.nn.sigmoid(x)


def _mod_kernel(condt_ref, w_ref, b_ref, out_ref, *, n_cond):
    s = _silu(condt_ref[...])
    w = w_ref[...]
    out_ref[...] = jnp.zeros(out_ref.shape, F32)
    for r in range(n_cond):
        out_ref[r] = jnp.sum(w * s[:, r:r + 1], axis=0, keepdims=True) + b_ref[...]


def _modulation(cond_rows, w_mod, b_mod):
    n_cond = cond_rows.shape[0]
    condt = jnp.zeros((D_MODEL, 8), F32).at[:, :n_cond].set(cond_rows.T)
    n_out = w_mod.shape[1]
    return pl.pallas_call(
        functools.partial(_mod_kernel, n_cond=n_cond),
        out_shape=jax.ShapeDtypeStruct((8, 1, n_out), F32),
        grid=(n_out // MOD_TILE,),
        in_specs=[
            pl.BlockSpec((D_MODEL, 8), lambda j: (0, 0)),
            pl.BlockSpec((D_MODEL, MOD_TILE), lambda j: (0, j)),
            pl.BlockSpec((1, MOD_TILE), lambda j: (0, j)),
        ],
        out_specs=pl.BlockSpec((8, 1, MOD_TILE), lambda j: (0, 0, j)),
        compiler_params=pltpu.CompilerParams(dimension_semantics=("arbitrary",)),
        name="mod",
    )(condt, w_mod, b_mod.reshape(1, n_out))


def _rms(x):
    return x * lax.rsqrt(jnp.mean(x * x, axis=-1, keepdims=True) + EPS)


def _groupnorm(o):
    mu = jnp.mean(o, axis=-1, keepdims=True)
    c = o - mu
    return c * lax.rsqrt(jnp.mean(c * c, axis=-1, keepdims=True) + EPS)


def _split_hi_lo(x):
    hi = x.astype(BF16)
    lo = (x - hi.astype(F32)).astype(BF16)
    return hi, lo


def _mixer_kernel(*refs, n, use_rope, has_state_in, emit_state):
    it = iter(refs)
    x_ref, mod_ref, g1_ref, win_ref, wfmix_ref = (next(it) for _ in range(5))
    cw_ref, cn_ref, sn_ref, dmat_ref, qdec_ref, kdec_ref, sdec_ref = (next(it) for _ in range(7))
    cos_ref = sin_ref = s0_ref = st_ref = None
    if use_rope:
        cos_ref, sin_ref = next(it), next(it)
    if has_state_in:
        s0_ref = next(it)
    mix_ref = next(it)
    if emit_state:
        st_ref = next(it)
    p_ref, of_ref, ob_ref = next(it), next(it), next(it)

    n_seq = MIXER_ROWS // n
    chunks_per_seq = n // CHUNK
    mod = mod_ref[0]
    shift1 = mod[:, 0:D_MODEL]
    scale1 = mod[:, D_MODEL:2 * D_MODEL]

    h = (_rms(x_ref[...]) * g1_ref[...] * (1.0 + scale1) + shift1).astype(BF16)
    for j in range(D_IN_PROJ // 512):
        p_ref[:, j * 512:(j + 1) * 512] = _dot(h, win_ref[:, j * 512:(j + 1) * 512].astype(BF16))

    xf = p_ref[:, 0:D_FOURIER].astype(BF16)
    xc, xs = [], []
    cw = cw_ref[...].astype(BF16)
    for g in range(N_FOURIER_GROUPS):
        t = _dot(xf[:, g * FOURIER_GROUP_W:(g + 1) * FOURIER_GROUP_W], cw)
        xc.append(t[:, :FOURIER_GROUP_W].astype(BF16))
        xs.append(t[:, FOURIER_GROUP_W:].astype(BF16))
    xc = jnp.concatenate(xc, axis=1)
    xs = jnp.concatenate(xs, axis=1)
    cn = cn_ref[...].astype(BF16)
    sn = sn_ref[...].astype(BF16)
    for s in range(n_seq):
        rs = slice(s * n, (s + 1) * n)
        fre = (_dot(cn, xc[rs]) - _dot(sn, xs[rs])) * (1.0 / math.sqrt(n * FOURIER_GROUP_W))
        fre = fre.astype(BF16)
        for g in range(N_FOURIER_GROUPS):
            sl = slice(g * FOURIER_GROUP_W, (g + 1) * FOURIER_GROUP_W)
            mix_ref[rs, sl] = _dot(fre[:, sl], wfmix_ref[g].astype(BF16)).astype(BF16)

    for hh in range(N_RET_HEADS):
        base = D_FOURIER + hh * HEAD_DIM
        q = p_ref[:, base:base + HEAD_DIM]
        k = p_ref[:, base + D_RET:base + D_RET + HEAD_DIM]
        v = p_ref[:, base + 2 * D_RET:base + 2 * D_RET + HEAD_DIM]
        if use_rope:
            lane = lax.broadcasted_iota(jnp.int32, (MIXER_ROWS, HEAD_DIM), 1)
            first = (lane % 64) < 32

            def rope(t):
                swapped = jnp.where(first, pltpu.roll(t, HEAD_DIM - 32, 1), pltpu.roll(t, 32, 1))
                return t * cos_ref[...] + swapped * sin_ref[...]

            q, k = rope(q), rope(k)
        k = k * (HEAD_DIM ** -0.5)
        qb, vb = q.astype(BF16), v.astype(BF16)
        kb = k.astype(BF16)

        def initial(s, direction):
            if has_state_in:
                return s0_ref[s, direction, hh]
            return jnp.zeros((HEAD_DIM, HEAD_DIM), F32)

        for s in range(n_seq):
            parts = []
            for ci in range(chunks_per_seq):
                c = s * chunks_per_seq + ci
                rs = slice(c * CHUNK, (c + 1) * CHUNK)
                qc, kc, vc = qb[rs], kb[rs], vb[rs]
                qk = _dot_nt(qc, kc)
                lhs = jnp.concatenate([(qk * dmat_ref[0, hh]).astype(BF16),
                                       (qk * dmat_ref[1, hh]).astype(BF16),
                                       (k[rs] * kdec_ref[0, hh]).T.astype(BF16),
                                       (k[rs] * kdec_ref[1, hh]).T.astype(BF16)], axis=0)
                parts.append((rs, qc, _dot(lhs, vc)))
            sf = initial(s, 0)
            for ci in range(chunks_per_seq):
                rs, qc, r = parts[ci]
                o = r[0:CHUNK]
                if has_state_in or ci > 0:
                    o = o + qdec_ref[0, hh] * _dot(qc, sf.astype(BF16))
                of_ref[rs, :] = o
                sf = sf * sdec_ref[0, hh] + r[2 * CHUNK:2 * CHUNK + HEAD_DIM]
            sb = initial(s, 1)
            for ci in reversed(range(chunks_per_seq)):
                rs, qc, r = parts[ci]
                o = r[CHUNK:2 * CHUNK]
                if has_state_in or ci < chunks_per_seq - 1:
                    o = o + qdec_ref[1, hh] * _dot(qc, sb.astype(BF16))
                ob_ref[rs, :] = o
                sb = sb * sdec_ref[1, hh] + r[2 * CHUNK + HEAD_DIM:]
            if emit_state:
                st_ref[s, 0, hh] = sf
                st_ref[s, 1, hh] = sb

        gf = p_ref[:, base + 3 * D_RET:base + 3 * D_RET + HEAD_DIM]
        gb = p_ref[:, base + 4 * D_RET:base + 4 * D_RET + HEAD_DIM]
        y = _silu(gf) * _groupnorm(of_ref[...]) + _silu(gb) * _groupnorm(ob_ref[...])
        mix_ref[:, base:base + HEAD_DIM] = y.astype(BF16)


def _post_kernel(x_hbm, mix_hbm, mod_ref, g2_ref, wout_ref, wr_ref, x1_ref, h2_ref, aff_ref,
                 xbuf_ref, mixbuf_ref, xsem_ref, mixsem_ref):
    i = pl.program_id(0)
    ahead = POST_BUFFERS - 1

    def block_copies(b):
        rows = pl.ds(pl.multiple_of(b * POST_ROWS, POST_ROWS), POST_ROWS)
        slot = b % POST_BUFFERS
        return (pltpu.make_async_copy(x_hbm.at[rows], xbuf_ref.at[slot], xsem_ref.at[slot]),
                pltpu.make_async_copy(mix_hbm.at[rows], mixbuf_ref.at[slot], mixsem_ref.at[slot]))

    @pl.when(i == 0)
    def _():
        for b in range(ahead):
            for copy in block_copies(b):
                copy.start()

    @pl.when(i + ahead < pl.num_programs(0))
    def _():
        for copy in block_copies(i + ahead):
            copy.start()

    for copy in block_copies(i):
        copy.wait()
    x_ref = xbuf_ref.at[i % POST_BUFFERS]
    mix_ref = mixbuf_ref.at[i % POST_BUFFERS]
    mod = mod_ref[0]
    gate1 = mod[:, 2 * D_MODEL:3 * D_MODEL]
    shift2 = mod[:, 3 * D_MODEL:4 * D_MODEL]
    scale2 = mod[:, 4 * D_MODEL:5 * D_MODEL]
    wr_hi, wr_lo = _split_hi_lo(wr_ref[...])
    wr_both = jnp.concatenate([wr_hi, wr_lo], axis=1)
    parts = [slice(p * POST_PART, (p + 1) * POST_PART) for p in range(POST_ROWS // POST_PART)]
    x1 = []
    for rows in parts:
        x1.append(x_ref[rows, :] + gate1 * _dot(mix_ref[rows, :], wout_ref[...]))
        x1_ref[rows, :] = x1[-1]
    for rows, x1_part in zip(parts, x1):
        h2 = _rms(x1_part) * g2_ref[...] * (1.0 + scale2) + shift2
        h2_hi, h2_lo = _split_hi_lo(h2)
        h2_ref[rows, :] = h2_hi
        by_hi = _dot(h2_hi, wr_both)
        logits = by_hi[:, :N_EXPERTS] + (_dot(h2_lo, wr_hi) + by_hi[:, N_EXPERTS:])
        z = jnp.exp(logits - jnp.max(logits, axis=-1, keepdims=True))
        aff = z / jnp.sum(z, axis=-1, keepdims=True)
        lanes = jnp.concatenate([aff, jnp.zeros((POST_PART, LANES - N_EXPERTS), F32)], axis=1)
        aff_ref[:, rows] = lanes.T[0:N_EXPERTS, :]


def _dft_consts(n):
    w = FOURIER_GROUP_W
    jw = np.arange(w)
    angw = 2.0 * np.pi * np.outer(jw, jw) / w
    cw = np.concatenate([np.cos(angw), np.sin(angw)], axis=1)
    jn = np.arange(n)
    angn = 2.0 * np.pi * (np.outer(jn, jn) % n) / n
    return (jnp.asarray(cw, F32), jnp.asarray(np.cos(angn), F32), jnp.asarray(np.sin(angn), F32))


def _retention_consts():
    i = np.arange(CHUNK, dtype=np.float64)
    diff = i[:, None] - i[None, :]
    dmat = np.zeros((2, N_RET_HEADS, CHUNK, CHUNK))
    qdec = np.zeros((2, N_RET_HEADS, CHUNK, HEAD_DIM))
    kdec = np.zeros((2, N_RET_HEADS, CHUNK, HEAD_DIM))
    sdec = np.zeros((2, N_RET_HEADS, HEAD_DIM, HEAD_DIM))
    for hh in range(N_RET_HEADS):
        lf = float(LOG_GAMMA_FWD[hh])
        lb = float(LOG_GAMMA_BWD[hh])
        dmat[0, hh] = np.where(diff >= 0, np.exp(lf * np.maximum(diff, 0.0)), 0.0)
        dmat[1, hh] = np.where(diff <= 0, np.exp(lb * np.maximum(-diff, 0.0)), 0.0)
        qdec[0, hh] = np.exp(lf * (i + 1.0))[:, None]
        qdec[1, hh] = np.exp(lb * (CHUNK - i))[:, None]
        kdec[0, hh] = np.exp(lf * (CHUNK - 1.0 - i))[:, None]
        kdec[1, hh] = np.exp(lb * i)[:, None]
        sdec[0, hh] = math.exp(lf * CHUNK)
        sdec[1, hh] = math.exp(lb * CHUNK)
    return tuple(jnp.asarray(a, F32) for a in (dmat, qdec, kdec, sdec))


def _rope_consts(n):
    rows_n = n // GRID_W
    row = np.repeat(np.arange(rows_n, dtype=np.float64), GRID_W)
    col = np.tile(np.arange(GRID_W, dtype=np.float64), rows_n)
    n_pairs = HEAD_DIM // 4
    freqs = (np.float32(ROPE_BASE) ** (-np.arange(n_pairs, dtype=np.float32) / n_pairs)).astype(np.float64)
    ar = row[:, None] * freqs[None, :]
    ac = col[:, None] * freqs[None, :]
    cos = np.concatenate([np.cos(ar), np.cos(ar), np.cos(ac), np.cos(ac)], axis=1)
    sin = np.concatenate([-np.sin(ar), np.sin(ar), -np.sin(ac), np.sin(ac)], axis=1)
    return jnp.asarray(cos, F32), jnp.asarray(sin, F32)


def _const_spec(shape):
    nd = len(shape)
    return pl.BlockSpec(shape, lambda b, _nd=nd: (0,) * _nd, pipeline_mode=pl.Buffered(1))


def _mod_row_spec(first_row, blocks_per_row):
    if blocks_per_row is None:
        return pl.BlockSpec((1, 1, 6 * D_MODEL), lambda b, *_: (first_row, 0, 0))
    return pl.BlockSpec((1, 1, 6 * D_MODEL), lambda b, *_: (first_row + b // blocks_per_row, 0, 0))


def _mixer(x, mod_rows, mod_first_row, mod_per_batch, state_in, emit_state, use_rope, g1, w_in,
           w_fmix):
    nb, n, _ = x.shape
    assert MIXER_ROWS % n == 0 and (nb * n) % MIXER_ROWS == 0
    n_seq = MIXER_ROWS // n
    has_state_in = state_in is not None
    cw, cn, sn = _dft_consts(n)
    dmat, qdec, kdec, sdec = _retention_consts()
    consts = [cw, cn, sn, dmat, qdec, kdec, sdec]
    if use_rope:
        assert n_seq == 1
        consts += list(_rope_consts(n))
    weights = [g1.reshape(1, D_MODEL), w_in, w_fmix]

    if mod_per_batch:
        assert n % MIXER_ROWS == 0
    state_spec = pl.BlockSpec((n_seq, 2, N_RET_HEADS, HEAD_DIM, HEAD_DIM), lambda b: (b, 0, 0, 0, 0))
    row_spec = pl.BlockSpec((MIXER_ROWS, D_MODEL), lambda b: (b, 0))
    in_specs = [row_spec, _mod_row_spec(mod_first_row, n // MIXER_ROWS if mod_per_batch else None)]
    in_specs += [_const_spec(a.shape) for a in weights + consts]
    args = [x.reshape(nb * n, D_MODEL), mod_rows] + weights + consts
    if has_state_in:
        in_specs.append(state_spec)
        args.append(state_in)

    out_shape = [jax.ShapeDtypeStruct((nb * n, D_MODEL), BF16)]
    out_specs = [row_spec]
    if emit_state:
        out_shape.append(jax.ShapeDtypeStruct((nb, 2, N_RET_HEADS, HEAD_DIM, HEAD_DIM), F32))
        out_specs.append(state_spec)

    return pl.pallas_call(
        functools.partial(_mixer_kernel, n=n, use_rope=use_rope, has_state_in=has_state_in,
                          emit_state=emit_state),
        out_shape=out_shape,
        grid=(nb * n // MIXER_ROWS,),
        in_specs=in_specs,
        out_specs=out_specs,
        scratch_shapes=[pltpu.VMEM((MIXER_ROWS, D_IN_PROJ), F32),
                        pltpu.VMEM((MIXER_ROWS, HEAD_DIM), F32), pltpu.VMEM((MIXER_ROWS, HEAD_DIM), F32)],
        compiler_params=pltpu.CompilerParams(dimension_semantics=("arbitrary",),
                                             vmem_limit_bytes=VMEM_LIMIT),
        name="mixer_rope" if use_rope else "mixer",
    )(*args)


def _post(x, mix, mod_rows, mod_first_row, tokens_per_mod_row, g2, w_out_bf, w_router):
    t = x.shape[0]
    assert tokens_per_mod_row is None or tokens_per_mod_row % POST_ROWS == 0
    blocks_per_row = None if tokens_per_mod_row is None else tokens_per_mod_row // POST_ROWS
    row_spec = pl.BlockSpec((POST_ROWS, D_MODEL), lambda b: (b, 0))
    assert t // POST_ROWS >= POST_BUFFERS - 1
    return pl.pallas_call(
        _post_kernel,
        out_shape=[jax.ShapeDtypeStruct((t, D_MODEL), F32),
                   jax.ShapeDtypeStruct((t, D_MODEL), BF16),
                   jax.ShapeDtypeStruct((N_EXPERTS, t), F32)],
        grid=(t // POST_ROWS,),
        in_specs=[pl.BlockSpec(memory_space=pl.ANY), pl.BlockSpec(memory_space=pl.ANY),
                  _mod_row_spec(mod_first_row, blocks_per_row),
                  _const_spec((1, D_MODEL)), _const_spec((D_MODEL, D_MODEL)),
                  _const_spec((D_MODEL, N_EXPERTS))],
        out_specs=[row_spec, row_spec, pl.BlockSpec((N_EXPERTS, POST_ROWS), lambda b: (0, b))],
        scratch_shapes=[pltpu.VMEM((POST_BUFFERS, POST_ROWS, D_MODEL), F32),
                        pltpu.VMEM((POST_BUFFERS, POST_ROWS, D_MODEL), BF16),
                        pltpu.SemaphoreType.DMA((POST_BUFFERS,)), pltpu.SemaphoreType.DMA((POST_BUFFERS,))],
        compiler_params=pltpu.CompilerParams(dimension_semantics=("arbitrary",),
                                             vmem_limit_bytes=VMEM_LIMIT),
        name="post",
    )(x, mix, mod_rows, g2.reshape(1, D_MODEL), w_out_bf, w_router)


def _route_kernel(*refs, sizes):
    n = len(sizes)
    affs = [ref[...] for ref in refs[:n]]
    u_ref = refs[n]
    outs = [refs[n + 1 + 4 * g:n + 5 + 4 * g] for g in range(n)]

    def count(mask):
        return jnp.sum(mask.astype(jnp.int32), axis=1, keepdims=True)

    def as_float(word):
        return lax.bitcast_convert_type(word, F32)

    def zeros():
        return tuple(jnp.zeros((N_EXPERTS, 1), jnp.int32) for _ in sizes)

    def value_step(i, curs):
        bit = jnp.left_shift(jnp.int32(1), 30 - i)
        return tuple(jnp.where(count(aff >= as_float(cur | bit)) >= cap, cur | bit, cur)
                     for aff, (_, cap), cur in zip(affs, sizes, curs))

    thrs = lax.fori_loop(0, 31, value_step, zeros())
    gts = [aff >= as_float(thr + 1) for aff, thr in zip(affs, thrs)]
    eqs = [(aff >= as_float(thr)) & jnp.logical_not(gt) for aff, thr, gt in zip(affs, thrs, gts)]
    needs = [cap - count(gt) for (_, cap), gt in zip(sizes, gts)]
    toks = [lax.broadcasted_iota(jnp.int32, (N_EXPERTS, t), 1) for t, _ in sizes]
    nbits = [t.bit_length() - 1 for t, _ in sizes]

    def index_step(i, curs):
        new = []
        for eq, need, tok, bits, cur in zip(eqs, needs, toks, nbits, curs):
            shift = bits - 1 - i
            cand = cur | jnp.where(shift >= 0, jnp.left_shift(jnp.int32(1), jnp.maximum(shift, 0)), 0)
            new.append(jnp.where(count(eq & (tok < cand)) < need, cand, cur))
        return tuple(new)

    lasts = lax.fori_loop(0, max(nbits), index_step, zeros())

    for aff, (t, _), gt, eq, tok, last, (slot_ref, slot_te_ref, gate_ref, starts_ref) in zip(
            affs, sizes, gts, eqs, toks, lasts, outs):
        self = jnp.where(gt | (eq & (tok <= last)), 1.0, 0.0).astype(F32)
        carry = jnp.zeros((N_EXPERTS, 1), F32)
        starts_ref[...] = jnp.zeros(starts_ref.shape, jnp.int32)
        for b in range(t // TOKEN_BLOCK):
            sl = slice(b * TOKEN_BLOCK, (b + 1) * TOKEN_BLOCK)
            sbf = self[:, sl]
            pre = _dot(sbf.astype(BF16), u_ref[...]) + carry
            slots = jnp.where(sbf > 0.5, pre.astype(jnp.int32), -1)
            slot_ref[:, b, :] = slots
            rows = jnp.concatenate([slots, jnp.zeros((LANES - N_EXPERTS, TOKEN_BLOCK), jnp.int32)],
                                   axis=0)
            slot_te_ref[sl, :] = rows.T[:, 0:N_EXPERTS]
            gate_ref[:, b, :] = aff[:, sl]
            starts_ref[:, b:b + 1] = carry.astype(jnp.int32)
            carry = carry + jnp.sum(sbf, axis=1, keepdims=True)
        nblk = t // TOKEN_BLOCK
        starts_ref[:, nblk:nblk + 1] = carry.astype(jnp.int32)


def _route(*affs_et):
    sizes = tuple((a.shape[1], EC_CAPACITY_FACTOR * a.shape[1] // N_EXPERTS) for a in affs_et)
    upper = jnp.asarray(np.triu(np.ones((TOKEN_BLOCK, TOKEN_BLOCK)), 1), BF16)
    whole = lambda shape: pl.BlockSpec(shape, lambda i, _n=len(shape): (0,) * _n)
    out_shape, out_specs = [], []
    for t, _ in sizes:
        nblk = t // TOKEN_BLOCK
        assert nblk + 1 <= LANES
        for shape, dtype in (((N_EXPERTS, nblk, TOKEN_BLOCK), jnp.int32), ((t, N_EXPERTS), jnp.int32),
                             ((N_EXPERTS, nblk, TOKEN_BLOCK), F32), ((N_EXPERTS, LANES), jnp.int32)):
            out_shape.append(jax.ShapeDtypeStruct(shape, dtype))
            out_specs.append(whole(shape))
    outs = pl.pallas_call(
        functools.partial(_route_kernel, sizes=sizes),
        out_shape=out_shape,
        grid=(1,),
        in_specs=[whole(a.shape) for a in affs_et] + [whole((TOKEN_BLOCK, TOKEN_BLOCK))],
        out_specs=out_specs,
        compiler_params=pltpu.CompilerParams(dimension_semantics=("arbitrary",)),
        name="route",
    )(*affs_et, upper)
    return [outs[4 * g:4 * g + 4] for g in range(len(sizes))]


def _pack_windows(starts_ref, b, experts, cap):
    first = [jnp.minimum((starts_ref[e, b] // BF16_ROWS) * BF16_ROWS, cap - SLOT_WINDOW)
             for e in experts]
    rows = [jnp.where(starts_ref[e, b + 1] > starts_ref[e, b], starts_ref[e, b + 1] - w, 0)
            for e, w in zip(experts, first)]
    return first, pl.cdiv(functools.reduce(jnp.maximum, rows), SLOT_WINDOW)


def _block_copy(hbm_ref, vmem_ref, sem_ref, b):
    rows = pl.ds(pl.multiple_of(b * TOKEN_BLOCK, TOKEN_BLOCK), TOKEN_BLOCK)
    return pltpu.make_async_copy(hbm_ref.at[rows], vmem_ref.at[rows], sem_ref.at[b])


def _gather_group(g, starts_ref, slot_ref, gate_ref, h2_ref, xs_ref, gs_ref, row0, t, cap, arrive):
    sub = lax.broadcasted_iota(jnp.int32, (SLOT_WINDOW, TOKEN_BLOCK), 0)
    experts = [g * PACK + j for j in range(PACK)]
    assert (t // TOKEN_BLOCK) % GATHER_UNROLL == 0

    def window(b, first, i):
        hb = pl.ds(pl.multiple_of(b * TOKEN_BLOCK, TOKEN_BLOCK), TOKEN_BLOCK)
        hits, dst = [], []
        for j in range(PACK):
            lo = first[j] + i * SLOT_WINDOW
            w = jnp.minimum(lo, cap - SLOT_WINDOW)
            srow = slot_ref[j, pl.ds(b, 1), :]
            hits.append((srow == w + sub) & (srow >= lo))
            dst.append(pl.ds(pl.multiple_of(row0 + w, BF16_ROWS), SLOT_WINDOW))
        onehot = jnp.concatenate([jnp.where(h, 1.0, 0.0) for h in hits], axis=0).astype(BF16)
        got = _dot(onehot, h2_ref[hb, :])
        for j in range(PACK):
            piece = got[j * SLOT_WINDOW:(j + 1) * SLOT_WINDOW].astype(BF16)
            xs_ref[j, dst[j], :] = xs_ref[j, dst[j], :] + piece
            grow = gate_ref[j, pl.ds(b, 1), :]
            gs_ref[j, dst[j], :] += jnp.sum(jnp.where(hits[j], grow, 0.0), axis=1, keepdims=True)

    def blocks(q, carry):
        pending = []
        for u in range(GATHER_UNROLL):
            arrive(q * GATHER_UNROLL + u)
        for u in range(GATHER_UNROLL):
            b = q * GATHER_UNROLL + u
            first, n_windows = _pack_windows(starts_ref, b, experts, cap)
            window(b, first, 0)
            pending.append((b, first, n_windows))
        for b, first, n_windows in pending:
            def more(i, carry, b=b, first=first):
                window(b, first, i)
                return carry

            lax.fori_loop(1, n_windows, more, 0)
        return carry

    lax.fori_loop(0, t // TOKEN_BLOCK // GATHER_UNROLL, blocks, 0)


def _experts_kernel(sp_ref, ss_ref, h2p_hbm, h2s_hbm, slotp_ref, slots_ref, gatep_ref, gates_ref,
                    wg_ref, wu_ref, wd_ref, yp_ref, ys_ref, xs_ref, gs_ref, acc_ref,
                    h2p_ref, h2s_ref, semp_ref, sems_ref, *, tp, ts, capp, caps):
    g = pl.program_id(0)
    step = pl.program_id(1)
    j = step // N_FF_TILES
    f = step % N_FF_TILES
    loading = [(h2p_hbm, h2p_ref, semp_ref, tp // TOKEN_BLOCK),
               (h2s_hbm, h2s_ref, sems_ref, ts // TOKEN_BLOCK)]

    @pl.when((step == 0) & (g == 0))
    def _():
        for hbm_ref, vmem_ref, sem_ref, n_blocks in loading:
            for b in range(n_blocks):
                _block_copy(hbm_ref, vmem_ref, sem_ref, b).start()

    def arrive(hbm_ref, vmem_ref, sem_ref, _):
        def wait(b):
            @pl.when(g == 0)
            def _():
                _block_copy(hbm_ref, vmem_ref, sem_ref, b).wait()

        return wait

    @pl.when(step == 0)
    def _():
        xs_ref[...] = jnp.zeros(xs_ref.shape, BF16)
        gs_ref[...] = jnp.zeros(gs_ref.shape, F32)
        _gather_group(g, sp_ref, slotp_ref, gatep_ref, h2p_ref, xs_ref, gs_ref, 0, tp, capp,
                      arrive(*loading[0]))
        _gather_group(g, ss_ref, slots_ref, gates_ref, h2s_ref, xs_ref, gs_ref, capp, ts, caps,
                      arrive(*loading[1]))

    @pl.when(f == 0)
    def _():
        acc_ref[...] = jnp.zeros(acc_ref.shape, F32)

    x = xs_ref[j]
    a = _dot(x, wg_ref[0].astype(BF16))
    u = _dot(x, wu_ref[0].astype(BF16))
    acc_ref[...] += _dot((_silu(a) * u).astype(BF16), wd_ref[0].astype(BF16))

    @pl.when(f == N_FF_TILES - 1)
    def _():
        yp_ref[0] = (acc_ref[0:capp, :] * gs_ref[j, 0:capp, :]).astype(BF16)
        ys_ref[0] = (acc_ref[capp:capp + caps, :] * gs_ref[j, capp:capp + caps, :]).astype(BF16)


def _experts(starts_p, starts_s, h2p, h2s, slot_p, slot_s, gate_p, gate_s, w_gate, w_up, w_down):
    tp, ts = h2p.shape[0], h2s.shape[0]
    capp = EC_CAPACITY_FACTOR * tp // N_EXPERTS
    caps = EC_CAPACITY_FACTOR * ts // N_EXPERTS
    rows = capp + caps
    nbp, nbs = tp // TOKEN_BLOCK, ts // TOKEN_BLOCK
    expert = lambda g, s: g * PACK + s // N_FF_TILES
    grid_spec = pltpu.PrefetchScalarGridSpec(
        num_scalar_prefetch=2,
        grid=(N_EXPERTS // PACK, PACK * N_FF_TILES),
        in_specs=[
            pl.BlockSpec(memory_space=pl.ANY),
            pl.BlockSpec(memory_space=pl.ANY),
            pl.BlockSpec((PACK, nbp, TOKEN_BLOCK), lambda g, s, *_: (g, 0, 0)),
            pl.BlockSpec((PACK, nbs, TOKEN_BLOCK), lambda g, s, *_: (g, 0, 0)),
            pl.BlockSpec((PACK, nbp, TOKEN_BLOCK), lambda g, s, *_: (g, 0, 0)),
            pl.BlockSpec((PACK, nbs, TOKEN_BLOCK), lambda g, s, *_: (g, 0, 0)),
            pl.BlockSpec((1, D_MODEL, FF_TILE), lambda g, s, *_: (expert(g, s), 0, s % N_FF_TILES)),
            pl.BlockSpec((1, D_MODEL, FF_TILE), lambda g, s, *_: (expert(g, s), 0, s % N_FF_TILES)),
            pl.BlockSpec((1, FF_TILE, D_MODEL), lambda g, s, *_: (expert(g, s), s % N_FF_TILES, 0)),
        ],
        out_specs=[
            pl.BlockSpec((1, capp, D_MODEL), lambda g, s, *_: (expert(g, s), 0, 0)),
            pl.BlockSpec((1, caps, D_MODEL), lambda g, s, *_: (expert(g, s), 0, 0)),
        ],
        scratch_shapes=[pltpu.VMEM((PACK, rows, D_MODEL), BF16), pltpu.VMEM((PACK, rows, 1), F32),
                        pltpu.VMEM((rows, D_MODEL), F32),
                        pltpu.VMEM((tp, D_MODEL), BF16), pltpu.VMEM((ts, D_MODEL), BF16),
                        pltpu.SemaphoreType.DMA((nbp,)), pltpu.SemaphoreType.DMA((nbs,))],
    )
    return pl.pallas_call(
        functools.partial(_experts_kernel, tp=tp, ts=ts, capp=capp, caps=caps),
        out_shape=[jax.ShapeDtypeStruct((N_EXPERTS, capp, D_MODEL), BF16),
                   jax.ShapeDtypeStruct((N_EXPERTS, caps, D_MODEL), BF16)],
        grid_spec=grid_spec,
        compiler_params=pltpu.CompilerParams(dimension_semantics=("arbitrary", "arbitrary"),
                                             vmem_limit_bytes=VMEM_LIMIT),
        name="experts",
    )(starts_p, starts_s, h2p, h2s, slot_p, slot_s, gate_p, gate_s, w_gate, w_up, w_down)


def _combine_kernel(st_ref, x1_ref, slot_ref, mod_ref, fg_ref, y_hbm, out_ref, acc_ref, y_ref,
                    sem_ref, arrived_ref, *, cap):
    step = pl.program_id(0)
    n_chunks = Y_CHUNKS
    chunk_rows = cap // Y_CHUNKS
    n_parts = COMBINE_ROWS // TOKEN_BLOCK
    last_block = step * n_parts + n_parts - 1

    def chunk_copy(c):
        rows = pl.ds(c * chunk_rows, chunk_rows)
        return pltpu.make_async_copy(y_hbm.at[:, rows, :], y_ref.at[:, rows, :], sem_ref.at[c])

    @pl.when(step == 0)
    def _():
        arrived_ref[0] = 0
        for c in range(n_chunks):
            chunk_copy(c).start()

    top = functools.reduce(jnp.maximum, [st_ref[e, last_block + 1] for e in range(N_EXPERTS)])
    want = pl.cdiv(jnp.minimum(top + SLOT_WINDOW, cap), chunk_rows)
    have = arrived_ref[0]
    for c in range(n_chunks):
        @pl.when((c >= have) & (c < want))
        def _(c=c):
            chunk_copy(c).wait()
    arrived_ref[0] = jnp.maximum(have, want)

    lane = lax.broadcasted_iota(jnp.int32, (TOKEN_BLOCK, PACK * SLOT_WINDOW), 1)
    gate2 = mod_ref[0][:, 5 * D_MODEL:6 * D_MODEL]

    def scatter(slot, experts, first, i):
        target = None
        windows = []
        for j, e in enumerate(experts):
            lo = first[j] + i * SLOT_WINDOW
            w = jnp.minimum(lo, cap - SLOT_WINDOW)
            sc = slot[:, e:e + 1]
            col = jnp.where(sc >= lo, sc - w + j * SLOT_WINDOW, -1)
            target = col if target is None else jnp.where(lane < j * SLOT_WINDOW, target, col)
            windows.append(y_ref[e, pl.ds(pl.multiple_of(w, BF16_ROWS), SLOT_WINDOW), :])
        onehot = jnp.where(target == lane, 1.0, 0.0).astype(BF16)
        return _dot(onehot, jnp.concatenate(windows, axis=0))

    def finish(rows):
        x2 = x1_ref[rows, :] + gate2 * acc_ref[rows, :]
        out_ref[rows, :] = _rms(x2) * fg_ref[...]

    parts = []
    for p in range(n_parts):
        rows = slice(p * TOKEN_BLOCK, (p + 1) * TOKEN_BLOCK)
        slot = slot_ref[rows, :]
        groups = []
        total = None
        for g in range(N_EXPERTS // PACK):
            experts = list(range(g * PACK, (g + 1) * PACK))
            first, n_windows = _pack_windows(st_ref, step * n_parts + p, experts, cap)
            groups.append((experts, first, n_windows))
            part = scatter(slot, experts, first, 0)
            total = part if total is None else total + part
        acc_ref[rows, :] = total
        parts.append((rows, slot, groups))
    for rows, _, _ in parts:
        finish(rows)

    for rows, slot, groups in parts:
        most = functools.reduce(jnp.maximum, [n_windows for _, _, n_windows in groups])

        @pl.when(most > 1)
        def _(rows=rows, slot=slot, groups=groups):
            for experts, first, n_windows in groups:
                def more(i, carry, experts=experts, first=first):
                    acc_ref[rows, :] += scatter(slot, experts, first, i)
                    return carry

                lax.fori_loop(1, n_windows, more, 0)
            finish(rows)


def _combine(starts, x1, slot_te, mod_rows, mod_first_row, tokens_per_mod_row, final_g, y):
    t = x1.shape[0]
    cap = y.shape[1]
    assert tokens_per_mod_row is None or tokens_per_mod_row % COMBINE_ROWS == 0
    steps_per_row = None if tokens_per_mod_row is None else tokens_per_mod_row // COMBINE_ROWS
    grid_spec = pltpu.PrefetchScalarGridSpec(
        num_scalar_prefetch=1,
        grid=(t // COMBINE_ROWS,),
        in_specs=[
            pl.BlockSpec((COMBINE_ROWS, D_MODEL), lambda b, *_: (b, 0)),
            pl.BlockSpec((COMBINE_ROWS, N_EXPERTS), lambda b, *_: (b, 0)),
            _mod_row_spec(mod_first_row, steps_per_row),
            pl.BlockSpec((1, D_MODEL), lambda b, *_: (0, 0)),
            pl.BlockSpec(memory_space=pl.ANY),
        ],
        out_specs=pl.BlockSpec((COMBINE_ROWS, D_MODEL), lambda b, *_: (b, 0)),
        scratch_shapes=[pltpu.VMEM((COMBINE_ROWS, D_MODEL), F32), pltpu.VMEM(y.shape, BF16),
                        pltpu.SemaphoreType.DMA((Y_CHUNKS,)), pltpu.SMEM((1,), jnp.int32)],
    )
    return pl.pallas_call(
        functools.partial(_combine_kernel, cap=cap),
        out_shape=jax.ShapeDtypeStruct((t, D_MODEL), F32),
        grid_spec=grid_spec,
        compiler_params=pltpu.CompilerParams(dimension_semantics=("arbitrary",),
                                             vmem_limit_bytes=VMEM_LIMIT),
        name="combine",
    )(starts, x1, slot_te, mod_rows, final_g.reshape(1, D_MODEL), y)


def kernel(x_prompt, x_sample, state_ret, c, c_ctx, norm1_g, norm2_g, final_g, w_mod, b_mod, w_in,
           w_fmix, w_out, w_router, w_gate, w_up, w_down):
    bp, seq, _ = x_prompt.shape
    bs, dec_seq, _ = x_sample.shape
    assert w_mod.shape[0] == 1, "single-layer trunk"
    tp, ts = bp * seq, bs * dec_seq

    cond = jnp.concatenate([c_ctx[None, :], c], axis=0)
    mod = _modulation(cond, w_mod[0], b_mod[0])
    ctx_row, lat_row = 0, 1

    w_out_bf = w_out[0].astype(BF16)
    mix_p, states = _mixer(x_prompt, mod, ctx_row, False, None, True, False, norm1_g[0], w_in[0],
                           w_fmix[0])
    (mix_s,) = _mixer(x_sample, mod, lat_row, True, state_ret[:, 0], False, True, norm1_g[0],
                      w_in[0], w_fmix[0])
    x1p, h2p, affp = _post(x_prompt.reshape(tp, D_MODEL), mix_p, mod, ctx_row, None,
                           norm2_g[0], w_out_bf, w_router[0])
    x1s, h2s, affs = _post(x_sample.reshape(ts, D_MODEL), mix_s, mod, lat_row, dec_seq,
                           norm2_g[0], w_out_bf, w_router[0])

    (slot_p, slot_te_p, gate_p, starts_p), (slot_s, slot_te_s, gate_s, starts_s) = _route(affp, affs)

    yp, ys = _experts(starts_p, starts_s, h2p, h2s, slot_p, slot_s, gate_p, gate_s,
                      w_gate[0], w_up[0], w_down[0])

    out_p = _combine(starts_p, x1p, slot_te_p, mod, ctx_row, None, final_g, yp)
    out_s = _combine(starts_s, x1s, slot_te_s, mod, lat_row, dec_seq, final_g, ys)

    y_prompt = out_p.reshape(bp, seq, D_MODEL)
    y_sample = out_s.reshape(bs, dec_seq, D_MODEL)
    state_new = states.reshape(bp, 1, 2, N_RET_HEADS, HEAD_DIM, HEAD_DIM).astype(x_prompt.dtype)
    return (y_prompt, y_sample, state_new)
```

```python
import functools
import math

import jax
import jax.numpy as jnp
import numpy as np
from jax import lax
from jax.experimental import pallas as pl
from jax.experimental.pallas import tpu as pltpu

D_MODEL = 1024
D_FOURIER = 512
N_FOURIER_GROUPS = 4
FOURIER_GROUP_W = 128
D_RET = 512
N_RET_HEADS = 4
HEAD_DIM = 128
CHUNK = 256
GRID_W = 64
N_EXPERTS = 16
EC_CAPACITY_FACTOR = 2
D_EXPERT_FF = 2816
ROPE_BASE = 10000.0
EPS = 1e-6
D_IN_PROJ = D_FOURIER + 5 * D_RET
LOG_GAMMA_FWD = np.log(1.0 - 2.0 ** (-5.0 - np.arange(N_RET_HEADS))).astype(np.float32)
LOG_GAMMA_BWD = np.log(1.0 - 2.0 ** (-5.5 - np.arange(N_RET_HEADS))).astype(np.float32)

LANES = 128
BF16_ROWS = 16
TOKEN_BLOCK = 256
SLOT_WINDOW = 64
PACK = TOKEN_BLOCK // SLOT_WINDOW
GATHER_UNROLL = 4
Y_CHUNKS = 4
FF_TILE = 256
N_FF_TILES = D_EXPERT_FF // FF_TILE
MOD_TILE = 1024
MIXER_ROWS = 1024
POST_ROWS = 512
POST_PART = 256
POST_BUFFERS = 3
COMBINE_ROWS = 512
VMEM_LIMIT = 56 * 1024 * 1024

F32 = jnp.float32
BF16 = jnp.bfloat16


def _dot(a, b):
    return jnp.dot(a, b, preferred_element_type=F32)


def _dot_nt(a, b):
    return lax.dot_general(a, b, (((1,), (1,)), ((), ())), preferred_element_type=F32)


def _silu(x):
    return x * jax.nn.sigmoid(x)


def _mod_kernel(condt_ref, w_ref, b_ref, out_ref, *, n_cond):
    s = _silu(condt_ref[...])
    w = w_ref[...]
    out_ref[...] = jnp.zeros(out_ref.shape, F32)
    for r in range(n_cond):
        out_ref[r] = jnp.sum(w * s[:, r:r + 1], axis=0, keepdims=True) + b_ref[...]


def _modulation(cond_rows, w_mod, b_mod):
    n_cond = cond_rows.shape[0]
    condt = jnp.zeros((D_MODEL, 8), F32).at[:, :n_cond].set(cond_rows.T)
    n_out = w_mod.shape[1]
    return pl.pallas_call(
        functools.partial(_mod_kernel, n_cond=n_cond),
        out_shape=jax.ShapeDtypeStruct((8, 1, n_out), F32),
        grid=(n_out // MOD_TILE,),
        in_specs=[
            pl.BlockSpec((D_MODEL, 8), lambda j: (0, 0)),
            pl.BlockSpec((D_MODEL, MOD_TILE), lambda j: (0, j)),
            pl.BlockSpec((1, MOD_TILE), lambda j: (0, j)),
        ],
        out_specs=pl.BlockSpec((8, 1, MOD_TILE), lambda j: (0, 0, j)),
        compiler_params=pltpu.CompilerParams(dimension_semantics=("arbitrary",)),
        name="mod",
    )(condt, w_mod, b_mod.reshape(1, n_out))


def _rms(x):
    return x * lax.rsqrt(jnp.mean(x * x, axis=-1, keepdims=True) + EPS)


def _groupnorm(o):
    mu = jnp.mean(o, axis=-1, keepdims=True)
    c = o - mu
    return c * lax.rsqrt(jnp.mean(c * c, axis=-1, keepdims=True) + EPS)


def _split_hi_lo(x):
    hi = x.astype(BF16)
    lo = (x - hi.astype(F32)).astype(BF16)
    return hi, lo


def _mixer_kernel(*refs, n, use_rope, has_state_in, emit_state):
    it = iter(refs)
    x_ref, mod_ref, g1_ref, win_ref, wfmix_ref = (next(it) for _ in range(5))
    cw_ref, cn_ref, sn_ref, dmat_ref, qdec_ref, kdec_ref, sdec_ref = (next(it) for _ in range(7))
    cos_ref = sin_ref = s0_ref = st_ref = None
    if use_rope:
        cos_ref, sin_ref = next(it), next(it)
    if has_state_in:
        s0_ref = next(it)
    mix_ref = next(it)
    if emit_state:
        st_ref = next(it)
    p_ref, of_ref, ob_ref = next(it), next(it), next(it)

    n_seq = MIXER_ROWS // n
    chunks_per_seq = n // CHUNK
    mod = mod_ref[0]
    shift1 = mod[:, 0:D_MODEL]
    scale1 = mod[:, D_MODEL:2 * D_MODEL]

    h = (_rms(x_ref[...]) * g1_ref[...] * (1.0 + scale1) + shift1).astype(BF16)
    for j in range(D_IN_PROJ // 512):
        p_ref[:, j * 512:(j + 1) * 512] = _dot(h, win_ref[:, j * 512:(j + 1) * 512].astype(BF16))

    xf = p_ref[:, 0:D_FOURIER].astype(BF16)
    xc, xs = [], []
    cw = cw_ref[...].astype(BF16)
    for g in range(N_FOURIER_GROUPS):
        t = _dot(xf[:, g * FOURIER_GROUP_W:(g + 1) * FOURIER_GROUP_W], cw)
        xc.append(t[:, :FOURIER_GROUP_W].astype(BF16))
        xs.append(t[:, FOURIER_GROUP_W:].astype(BF16))
    xc = jnp.concatenate(xc, axis=1)
    xs = jnp.concatenate(xs, axis=1)
    cn = cn_ref[...].astype(BF16)
    sn = sn_ref[...].astype(BF16)
    for s in range(n_seq):
        rs = slice(s * n, (s + 1) * n)
        fre = (_dot(cn, xc[rs]) - _dot(sn, xs[rs])) * (1.0 / math.sqrt(n * FOURIER_GROUP_W))
        fre = fre.astype(BF16)
        for g in range(N_FOURIER_GROUPS):
            sl = slice(g * FOURIER_GROUP_W, (g + 1) * FOURIER_GROUP_W)
            mix_ref[rs, sl] = _dot(fre[:, sl], wfmix_ref[g].astype(BF16)).astype(BF16)

    for hh in range(N_RET_HEADS):
        base = D_FOURIER + hh * HEAD_DIM
        q = p_ref[:, base:base + HEAD_DIM]
        k = p_ref[:, base + D_RET:base + D_RET + HEAD_DIM]
        v = p_ref[:, base + 2 * D_RET:base + 2 * D_RET + HEAD_DIM]
        if use_rope:
            lane = lax.broadcasted_iota(jnp.int32, (MIXER_ROWS, HEAD_DIM), 1)
            first = (lane % 64) < 32

            def rope(t):
                swapped = jnp.where(first, pltpu.roll(t, HEAD_DIM - 32, 1), pltpu.roll(t, 32, 1))
                return t * cos_ref[...] + swapped * sin_ref[...]

            q, k = rope(q), rope(k)
        k = k * (HEAD_DIM ** -0.5)
        qb, vb = q.astype(BF16), v.astype(BF16)
        kb = k.astype(BF16)

        def initial(s, direction):
            if has_state_in:
                return s0_ref[s, direction, hh]
            return jnp.zeros((HEAD_DIM, HEAD_DIM), F32)

        for s in range(n_seq):
            parts = []
            for ci in range(chunks_per_seq):
                c = s * chunks_per_seq + ci
                rs = slice(c * CHUNK, (c + 1) * CHUNK)
                qc, kc, vc = qb[rs], kb[rs], vb[rs]
                qk = _dot_nt(qc, kc)
                lhs = jnp.concatenate([(qk * dmat_ref[0, hh]).astype(BF16),
                                       (qk * dmat_ref[1, hh]).astype(BF16),
                                       (k[rs] * kdec_ref[0, hh]).T.astype(BF16),
                                       (k[rs] * kdec_ref[1, hh]).T.astype(BF16)], axis=0)
                parts.append((rs, qc, _dot(lhs, vc)))
            sf = initial(s, 0)
            for ci in range(chunks_per_seq):
                rs, qc, r = parts[ci]
                o = r[0:CHUNK]
                if has_state_in or ci > 0:
                    o = o + qdec_ref[0, hh] * _dot(qc, sf.astype(BF16))
                of_ref[rs, :] = o
                sf = sf * sdec_ref[0, hh] + r[2 * CHUNK:2 * CHUNK + HEAD_DIM]
            sb = initial(s, 1)
            for ci in reversed(range(chunks_per_seq)):
                rs, qc, r = parts[ci]
                o = r[CHUNK:2 * CHUNK]
                if has_state_in or ci < chunks_per_seq - 1:
                    o = o + qdec_ref[1, hh] * _dot(qc, sb.astype(BF16))
                ob_ref[rs, :] = o
                sb = sb * sdec_ref[1, hh] + r[2 * CHUNK + HEAD_DIM:]
            if emit_state:
                st_ref[s, 0, hh] = sf
                st_ref[s, 1, hh] = sb

        gf = p_ref[:, base + 3 * D_RET:base + 3 * D_RET + HEAD_DIM]
        gb = p_ref[:, base + 4 * D_RET:base + 4 * D_RET + HEAD_DIM]
        y = _silu(gf) * _groupnorm(of_ref[...]) + _silu(gb) * _groupnorm(ob_ref[...])
        mix_ref[:, base:base + HEAD_DIM] = y.astype(BF16)


def _post_kernel(x_hbm, mix_hbm, mod_ref, g2_ref, wout_ref, wr_ref, x1_ref, h2_ref, aff_ref,
                 xbuf_ref, mixbuf_ref, xsem_ref, mixsem_ref):
    i = pl.program_id(0)
    ahead = POST_BUFFERS - 1

    def block_copies(b):
        rows = pl.ds(pl.multiple_of(b * POST_ROWS, POST_ROWS), POST_ROWS)
        slot = b % POST_BUFFERS
        return (pltpu.make_async_copy(x_hbm.at[rows], xbuf_ref.at[slot], xsem_ref.at[slot]),
                pltpu.make_async_copy(mix_hbm.at[rows], mixbuf_ref.at[slot], mixsem_ref.at[slot]))

    @pl.when(i == 0)
    def _():
        for b in range(ahead):
            for copy in block_copies(b):
                copy.start()

    @pl.when(i + ahead < pl.num_programs(0))
    def _():
        for copy in block_copies(i + ahead):
            copy.start()

    for copy in block_copies(i):
        copy.wait()
    x_ref = xbuf_ref.at[i % POST_BUFFERS]
    mix_ref = mixbuf_ref.at[i % POST_BUFFERS]
    mod = mod_ref[0]
    gate1 = mod[:, 2 * D_MODEL:3 * D_MODEL]
    shift2 = mod[:, 3 * D_MODEL:4 * D_MODEL]
    scale2 = mod[:, 4 * D_MODEL:5 * D_MODEL]
    wr_hi, wr_lo = _split_hi_lo(wr_ref[...])
    wr_both = jnp.concatenate([wr_hi, wr_lo], axis=1)
    parts = [slice(p * POST_PART, (p + 1) * POST_PART) for p in range(POST_ROWS // POST_PART)]
    x1 = []
    for rows in parts:
        x1.append(x_ref[rows, :] + gate1 * _dot(mix_ref[rows, :], wout_ref[...]))
        x1_ref[rows, :] = x1[-1]
    for rows, x1_part in zip(parts, x1):
        h2 = _rms(x1_part) * g2_ref[...] * (1.0 + scale2) + shift2
        h2_hi, h2_lo = _split_hi_lo(h2)
        h2_ref[rows, :] = h2_hi
        by_hi = _dot(h2_hi, wr_both)
        logits = by_hi[:, :N_EXPERTS] + (_dot(h2_lo, wr_hi) + by_hi[:, N_EXPERTS:])
        z = jnp.exp(logits - jnp.max(logits, axis=-1, keepdims=True))
        aff = z / jnp.sum(z, axis=-1, keepdims=True)
        lanes = jnp.concatenate([aff, jnp.zeros((POST_PART, LANES - N_EXPERTS), F32)], axis=1)
        aff_ref[:, rows] = lanes.T[0:N_EXPERTS, :]


def _dft_consts(n):
    w = FOURIER_GROUP_W
    jw = np.arange(w)
    angw = 2.0 * np.pi * np.outer(jw, jw) / w
    cw = np.concatenate([np.cos(angw), np.sin(angw)], axis=1)
    jn = np.arange(n)
    angn = 2.0 * np.pi * (np.outer(jn, jn) % n) / n
    return (jnp.asarray(cw, F32), jnp.asarray(np.cos(angn), F32), jnp.asarray(np.sin(angn), F32))


def _retention_consts():
    i = np.arange(CHUNK, dtype=np.float64)
    diff = i[:, None] - i[None, :]
    dmat = np.zeros((2, N_RET_HEADS, CHUNK, CHUNK))
    qdec = np.zeros((2, N_RET_HEADS, CHUNK, HEAD_DIM))
    kdec = np.zeros((2, N_RET_HEADS, CHUNK, HEAD_DIM))
    sdec = np.zeros((2, N_RET_HEADS, HEAD_DIM, HEAD_DIM))
    for hh in range(N_RET_HEADS):
        lf = float(LOG_GAMMA_FWD[hh])
        lb = float(LOG_GAMMA_BWD[hh])
        dmat[0, hh] = np.where(diff >= 0, np.exp(lf * np.maximum(diff, 0.0)), 0.0)
        dmat[1, hh] = np.where(diff <= 0, np.exp(lb * np.maximum(-diff, 0.0)), 0.0)
        qdec[0, hh] = np.exp(lf * (i + 1.0))[:, None]
        qdec[1, hh] = np.exp(lb * (CHUNK - i))[:, None]
        kdec[0, hh] = np.exp(lf * (CHUNK - 1.0 - i))[:, None]
        kdec[1, hh] = np.exp(lb * i)[:, None]
        sdec[0, hh] = math.exp(lf * CHUNK)
        sdec[1, hh] = math.exp(lb * CHUNK)
    return tuple(jnp.asarray(a, F32) for a in (dmat, qdec, kdec, sdec))


def _rope_consts(n):
    rows_n = n // GRID_W
    row = np.repeat(np.arange(rows_n, dtype=np.float64), GRID_W)
    col = np.tile(np.arange(GRID_W, dtype=np.float64), rows_n)
    n_pairs = HEAD_DIM // 4
    freqs = (np.float32(ROPE_BASE) ** (-np.arange(n_pairs, dtype=np.float32) / n_pairs)).astype(np.float64)
    ar = row[:, None] * freqs[None, :]
    ac = col[:, None] * freqs[None, :]
    cos = np.concatenate([np.cos(ar), np.cos(ar), np.cos(ac), np.cos(ac)], axis=1)
    sin = np.concatenate([-np.sin(ar), np.sin(ar), -np.sin(ac), np.sin(ac)], axis=1)
    return jnp.asarray(cos, F32), jnp.asarray(sin, F32)


def _const_spec(shape):
    nd = len(shape)
    return pl.BlockSpec(shape, lambda b, _nd=nd: (0,) * _nd, pipeline_mode=pl.Buffered(1))


def _mod_row_spec(first_row, blocks_per_row):
    if blocks_per_row is None:
        return pl.BlockSpec((1, 1, 6 * D_MODEL), lambda b, *_: (first_row, 0, 0))
    return pl.BlockSpec((1, 1, 6 * D_MODEL), lambda b, *_: (first_row + b // blocks_per_row, 0, 0))


def _mixer(x, mod_rows, mod_first_row, mod_per_batch, state_in, emit_state, use_rope, g1, w_in,
           w_fmix):
    nb, n, _ = x.shape
    assert MIXER_ROWS % n == 0 and (nb * n) % MIXER_ROWS == 0
    n_seq = MIXER_ROWS // n
    has_state_in = state_in is not None
    cw, cn, sn = _dft_consts(n)
    dmat, qdec, kdec, sdec = _retention_consts()
    consts = [cw, cn, sn, dmat, qdec, kdec, sdec]
    if use_rope:
        assert n_seq == 1
        consts += list(_rope_consts(n))
    weights = [g1.reshape(1, D_MODEL), w_in, w_fmix]

    if mod_per_batch:
        assert n % MIXER_ROWS == 0
    state_spec = pl.BlockSpec((n_seq, 2, N_RET_HEADS, HEAD_DIM, HEAD_DIM), lambda b: (b, 0, 0, 0, 0))
    row_spec = pl.BlockSpec((MIXER_ROWS, D_MODEL), lambda b: (b, 0))
    in_specs = [row_spec, _mod_row_spec(mod_first_row, n // MIXER_ROWS if mod_per_batch else None)]
    in_specs += [_const_spec(a.shape) for a in weights + consts]
    args = [x.reshape(nb * n, D_MODEL), mod_rows] + weights + consts
    if has_state_in:
        in_specs.append(state_spec)
        args.append(state_in)

    out_shape = [jax.ShapeDtypeStruct((nb * n, D_MODEL), BF16)]
    out_specs = [row_spec]
    if emit_state:
        out_shape.append(jax.ShapeDtypeStruct((nb, 2, N_RET_HEADS, HEAD_DIM, HEAD_DIM), F32))
        out_specs.append(state_spec)

    return pl.pallas_call(
        functools.partial(_mixer_kernel, n=n, use_rope=use_rope, has_state_in=has_state_in,
                          emit_state=emit_state),
        out_shape=out_shape,
        grid=(nb * n // MIXER_ROWS,),
        in_specs=in_specs,
        out_specs=out_specs,
        scratch_shapes=[pltpu.VMEM((MIXER_ROWS, D_IN_PROJ), F32),
                        pltpu.VMEM((MIXER_ROWS, HEAD_DIM), F32), pltpu.VMEM((MIXER_ROWS, HEAD_DIM), F32)],
        compiler_params=pltpu.CompilerParams(dimension_semantics=("arbitrary",),
                                             vmem_limit_bytes=VMEM_LIMIT),
        name="mixer_rope" if use_rope else "mixer",
    )(*args)


def _post(x, mix, mod_rows, mod_first_row, tokens_per_mod_row, g2, w_out_bf, w_router):
    t = x.shape[0]
    assert tokens_per_mod_row is None or tokens_per_mod_row % POST_ROWS == 0
    blocks_per_row = None if tokens_per_mod_row is None else tokens_per_mod_row // POST_ROWS
    row_spec = pl.BlockSpec((POST_ROWS, D_MODEL), lambda b: (b, 0))
    assert t // POST_ROWS >= POST_BUFFERS - 1
    return pl.pallas_call(
        _post_kernel,
        out_shape=[jax.ShapeDtypeStruct((t, D_MODEL), F32),
                   jax.ShapeDtypeStruct((t, D_MODEL), BF16),
                   jax.ShapeDtypeStruct((N_EXPERTS, t), F32)],
        grid=(t // POST_ROWS,),
        in_specs=[pl.BlockSpec(memory_space=pl.ANY), pl.BlockSpec(memory_space=pl.ANY),
                  _mod_row_spec(mod_first_row, blocks_per_row),
                  _const_spec((1, D_MODEL)), _const_spec((D_MODEL, D_MODEL)),
                  _const_spec((D_MODEL, N_EXPERTS))],
        out_specs=[row_spec, row_spec, pl.BlockSpec((N_EXPERTS, POST_ROWS), lambda b: (0, b))],
        scratch_shapes=[pltpu.VMEM((POST_BUFFERS, POST_ROWS, D_MODEL), F32),
                        pltpu.VMEM((POST_BUFFERS, POST_ROWS, D_MODEL), BF16),
                        pltpu.SemaphoreType.DMA((POST_BUFFERS,)), pltpu.SemaphoreType.DMA((POST_BUFFERS,))],
        compiler_params=pltpu.CompilerParams(dimension_semantics=("arbitrary",),
                                             vmem_limit_bytes=VMEM_LIMIT),
        name="post",
    )(x, mix, mod_rows, g2.reshape(1, D_MODEL), w_out_bf, w_router)


def _route_kernel(*refs, sizes):
    n = len(sizes)
    affs = [ref[...] for ref in refs[:n]]
    u_ref = refs[n]
    outs = [refs[n + 1 + 4 * g:n + 5 + 4 * g] for g in range(n)]

    def count(mask):
        return jnp.sum(mask.astype(jnp.int32), axis=1, keepdims=True)

    def as_float(word):
        return lax.bitcast_convert_type(word, F32)

    def zeros():
        return tuple(jnp.zeros((N_EXPERTS, 1), jnp.int32) for _ in sizes)

    def value_step(i, curs):
        bit = jnp.left_shift(jnp.int32(1), 30 - i)
        return tuple(jnp.where(count(aff >= as_float(cur | bit)) >= cap, cur | bit, cur)
                     for aff, (_, cap), cur in zip(affs, sizes, curs))

    thrs = lax.fori_loop(0, 31, value_step, zeros())
    gts = [aff >= as_float(thr + 1) for aff, thr in zip(affs, thrs)]
    eqs = [(aff >= as_float(thr)) & jnp.logical_not(gt) for aff, thr, gt in zip(affs, thrs, gts)]
    needs = [cap - count(gt) for (_, cap), gt in zip(sizes, gts)]
    toks = [lax.broadcasted_iota(jnp.int32, (N_EXPERTS, t), 1) for t, _ in sizes]
    nbits = [t.bit_length() - 1 for t, _ in sizes]

    def index_step(i, curs):
        new = []
        for eq, need, tok, bits, cur in zip(eqs, needs, toks, nbits, curs):
            shift = bits - 1 - i
            cand = cur | jnp.where(shift >= 0, jnp.left_shift(jnp.int32(1), jnp.maximum(shift, 0)), 0)
            new.append(jnp.where(count(eq & (tok < cand)) < need, cand, cur))
        return tuple(new)

    lasts = lax.fori_loop(0, max(nbits), index_step, zeros())

    for aff, (t, _), gt, eq, tok, last, (slot_ref, slot_te_ref, gate_ref, starts_ref) in zip(
            affs, sizes, gts, eqs, toks, lasts, outs):
        self = jnp.where(gt | (eq & (tok <= last)), 1.0, 0.0).astype(F32)
        carry = jnp.zeros((N_EXPERTS, 1), F32)
        starts_ref[...] = jnp.zeros(starts_ref.shape, jnp.int32)
        for b in range(t // TOKEN_BLOCK):
            sl = slice(b * TOKEN_BLOCK, (b + 1) * TOKEN_BLOCK)
            sbf = self[:, sl]
            pre = _dot(sbf.astype(BF16), u_ref[...]) + carry
            slots = jnp.where(sbf > 0.5, pre.astype(jnp.int32), -1)
            slot_ref[:, b, :] = slots
            rows = jnp.concatenate([slots, jnp.zeros((LANES - N_EXPERTS, TOKEN_BLOCK), jnp.int32)],
                                   axis=0)
            slot_te_ref[sl, :] = rows.T[:, 0:N_EXPERTS]
            gate_ref[:, b, :] = aff[:, sl]
            starts_ref[:, b:b + 1] = carry.astype(jnp.int32)
            carry = carry + jnp.sum(sbf, axis=1, keepdims=True)
        nblk = t // TOKEN_BLOCK
        starts_ref[:, nblk:nblk + 1] = carry.astype(jnp.int32)


def _route(*affs_et):
    sizes = tuple((a.shape[1], EC_CAPACITY_FACTOR * a.shape[1] // N_EXPERTS) for a in affs_et)
    upper = jnp.asarray(np.triu(np.ones((TOKEN_BLOCK, TOKEN_BLOCK)), 1), BF16)
    whole = lambda shape: pl.BlockSpec(shape, lambda i, _n=len(shape): (0,) * _n)
    out_shape, out_specs = [], []
    for t, _ in sizes:
        nblk = t // TOKEN_BLOCK
        assert nblk + 1 <= LANES
        for shape, dtype in (((N_EXPERTS, nblk, TOKEN_BLOCK), jnp.int32), ((t, N_EXPERTS), jnp.int32),
                             ((N_EXPERTS, nblk, TOKEN_BLOCK), F32), ((N_EXPERTS, LANES), jnp.int32)):
            out_shape.append(jax.ShapeDtypeStruct(shape, dtype))
            out_specs.append(whole(shape))
    outs = pl.pallas_call(
        functools.partial(_route_kernel, sizes=sizes),
        out_shape=out_shape,
        grid=(1,),
        in_specs=[whole(a.shape) for a in affs_et] + [whole((TOKEN_BLOCK, TOKEN_BLOCK))],
        out_specs=out_specs,
        compiler_params=pltpu.CompilerParams(dimension_semantics=("arbitrary",)),
        name="route",
    )(*affs_et, upper)
    return [outs[4 * g:4 * g + 4] for g in range(len(sizes))]


def _pack_windows(starts_ref, b, experts, cap):
    first = [jnp.minimum((starts_ref[e, b] // BF16_ROWS) * BF16_ROWS, cap - SLOT_WINDOW)
             for e in experts]
    rows = [jnp.where(starts_ref[e, b + 1] > starts_ref[e, b], starts_ref[e, b + 1] - w, 0)
            for e, w in zip(experts, first)]
    return first, pl.cdiv(functools.reduce(jnp.maximum, rows), SLOT_WINDOW)


def _block_copy(hbm_ref, vmem_ref, sem_ref, b):
    rows = pl.ds(pl.multiple_of(b * TOKEN_BLOCK, TOKEN_BLOCK), TOKEN_BLOCK)
    return pltpu.make_async_copy(hbm_ref.at[rows], vmem_ref.at[rows], sem_ref.at[b])


def _gather_group(g, starts_ref, slot_ref, gate_ref, h2_ref, xs_ref, gs_ref, row0, t, cap, arrive):
    sub = lax.broadcasted_iota(jnp.int32, (SLOT_WINDOW, TOKEN_BLOCK), 0)
    experts = [g * PACK + j for j in range(PACK)]
    assert (t // TOKEN_BLOCK) % GATHER_UNROLL == 0

    def window(b, first, i):
        hb = pl.ds(pl.multiple_of(b * TOKEN_BLOCK, TOKEN_BLOCK), TOKEN_BLOCK)
        hits, dst = [], []
        for j in range(PACK):
            lo = first[j] + i * SLOT_WINDOW
            w = jnp.minimum(lo, cap - SLOT_WINDOW)
            srow = slot_ref[j, pl.ds(b, 1), :]
            hits.append((srow == w + sub) & (srow >= lo))
            dst.append(pl.ds(pl.multiple_of(row0 + w, BF16_ROWS), SLOT_WINDOW))
        onehot = jnp.concatenate([jnp.where(h, 1.0, 0.0) for h in hits], axis=0).astype(BF16)
        got = _dot(onehot, h2_ref[hb, :])
        for j in range(PACK):
            piece = got[j * SLOT_WINDOW:(j + 1) * SLOT_WINDOW].astype(BF16)
            xs_ref[j, dst[j], :] = xs_ref[j, dst[j], :] + piece
            grow = gate_ref[j, pl.ds(b, 1), :]
            gs_ref[j, dst[j], :] += jnp.sum(jnp.where(hits[j], grow, 0.0), axis=1, keepdims=True)

    def blocks(q, carry):
        pending = []
        for u in range(GATHER_UNROLL):
            arrive(q * GATHER_UNROLL + u)
        for u in range(GATHER_UNROLL):
            b = q * GATHER_UNROLL + u
            first, n_windows = _pack_windows(starts_ref, b, experts, cap)
            window(b, first, 0)
            pending.append((b, first, n_windows))
        for b, first, n_windows in pending:
            def more(i, carry, b=b, first=first):
                window(b, first, i)
                return carry

            lax.fori_loop(1, n_windows, more, 0)
        return carry

    lax.fori_loop(0, t // TOKEN_BLOCK // GATHER_UNROLL, blocks, 0)


def _experts_kernel(sp_ref, ss_ref, h2p_hbm, h2s_hbm, slotp_ref, slots_ref, gatep_ref, gates_ref,
                    wg_ref, wu_ref, wd_ref, yp_ref, ys_ref, xs_ref, gs_ref, acc_ref,
                    h2p_ref, h2s_ref, semp_ref, sems_ref, *, tp, ts, capp, caps):
    g = pl.program_id(0)
    step = pl.program_id(1)
    j = step // N_FF_TILES
    f = step % N_FF_TILES
    loading = [(h2p_hbm, h2p_ref, semp_ref, tp // TOKEN_BLOCK),
               (h2s_hbm, h2s_ref, sems_ref, ts // TOKEN_BLOCK)]

    @pl.when((step == 0) & (g == 0))
    def _():
        for hbm_ref, vmem_ref, sem_ref, n_blocks in loading:
            for b in range(n_blocks):
                _block_copy(hbm_ref, vmem_ref, sem_ref, b).start()

    def arrive(hbm_ref, vmem_ref, sem_ref, _):
        def wait(b):
            @pl.when(g == 0)
            def _():
                _block_copy(hbm_ref, vmem_ref, sem_ref, b).wait()

        return wait

    @pl.when(step == 0)
    def _():
        xs_ref[...] = jnp.zeros(xs_ref.shape, BF16)
        gs_ref[...] = jnp.zeros(gs_ref.shape, F32)
        _gather_group(g, sp_ref, slotp_ref, gatep_ref, h2p_ref, xs_ref, gs_ref, 0, tp, capp,
                      arrive(*loading[0]))
        _gather_group(g, ss_ref, slots_ref, gates_ref, h2s_ref, xs_ref, gs_ref, capp, ts, caps,
                      arrive(*loading[1]))

    @pl.when(f == 0)
    def _():
        acc_ref[...] = jnp.zeros(acc_ref.shape, F32)

    x = xs_ref[j]
    a = _dot(x, wg_ref[0].astype(BF16))
    u = _dot(x, wu_ref[0].astype(BF16))
    acc_ref[...] += _dot((_silu(a) * u).astype(BF16), wd_ref[0].astype(BF16))

    @pl.when(f == N_FF_TILES - 1)
    def _():
        yp_ref[0] = (acc_ref[0:capp, :] * gs_ref[j, 0:capp, :]).astype(BF16)
        ys_ref[0] = (acc_ref[capp:capp + caps, :] * gs_ref[j, capp:capp + caps, :]).astype(BF16)


def _experts(starts_p, starts_s, h2p, h2s, slot_p, slot_s, gate_p, gate_s, w_gate, w_up, w_down):
    tp, ts = h2p.shape[0], h2s.shape[0]
    capp = EC_CAPACITY_FACTOR * tp // N_EXPERTS
    caps = EC_CAPACITY_FACTOR * ts // N_EXPERTS
    rows = capp + caps
    nbp, nbs = tp // TOKEN_BLOCK, ts // TOKEN_BLOCK
    expert = lambda g, s: g * PACK + s // N_FF_TILES
    grid_spec = pltpu.PrefetchScalarGridSpec(
        num_scalar_prefetch=2,
        grid=(N_EXPERTS // PACK, PACK * N_FF_TILES),
        in_specs=[
            pl.BlockSpec(memory_space=pl.ANY),
            pl.BlockSpec(memory_space=pl.ANY),
            pl.BlockSpec((PACK, nbp, TOKEN_BLOCK), lambda g, s, *_: (g, 0, 0)),
            pl.BlockSpec((PACK, nbs, TOKEN_BLOCK), lambda g, s, *_: (g, 0, 0)),
            pl.BlockSpec((PACK, nbp, TOKEN_BLOCK), lambda g, s, *_: (g, 0, 0)),
            pl.BlockSpec((PACK, nbs, TOKEN_BLOCK), lambda g, s, *_: (g, 0, 0)),
            pl.BlockSpec((1, D_MODEL, FF_TILE), lambda g, s, *_: (expert(g, s), 0, s % N_FF_TILES)),
            pl.BlockSpec((1, D_MODEL, FF_TILE), lambda g, s, *_: (expert(g, s), 0, s % N_FF_TILES)),
            pl.BlockSpec((1, FF_TILE, D_MODEL), lambda g, s, *_: (expert(g, s), s % N_FF_TILES, 0)),
        ],
        out_specs=[
            pl.BlockSpec((1, capp, D_MODEL), lambda g, s, *_: (expert(g, s), 0, 0)),
            pl.BlockSpec((1, caps, D_MODEL), lambda g, s, *_: (expert(g, s), 0, 0)),
        ],
        scratch_shapes=[pltpu.VMEM((PACK, rows, D_MODEL), BF16), pltpu.VMEM((PACK, rows, 1), F32),
                        pltpu.VMEM((rows, D_MODEL), F32),
                        pltpu.VMEM((tp, D_MODEL), BF16), pltpu.VMEM((ts, D_MODEL), BF16),
                        pltpu.SemaphoreType.DMA((nbp,)), pltpu.SemaphoreType.DMA((nbs,))],
    )
    return pl.pallas_call(
        functools.partial(_experts_kernel, tp=tp, ts=ts, capp=capp, caps=caps),
        out_shape=[jax.ShapeDtypeStruct((N_EXPERTS, capp, D_MODEL), BF16),
                   jax.ShapeDtypeStruct((N_EXPERTS, caps, D_MODEL), BF16)],
        grid_spec=grid_spec,
        compiler_params=pltpu.CompilerParams(dimension_semantics=("arbitrary", "arbitrary"),
                                             vmem_limit_bytes=VMEM_LIMIT),
        name="experts",
    )(starts_p, starts_s, h2p, h2s, slot_p, slot_s, gate_p, gate_s, w_gate, w_up, w_down)


def _combine_kernel(st_ref, x1_hbm, slot_ref, mod_ref, fg_ref, y_hbm, out_ref, acc_ref, y_ref,
                    sem_ref, arrived_ref, x1buf_ref, x1sem_ref, *, cap):
    step = pl.program_id(0)
    n_chunks = Y_CHUNKS
    chunk_rows = cap // Y_CHUNKS
    n_parts = COMBINE_ROWS // TOKEN_BLOCK
    last_block = step * n_parts + n_parts - 1

    def chunk_copy(c):
        rows = pl.ds(c * chunk_rows, chunk_rows)
        return pltpu.make_async_copy(y_hbm.at[:, rows, :], y_ref.at[:, rows, :], sem_ref.at[c])

    @pl.when(step == 0)
    def _():
        arrived_ref[0] = 0
        for c in range(n_chunks):
            chunk_copy(c).start()

    top = functools.reduce(jnp.maximum, [st_ref[e, last_block + 1] for e in range(N_EXPERTS)])
    want = pl.cdiv(jnp.minimum(top + SLOT_WINDOW, cap), chunk_rows)
    have = arrived_ref[0]
    for c in range(n_chunks):
        @pl.when((c >= have) & (c < want))
        def _(c=c):
            chunk_copy(c).wait()
    arrived_ref[0] = jnp.maximum(have, want)

    ahead = POST_BUFFERS - 1

    def x1_copy(b):
        rows = pl.ds(pl.multiple_of(b * COMBINE_ROWS, COMBINE_ROWS), COMBINE_ROWS)
        slot = b % POST_BUFFERS
        return pltpu.make_async_copy(x1_hbm.at[rows], x1buf_ref.at[slot], x1sem_ref.at[slot])

    @pl.when(step == 0)
    def _():
        for b in range(ahead):
            x1_copy(b).start()

    @pl.when(step + ahead < pl.num_programs(0))
    def _():
        x1_copy(step + ahead).start()

    x1_copy(step).wait()
    x1_ref = x1buf_ref.at[step % POST_BUFFERS]

    lane = lax.broadcasted_iota(jnp.int32, (TOKEN_BLOCK, PACK * SLOT_WINDOW), 1)
    gate2 = mod_ref[0][:, 5 * D_MODEL:6 * D_MODEL]

    def scatter(slot, experts, first, i):
        target = None
        windows = []
        for j, e in enumerate(experts):
            lo = first[j] + i * SLOT_WINDOW
            w = jnp.minimum(lo, cap - SLOT_WINDOW)
            sc = slot[:, e:e + 1]
            col = jnp.where(sc >= lo, sc - w + j * SLOT_WINDOW, -1)
            target = col if target is None else jnp.where(lane < j * SLOT_WINDOW, target, col)
            windows.append(y_ref[e, pl.ds(pl.multiple_of(w, BF16_ROWS), SLOT_WINDOW), :])
        onehot = jnp.where(target == lane, 1.0, 0.0).astype(BF16)
        return _dot(onehot, jnp.concatenate(windows, axis=0))

    def finish(rows):
        x2 = x1_ref[rows, :] + gate2 * acc_ref[rows, :]
        out_ref[rows, :] = _rms(x2) * fg_ref[...]

    parts = []
    for p in range(n_parts):
        rows = slice(p * TOKEN_BLOCK, (p + 1) * TOKEN_BLOCK)
        slot = slot_ref[rows, :]
        groups = []
        total = None
        for g in range(N_EXPERTS // PACK):
            experts = list(range(g * PACK, (g + 1) * PACK))
            first, n_windows = _pack_windows(st_ref, step * n_parts + p, experts, cap)
            groups.append((experts, first, n_windows))
            part = scatter(slot, experts, first, 0)
            total = part if total is None else total + part
        acc_ref[rows, :] = total
        parts.append((rows, slot, groups))
    for rows, _, _ in parts:
        finish(rows)

    for rows, slot, groups in parts:
        most = functools.reduce(jnp.maximum, [n_windows for _, _, n_windows in groups])

        @pl.when(most > 1)
        def _(rows=rows, slot=slot, groups=groups):
            for experts, first, n_windows in groups:
                def more(i, carry, experts=experts, first=first):
                    acc_ref[rows, :] += scatter(slot, experts, first, i)
                    return carry

                lax.fori_loop(1, n_windows, more, 0)
            finish(rows)


def _combine(starts, x1, slot_te, mod_rows, mod_first_row, tokens_per_mod_row, final_g, y):
    t = x1.shape[0]
    cap = y.shape[1]
    assert tokens_per_mod_row is None or tokens_per_mod_row % COMBINE_ROWS == 0
    assert t // COMBINE_ROWS >= POST_BUFFERS - 1
    steps_per_row = None if tokens_per_mod_row is None else tokens_per_mod_row // COMBINE_ROWS
    grid_spec = pltpu.PrefetchScalarGridSpec(
        num_scalar_prefetch=1,
        grid=(t // COMBINE_ROWS,),
        in_specs=[
            pl.BlockSpec(memory_space=pl.ANY),
            pl.BlockSpec((COMBINE_ROWS, N_EXPERTS), lambda b, *_: (b, 0)),
            _mod_row_spec(mod_first_row, steps_per_row),
            pl.BlockSpec((1, D_MODEL), lambda b, *_: (0, 0)),
            pl.BlockSpec(memory_space=pl.ANY),
        ],
        out_specs=pl.BlockSpec((COMBINE_ROWS, D_MODEL), lambda b, *_: (b, 0)),
        scratch_shapes=[pltpu.VMEM((COMBINE_ROWS, D_MODEL), F32), pltpu.VMEM(y.shape, BF16),
                        pltpu.SemaphoreType.DMA((Y_CHUNKS,)), pltpu.SMEM((1,), jnp.int32),
                        pltpu.VMEM((POST_BUFFERS, COMBINE_ROWS, D_MODEL), F32),
                        pltpu.SemaphoreType.DMA((POST_BUFFERS,))],
    )
    return pl.pallas_call(
        functools.partial(_combine_kernel, cap=cap),
        out_shape=jax.ShapeDtypeStruct((t, D_MODEL), F32),
        grid_spec=grid_spec,
        compiler_params=pltpu.CompilerParams(dimension_semantics=("arbitrary",),
                                             vmem_limit_bytes=VMEM_LIMIT),
        name="combine",
    )(starts, x1, slot_te, mod_rows, final_g.reshape(1, D_MODEL), y)


def kernel(x_prompt, x_sample, state_ret, c, c_ctx, norm1_g, norm2_g, final_g, w_mod, b_mod, w_in,
           w_fmix, w_out, w_router, w_gate, w_up, w_down):
    bp, seq, _ = x_prompt.shape
    bs, dec_seq, _ = x_sample.shape
    assert w_mod.shape[0] == 1, "single-layer trunk"
    tp, ts = bp * seq, bs * dec_seq

    cond = jnp.concatenate([c_ctx[None, :], c], axis=0)
    mod = _modulation(cond, w_mod[0], b_mod[0])
    ctx_row, lat_row = 0, 1

    w_out_bf = w_out[0].astype(BF16)
    mix_p, states = _mixer(x_prompt, mod, ctx_row, False, None, True, False, norm1_g[0], w_in[0],
                           w_fmix[0])
    (mix_s,) = _mixer(x_sample, mod, lat_row, True, state_ret[:, 0], False, True, norm1_g[0],
                      w_in[0], w_fmix[0])
    x1p, h2p, affp = _post(x_prompt.reshape(tp, D_MODEL), mix_p, mod, ctx_row, None,
                           norm2_g[0], w_out_bf, w_router[0])
    x1s, h2s, affs = _post(x_sample.reshape(ts, D_MODEL), mix_s, mod, lat_row, dec_seq,
                           norm2_g[0], w_out_bf, w_router[0])

    (slot_p, slot_te_p, gate_p, starts_p), (slot_s, slot_te_s, gate_s, starts_s) = _route(affp, affs)

    yp, ys = _experts(starts_p, starts_s, h2p, h2s, slot_p, slot_s, gate_p, gate_s,
                      w_gate[0], w_up[0], w_down[0])

    out_p = _combine(starts_p, x1p, slot_te_p, mod, ctx_row, None, final_g, yp)
    out_s = _combine(starts_s, x1s, slot_te_s, mod, lat_row, dec_seq, final_g, ys)

    y_prompt = out_p.reshape(bp, seq, D_MODEL)
    y_sample = out_s.reshape(bs, dec_seq, D_MODEL)
    state_new = states.reshape(bp, 1, 2, N_RET_HEADS, HEAD_DIM, HEAD_DIM).astype(x_prompt.dtype)
    return (y_prompt, y_sample, state_new)
```

```python
import functools
import math

import jax
import jax.numpy as jnp
import numpy as np
from jax import lax
from jax.experimental import pallas as pl
from jax.experimental.pallas import tpu as pltpu

D_MODEL = 1024
D_FOURIER = 512
N_FOURIER_GROUPS = 4
FOURIER_GROUP_W = 128
D_RET = 512
N_RET_HEADS = 4
HEAD_DIM = 128
CHUNK = 256
GRID_W = 64
N_EXPERTS = 16
EC_CAPACITY_FACTOR = 2
D_EXPERT_FF = 2816
ROPE_BASE = 10000.0
EPS = 1e-6
D_IN_PROJ = D_FOURIER + 5 * D_RET
LOG_GAMMA_FWD = np.log(1.0 - 2.0 ** (-5.0 - np.arange(N_RET_HEADS))).astype(np.float32)
LOG_GAMMA_BWD = np.log(1.0 - 2.0 ** (-5.5 - np.arange(N_RET_HEADS))).astype(np.float32)

LANES = 128
BF16_ROWS = 16
TOKEN_BLOCK = 256
SLOT_WINDOW = 64
PACK = TOKEN_BLOCK // SLOT_WINDOW
GATHER_UNROLL = 4
Y_CHUNKS = 4
FF_TILE = 256
N_FF_TILES = D_EXPERT_FF // FF_TILE
MOD_TILE = 1024
MIXER_ROWS = 1024
POST_ROWS = 512
POST_PART = 256
POST_BUFFERS = 3
COMBINE_ROWS = 512
VMEM_LIMIT = 56 * 1024 * 1024

F32 = jnp.float32
BF16 = jnp.bfloat16


def _dot(a, b):
    return jnp.dot(a, b, preferred_element_type=F32)


def _dot_nt(a, b):
    return lax.dot_general(a, b, (((1,), (1,)), ((), ())), preferred_element_type=F32)


def _silu(x):
    return x * jax.nn.sigmoid(x)


def _mod_kernel(condt_ref, w_hbm, b_ref, out_ref, wbuf_ref, wsem_ref, *, n_cond, n_tiles):
    def tile_copy(j):
        cols = pl.ds(j * MOD_TILE, MOD_TILE)
        return pltpu.make_async_copy(w_hbm.at[:, cols], wbuf_ref.at[j], wsem_ref.at[j])

    for j in range(n_tiles):
        tile_copy(j).start()
    s = _silu(condt_ref[...])
    out_ref[...] = jnp.zeros(out_ref.shape, F32)
    for j in range(n_tiles):
        cols = pl.ds(j * MOD_TILE, MOD_TILE)
        tile_copy(j).wait()
        w = wbuf_ref[j]
        for r in range(n_cond):
            out_ref[r, :, cols] = jnp.sum(w * s[:, r:r + 1], axis=0, keepdims=True) + b_ref[:, cols]


def _modulation(cond_rows, w_mod, b_mod):
    n_cond = cond_rows.shape[0]
    condt = jnp.zeros((D_MODEL, 8), F32).at[:, :n_cond].set(cond_rows.T)
    n_out = w_mod.shape[1]
    n_tiles = n_out // MOD_TILE
    return pl.pallas_call(
        functools.partial(_mod_kernel, n_cond=n_cond, n_tiles=n_tiles),
        out_shape=jax.ShapeDtypeStruct((8, 1, n_out), F32),
        grid=(1,),
        in_specs=[
            pl.BlockSpec((D_MODEL, 8), lambda j: (0, 0)),
            pl.BlockSpec(memory_space=pl.ANY),
            pl.BlockSpec((1, n_out), lambda j: (0, 0)),
        ],
        out_specs=pl.BlockSpec((8, 1, n_out), lambda j: (0, 0, 0)),
        scratch_shapes=[
            pltpu.VMEM((n_tiles, D_MODEL, MOD_TILE), F32),
            pltpu.SemaphoreType.DMA((n_tiles,)),
        ],
        compiler_params=pltpu.CompilerParams(
            dimension_semantics=("arbitrary",), vmem_limit_bytes=VMEM_LIMIT),
        name="mod",
    )(condt, w_mod, b_mod.reshape(1, n_out))


def _rms(x):
    return x * lax.rsqrt(jnp.mean(x * x, axis=-1, keepdims=True) + EPS)


def _groupnorm(o):
    mu = jnp.mean(o, axis=-1, keepdims=True)
    c = o - mu
    return c * lax.rsqrt(jnp.mean(c * c, axis=-1, keepdims=True) + EPS)


def _split_hi_lo(x):
    hi = x.astype(BF16)
    lo = (x - hi.astype(F32)).astype(BF16)
    return hi, lo


def _mixer_kernel(*refs, n, use_rope, has_state_in, emit_state):
    it = iter(refs)
    x_ref, mod_ref, g1_ref, win_ref, wfmix_ref = (next(it) for _ in range(5))
    cw_ref, cn_ref, sn_ref, dmat_ref, qdec_ref, kdec_ref, sdec_ref = (next(it) for _ in range(7))
    cos_ref = sin_ref = s0_ref = st_ref = None
    if use_rope:
        cos_ref, sin_ref = next(it), next(it)
    if has_state_in:
        s0_ref = next(it)
    mix_ref = next(it)
    if emit_state:
        st_ref = next(it)
    p_ref, of_ref, ob_ref = next(it), next(it), next(it)

    n_seq = MIXER_ROWS // n
    chunks_per_seq = n // CHUNK
    mod = mod_ref[0]
    shift1 = mod[:, 0:D_MODEL]
    scale1 = mod[:, D_MODEL:2 * D_MODEL]

    h = (_rms(x_ref[...]) * g1_ref[...] * (1.0 + scale1) + shift1).astype(BF16)
    for j in range(D_IN_PROJ // 512):
        p_ref[:, j * 512:(j + 1) * 512] = _dot(h, win_ref[:, j * 512:(j + 1) * 512].astype(BF16))

    xf = p_ref[:, 0:D_FOURIER].astype(BF16)
    xc, xs = [], []
    cw = cw_ref[...].astype(BF16)
    for g in range(N_FOURIER_GROUPS):
        t = _dot(xf[:, g * FOURIER_GROUP_W:(g + 1) * FOURIER_GROUP_W], cw)
        xc.append(t[:, :FOURIER_GROUP_W].astype(BF16))
        xs.append(t[:, FOURIER_GROUP_W:].astype(BF16))
    xc = jnp.concatenate(xc, axis=1)
    xs = jnp.concatenate(xs, axis=1)
    cn = cn_ref[...].astype(BF16)
    sn = sn_ref[...].astype(BF16)
    for s in range(n_seq):
        rs = slice(s * n, (s + 1) * n)
        fre = (_dot(cn, xc[rs]) - _dot(sn, xs[rs])) * (1.0 / math.sqrt(n * FOURIER_GROUP_W))
        fre = fre.astype(BF16)
        for g in range(N_FOURIER_GROUPS):
            sl = slice(g * FOURIER_GROUP_W, (g + 1) * FOURIER_GROUP_W)
            mix_ref[rs, sl] = _dot(fre[:, sl], wfmix_ref[g].astype(BF16)).astype(BF16)

    for hh in range(N_RET_HEADS):
        base = D_FOURIER + hh * HEAD_DIM
        q = p_ref[:, base:base + HEAD_DIM]
        k = p_ref[:, base + D_RET:base + D_RET + HEAD_DIM]
        v = p_ref[:, base + 2 * D_RET:base + 2 * D_RET + HEAD_DIM]
        if use_rope:
            lane = lax.broadcasted_iota(jnp.int32, (MIXER_ROWS, HEAD_DIM), 1)
            first = (lane % 64) < 32

            def rope(t):
                swapped = jnp.where(first, pltpu.roll(t, HEAD_DIM - 32, 1), pltpu.roll(t, 32, 1))
                return t * cos_ref[...] + swapped * sin_ref[...]

            q, k = rope(q), rope(k)
        k = k * (HEAD_DIM ** -0.5)
        qb, vb = q.astype(BF16), v.astype(BF16)
        kb = k.astype(BF16)

        def initial(s, direction):
            if has_state_in:
                return s0_ref[s, direction, hh]
            return jnp.zeros((HEAD_DIM, HEAD_DIM), F32)

        for s in range(n_seq):
            parts = []
            for ci in range(chunks_per_seq):
                c = s * chunks_per_seq + ci
                rs = slice(c * CHUNK, (c + 1) * CHUNK)
                qc, kc, vc = qb[rs], kb[rs], vb[rs]
                qk = _dot_nt(qc, kc)
                lhs = jnp.concatenate([(qk * dmat_ref[0, hh]).astype(BF16),
                                       (qk * dmat_ref[1, hh]).astype(BF16),
                                       (k[rs] * kdec_ref[0, hh]).T.astype(BF16),
                                       (k[rs] * kdec_ref[1, hh]).T.astype(BF16)], axis=0)
                parts.append((rs, qc, _dot(lhs, vc)))
            sf = initial(s, 0)
            for ci in range(chunks_per_seq):
                rs, qc, r = parts[ci]
                o = r[0:CHUNK]
                if has_state_in or ci > 0:
                    o = o + qdec_ref[0, hh] * _dot(qc, sf.astype(BF16))
                of_ref[rs, :] = o
                sf = sf * sdec_ref[0, hh] + r[2 * CHUNK:2 * CHUNK + HEAD_DIM]
            sb = initial(s, 1)
            for ci in reversed(range(chunks_per_seq)):
                rs, qc, r = parts[ci]
                o = r[CHUNK:2 * CHUNK]
                if has_state_in or ci < chunks_per_seq - 1:
                    o = o + qdec_ref[1, hh] * _dot(qc, sb.astype(BF16))
                ob_ref[rs, :] = o
                sb = sb * sdec_ref[1, hh] + r[2 * CHUNK + HEAD_DIM:]
            if emit_state:
                st_ref[s, 0, hh] = sf
                st_ref[s, 1, hh] = sb

        gf = p_ref[:, base + 3 * D_RET:base + 3 * D_RET + HEAD_DIM]
        gb = p_ref[:, base + 4 * D_RET:base + 4 * D_RET + HEAD_DIM]
        y = _silu(gf) * _groupnorm(of_ref[...]) + _silu(gb) * _groupnorm(ob_ref[...])
        mix_ref[:, base:base + HEAD_DIM] = y.astype(BF16)


def _post_kernel(x_hbm, mix_hbm, mod_ref, g2_ref, wout_ref, wr_ref, x1_ref, h2_ref, aff_ref,
                 xbuf_ref, mixbuf_ref, xsem_ref, mixsem_ref):
    i = pl.program_id(0)
    ahead = POST_BUFFERS - 1

    def block_copies(b):
        rows = pl.ds(pl.multiple_of(b * POST_ROWS, POST_ROWS), POST_ROWS)
        slot = b % POST_BUFFERS
        return (pltpu.make_async_copy(x_hbm.at[rows], xbuf_ref.at[slot], xsem_ref.at[slot]),
                pltpu.make_async_copy(mix_hbm.at[rows], mixbuf_ref.at[slot], mixsem_ref.at[slot]))

    @pl.when(i == 0)
    def _():
        for b in range(ahead):
            for copy in block_copies(b):
                copy.start()

    @pl.when(i + ahead < pl.num_programs(0))
    def _():
        for copy in block_copies(i + ahead):
            copy.start()

    for copy in block_copies(i):
        copy.wait()
    x_ref = xbuf_ref.at[i % POST_BUFFERS]
    mix_ref = mixbuf_ref.at[i % POST_BUFFERS]
    mod = mod_ref[0]
    gate1 = mod[:, 2 * D_MODEL:3 * D_MODEL]
    shift2 = mod[:, 3 * D_MODEL:4 * D_MODEL]
    scale2 = mod[:, 4 * D_MODEL:5 * D_MODEL]
    wr_hi, wr_lo = _split_hi_lo(wr_ref[...])
    wr_both = jnp.concatenate([wr_hi, wr_lo], axis=1)
    parts = [slice(p * POST_PART, (p + 1) * POST_PART) for p in range(POST_ROWS // POST_PART)]
    x1 = []
    for rows in parts:
        x1.append(x_ref[rows, :] + gate1 * _dot(mix_ref[rows, :], wout_ref[...]))
        x1_ref[rows, :] = x1[-1]
    for rows, x1_part in zip(parts, x1):
        h2 = _rms(x1_part) * g2_ref[...] * (1.0 + scale2) + shift2
        h2_hi, h2_lo = _split_hi_lo(h2)
        h2_ref[rows, :] = h2_hi
        by_hi = _dot(h2_hi, wr_both)
        logits = by_hi[:, :N_EXPERTS] + (_dot(h2_lo, wr_hi) + by_hi[:, N_EXPERTS:])
        z = jnp.exp(logits - jnp.max(logits, axis=-1, keepdims=True))
        aff = z / jnp.sum(z, axis=-1, keepdims=True)
        lanes = jnp.concatenate([aff, jnp.zeros((POST_PART, LANES - N_EXPERTS), F32)], axis=1)
        aff_ref[:, rows] = lanes.T[0:N_EXPERTS, :]


def _dft_consts(n):
    w = FOURIER_GROUP_W
    jw = np.arange(w)
    angw = 2.0 * np.pi * np.outer(jw, jw) / w
    cw = np.concatenate([np.cos(angw), np.sin(angw)], axis=1)
    jn = np.arange(n)
    angn = 2.0 * np.pi * (np.outer(jn, jn) % n) / n
    return (jnp.asarray(cw, F32), jnp.asarray(np.cos(angn), F32), jnp.asarray(np.sin(angn), F32))


def _retention_consts():
    i = np.arange(CHUNK, dtype=np.float64)
    diff = i[:, None] - i[None, :]
    dmat = np.zeros((2, N_RET_HEADS, CHUNK, CHUNK))
    qdec = np.zeros((2, N_RET_HEADS, CHUNK, HEAD_DIM))
    kdec = np.zeros((2, N_RET_HEADS, CHUNK, HEAD_DIM))
    sdec = np.zeros((2, N_RET_HEADS, HEAD_DIM, HEAD_DIM))
    for hh in range(N_RET_HEADS):
        lf = float(LOG_GAMMA_FWD[hh])
        lb = float(LOG_GAMMA_BWD[hh])
        dmat[0, hh] = np.where(diff >= 0, np.exp(lf * np.maximum(diff, 0.0)), 0.0)
        dmat[1, hh] = np.where(diff <= 0, np.exp(lb * np.maximum(-diff, 0.0)), 0.0)
        qdec[0, hh] = np.exp(lf * (i + 1.0))[:, None]
        qdec[1, hh] = np.exp(lb * (CHUNK - i))[:, None]
        kdec[0, hh] = np.exp(lf * (CHUNK - 1.0 - i))[:, None]
        kdec[1, hh] = np.exp(lb * i)[:, None]
        sdec[0, hh] = math.exp(lf * CHUNK)
        sdec[1, hh] = math.exp(lb * CHUNK)
    return tuple(jnp.asarray(a, F32) for a in (dmat, qdec, kdec, sdec))


def _rope_consts(n):
    rows_n = n // GRID_W
    row = np.repeat(np.arange(rows_n, dtype=np.float64), GRID_W)
    col = np.tile(np.arange(GRID_W, dtype=np.float64), rows_n)
    n_pairs = HEAD_DIM // 4
    freqs = (np.float32(ROPE_BASE) ** (-np.arange(n_pairs, dtype=np.float32) / n_pairs)).astype(np.float64)
    ar = row[:, None] * freqs[None, :]
    ac = col[:, None] * freqs[None, :]
    cos = np.concatenate([np.cos(ar), np.cos(ar), np.cos(ac), np.cos(ac)], axis=1)
    sin = np.concatenate([-np.sin(ar), np.sin(ar), -np.sin(ac), np.sin(ac)], axis=1)
    return jnp.asarray(cos, F32), jnp.asarray(sin, F32)


def _const_spec(shape):
    nd = len(shape)
    return pl.BlockSpec(shape, lambda b, _nd=nd: (0,) * _nd, pipeline_mode=pl.Buffered(1))


def _mod_row_spec(first_row, blocks_per_row):
    if blocks_per_row is None:
        return pl.BlockSpec((1, 1, 6 * D_MODEL), lambda b, *_: (first_row, 0, 0))
    return pl.BlockSpec((1, 1, 6 * D_MODEL), lambda b, *_: (first_row + b // blocks_per_row, 0, 0))


def _mixer(x, mod_rows, mod_first_row, mod_per_batch, state_in, emit_state, use_rope, g1, w_in,
           w_fmix):
    nb, n, _ = x.shape
    assert MIXER_ROWS % n == 0 and (nb * n) % MIXER_ROWS == 0
    n_seq = MIXER_ROWS // n
    has_state_in = state_in is not None
    cw, cn, sn = _dft_consts(n)
    dmat, qdec, kdec, sdec = _retention_consts()
    consts = [cw, cn, sn, dmat, qdec, kdec, sdec]
    if use_rope:
        assert n_seq == 1
        consts += list(_rope_consts(n))
    weights = [g1.reshape(1, D_MODEL), w_in, w_fmix]

    if mod_per_batch:
        assert n % MIXER_ROWS == 0
    state_spec = pl.BlockSpec((n_seq, 2, N_RET_HEADS, HEAD_DIM, HEAD_DIM), lambda b: (b, 0, 0, 0, 0))
    row_spec = pl.BlockSpec((MIXER_ROWS, D_MODEL), lambda b: (b, 0))
    in_specs = [row_spec, _mod_row_spec(mod_first_row, n // MIXER_ROWS if mod_per_batch else None)]
    in_specs += [_const_spec(a.shape) for a in weights + consts]
    args = [x.reshape(nb * n, D_MODEL), mod_rows] + weights + consts
    if has_state_in:
        in_specs.append(state_spec)
        args.append(state_in)

    out_shape = [jax.ShapeDtypeStruct((nb * n, D_MODEL), BF16)]
    out_specs = [row_spec]
    if emit_state:
        out_shape.append(jax.ShapeDtypeStruct((nb, 2, N_RET_HEADS, HEAD_DIM, HEAD_DIM), F32))
        out_specs.append(state_spec)

    return pl.pallas_call(
        functools.partial(_mixer_kernel, n=n, use_rope=use_rope, has_state_in=has_state_in,
                          emit_state=emit_state),
        out_shape=out_shape,
        grid=(nb * n // MIXER_ROWS,),
        in_specs=in_specs,
        out_specs=out_specs,
        scratch_shapes=[pltpu.VMEM((MIXER_ROWS, D_IN_PROJ), F32),
                        pltpu.VMEM((MIXER_ROWS, HEAD_DIM), F32), pltpu.VMEM((MIXER_ROWS, HEAD_DIM), F32)],
        compiler_params=pltpu.CompilerParams(dimension_semantics=("arbitrary",),
                                             vmem_limit_bytes=VMEM_LIMIT),
        name="mixer_rope" if use_rope else "mixer",
    )(*args)


def _post(x, mix, mod_rows, mod_first_row, tokens_per_mod_row, g2, w_out_bf, w_router):
    t = x.shape[0]
    assert tokens_per_mod_row is None or tokens_per_mod_row % POST_ROWS == 0
    blocks_per_row = None if tokens_per_mod_row is None else tokens_per_mod_row // POST_ROWS
    row_spec = pl.BlockSpec((POST_ROWS, D_MODEL), lambda b: (b, 0))
    assert t // POST_ROWS >= POST_BUFFERS - 1
    return pl.pallas_call(
        _post_kernel,
        out_shape=[jax.ShapeDtypeStruct((t, D_MODEL), F32),
                   jax.ShapeDtypeStruct((t, D_MODEL), BF16),
                   jax.ShapeDtypeStruct((N_EXPERTS, t), F32)],
        grid=(t // POST_ROWS,),
        in_specs=[pl.BlockSpec(memory_space=pl.ANY), pl.BlockSpec(memory_space=pl.ANY),
                  _mod_row_spec(mod_first_row, blocks_per_row),
                  _const_spec((1, D_MODEL)), _const_spec((D_MODEL, D_MODEL)),
                  _const_spec((D_MODEL, N_EXPERTS))],
        out_specs=[row_spec, row_spec, pl.BlockSpec((N_EXPERTS, POST_ROWS), lambda b: (0, b))],
        scratch_shapes=[pltpu.VMEM((POST_BUFFERS, POST_ROWS, D_MODEL), F32),
                        pltpu.VMEM((POST_BUFFERS, POST_ROWS, D_MODEL), BF16),
                        pltpu.SemaphoreType.DMA((POST_BUFFERS,)), pltpu.SemaphoreType.DMA((POST_BUFFERS,))],
        compiler_params=pltpu.CompilerParams(dimension_semantics=("arbitrary",),
                                             vmem_limit_bytes=VMEM_LIMIT),
        name="post",
    )(x, mix, mod_rows, g2.reshape(1, D_MODEL), w_out_bf, w_router)


def _route_kernel(*refs, sizes):
    n = len(sizes)
    affs = [ref[...] for ref in refs[:n]]
    u_ref = refs[n]
    outs = [refs[n + 1 + 4 * g:n + 5 + 4 * g] for g in range(n)]

    def count(mask):
        return jnp.sum(mask.astype(jnp.int32), axis=1, keepdims=True)

    def as_float(word):
        return lax.bitcast_convert_type(word, F32)

    def zeros():
        return tuple(jnp.zeros((N_EXPERTS, 1), jnp.int32) for _ in sizes)

    def value_step(i, curs):
        bit = jnp.left_shift(jnp.int32(1), 30 - i)
        return tuple(jnp.where(count(aff >= as_float(cur | bit)) >= cap, cur | bit, cur)
                     for aff, (_, cap), cur in zip(affs, sizes, curs))

    thrs = lax.fori_loop(0, 31, value_step, zeros())
    gts = [aff >= as_float(thr + 1) for aff, thr in zip(affs, thrs)]
    eqs = [(aff >= as_float(thr)) & jnp.logical_not(gt) for aff, thr, gt in zip(affs, thrs, gts)]
    needs = [cap - count(gt) for (_, cap), gt in zip(sizes, gts)]
    toks = [lax.broadcasted_iota(jnp.int32, (N_EXPERTS, t), 1) for t, _ in sizes]
    nbits = [t.bit_length() - 1 for t, _ in sizes]

    def index_step(i, curs):
        new = []
        for eq, need, tok, bits, cur in zip(eqs, needs, toks, nbits, curs):
            shift = bits - 1 - i
            cand = cur | jnp.where(shift >= 0, jnp.left_shift(jnp.int32(1), jnp.maximum(shift, 0)), 0)
            new.append(jnp.where(count(eq & (tok < cand)) < need, cand, cur))
        return tuple(new)

    lasts = lax.fori_loop(0, max(nbits), index_step, zeros())

    for aff, (t, _), gt, eq, tok, last, (slot_ref, slot_te_ref, gate_ref, starts_ref) in zip(
            affs, sizes, gts, eqs, toks, lasts, outs):
        self = jnp.where(gt | (eq & (tok <= last)), 1.0, 0.0).astype(F32)
        carry = jnp.zeros((N_EXPERTS, 1), F32)
        starts_ref[...] = jnp.zeros(starts_ref.shape, jnp.int32)
        for b in range(t // TOKEN_BLOCK):
            sl = slice(b * TOKEN_BLOCK, (b + 1) * TOKEN_BLOCK)
            sbf = self[:, sl]
            pre = _dot(sbf.astype(BF16), u_ref[...]) + carry
            slots = jnp.where(sbf > 0.5, pre.astype(jnp.int32), -1)
            slot_ref[:, b, :] = slots
            rows = jnp.concatenate([slots, jnp.zeros((LANES - N_EXPERTS, TOKEN_BLOCK), jnp.int32)],
                                   axis=0)
            slot_te_ref[sl, :] = rows.T[:, 0:N_EXPERTS]
            gate_ref[:, b, :] = aff[:, sl]
            starts_ref[:, b:b + 1] = carry.astype(jnp.int32)
            carry = carry + jnp.sum(sbf, axis=1, keepdims=True)
        nblk = t // TOKEN_BLOCK
        starts_ref[:, nblk:nblk + 1] = carry.astype(jnp.int32)


def _route(*affs_et):
    sizes = tuple((a.shape[1], EC_CAPACITY_FACTOR * a.shape[1] // N_EXPERTS) for a in affs_et)
    upper = jnp.asarray(np.triu(np.ones((TOKEN_BLOCK, TOKEN_BLOCK)), 1), BF16)
    whole = lambda shape: pl.BlockSpec(shape, lambda i, _n=len(shape): (0,) * _n)
    out_shape, out_specs = [], []
    for t, _ in sizes:
        nblk = t // TOKEN_BLOCK
        assert nblk + 1 <= LANES
        for shape, dtype in (((N_EXPERTS, nblk, TOKEN_BLOCK), jnp.int32), ((t, N_EXPERTS), jnp.int32),
                             ((N_EXPERTS, nblk, TOKEN_BLOCK), F32), ((N_EXPERTS, LANES), jnp.int32)):
            out_shape.append(jax.ShapeDtypeStruct(shape, dtype))
            out_specs.append(whole(shape))
    outs = pl.pallas_call(
        functools.partial(_route_kernel, sizes=sizes),
        out_shape=out_shape,
        grid=(1,),
        in_specs=[whole(a.shape) for a in affs_et] + [whole((TOKEN_BLOCK, TOKEN_BLOCK))],
        out_specs=out_specs,
        compiler_params=pltpu.CompilerParams(dimension_semantics=("arbitrary",)),
        name="route",
    )(*affs_et, upper)
    return [outs[4 * g:4 * g + 4] for g in range(len(sizes))]


def _pack_windows(starts_ref, b, experts, cap):
    first = [jnp.minimum((starts_ref[e, b] // BF16_ROWS) * BF16_ROWS, cap - SLOT_WINDOW)
             for e in experts]
    rows = [jnp.where(starts_ref[e, b + 1] > starts_ref[e, b], starts_ref[e, b + 1] - w, 0)
            for e, w in zip(experts, first)]
    return first, pl.cdiv(functools.reduce(jnp.maximum, rows), SLOT_WINDOW)


def _block_copy(hbm_ref, vmem_ref, sem_ref, b):
    rows = pl.ds(pl.multiple_of(b * TOKEN_BLOCK, TOKEN_BLOCK), TOKEN_BLOCK)
    return pltpu.make_async_copy(hbm_ref.at[rows], vmem_ref.at[rows], sem_ref.at[b])


def _gather_group(g, starts_ref, slot_ref, gate_ref, h2_ref, xs_ref, gs_ref, row0, t, cap, arrive):
    sub = lax.broadcasted_iota(jnp.int32, (SLOT_WINDOW, TOKEN_BLOCK), 0)
    experts = [g * PACK + j for j in range(PACK)]
    assert (t // TOKEN_BLOCK) % GATHER_UNROLL == 0

    def window(b, first, i):
        hb = pl.ds(pl.multiple_of(b * TOKEN_BLOCK, TOKEN_BLOCK), TOKEN_BLOCK)
        hits, dst = [], []
        for j in range(PACK):
            lo = first[j] + i * SLOT_WINDOW
            w = jnp.minimum(lo, cap - SLOT_WINDOW)
            srow = slot_ref[j, pl.ds(b, 1), :]
            hits.append((srow == w + sub) & (srow >= lo))
            dst.append(pl.ds(pl.multiple_of(row0 + w, BF16_ROWS), SLOT_WINDOW))
        onehot = jnp.concatenate([jnp.where(h, 1.0, 0.0) for h in hits], axis=0).astype(BF16)
        got = _dot(onehot, h2_ref[hb, :])
        for j in range(PACK):
            piece = got[j * SLOT_WINDOW:(j + 1) * SLOT_WINDOW].astype(BF16)
            xs_ref[j, dst[j], :] = xs_ref[j, dst[j], :] + piece
            grow = gate_ref[j, pl.ds(b, 1), :]
            gs_ref[j, dst[j], :] += jnp.sum(jnp.where(hits[j], grow, 0.0), axis=1, keepdims=True)

    def blocks(q, carry):
        pending = []
        for u in range(GATHER_UNROLL):
            arrive(q * GATHER_UNROLL + u)
        for u in range(GATHER_UNROLL):
            b = q * GATHER_UNROLL + u
            first, n_windows = _pack_windows(starts_ref, b, experts, cap)
            window(b, first, 0)
            pending.append((b, first, n_windows))
        for b, first, n_windows in pending:
            def more(i, carry, b=b, first=first):
                window(b, first, i)
                return carry

            lax.fori_loop(1, n_windows, more, 0)
        return carry

    lax.fori_loop(0, t // TOKEN_BLOCK // GATHER_UNROLL, blocks, 0)


def _experts_kernel(sp_ref, ss_ref, h2p_hbm, h2s_hbm, slotp_ref, slots_ref, gatep_ref, gates_ref,
                    wg_ref, wu_ref, wd_ref, yp_ref, ys_ref, xs_ref, gs_ref, acc_ref,
                    h2p_ref, h2s_ref, semp_ref, sems_ref, *, tp, ts, capp, caps):
    g = pl.program_id(0)
    step = pl.program_id(1)
    j = step // N_FF_TILES
    f = step % N_FF_TILES
    loading = [(h2p_hbm, h2p_ref, semp_ref, tp // TOKEN_BLOCK),
               (h2s_hbm, h2s_ref, sems_ref, ts // TOKEN_BLOCK)]

    @pl.when((step == 0) & (g == 0))
    def _():
        for hbm_ref, vmem_ref, sem_ref, n_blocks in loading:
            for b in range(n_blocks):
                _block_copy(hbm_ref, vmem_ref, sem_ref, b).start()

    def arrive(hbm_ref, vmem_ref, sem_ref, _):
        def wait(b):
            @pl.when(g == 0)
            def _():
                _block_copy(hbm_ref, vmem_ref, sem_ref, b).wait()

        return wait

    @pl.when(step == 0)
    def _():
        xs_ref[...] = jnp.zeros(xs_ref.shape, BF16)
        gs_ref[...] = jnp.zeros(gs_ref.shape, F32)
        _gather_group(g, sp_ref, slotp_ref, gatep_ref, h2p_ref, xs_ref, gs_ref, 0, tp, capp,
                      arrive(*loading[0]))
        _gather_group(g, ss_ref, slots_ref, gates_ref, h2s_ref, xs_ref, gs_ref, capp, ts, caps,
                      arrive(*loading[1]))

    @pl.when(f == 0)
    def _():
        acc_ref[...] = jnp.zeros(acc_ref.shape, F32)

    x = xs_ref[j]
    a = _dot(x, wg_ref[0].astype(BF16))
    u = _dot(x, wu_ref[0].astype(BF16))
    acc_ref[...] += _dot((_silu(a) * u).astype(BF16), wd_ref[0].astype(BF16))

    @pl.when(f == N_FF_TILES - 1)
    def _():
        yp_ref[0] = (acc_ref[0:capp, :] * gs_ref[j, 0:capp, :]).astype(BF16)
        ys_ref[0] = (acc_ref[capp:capp + caps, :] * gs_ref[j, capp:capp + caps, :]).astype(BF16)


def _experts(starts_p, starts_s, h2p, h2s, slot_p, slot_s, gate_p, gate_s, w_gate, w_up, w_down):
    tp, ts = h2p.shape[0], h2s.shape[0]
    capp = EC_CAPACITY_FACTOR * tp // N_EXPERTS
    caps = EC_CAPACITY_FACTOR * ts // N_EXPERTS
    rows = capp + caps
    nbp, nbs = tp // TOKEN_BLOCK, ts // TOKEN_BLOCK
    expert = lambda g, s: g * PACK + s // N_FF_TILES
    grid_spec = pltpu.PrefetchScalarGridSpec(
        num_scalar_prefetch=2,
        grid=(N_EXPERTS // PACK, PACK * N_FF_TILES),
        in_specs=[
            pl.BlockSpec(memory_space=pl.ANY),
            pl.BlockSpec(memory_space=pl.ANY),
            pl.BlockSpec((PACK, nbp, TOKEN_BLOCK), lambda g, s, *_: (g, 0, 0)),
            pl.BlockSpec((PACK, nbs, TOKEN_BLOCK), lambda g, s, *_: (g, 0, 0)),
            pl.BlockSpec((PACK, nbp, TOKEN_BLOCK), lambda g, s, *_: (g, 0, 0)),
            pl.BlockSpec((PACK, nbs, TOKEN_BLOCK), lambda g, s, *_: (g, 0, 0)),
            pl.BlockSpec((1, D_MODEL, FF_TILE), lambda g, s, *_: (expert(g, s), 0, s % N_FF_TILES)),
            pl.BlockSpec((1, D_MODEL, FF_TILE), lambda g, s, *_: (expert(g, s), 0, s % N_FF_TILES)),
            pl.BlockSpec((1, FF_TILE, D_MODEL), lambda g, s, *_: (expert(g, s), s % N_FF_TILES, 0)),
        ],
        out_specs=[
            pl.BlockSpec((1, capp, D_MODEL), lambda g, s, *_: (expert(g, s), 0, 0)),
            pl.BlockSpec((1, caps, D_MODEL), lambda g, s, *_: (expert(g, s), 0, 0)),
        ],
        scratch_shapes=[pltpu.VMEM((PACK, rows, D_MODEL), BF16), pltpu.VMEM((PACK, rows, 1), F32),
                        pltpu.VMEM((rows, D_MODEL), F32),
                        pltpu.VMEM((tp, D_MODEL), BF16), pltpu.VMEM((ts, D_MODEL), BF16),
                        pltpu.SemaphoreType.DMA((nbp,)), pltpu.SemaphoreType.DMA((nbs,))],
    )
    return pl.pallas_call(
        functools.partial(_experts_kernel, tp=tp, ts=ts, capp=capp, caps=caps),
        out_shape=[jax.ShapeDtypeStruct((N_EXPERTS, capp, D_MODEL), BF16),
                   jax.ShapeDtypeStruct((N_EXPERTS, caps, D_MODEL), BF16)],
        grid_spec=grid_spec,
        compiler_params=pltpu.CompilerParams(dimension_semantics=("arbitrary", "arbitrary"),
                                             vmem_limit_bytes=VMEM_LIMIT),
        name="experts",
    )(starts_p, starts_s, h2p, h2s, slot_p, slot_s, gate_p, gate_s, w_gate, w_up, w_down)


def _combine_kernel(st_ref, x1_ref, slot_ref, mod_ref, fg_ref, y_hbm, out_ref, acc_ref, y_ref,
                    sem_ref, arrived_ref, *, cap):
    step = pl.program_id(0)
    n_chunks = Y_CHUNKS
    chunk_rows = cap // Y_CHUNKS
    n_parts = COMBINE_ROWS // TOKEN_BLOCK
    last_block = step * n_parts + n_parts - 1

    def chunk_copy(c):
        rows = pl.ds(c * chunk_rows, chunk_rows)
        return pltpu.make_async_copy(y_hbm.at[:, rows, :], y_ref.at[:, rows, :], sem_ref.at[c])

    @pl.when(step == 0)
    def _():
        arrived_ref[0] = 0
        for c in range(n_chunks):
            chunk_copy(c).start()

    top = functools.reduce(jnp.maximum, [st_ref[e, last_block + 1] for e in range(N_EXPERTS)])
    want = pl.cdiv(jnp.minimum(top + SLOT_WINDOW, cap), chunk_rows)
    have = arrived_ref[0]
    for c in range(n_chunks):
        @pl.when((c >= have) & (c < want))
        def _(c=c):
            chunk_copy(c).wait()
    arrived_ref[0] = jnp.maximum(have, want)

    lane = lax.broadcasted_iota(jnp.int32, (TOKEN_BLOCK, PACK * SLOT_WINDOW), 1)
    gate2 = mod_ref[0][:, 5 * D_MODEL:6 * D_MODEL]

    def scatter(slot, experts, first, i):
        target = None
        windows = []
        for j, e in enumerate(experts):
            lo = first[j] + i * SLOT_WINDOW
            w = jnp.minimum(lo, cap - SLOT_WINDOW)
            sc = slot[:, e:e + 1]
            col = jnp.where(sc >= lo, sc - w + j * SLOT_WINDOW, -1)
            target = col if target is None else jnp.where(lane < j * SLOT_WINDOW, target, col)
            windows.append(y_ref[e, pl.ds(pl.multiple_of(w, BF16_ROWS), SLOT_WINDOW), :])
        onehot = jnp.where(target == lane, 1.0, 0.0).astype(BF16)
        return _dot(onehot, jnp.concatenate(windows, axis=0))

    def finish(rows):
        x2 = x1_ref[rows, :] + gate2 * acc_ref[rows, :]
        out_ref[rows, :] = _rms(x2) * fg_ref[...]

    parts = []
    for p in range(n_parts):
        rows = slice(p * TOKEN_BLOCK, (p + 1) * TOKEN_BLOCK)
        slot = slot_ref[rows, :]
        groups = []
        total = None
        for g in range(N_EXPERTS // PACK):
            experts = list(range(g * PACK, (g + 1) * PACK))
            first, n_windows = _pack_windows(st_ref, step * n_parts + p, experts, cap)
            groups.append((experts, first, n_windows))
            part = scatter(slot, experts, first, 0)
            total = part if total is None else total + part
        acc_ref[rows, :] = total
        parts.append((rows, slot, groups))
    for rows, _, _ in parts:
        finish(rows)

    for rows, slot, groups in parts:
        most = functools.reduce(jnp.maximum, [n_windows for _, _, n_windows in groups])

        @pl.when(most > 1)
        def _(rows=rows, slot=slot, groups=groups):
            for experts, first, n_windows in groups:
                def more(i, carry, experts=experts, first=first):
                    acc_ref[rows, :] += scatter(slot, experts, first, i)
                    return carry

                lax.fori_loop(1, n_windows, more, 0)
            finish(rows)


def _combine(starts, x1, slot_te, mod_rows, mod_first_row, tokens_per_mod_row, final_g, y):
    t = x1.shape[0]
    cap = y.shape[1]
    assert tokens_per_mod_row is None or tokens_per_mod_row % COMBINE_ROWS == 0
    steps_per_row = None if tokens_per_mod_row is None else tokens_per_mod_row // COMBINE_ROWS
    grid_spec = pltpu.PrefetchScalarGridSpec(
        num_scalar_prefetch=1,
        grid=(t // COMBINE_ROWS,),
        in_specs=[
            pl.BlockSpec((COMBINE_ROWS, D_MODEL), lambda b, *_: (b, 0)),
            pl.BlockSpec((COMBINE_ROWS, N_EXPERTS), lambda b, *_: (b, 0)),
            _mod_row_spec(mod_first_row, steps_per_row),
            pl.BlockSpec((1, D_MODEL), lambda b, *_: (0, 0)),
            pl.BlockSpec(memory_space=pl.ANY),
        ],
        out_specs=pl.BlockSpec((COMBINE_ROWS, D_MODEL), lambda b, *_: (b, 0)),
        scratch_shapes=[pltpu.VMEM((COMBINE_ROWS, D_MODEL), F32), pltpu.VMEM(y.shape, BF16),
                        pltpu.SemaphoreType.DMA((Y_CHUNKS,)), pltpu.SMEM((1,), jnp.int32)],
    )
    return pl.pallas_call(
        functools.partial(_combine_kernel, cap=cap),
        out_shape=jax.ShapeDtypeStruct((t, D_MODEL), F32),
        grid_spec=grid_spec,
        compiler_params=pltpu.CompilerParams(dimension_semantics=("arbitrary",),
                                             vmem_limit_bytes=VMEM_LIMIT),
        name="combine",
    )(starts, x1, slot_te, mod_rows, final_g.reshape(1, D_MODEL), y)


def kernel(x_prompt, x_sample, state_ret, c, c_ctx, norm1_g, norm2_g, final_g, w_mod, b_mod, w_in,
           w_fmix, w_out, w_router, w_gate, w_up, w_down):
    bp, seq, _ = x_prompt.shape
    bs, dec_seq, _ = x_sample.shape
    assert w_mod.shape[0] == 1, "single-layer trunk"
    tp, ts = bp * seq, bs * dec_seq

    cond = jnp.concatenate([c_ctx[None, :], c], axis=0)
    mod = _modulation(cond, w_mod[0], b_mod[0])
    ctx_row, lat_row = 0, 1

    w_out_bf = w_out[0].astype(BF16)
    mix_p, states = _mixer(x_prompt, mod, ctx_row, False, None, True, False, norm1_g[0], w_in[0],
                           w_fmix[0])
    (mix_s,) = _mixer(x_sample, mod, lat_row, True, state_ret[:, 0], False, True, norm1_g[0],
                      w_in[0], w_fmix[0])
    x1p, h2p, affp = _post(x_prompt.reshape(tp, D_MODEL), mix_p, mod, ctx_row, None,
                           norm2_g[0], w_out_bf, w_router[0])
    x1s, h2s, affs = _post(x_sample.reshape(ts, D_MODEL), mix_s, mod, lat_row, dec_seq,
                           norm2_g[0], w_out_bf, w_router[0])

    (slot_p, slot_te_p, gate_p, starts_p), (slot_s, slot_te_s, gate_s, starts_s) = _route(affp, affs)

    yp, ys = _experts(starts_p, starts_s, h2p, h2s, slot_p, slot_s, gate_p, gate_s,
                      w_gate[0], w_up[0], w_down[0])

    out_p = _combine(starts_p, x1p, slot_te_p, mod, ctx_row, None, final_g, yp)
    out_s = _combine(starts_s, x1s, slot_te_s, mod, lat_row, dec_seq, final_g, ys)

    y_prompt = out_p.reshape(bp, seq, D_MODEL)
    y_sample = out_s.reshape(bs, dec_seq, D_MODEL)
    state_new = states.reshape(bp, 1, 2, N_RET_HEADS, HEAD_DIM, HEAD_DIM).astype(x_prompt.dtype)
    return (y_prompt, y_sample, state_new)
```

```python
import functools
import math

import jax
import jax.numpy as jnp
import numpy as np
from jax import lax
from jax.experimental import pallas as pl
from jax.experimental.pallas import tpu as pltpu

D_MODEL = 1024
D_FOURIER = 512
N_FOURIER_GROUPS = 4
FOURIER_GROUP_W = 128
D_RET = 512
N_RET_HEADS = 4
HEAD_DIM = 128
CHUNK = 256
GRID_W = 64
N_EXPERTS = 16
EC_CAPACITY_FACTOR = 2
D_EXPERT_FF = 2816
ROPE_BASE = 10000.0
EPS = 1e-6
D_IN_PROJ = D_FOURIER + 5 * D_RET
LOG_GAMMA_FWD = np.log(1.0 - 2.0 ** (-5.0 - np.arange(N_RET_HEADS))).astype(np.float32)
LOG_GAMMA_BWD = np.log(1.0 - 2.0 ** (-5.5 - np.arange(N_RET_HEADS))).astype(np.float32)

LANES = 128
BF16_ROWS = 16
TOKEN_BLOCK = 256
SLOT_WINDOW = 64
PACK = TOKEN_BLOCK // SLOT_WINDOW
GATHER_UNROLL = 4
Y_CHUNKS = 4
FF_TILE = 256
N_FF_TILES = D_EXPERT_FF // FF_TILE
MOD_TILE = 1024
MIXER_ROWS = 1024
POST_ROWS = 512
POST_PART = 256
POST_BUFFERS = 3
COMBINE_ROWS = 512
VMEM_LIMIT = 56 * 1024 * 1024

F32 = jnp.float32
BF16 = jnp.bfloat16


def _dot(a, b):
    return jnp.dot(a, b, preferred_element_type=F32)


def _dot_nt(a, b):
    return lax.dot_general(a, b, (((1,), (1,)), ((), ())), preferred_element_type=F32)


def _silu(x):
    return x * jax.nn.sigmoid(x)


def _mod_kernel(condt_ref, w_ref, b_ref, out_ref, *, n_cond):
    s = _silu(condt_ref[...])
    w = w_ref[...]
    out_ref[...] = jnp.zeros(out_ref.shape, F32)
    for r in range(n_cond):
        out_ref[r] = jnp.sum(w * s[:, r:r + 1], axis=0, keepdims=True) + b_ref[...]


def _modulation(cond_rows, w_mod, b_mod):
    n_cond = cond_rows.shape[0]
    condt = jnp.zeros((D_MODEL, 8), F32).at[:, :n_cond].set(cond_rows.T)
    n_out = w_mod.shape[1]
    return pl.pallas_call(
        functools.partial(_mod_kernel, n_cond=n_cond),
        out_shape=jax.ShapeDtypeStruct((8, 1, n_out), F32),
        grid=(n_out // MOD_TILE,),
        in_specs=[
            pl.BlockSpec((D_MODEL, 8), lambda j: (0, 0)),
            pl.BlockSpec((D_MODEL, MOD_TILE), lambda j: (0, j)),
            pl.BlockSpec((1, MOD_TILE), lambda j: (0, j)),
        ],
        out_specs=pl.BlockSpec((8, 1, MOD_TILE), lambda j: (0, 0, j)),
        compiler_params=pltpu.CompilerParams(dimension_semantics=("arbitrary",)),
        name="mod",
    )(condt, w_mod, b_mod.reshape(1, n_out))


def _rms(x):
    return x * lax.rsqrt(jnp.mean(x * x, axis=-1, keepdims=True) + EPS)


def _groupnorm(o):
    mu = jnp.mean(o, axis=-1, keepdims=True)
    c = o - mu
    return c * lax.rsqrt(jnp.mean(c * c, axis=-1, keepdims=True) + EPS)


def _split_hi_lo(x):
    hi = x.astype(BF16)
    lo = (x - hi.astype(F32)).astype(BF16)
    return hi, lo


def _mixer_kernel(*refs, n, use_rope, has_state_in, emit_state):
    it = iter(refs)
    x_ref, mod_ref, g1_ref, win_ref, wfmix_ref = (next(it) for _ in range(5))
    cw_ref, cn_ref, sn_ref, dmat_ref, qdec_ref, kdec_ref, sdec_ref = (next(it) for _ in range(7))
    cos_ref = sin_ref = s0_ref = st_ref = None
    if use_rope:
        cos_ref, sin_ref = next(it), next(it)
    if has_state_in:
        s0_ref = next(it)
    mix_ref = next(it)
    if emit_state:
        st_ref = next(it)
    p_ref, of_ref, ob_ref = next(it), next(it), next(it)

    n_seq = MIXER_ROWS // n
    chunks_per_seq = n // CHUNK
    mod = mod_ref[0]
    shift1 = mod[:, 0:D_MODEL]
    scale1 = mod[:, D_MODEL:2 * D_MODEL]

    h = (_rms(x_ref[...]) * g1_ref[...] * (1.0 + scale1) + shift1).astype(BF16)
    for j in range(D_IN_PROJ // 512):
        p_ref[:, j * 512:(j + 1) * 512] = _dot(h, win_ref[:, j * 512:(j + 1) * 512].astype(BF16))

    xf = p_ref[:, 0:D_FOURIER].astype(BF16)
    xc, xs = [], []
    cw = cw_ref[...].astype(BF16)
    for g in range(N_FOURIER_GROUPS):
        t = _dot(xf[:, g * FOURIER_GROUP_W:(g + 1) * FOURIER_GROUP_W], cw)
        xc.append(t[:, :FOURIER_GROUP_W].astype(BF16))
        xs.append(t[:, FOURIER_GROUP_W:].astype(BF16))
    xc = jnp.concatenate(xc, axis=1)
    xs = jnp.concatenate(xs, axis=1)
    cn = cn_ref[...].astype(BF16)
    sn = sn_ref[...].astype(BF16)
    for s in range(n_seq):
        rs = slice(s * n, (s + 1) * n)
        fre = (_dot(cn, xc[rs]) - _dot(sn, xs[rs])) * (1.0 / math.sqrt(n * FOURIER_GROUP_W))
        fre = fre.astype(BF16)
        for g in range(N_FOURIER_GROUPS):
            sl = slice(g * FOURIER_GROUP_W, (g + 1) * FOURIER_GROUP_W)
            mix_ref[rs, sl] = _dot(fre[:, sl], wfmix_ref[g].astype(BF16)).astype(BF16)

    for hh in range(N_RET_HEADS):
        base = D_FOURIER + hh * HEAD_DIM
        q = p_ref[:, base:base + HEAD_DIM]
        k = p_ref[:, base + D_RET:base + D_RET + HEAD_DIM]
        v = p_ref[:, base + 2 * D_RET:base + 2 * D_RET + HEAD_DIM]
        if use_rope:
            lane = lax.broadcasted_iota(jnp.int32, (MIXER_ROWS, HEAD_DIM), 1)
            first = (lane % 64) < 32

            def rope(t):
                swapped = jnp.where(first, pltpu.roll(t, HEAD_DIM - 32, 1), pltpu.roll(t, 32, 1))
                return t * cos_ref[...] + swapped * sin_ref[...]

            q, k = rope(q), rope(k)
        k = k * (HEAD_DIM ** -0.5)
        qb, vb = q.astype(BF16), v.astype(BF16)
        kb = k.astype(BF16)

        def initial(s, direction):
            if has_state_in:
                return s0_ref[s, direction, hh]
            return jnp.zeros((HEAD_DIM, HEAD_DIM), F32)

        for s in range(n_seq):
            parts = []
            for ci in range(chunks_per_seq):
                c = s * chunks_per_seq + ci
                rs = slice(c * CHUNK, (c + 1) * CHUNK)
                qc, kc, vc = qb[rs], kb[rs], vb[rs]
                qk = _dot_nt(qc, kc)
                lhs = jnp.concatenate([(qk * dmat_ref[0, hh]).astype(BF16),
                                       (qk * dmat_ref[1, hh]).astype(BF16),
                                       (k[rs] * kdec_ref[0, hh]).T.astype(BF16),
                                       (k[rs] * kdec_ref[1, hh]).T.astype(BF16)], axis=0)
                parts.append((rs, qc, _dot(lhs, vc)))
            sf = initial(s, 0)
            for ci in range(chunks_per_seq):
                rs, qc, r = parts[ci]
                o = r[0:CHUNK]
                if has_state_in or ci > 0:
                    o = o + qdec_ref[0, hh] * _dot(qc, sf.astype(BF16))
                of_ref[rs, :] = o
                sf = sf * sdec_ref[0, hh] + r[2 * CHUNK:2 * CHUNK + HEAD_DIM]
            sb = initial(s, 1)
            for ci in reversed(range(chunks_per_seq)):
                rs, qc, r = parts[ci]
                o = r[CHUNK:2 * CHUNK]
                if has_state_in or ci < chunks_per_seq - 1:
                    o = o + qdec_ref[1, hh] * _dot(qc, sb.astype(BF16))
                ob_ref[rs, :] = o
                sb = sb * sdec_ref[1, hh] + r[2 * CHUNK + HEAD_DIM:]
            if emit_state:
                st_ref[s, 0, hh] = sf
                st_ref[s, 1, hh] = sb

        gf = p_ref[:, base + 3 * D_RET:base + 3 * D_RET + HEAD_DIM]
        gb = p_ref[:, base + 4 * D_RET:base + 4 * D_RET + HEAD_DIM]
        y = _silu(gf) * _groupnorm(of_ref[...]) + _silu(gb) * _groupnorm(ob_ref[...])
        mix_ref[:, base:base + HEAD_DIM] = y.astype(BF16)


def _post_kernel(x_hbm, mix_hbm, mod_ref, g2_ref, wout_ref, wr_ref, x1_ref, h2_ref, aff_ref,
                 xbuf_ref, mixbuf_ref, xsem_ref, mixsem_ref):
    i = pl.program_id(0)
    ahead = POST_BUFFERS - 1

    def block_copies(b):
        rows = pl.ds(pl.multiple_of(b * POST_ROWS, POST_ROWS), POST_ROWS)
        slot = b % POST_BUFFERS
        return (pltpu.make_async_copy(x_hbm.at[rows], xbuf_ref.at[slot], xsem_ref.at[slot]),
                pltpu.make_async_copy(mix_hbm.at[rows], mixbuf_ref.at[slot], mixsem_ref.at[slot]))

    @pl.when(i == 0)
    def _():
        for b in range(ahead):
            for copy in block_copies(b):
                copy.start()

    @pl.when(i + ahead < pl.num_programs(0))
    def _():
        for copy in block_copies(i + ahead):
            copy.start()

    for copy in block_copies(i):
        copy.wait()
    x_ref = xbuf_ref.at[i % POST_BUFFERS]
    mix_ref = mixbuf_ref.at[i % POST_BUFFERS]
    mod = mod_ref[0]
    gate1 = mod[:, 2 * D_MODEL:3 * D_MODEL]
    shift2 = mod[:, 3 * D_MODEL:4 * D_MODEL]
    scale2 = mod[:, 4 * D_MODEL:5 * D_MODEL]
    wr_hi, wr_lo = _split_hi_lo(wr_ref[...])
    wr_both = jnp.concatenate([wr_hi, wr_lo], axis=1)
    parts = [slice(p * POST_PART, (p + 1) * POST_PART) for p in range(POST_ROWS // POST_PART)]
    x1 = []
    for rows in parts:
        x1.append(x_ref[rows, :] + gate1 * _dot(mix_ref[rows, :], wout_ref[...]))
        x1_ref[rows, :] = x1[-1]
    for rows, x1_part in zip(parts, x1):
        h2 = _rms(x1_part) * g2_ref[...] * (1.0 + scale2) + shift2
        h2_hi, h2_lo = _split_hi_lo(h2)
        h2_ref[rows, :] = h2_hi
        by_hi = _dot(h2_hi, wr_both)
        logits = by_hi[:, :N_EXPERTS] + (_dot(h2_lo, wr_hi) + by_hi[:, N_EXPERTS:])
        z = jnp.exp(logits - jnp.max(logits, axis=-1, keepdims=True))
        aff = z / jnp.sum(z, axis=-1, keepdims=True)
        lanes = jnp.concatenate([aff, jnp.zeros((POST_PART, LANES - N_EXPERTS), F32)], axis=1)
        aff_ref[:, rows] = lanes.T[0:N_EXPERTS, :]


def _dft_consts(n):
    w = FOURIER_GROUP_W
    jw = np.arange(w)
    angw = 2.0 * np.pi * np.outer(jw, jw) / w
    cw = np.concatenate([np.cos(angw), np.sin(angw)], axis=1)
    jn = np.arange(n)
    angn = 2.0 * np.pi * (np.outer(jn, jn) % n) / n
    return (jnp.asarray(cw, F32), jnp.asarray(np.cos(angn), F32), jnp.asarray(np.sin(angn), F32))


def _retention_consts():
    i = np.arange(CHUNK, dtype=np.float64)
    diff = i[:, None] - i[None, :]
    dmat = np.zeros((2, N_RET_HEADS, CHUNK, CHUNK))
    qdec = np.zeros((2, N_RET_HEADS, CHUNK, HEAD_DIM))
    kdec = np.zeros((2, N_RET_HEADS, CHUNK, HEAD_DIM))
    sdec = np.zeros((2, N_RET_HEADS, HEAD_DIM, HEAD_DIM))
    for hh in range(N_RET_HEADS):
        lf = float(LOG_GAMMA_FWD[hh])
        lb = float(LOG_GAMMA_BWD[hh])
        dmat[0, hh] = np.where(diff >= 0, np.exp(lf * np.maximum(diff, 0.0)), 0.0)
        dmat[1, hh] = np.where(diff <= 0, np.exp(lb * np.maximum(-diff, 0.0)), 0.0)
        qdec[0, hh] = np.exp(lf * (i + 1.0))[:, None]
        qdec[1, hh] = np.exp(lb * (CHUNK - i))[:, None]
        kdec[0, hh] = np.exp(lf * (CHUNK - 1.0 - i))[:, None]
        kdec[1, hh] = np.exp(lb * i)[:, None]
        sdec[0, hh] = math.exp(lf * CHUNK)
        sdec[1, hh] = math.exp(lb * CHUNK)
    return tuple(jnp.asarray(a, F32) for a in (dmat, qdec, kdec, sdec))


def _rope_consts(n):
    rows_n = n // GRID_W
    row = np.repeat(np.arange(rows_n, dtype=np.float64), GRID_W)
    col = np.tile(np.arange(GRID_W, dtype=np.float64), rows_n)
    n_pairs = HEAD_DIM // 4
    freqs = (np.float32(ROPE_BASE) ** (-np.arange(n_pairs, dtype=np.float32) / n_pairs)).astype(np.float64)
    ar = row[:, None] * freqs[None, :]
    ac = col[:, None] * freqs[None, :]
    cos = np.concatenate([np.cos(ar), np.cos(ar), np.cos(ac), np.cos(ac)], axis=1)
    sin = np.concatenate([-np.sin(ar), np.sin(ar), -np.sin(ac), np.sin(ac)], axis=1)
    return jnp.asarray(cos, F32), jnp.asarray(sin, F32)


def _const_spec(shape):
    nd = len(shape)
    return pl.BlockSpec(shape, lambda b, _nd=nd: (0,) * _nd, pipeline_mode=pl.Buffered(1))


def _mod_row_spec(first_row, blocks_per_row):
    if blocks_per_row is None:
        return pl.BlockSpec((1, 1, 6 * D_MODEL), lambda b, *_: (first_row, 0, 0))
    return pl.BlockSpec((1, 1, 6 * D_MODEL), lambda b, *_: (first_row + b // blocks_per_row, 0, 0))


def _mixer(x, mod_rows, mod_first_row, mod_per_batch, state_in, emit_state, use_rope, g1, w_in,
           w_fmix):
    nb, n, _ = x.shape
    assert MIXER_ROWS % n == 0 and (nb * n) % MIXER_ROWS == 0
    n_seq = MIXER_ROWS // n
    has_state_in = state_in is not None
    cw, cn, sn = _dft_consts(n)
    dmat, qdec, kdec, sdec = _retention_consts()
    consts = [cw, cn, sn, dmat, qdec, kdec, sdec]
    if use_rope:
        assert n_seq == 1
        consts += list(_rope_consts(n))
    weights = [g1.reshape(1, D_MODEL), w_in, w_fmix]

    if mod_per_batch:
        assert n % MIXER_ROWS == 0
    state_spec = pl.BlockSpec((n_seq, 2, N_RET_HEADS, HEAD_DIM, HEAD_DIM), lambda b: (b, 0, 0, 0, 0))
    row_spec = pl.BlockSpec((MIXER_ROWS, D_MODEL), lambda b: (b, 0))
    in_specs = [row_spec, _mod_row_spec(mod_first_row, n // MIXER_ROWS if mod_per_batch else None)]
    in_specs += [_const_spec(a.shape) for a in weights + consts]
    args = [x.reshape(nb * n, D_MODEL), mod_rows] + weights + consts
    if has_state_in:
        in_specs.append(state_spec)
        args.append(state_in)

    out_shape = [jax.ShapeDtypeStruct((nb * n, D_MODEL), BF16)]
    out_specs = [row_spec]
    if emit_state:
        out_shape.append(jax.ShapeDtypeStruct((nb, 2, N_RET_HEADS, HEAD_DIM, HEAD_DIM), F32))
        out_specs.append(state_spec)

    return pl.pallas_call(
        functools.partial(_mixer_kernel, n=n, use_rope=use_rope, has_state_in=has_state_in,
                          emit_state=emit_state),
        out_shape=out_shape,
        grid=(nb * n // MIXER_ROWS,),
        in_specs=in_specs,
        out_specs=out_specs,
        scratch_shapes=[pltpu.VMEM((MIXER_ROWS, D_IN_PROJ), F32),
                        pltpu.VMEM((MIXER_ROWS, HEAD_DIM), F32), pltpu.VMEM((MIXER_ROWS, HEAD_DIM), F32)],
        compiler_params=pltpu.CompilerParams(dimension_semantics=("arbitrary",),
                                             vmem_limit_bytes=VMEM_LIMIT),
        name="mixer_rope" if use_rope else "mixer",
    )(*args)


def _post(x, mix, mod_rows, mod_first_row, tokens_per_mod_row, g2, w_out_bf, w_router):
    t = x.shape[0]
    assert tokens_per_mod_row is None or tokens_per_mod_row % POST_ROWS == 0
    blocks_per_row = None if tokens_per_mod_row is None else tokens_per_mod_row // POST_ROWS
    row_spec = pl.BlockSpec((POST_ROWS, D_MODEL), lambda b: (b, 0))
    assert t // POST_ROWS >= POST_BUFFERS - 1
    return pl.pallas_call(
        _post_kernel,
        out_shape=[jax.ShapeDtypeStruct((t, D_MODEL), F32),
                   jax.ShapeDtypeStruct((t, D_MODEL), BF16),
                   jax.ShapeDtypeStruct((N_EXPERTS, t), F32)],
        grid=(t // POST_ROWS,),
        in_specs=[pl.BlockSpec(memory_space=pl.ANY), pl.BlockSpec(memory_space=pl.ANY),
                  _mod_row_spec(mod_first_row, blocks_per_row),
                  _const_spec((1, D_MODEL)), _const_spec((D_MODEL, D_MODEL)),
                  _const_spec((D_MODEL, N_EXPERTS))],
        out_specs=[row_spec, row_spec, pl.BlockSpec((N_EXPERTS, POST_ROWS), lambda b: (0, b))],
        scratch_shapes=[pltpu.VMEM((POST_BUFFERS, POST_ROWS, D_MODEL), F32),
                        pltpu.VMEM((POST_BUFFERS, POST_ROWS, D_MODEL), BF16),
                        pltpu.SemaphoreType.DMA((POST_BUFFERS,)), pltpu.SemaphoreType.DMA((POST_BUFFERS,))],
        compiler_params=pltpu.CompilerParams(dimension_semantics=("arbitrary",),
                                             vmem_limit_bytes=VMEM_LIMIT),
        name="post",
    )(x, mix, mod_rows, g2.reshape(1, D_MODEL), w_out_bf, w_router)


def _route_kernel(*refs, sizes):
    n = len(sizes)
    affs = [ref[...] for ref in refs[:n]]
    u_ref = refs[n]
    outs = [refs[n + 1 + 4 * g:n + 5 + 4 * g] for g in range(n)]

    def count(mask):
        return jnp.sum(mask.astype(jnp.int32), axis=1, keepdims=True)

    def as_float(word):
        return lax.bitcast_convert_type(word, F32)

    def zeros():
        return tuple(jnp.zeros((N_EXPERTS, 1), jnp.int32) for _ in sizes)

    def value_step(i, curs):
        bit = jnp.left_shift(jnp.int32(1), 30 - i)
        return tuple(jnp.where(count(aff >= as_float(cur | bit)) >= cap, cur | bit, cur)
                     for aff, (_, cap), cur in zip(affs, sizes, curs))

    thrs = lax.fori_loop(0, 31, value_step, zeros())
    gts = [aff >= as_float(thr + 1) for aff, thr in zip(affs, thrs)]
    eqs = [(aff >= as_float(thr)) & jnp.logical_not(gt) for aff, thr, gt in zip(affs, thrs, gts)]
    needs = [cap - count(gt) for (_, cap), gt in zip(sizes, gts)]
    toks = [lax.broadcasted_iota(jnp.int32, (N_EXPERTS, t), 1) for t, _ in sizes]
    nbits = [t.bit_length() - 1 for t, _ in sizes]

    def index_step(i, curs):
        new = []
        for eq, need, tok, bits, cur in zip(eqs, needs, toks, nbits, curs):
            shift = bits - 1 - i
            cand = cur | jnp.where(shift >= 0, jnp.left_shift(jnp.int32(1), jnp.maximum(shift, 0)), 0)
            new.append(jnp.where(count(eq & (tok < cand)) < need, cand, cur))
        return tuple(new)

    lasts = lax.fori_loop(0, max(nbits), index_step, zeros())

    for aff, (t, _), gt, eq, tok, last, (slot_ref, slot_te_ref, gate_ref, starts_ref) in zip(
            affs, sizes, gts, eqs, toks, lasts, outs):
        self = jnp.where(gt | (eq & (tok <= last)), 1.0, 0.0).astype(F32)
        carry = jnp.zeros((N_EXPERTS, 1), F32)
        starts_ref[...] = jnp.zeros(starts_ref.shape, jnp.int32)
        for b in range(t // TOKEN_BLOCK):
            sl = slice(b * TOKEN_BLOCK, (b + 1) * TOKEN_BLOCK)
            sbf = self[:, sl]
            pre = _dot(sbf.astype(BF16), u_ref[...]) + carry
            slots = jnp.where(sbf > 0.5, pre.astype(jnp.int32), -1)
            slot_ref[:, b, :] = slots
            rows = jnp.concatenate([slots, jnp.zeros((LANES - N_EXPERTS, TOKEN_BLOCK), jnp.int32)],
                                   axis=0)
            slot_te_ref[sl, :] = rows.T[:, 0:N_EXPERTS]
            gate_ref[:, b, :] = aff[:, sl]
            starts_ref[:, b:b + 1] = carry.astype(jnp.int32)
            carry = carry + jnp.sum(sbf, axis=1, keepdims=True)
        nblk = t // TOKEN_BLOCK
        starts_ref[:, nblk:nblk + 1] = carry.astype(jnp.int32)


def _route(*affs_et):
    sizes = tuple((a.shape[1], EC_CAPACITY_FACTOR * a.shape[1] // N_EXPERTS) for a in affs_et)
    upper = jnp.asarray(np.triu(np.ones((TOKEN_BLOCK, TOKEN_BLOCK)), 1), BF16)
    whole = lambda shape: pl.BlockSpec(shape, lambda i, _n=len(shape): (0,) * _n)
    out_shape, out_specs = [], []
    for t, _ in sizes:
        nblk = t // TOKEN_BLOCK
        assert nblk + 1 <= LANES
        for shape, dtype in (((N_EXPERTS, nblk, TOKEN_BLOCK), jnp.int32), ((t, N_EXPERTS), jnp.int32),
                             ((N_EXPERTS, nblk, TOKEN_BLOCK), F32), ((N_EXPERTS, LANES), jnp.int32)):
            out_shape.append(jax.ShapeDtypeStruct(shape, dtype))
            out_specs.append(whole(shape))
    outs = pl.pallas_call(
        functools.partial(_route_kernel, sizes=sizes),
        out_shape=out_shape,
        grid=(1,),
        in_specs=[whole(a.shape) for a in affs_et] + [whole((TOKEN_BLOCK, TOKEN_BLOCK))],
        out_specs=out_specs,
        compiler_params=pltpu.CompilerParams(dimension_semantics=("arbitrary",)),
        name="route",
    )(*affs_et, upper)
    return [outs[4 * g:4 * g + 4] for g in range(len(sizes))]


def _pack_windows(starts_ref, b, experts, cap):
    first = [jnp.minimum((starts_ref[e, b] // BF16_ROWS) * BF16_ROWS, cap - SLOT_WINDOW)
             for e in experts]
    rows = [jnp.where(starts_ref[e, b + 1] > starts_ref[e, b], starts_ref[e, b + 1] - w, 0)
            for e, w in zip(experts, first)]
    return first, pl.cdiv(functools.reduce(jnp.maximum, rows), SLOT_WINDOW)


def _block_copy(hbm_ref, vmem_ref, sem_ref, b):
    rows = pl.ds(pl.multiple_of(b * TOKEN_BLOCK, TOKEN_BLOCK), TOKEN_BLOCK)
    return pltpu.make_async_copy(hbm_ref.at[rows], vmem_ref.at[rows], sem_ref.at[b])


def _gather_group(g, starts_ref, slot_ref, gate_ref, h2_ref, xs_ref, gs_ref, row0, t, cap, arrive):
    sub = lax.broadcasted_iota(jnp.int32, (SLOT_WINDOW, TOKEN_BLOCK), 0)
    experts = [g * PACK + j for j in range(PACK)]
    assert (t // TOKEN_BLOCK) % GATHER_UNROLL == 0

    def window(b, first, i):
        hb = pl.ds(pl.multiple_of(b * TOKEN_BLOCK, TOKEN_BLOCK), TOKEN_BLOCK)
        hits, dst = [], []
        for j in range(PACK):
            lo = first[j] + i * SLOT_WINDOW
            w = jnp.minimum(lo, cap - SLOT_WINDOW)
            srow = slot_ref[j, pl.ds(b, 1), :]
            hits.append((srow == w + sub) & (srow >= lo))
            dst.append(pl.ds(pl.multiple_of(row0 + w, BF16_ROWS), SLOT_WINDOW))
        onehot = jnp.concatenate([jnp.where(h, 1.0, 0.0) for h in hits], axis=0).astype(BF16)
        got = _dot(onehot, h2_ref[hb, :])
        for j in range(PACK):
            piece = got[j * SLOT_WINDOW:(j + 1) * SLOT_WINDOW].astype(BF16)
            xs_ref[j, dst[j], :] = xs_ref[j, dst[j], :] + piece
            grow = gate_ref[j, pl.ds(b, 1), :]
            gs_ref[j, dst[j], :] += jnp.sum(jnp.where(hits[j], grow, 0.0), axis=1, keepdims=True)

    def blocks(q, carry):
        pending = []
        for u in range(GATHER_UNROLL):
            arrive(q * GATHER_UNROLL + u)
        for u in range(GATHER_UNROLL):
            b = q * GATHER_UNROLL + u
            first, n_windows = _pack_windows(starts_ref, b, experts, cap)
            window(b, first, 0)
            pending.append((b, first, n_windows))
        for b, first, n_windows in pending:
            def more(i, carry, b=b, first=first):
                window(b, first, i)
                return carry

            lax.fori_loop(1, n_windows, more, 0)
        return carry

    lax.fori_loop(0, t // TOKEN_BLOCK // GATHER_UNROLL, blocks, 0)


def _experts_kernel(sp_ref, ss_ref, h2p_hbm, h2s_hbm, slotp_ref, slots_ref, gatep_ref, gates_ref,
                    wg_ref, wu_ref, wd_ref, yp_ref, ys_ref, xs_ref, gs_ref, acc_ref,
                    h2p_ref, h2s_ref, semp_ref, sems_ref, *, tp, ts, capp, caps):
    g = pl.program_id(0)
    step = pl.program_id(1)
    j = step // N_FF_TILES
    f = step % N_FF_TILES
    loading = [(h2p_hbm, h2p_ref, semp_ref, tp // TOKEN_BLOCK),
               (h2s_hbm, h2s_ref, sems_ref, ts // TOKEN_BLOCK)]

    @pl.when((step == 0) & (g == 0))
    def _():
        for hbm_ref, vmem_ref, sem_ref, n_blocks in loading:
            for b in range(n_blocks):
                _block_copy(hbm_ref, vmem_ref, sem_ref, b).start(priority=1)

    def arrive(hbm_ref, vmem_ref, sem_ref, _):
        def wait(b):
            @pl.when(g == 0)
            def _():
                _block_copy(hbm_ref, vmem_ref, sem_ref, b).wait()

        return wait

    @pl.when(step == 0)
    def _():
        xs_ref[...] = jnp.zeros(xs_ref.shape, BF16)
        gs_ref[...] = jnp.zeros(gs_ref.shape, F32)
        _gather_group(g, sp_ref, slotp_ref, gatep_ref, h2p_ref, xs_ref, gs_ref, 0, tp, capp,
                      arrive(*loading[0]))
        _gather_group(g, ss_ref, slots_ref, gates_ref, h2s_ref, xs_ref, gs_ref, capp, ts, caps,
                      arrive(*loading[1]))

    @pl.when(f == 0)
    def _():
        acc_ref[...] = jnp.zeros(acc_ref.shape, F32)

    x = xs_ref[j]
    a = _dot(x, wg_ref[0].astype(BF16))
    u = _dot(x, wu_ref[0].astype(BF16))
    acc_ref[...] += _dot((_silu(a) * u).astype(BF16), wd_ref[0].astype(BF16))

    @pl.when(f == N_FF_TILES - 1)
    def _():
        yp_ref[0] = (acc_ref[0:capp, :] * gs_ref[j, 0:capp, :]).astype(BF16)
        ys_ref[0] = (acc_ref[capp:capp + caps, :] * gs_ref[j, capp:capp + caps, :]).astype(BF16)


def _experts(starts_p, starts_s, h2p, h2s, slot_p, slot_s, gate_p, gate_s, w_gate, w_up, w_down):
    tp, ts = h2p.shape[0], h2s.shape[0]
    capp = EC_CAPACITY_FACTOR * tp // N_EXPERTS
    caps = EC_CAPACITY_FACTOR * ts // N_EXPERTS
    rows = capp + caps
    nbp, nbs = tp // TOKEN_BLOCK, ts // TOKEN_BLOCK
    expert = lambda g, s: g * PACK + s // N_FF_TILES
    grid_spec = pltpu.PrefetchScalarGridSpec(
        num_scalar_prefetch=2,
        grid=(N_EXPERTS // PACK, PACK * N_FF_TILES),
        in_specs=[
            pl.BlockSpec(memory_space=pl.ANY),
            pl.BlockSpec(memory_space=pl.ANY),
            pl.BlockSpec((PACK, nbp, TOKEN_BLOCK), lambda g, s, *_: (g, 0, 0)),
            pl.BlockSpec((PACK, nbs, TOKEN_BLOCK), lambda g, s, *_: (g, 0, 0)),
            pl.BlockSpec((PACK, nbp, TOKEN_BLOCK), lambda g, s, *_: (g, 0, 0)),
            pl.BlockSpec((PACK, nbs, TOKEN_BLOCK), lambda g, s, *_: (g, 0, 0)),
            pl.BlockSpec((1, D_MODEL, FF_TILE), lambda g, s, *_: (expert(g, s), 0, s % N_FF_TILES)),
            pl.BlockSpec((1, D_MODEL, FF_TILE), lambda g, s, *_: (expert(g, s), 0, s % N_FF_TILES)),
            pl.BlockSpec((1, FF_TILE, D_MODEL), lambda g, s, *_: (expert(g, s), s % N_FF_TILES, 0)),
        ],
        out_specs=[
            pl.BlockSpec((1, capp, D_MODEL), lambda g, s, *_: (expert(g, s), 0, 0)),
            pl.BlockSpec((1, caps, D_MODEL), lambda g, s, *_: (expert(g, s), 0, 0)),
        ],
        scratch_shapes=[pltpu.VMEM((PACK, rows, D_MODEL), BF16), pltpu.VMEM((PACK, rows, 1), F32),
                        pltpu.VMEM((rows, D_MODEL), F32),
                        pltpu.VMEM((tp, D_MODEL), BF16), pltpu.VMEM((ts, D_MODEL), BF16),
                        pltpu.SemaphoreType.DMA((nbp,)), pltpu.SemaphoreType.DMA((nbs,))],
    )
    return pl.pallas_call(
        functools.partial(_experts_kernel, tp=tp, ts=ts, capp=capp, caps=caps),
        out_shape=[jax.ShapeDtypeStruct((N_EXPERTS, capp, D_MODEL), BF16),
                   jax.ShapeDtypeStruct((N_EXPERTS, caps, D_MODEL), BF16)],
        grid_spec=grid_spec,
        compiler_params=pltpu.CompilerParams(dimension_semantics=("arbitrary", "arbitrary"),
                                             vmem_limit_bytes=VMEM_LIMIT),
        name="experts",
    )(starts_p, starts_s, h2p, h2s, slot_p, slot_s, gate_p, gate_s, w_gate, w_up, w_down)


def _combine_kernel(st_ref, x1_ref, slot_ref, mod_ref, fg_ref, y_hbm, out_ref, acc_ref, y_ref,
                    sem_ref, arrived_ref, *, cap):
    step = pl.program_id(0)
    n_chunks = Y_CHUNKS
    chunk_rows = cap // Y_CHUNKS
    n_parts = COMBINE_ROWS // TOKEN_BLOCK
    last_block = step * n_parts + n_parts - 1

    def chunk_copy(c):
        rows = pl.ds(c * chunk_rows, chunk_rows)
        return pltpu.make_async_copy(y_hbm.at[:, rows, :], y_ref.at[:, rows, :], sem_ref.at[c])

    @pl.when(step == 0)
    def _():
        arrived_ref[0] = 0
        for c in range(n_chunks):
            chunk_copy(c).start(priority=1)

    top = functools.reduce(jnp.maximum, [st_ref[e, last_block + 1] for e in range(N_EXPERTS)])
    want = pl.cdiv(jnp.minimum(top + SLOT_WINDOW, cap), chunk_rows)
    have = arrived_ref[0]
    for c in range(n_chunks):
        @pl.when((c >= have) & (c < want))
        def _(c=c):
            chunk_copy(c).wait()
    arrived_ref[0] = jnp.maximum(have, want)

    lane = lax.broadcasted_iota(jnp.int32, (TOKEN_BLOCK, PACK * SLOT_WINDOW), 1)
    gate2 = mod_ref[0][:, 5 * D_MODEL:6 * D_MODEL]

    def scatter(slot, experts, first, i):
        target = None
        windows = []
        for j, e in enumerate(experts):
            lo = first[j] + i * SLOT_WINDOW
            w = jnp.minimum(lo, cap - SLOT_WINDOW)
            sc = slot[:, e:e + 1]
            col = jnp.where(sc >= lo, sc - w + j * SLOT_WINDOW, -1)
            target = col if target is None else jnp.where(lane < j * SLOT_WINDOW, target, col)
            windows.append(y_ref[e, pl.ds(pl.multiple_of(w, BF16_ROWS), SLOT_WINDOW), :])
        onehot = jnp.where(target == lane, 1.0, 0.0).astype(BF16)
        return _dot(onehot, jnp.concatenate(windows, axis=0))

    def finish(rows):
        x2 = x1_ref[rows, :] + gate2 * acc_ref[rows, :]
        out_ref[rows, :] = _rms(x2) * fg_ref[...]

    parts = []
    for p in range(n_parts):
        rows = slice(p * TOKEN_BLOCK, (p + 1) * TOKEN_BLOCK)
        slot = slot_ref[rows, :]
        groups = []
        total = None
        for g in range(N_EXPERTS // PACK):
            experts = list(range(g * PACK, (g + 1) * PACK))
            first, n_windows = _pack_windows(st_ref, step * n_parts + p, experts, cap)
            groups.append((experts, first, n_windows))
            part = scatter(slot, experts, first, 0)
            total = part if total is None else total + part
        acc_ref[rows, :] = total
        parts.append((rows, slot, groups))
    for rows, _, _ in parts:
        finish(rows)

    for rows, slot, groups in parts:
        most = functools.reduce(jnp.maximum, [n_windows for _, _, n_windows in groups])

        @pl.when(most > 1)
        def _(rows=rows, slot=slot, groups=groups):
            for experts, first, n_windows in groups:
                def more(i, carry, experts=experts, first=first):
                    acc_ref[rows, :] += scatter(slot, experts, first, i)
                    return carry

                lax.fori_loop(1, n_windows, more, 0)
            finish(rows)


def _combine(starts, x1, slot_te, mod_rows, mod_first_row, tokens_per_mod_row, final_g, y):
    t = x1.shape[0]
    cap = y.shape[1]
    assert tokens_per_mod_row is None or tokens_per_mod_row % COMBINE_ROWS == 0
    steps_per_row = None if tokens_per_mod_row is None else tokens_per_mod_row // COMBINE_ROWS
    grid_spec = pltpu.PrefetchScalarGridSpec(
        num_scalar_prefetch=1,
        grid=(t // COMBINE_ROWS,),
        in_specs=[
            pl.BlockSpec((COMBINE_ROWS, D_MODEL), lambda b, *_: (b, 0)),
            pl.BlockSpec((COMBINE_ROWS, N_EXPERTS), lambda b, *_: (b, 0)),
            _mod_row_spec(mod_first_row, steps_per_row),
            pl.BlockSpec((1, D_MODEL), lambda b, *_: (0, 0)),
            pl.BlockSpec(memory_space=pl.ANY),
        ],
        out_specs=pl.BlockSpec((COMBINE_ROWS, D_MODEL), lambda b, *_: (b, 0)),
        scratch_shapes=[pltpu.VMEM((COMBINE_ROWS, D_MODEL), F32), pltpu.VMEM(y.shape, BF16),
                        pltpu.SemaphoreType.DMA((Y_CHUNKS,)), pltpu.SMEM((1,), jnp.int32)],
    )
    return pl.pallas_call(
        functools.partial(_combine_kernel, cap=cap),
        out_shape=jax.ShapeDtypeStruct((t, D_MODEL), F32),
        grid_spec=grid_spec,
        compiler_params=pltpu.CompilerParams(dimension_semantics=("arbitrary",),
                                             vmem_limit_bytes=VMEM_LIMIT),
        name="combine",
    )(starts, x1, slot_te, mod_rows, final_g.reshape(1, D_MODEL), y)


def kernel(x_prompt, x_sample, state_ret, c, c_ctx, norm1_g, norm2_g, final_g, w_mod, b_mod, w_in,
           w_fmix, w_out, w_router, w_gate, w_up, w_down):
    bp, seq, _ = x_prompt.shape
    bs, dec_seq, _ = x_sample.shape
    assert w_mod.shape[0] == 1, "single-layer trunk"
    tp, ts = bp * seq, bs * dec_seq

    cond = jnp.concatenate([c_ctx[None, :], c], axis=0)
    mod = _modulation(cond, w_mod[0], b_mod[0])
    ctx_row, lat_row = 0, 1

    w_out_bf = w_out[0].astype(BF16)
    mix_p, states = _mixer(x_prompt, mod, ctx_row, False, None, True, False, norm1_g[0], w_in[0],
                           w_fmix[0])
    (mix_s,) = _mixer(x_sample, mod, lat_row, True, state_ret[:, 0], False, True, norm1_g[0],
                      w_in[0], w_fmix[0])
    x1p, h2p, affp = _post(x_prompt.reshape(tp, D_MODEL), mix_p, mod, ctx_row, None,
                           norm2_g[0], w_out_bf, w_router[0])
    x1s, h2s, affs = _post(x_sample.reshape(ts, D_MODEL), mix_s, mod, lat_row, dec_seq,
                           norm2_g[0], w_out_bf, w_router[0])

    (slot_p, slot_te_p, gate_p, starts_p), (slot_s, slot_te_s, gate_s, starts_s) = _route(affp, affs)

    yp, ys = _experts(starts_p, starts_s, h2p, h2s, slot_p, slot_s, gate_p, gate_s,
                      w_gate[0], w_up[0], w_down[0])

    out_p = _combine(starts_p, x1p, slot_te_p, mod, ctx_row, None, final_g, yp)
    out_s = _combine(starts_s, x1s, slot_te_s, mod, lat_row, dec_seq, final_g, ys)

    y_prompt = out_p.reshape(bp, seq, D_MODEL)
    y_sample = out_s.reshape(bs, dec_seq, D_MODEL)
    state_new = states.reshape(bp, 1, 2, N_RET_HEADS, HEAD_DIM, HEAD_DIM).astype(x_prompt.dtype)
    return (y_prompt, y_sample, state_new)
```
